```python
import jax, jax.numpy as jnp
from jax import lax
import numpy as np

D_MODEL = 1024
BATCH = 2
SEQ = 8192
DEPTH = 1

N_META = 16
GRID_W = 64
EXPAND = 2
D_MIX = EXPAND * D_MODEL
GLA_WIDTH = D_MIX // 2
GLA_HEADS = 4
GLA_DV = GLA_WIDTH // GLA_HEADS
GLA_DK = GLA_DV // 2
GLA_KEY_WIDTH = GLA_HEADS * GLA_DK
GLA_GATE_RANK = 16
GLA_TAU = 16.0
GLA_CHUNK = 64
NA_WIDTH = D_MIX - GLA_WIDTH
NA_HEAD_DIM = 64
NA_HEADS = NA_WIDTH // NA_HEAD_DIM
NA_WIN_H_MAX = 8
NA_WIN_W = 16
RMS_EPS = 1e-6

IN_SPLIT_WIDTHS = (GLA_KEY_WIDTH, GLA_KEY_WIDTH, GLA_WIDTH, GLA_GATE_RANK, GLA_GATE_RANK, GLA_WIDTH,
                   NA_WIDTH, NA_WIDTH, NA_WIDTH, NA_WIDTH)
D_IN_PROJ = sum(IN_SPLIT_WIDTHS)

kernel_name = "hybrid_gla_natten_meta_encoder"


def _rmsnorm(x, g):
    xf = x.astype(jnp.float32)
    y = xf * lax.rsqrt(jnp.mean(xf * xf, axis=-1, keepdims=True) + RMS_EPS)
    return (y * g.astype(jnp.float32)).astype(x.dtype)


def _gla_chunked(q, k, v, g):
    B, T, H, DK = q.shape
    DV = v.shape[-1]
    C = GLA_CHUNK
    N = T // C
    q = q.reshape(B, N, C, H, DK)
    k = k.reshape(B, N, C, H, DK)
    v = v.reshape(B, N, C, H, DV)
    G = jnp.cumsum(g.astype(jnp.float32).reshape(B, N, C, H, DK), axis=2)
    G_last = G[:, :, -1:]
    q_in = q * jnp.exp(G)
    k_in = k * jnp.exp(-G)
    k_st = k * jnp.exp(G_last - G)
    mask = jnp.tril(jnp.ones((C, C), dtype=bool))
    A = jnp.einsum('bnihd,bnjhd->bnhij', q_in, k_in)
    A = jnp.where(mask, A, 0.0)
    o_intra = jnp.einsum('bnhij,bnjhv->bnihv', A, v)
    U = jnp.einsum('bnjhd,bnjhv->bnhdv', k_st, v)
    a = jnp.exp(G_last[:, :, 0])

    def step(S, inp):
        a_n, U_n = inp
        return a_n[..., None] * S + U_n, S

    S0 = jnp.zeros((B, H, DK, DV), dtype=U.dtype)
    _, S_prev = lax.scan(step, S0, (jnp.moveaxis(a, 1, 0), jnp.moveaxis(U, 1, 0)))
    S_prev = jnp.moveaxis(S_prev, 0, 1)
    o_inter = jnp.einsum('bnihd,bnhdv->bnihv', q_in, S_prev)
    return (o_intra + o_inter).reshape(B, T, H, DV).astype(v.dtype)


def _neighbourhood_attention(q, k, v, qm, km, vm, rpb, meta_bias):
    B, R, W, H, D = q.shape
    wh = min(NA_WIN_H_MAX, R)
    ww = NA_WIN_W
    scale = D ** -0.5
    cols = jnp.arange(W)
    cs = jnp.clip(cols - ww // 2, 0, W - ww)
    col_idx = cs[:, None] + jnp.arange(ww)[None, :]
    col_bias_idx = col_idx - cols[:, None] + (NA_WIN_W - 1)
    mb = meta_bias[None, :, None, :]

    def row_block(r):
        rs = jnp.clip(r - wh // 2, 0, R - wh)
        q_r = lax.dynamic_index_in_dim(q, r, axis=1, keepdims=False)
        k_blk = lax.dynamic_slice_in_dim(k, rs, wh, axis=1)
        v_blk = lax.dynamic_slice_in_dim(v, rs, wh, axis=1)
        k_g = k_blk[:, :, col_idx]
        v_g = v_blk[:, :, col_idx]
        s_loc = jnp.einsum('bchd,bicjhd->bhcij', q_r, k_g) * scale
        row_bias_idx = rs + jnp.arange(wh) - r + (NA_WIN_H_MAX - 1)
        bias = rpb[:, row_bias_idx][:, :, col_bias_idx]
        s_loc = s_loc + jnp.transpose(bias, (0, 2, 1, 3))[None]
        s_meta = jnp.einsum('bchd,bmhd->bhcm', q_r, km) * scale + mb
        s = jnp.concatenate([s_loc.reshape(B, H, W, wh * ww), s_meta], axis=-1)
        p = jax.nn.softmax(s.astype(jnp.float32), axis=-1).astype(v.dtype)
        p_loc = p[..., :wh * ww].reshape(B, H, W, wh, ww)
        p_meta = p[..., wh * ww:]
        return (jnp.einsum('bhcij,bicjhd->bchd', p_loc, v_g)
                + jnp.einsum('bhcm,bmhd->bchd', p_meta, vm))

    o = lax.map(row_block, jnp.arange(R))
    o = jnp.moveaxis(o, 0, 1)
    s_mm = jnp.einsum('bqhd,bmhd->bhqm', qm, km) * scale + mb
    p_mm = jax.nn.softmax(s_mm.astype(jnp.float32), axis=-1).astype(vm.dtype)
    om = jnp.einsum('bhqm,bmhd->bqhd', p_mm, vm)
    return o, om


def setup_inputs(seed: int = 0) -> dict:
    key = jax.random.key(seed)
    ks = jax.random.split(key, 16)
    f32 = jnp.float32
    nrm = lambda k, s, sc: jax.random.normal(k, s, f32) * sc
    return {
        "x": nrm(ks[0], (BATCH, SEQ, D_MODEL), 1.0),
        "meta_tokens": nrm(ks[1], (N_META, D_MODEL), 1.0),
        "norm_g": 1.0 + nrm(ks[2], (DEPTH, D_MODEL), 0.02),
        "w_in": nrm(ks[3], (DEPTH, D_MODEL, D_IN_PROJ), D_MODEL ** -0.5),
        "w_decay_fwd": nrm(ks[4], (DEPTH, GLA_GATE_RANK, GLA_KEY_WIDTH), GLA_GATE_RANK ** -0.5),
        "b_decay_fwd": nrm(ks[5], (DEPTH, GLA_KEY_WIDTH), 0.1),
        "w_decay_bwd": nrm(ks[6], (DEPTH, GLA_GATE_RANK, GLA_KEY_WIDTH), GLA_GATE_RANK ** -0.5),
        "b_decay_bwd": nrm(ks[7], (DEPTH, GLA_KEY_WIDTH), 0.1),
        "gla_out_norm_g": 1.0 + nrm(ks[8], (DEPTH, GLA_DV), 0.02),
        "q_norm_g": 1.0 + nrm(ks[9], (DEPTH, NA_HEAD_DIM), 0.02),
        "k_norm_g": 1.0 + nrm(ks[10], (DEPTH, NA_HEAD_DIM), 0.02),
        "rpb": nrm(ks[11], (DEPTH, NA_HEADS, 2 * NA_WIN_H_MAX - 1, 2 * NA_WIN_W - 1), 0.02),
        "meta_bias": nrm(ks[12], (DEPTH, NA_HEADS, N_META), 0.02),
        "w_out": nrm(ks[13], (DEPTH, D_MIX, D_MODEL), D_MIX ** -0.5),
    }


def reference(x, meta_tokens, norm_g, w_in, w_decay_fwd, b_decay_fwd, w_decay_bwd, b_decay_bwd,
              gla_out_norm_g, q_norm_g, k_norm_g, rpb, meta_bias, w_out):
    B, S, _ = x.shape
    R = S // GRID_W
    L = N_META + S
    pad = GLA_CHUNK - N_META
    split_at = [int(v) for v in np.cumsum(IN_SPLIT_WIDTHS)[:-1]]
    h = jnp.concatenate([jnp.broadcast_to(meta_tokens[None], (B, N_META, D_MODEL)).astype(x.dtype), x], axis=1)

    def pad_front(t):
        return jnp.pad(t, ((0, 0), (pad, 0), (0, 0), (0, 0)))

    def flip(t):
        return jnp.flip(t, axis=1)

    for l in range(DEPTH):
        u = _rmsnorm(h, norm_g[l])
        z = u @ w_in[l]
        gq, gk, gv, r_f, r_b, g_gate, nq, nk, nv, n_gate = jnp.split(z, split_at, axis=-1)

        q = (gq * GLA_DK ** -0.5).reshape(B, L, GLA_HEADS, GLA_DK)
        k = gk.reshape(B, L, GLA_HEADS, GLA_DK)
        v = gv.reshape(B, L, GLA_HEADS, GLA_DV)
        lg_f = (jax.nn.log_sigmoid(r_f @ w_decay_fwd[l] + b_decay_fwd[l]) / GLA_TAU).reshape(B, L, GLA_HEADS, GLA_DK)
        lg_b = (jax.nn.log_sigmoid(r_b @ w_decay_bwd[l] + b_decay_bwd[l]) / GLA_TAU).reshape(B, L, GLA_HEADS, GLA_DK)
        qp, kp, vp = pad_front(q), pad_front(k), pad_front(v)
        o_f = _gla_chunked(qp, kp, vp, pad_front(lg_f))
        o_b = flip(_gla_chunked(flip(qp), flip(kp), flip(vp), flip(pad_front(lg_b))))
        o_gla = _rmsnorm((o_f + o_b)[:, pad:], gla_out_norm_g[l]).reshape(B, L, GLA_WIDTH)
        o_gla = o_gla * jax.nn.silu(g_gate)

        qn = _rmsnorm(nq.reshape(B, L, NA_HEADS, NA_HEAD_DIM), q_norm_g[l])
        kn = _rmsnorm(nk.reshape(B, L, NA_HEADS, NA_HEAD_DIM), k_norm_g[l])
        vn = nv.reshape(B, L, NA_HEADS, NA_HEAD_DIM)
        grid = lambda t: t[:, N_META:].reshape(B, R, GRID_W, NA_HEADS, NA_HEAD_DIM)
        o_grid, o_meta = _neighbourhood_attention(grid(qn), grid(kn), grid(vn),
                                                  qn[:, :N_META], kn[:, :N_META], vn[:, :N_META],
                                                  rpb[l], meta_bias[l])
        o_na = jnp.concatenate([o_meta, o_grid.reshape(B, S, NA_HEADS, NA_HEAD_DIM)], axis=1).reshape(B, L, NA_WIDTH)
        o_na = o_na * jax.nn.silu(n_gate)

        h = h + jnp.concatenate([o_gla, o_na], axis=-1) @ w_out[l]

    return h[:, N_META:]
```

```python
import functools

import jax
import jax.numpy as jnp
import numpy as np
from jax import lax
from jax.experimental import pallas as pl
from jax.experimental.pallas import tpu as pltpu

F32 = jnp.float32
BF16 = jnp.bfloat16

D_MODEL = 1024
N_META = 16
GRID_W = 64
GLA_HEADS = 4
GLA_DK = 128
GLA_DV = 256
GLA_KEY_WIDTH = GLA_HEADS * GLA_DK
GLA_WIDTH = GLA_HEADS * GLA_DV
GLA_RANK = 16
GLA_TAU = 16.0
GLA_CHUNK = 64
NA_HEADS = 16
NA_DH = 64
NA_WIDTH = NA_HEADS * NA_DH
NA_WIN_H = 8
NA_WIN_W = 16
RMS_EPS = 1e-6
NEG_BIG = -1e30

MAIN_W = 4 * 1024 + GLA_KEY_WIDTH
COL_GV, COL_GG, COL_NV, COL_NG, COL_GQ = 0, 1024, 2048, 3072, 4096
TR_W = GLA_KEY_WIDTH + 2 * GLA_RANK

TILE = 2 * GLA_CHUNK
VMEM_LIMIT = 56 * 1024 * 1024


def _split_bf16(x):
    hi = x.astype(BF16)
    lo = (x - hi.astype(F32)).astype(BF16)
    return hi, lo


def _dot(a, b):
    return jnp.dot(a, b, preferred_element_type=F32)


def _log_sigmoid(x):
    return jnp.minimum(x, 0.0) - jnp.log1p(jnp.exp(-jnp.abs(x)))


def _silu(x):
    return x * (1.0 / (1.0 + jnp.exp(-x)))


def _normed_bf16(x_ref, g_ref):
    x = x_ref[...]
    ms = jnp.mean(x * x, axis=-1, keepdims=True)
    return (x * lax.rsqrt(ms + RMS_EPS) * g_ref[...]).astype(BF16)


def _proj_kernel(x_ref, g_ref, w_ref, cs_ref, *rest, mode):
    if mode == "qknorm":
        bd_ref, o_ref, u_ref = rest
    else:
        o_ref, u_ref = rest

    @pl.when(pl.program_id(1) == 0)
    def _():
        u_ref[...] = _normed_bf16(x_ref, g_ref)

    acc = _dot(u_ref[...], w_ref[...])
    if mode == "qknorm":
        ms = _dot((acc * acc).astype(BF16), bd_ref[...])
        acc = acc * lax.rsqrt(ms + RMS_EPS)
    o_ref[...] = (acc * cs_ref[...]).astype(o_ref.dtype)


def _proj_call(x2d, g, w, colscale, *, mode, out_dtype, tm, tn):
    m, d = x2d.shape
    n = w.shape[1]
    assert m % tm == 0 and n % tn == 0
    in_specs = [
        pl.BlockSpec((tm, d), lambda i, j: (i, 0)),
        pl.BlockSpec((1, d), lambda i, j: (0, 0)),
        pl.BlockSpec((d, tn), lambda i, j: (0, j)),
        pl.BlockSpec((1, tn), lambda i, j: (0, j)),
    ]
    args = [x2d, g, w, colscale]
    if mode == "qknorm":
        grp = jnp.arange(tn) // NA_DH
        bd = jnp.where(grp[:, None] == grp[None, :], 1.0 / NA_DH, 0.0).astype(BF16)
        in_specs.append(pl.BlockSpec((tn, tn), lambda i, j: (0, 0)))
        args.append(bd)
    return pl.pallas_call(
        functools.partial(_proj_kernel, mode=mode),
        grid=(m // tm, n // tn),
        in_specs=in_specs,
        out_specs=pl.BlockSpec((tm, tn), lambda i, j: (i, j)),
        out_shape=jax.ShapeDtypeStruct((m, n), out_dtype),
        scratch_shapes=[pltpu.VMEM((tm, d), BF16)],
        compiler_params=pltpu.CompilerParams(
            dimension_semantics=("parallel", "arbitrary"), vmem_limit_bytes=VMEM_LIMIT),
    )(*args)


def _proj_t_kernel(x_ref, g_ref, wt_ref, o_ref):
    u = _normed_bf16(x_ref, g_ref)
    o_ref[...] = lax.dot_general(wt_ref[...], u, (((1,), (1,)), ((), ())),
                                 preferred_element_type=F32)


def _proj_t_call(x2d, g, wt, *, tm):
    m, d = x2d.shape
    n = wt.shape[0]
    assert m % tm == 0
    return pl.pallas_call(
        _proj_t_kernel,
        grid=(m // tm,),
        in_specs=[
            pl.BlockSpec((tm, d), lambda i: (i, 0)),
            pl.BlockSpec((1, d), lambda i: (0, 0)),
            pl.BlockSpec((n, d), lambda i: (0, 0)),
        ],
        out_specs=pl.BlockSpec((n, tm), lambda i: (0, i)),
        out_shape=jax.ShapeDtypeStruct((n, m), F32),
        compiler_params=pltpu.CompilerParams(
            dimension_semantics=("parallel",), vmem_limit_bytes=VMEM_LIMIT),
    )(x2d, g, wt)


def _decay_logits(r, rt, wd_ref, wdt_ref, b_ref, bt_ref):
    r_hi, r_lo = _split_bf16(r)
    rt_hi, rt_lo = _split_bf16(rt)
    w_hi, w_lo = wd_ref[0], wd_ref[1]
    wt_hi, wt_lo = wdt_ref[0], wdt_ref[1]
    x = _dot(r_hi, w_hi) + _dot(r_lo, w_hi) + _dot(r_hi, w_lo) + b_ref[...]
    xt = _dot(wt_hi, rt_hi) + _dot(wt_hi, rt_lo) + _dot(wt_lo, rt_hi) + bt_ref[...]
    return _log_sigmoid(x) * (1.0 / GLA_TAU), _log_sigmoid(xt) * (1.0 / GLA_TAU)


def _gla_tile(q, kt, v, lg, lgt, tm_ref, mm_ref, amask, s, *, reverse):
    hi, lo = _split_bf16(lg)
    g = _dot(tm_ref[...], jnp.concatenate([hi, lo], axis=0))
    hit, lot = _split_bf16(lgt)
    gd = _dot(jnp.concatenate([hit, lot], axis=1), mm_ref[...])
    gt, dt = gd[:, :TILE], gd[:, TILE:]
    q_in = (q.astype(F32) * jnp.exp(g)).astype(BF16)
    k_in_t = (kt * jnp.exp(-gt)).astype(BF16)
    k_st_t = kt * jnp.exp(dt)
    a_cols = jnp.exp(gt + dt)
    a_mat = _dot(q_in, k_in_t)
    a_mat = jnp.where(amask, a_mat, 0.0).astype(BF16)
    o_intra = _dot(a_mat, v)
    lane = lax.broadcasted_iota(jnp.int32, (GLA_DK, TILE), 1)
    outs = [None, None]
    for half in ((1, 0) if reverse else (0, 1)):
        rows = slice(half * GLA_CHUNK, (half + 1) * GLA_CHUNK)
        outs[half] = o_intra[rows] + _dot(q_in[rows], s.astype(BF16))
        in_half = (lane >= half * GLA_CHUNK) & (lane < (half + 1) * GLA_CHUNK)
        u = _dot(jnp.where(in_half, k_st_t, 0.0).astype(BF16), v)
        a = a_cols[:, half * GLA_CHUNK:half * GLA_CHUNK + 1]
        s = a * s + u
    return jnp.concatenate(outs, axis=0), s


def _gla_kernel(qf_ref, ktf_ref, vf_ref, rf_ref, rtf_ref,
                qb_ref, ktb_ref, vb_ref, rb_ref, rtb_ref,
                ktm_ref, vm_ref, rtm_ref,
                wdf_ref, wdtf_ref, bf_ref, btf_ref,
                wdb_ref, wdtb_ref, bb_ref, btb_ref,
                tmf_ref, mmf_ref, tmb_ref, mmb_ref, mex_ref,
                of_ref, ob_ref, sf_ref, sb_ref, *, n_tiles):
    row = lax.broadcasted_iota(jnp.int32, (TILE, TILE), 0)
    col = lax.broadcasted_iota(jnp.int32, (TILE, TILE), 1)
    same = (row // GLA_CHUNK) == (col // GLA_CHUNK)
    mask_f = same & (row >= col)
    mask_b = same & (row <= col)

    @pl.when(pl.program_id(2) == 0)
    def _():
        rtm = rtm_ref[...]
        rt_hi, rt_lo = _split_bf16(rtm)
        xt = (_dot(wdtf_ref[0], rt_hi) + _dot(wdtf_ref[0], rt_lo) + _dot(wdtf_ref[1], rt_hi)
              + btf_ref[...])
        lane = lax.broadcasted_iota(jnp.int32, (GLA_DK, TILE), 1)
        lgt = jnp.where(lane < N_META, _log_sigmoid(xt) * (1.0 / GLA_TAU), 0.0)
        hit, lot = _split_bf16(lgt)
        dt = _dot(jnp.concatenate([hit, lot], axis=1), mex_ref[...])
        k_st_t = (ktm_ref[...] * jnp.exp(dt)).astype(BF16)
        sf_ref[...] = _dot(k_st_t, vm_ref[...])
        sb_ref[...] = jnp.zeros_like(sb_ref)

    s_f = sf_ref[...]
    s_b = sb_ref[...]
    for t in range(n_tiles):
        tf = t
        tb = n_tiles - 1 - t
        rows_f = slice(tf * TILE, (tf + 1) * TILE)
        rows_b = slice(tb * TILE, (tb + 1) * TILE)
        lg, lgt = _decay_logits(rf_ref[rows_f, :], rtf_ref[:, rows_f], wdf_ref, wdtf_ref, bf_ref, btf_ref)
        o, s_f = _gla_tile(qf_ref[rows_f, :], ktf_ref[:, rows_f], vf_ref[rows_f, :], lg, lgt,
                           tmf_ref, mmf_ref, mask_f, s_f, reverse=False)
        of_ref[rows_f, :] = o.astype(of_ref.dtype)
        lg, lgt = _decay_logits(rb_ref[rows_b, :], rtb_ref[:, rows_b], wdb_ref, wdtb_ref, bb_ref, btb_ref)
        o, s_b = _gla_tile(qb_ref[rows_b, :], ktb_ref[:, rows_b], vb_ref[rows_b, :], lg, lgt,
                           tmb_ref, mmb_ref, mask_b, s_b, reverse=True)
        ob_ref[rows_b, :] = o.astype(ob_ref.dtype)
    sf_ref[...] = s_f
    sb_ref[...] = s_b


def _gla_constants():
    i = np.arange(TILE)
    same = (i[:, None] // GLA_CHUNK) == (i[None, :] // GLA_CHUNK)
    tm_f = same & (i[None, :] <= i[:, None])
    tm_b = same & (i[None, :] >= i[:, None])
    minc_f, mexc_f = tm_f.T, same & (i[:, None] > i[None, :])
    minc_b, mexc_b = tm_b.T, same & (i[:, None] < i[None, :])
    mex_meta = i[:, None] > i[None, :]

    def stack_tm(t):
        return jnp.asarray(np.concatenate([t, t], axis=1), BF16)

    def stack_mm(a, b):
        m = np.concatenate([a, b], axis=1)
        return jnp.asarray(np.concatenate([m, m], axis=0), BF16)

    mex = jnp.asarray(np.concatenate([mex_meta, mex_meta], axis=0), BF16)
    return stack_tm(tm_f), stack_mm(minc_f, mexc_f), stack_tm(tm_b), stack_mm(minc_b, mexc_b), mex


def _gla_call(zmain, ztr, rtok, zmain_m, ztr_m, wd, wdt, bias, biast, *, batch, seq, tokens_per_step):
    t_blk = tokens_per_step
    nb = seq // t_blk
    n_tiles = t_blk // TILE
    h = GLA_HEADS
    tm_f, mm_f, tm_b, mm_b, mex = _gla_constants()

    def tok(fwd):
        if fwd:
            return lambda b, hh, i: b * nb + i
        return lambda b, hh, i: b * nb + (nb - 1 - i)

    def dir_specs(fwd, d):
        t = tok(fwd)
        return [
            pl.BlockSpec((t_blk, GLA_DK), lambda b, hh, i: (t(b, hh, i), COL_GQ // GLA_DK + hh)),
            pl.BlockSpec((GLA_DK, t_blk), lambda b, hh, i: (hh, t(b, hh, i))),
            pl.BlockSpec((t_blk, GLA_DV), lambda b, hh, i: (t(b, hh, i), COL_GV // GLA_DV + hh)),
            pl.BlockSpec((t_blk, 2 * GLA_RANK), lambda b, hh, i: (t(b, hh, i), 0)),
            pl.BlockSpec((GLA_RANK, t_blk), lambda b, hh, i: (GLA_KEY_WIDTH // GLA_RANK + d, t(b, hh, i))),
        ]

    def w_specs(d):
        return [
            pl.BlockSpec((None, 2, 2 * GLA_RANK, GLA_DK), lambda b, hh, i: (d, 0, 0, hh)),
            pl.BlockSpec((None, 2, GLA_DK, GLA_RANK), lambda b, hh, i: (d, 0, hh, 0)),
            pl.BlockSpec((None, 1, GLA_DK), lambda b, hh, i: (d, 0, hh)),
            pl.BlockSpec((None, GLA_DK, 1), lambda b, hh, i: (d, hh, 0)),
        ]

    const = lambda shape: pl.BlockSpec(shape, lambda b, hh, i: (0,) * len(shape))
    in_specs = (
        dir_specs(True, 0) + dir_specs(False, 1)
        + [pl.BlockSpec((GLA_DK, TILE), lambda b, hh, i: (hh, 0)),
           pl.BlockSpec((TILE, GLA_DV), lambda b, hh, i: (0, COL_GV // GLA_DV + hh)),
           pl.BlockSpec((GLA_RANK, TILE), lambda b, hh, i: (GLA_KEY_WIDTH // GLA_RANK, 0))]
        + w_specs(0) + w_specs(1)
        + [const((TILE, 2 * TILE)), const((2 * TILE, 2 * TILE)),
           const((TILE, 2 * TILE)), const((2 * TILE, 2 * TILE)), const((2 * TILE, TILE))]
    )
    args = ([zmain, ztr, zmain, rtok, ztr] * 2 + [ztr_m, zmain_m, ztr_m]
            + [wd, wdt, bias, biast] * 2 + [tm_f, mm_f, tm_b, mm_b, mex])
    out_spec_f = pl.BlockSpec((t_blk, GLA_DV), lambda b, hh, i: (b * nb + i, hh))
    out_spec_b = pl.BlockSpec((t_blk, GLA_DV), lambda b, hh, i: (b * nb + (nb - 1 - i), hh))
    o_shape = jax.ShapeDtypeStruct((batch * seq, GLA_WIDTH), BF16)
    return pl.pallas_call(
        functools.partial(_gla_kernel, n_tiles=n_tiles),
        grid=(batch, h, nb),
        in_specs=in_specs,
        out_specs=[out_spec_f, out_spec_b],
        out_shape=[o_shape, o_shape],
        scratch_shapes=[pltpu.VMEM((GLA_DK, GLA_DV), F32), pltpu.VMEM((GLA_DK, GLA_DV), F32)],
        compiler_params=pltpu.CompilerParams(
            dimension_semantics=("parallel", "parallel", "arbitrary"), vmem_limit_bytes=VMEM_LIMIT),
    )(*args)


def _na_kernel(q_ref, k_ref, v_ref, gate_ref, km_ref, vm_ref, rpb_ref, mb_ref, o_ref, tab_ref,
               *, rows_per_step, n_rows):
    w = GRID_W
    two = 2 * w
    n_idx = 2 * NA_WIN_H - 1

    @pl.when(pl.program_id(2) == 0)
    def _():
        r_i = lax.broadcasted_iota(jnp.int32, (w, two), 0)
        l_i = lax.broadcasted_iota(jnp.int32, (w, two), 1)
        cp = l_i % w
        cs = jnp.clip(r_i - NA_WIN_W // 2, 0, w - NA_WIN_W)
        valid = (cp >= cs) & (cp < cs + NA_WIN_W)
        for hh in range(2):
            rolled = []
            for idx in range(n_idx):
                vrow = jnp.broadcast_to(rpb_ref[hh, idx:idx + 1, :], (w, two))
                lo = pltpu.roll(vrow, two - (NA_WIN_W - 1), 1, stride=1, stride_axis=0)
                hi_ = pltpu.roll(vrow, two - (NA_WIN_W - 1) + w, 1, stride=1, stride_axis=0)
                rolled.append((lo, hi_))
            for j in range(n_idx - 1):
                t = jnp.where(l_i < w, rolled[j][0], rolled[j + 1][1])
                tab_ref[j, hh * w:(hh + 1) * w, :] = jnp.where(valid, t, NEG_BIG)

    lane = lax.broadcasted_iota(jnp.int32, (w, two), 1)
    head0 = lane < w
    srow = lax.broadcasted_iota(jnp.int32, (two, N_META), 0)
    mb = jnp.where(srow < w, mb_ref[0], mb_ref[1])
    km = km_ref[...]
    vm = vm_ref[...]
    zero = jnp.zeros((), BF16)

    def one_row(rl, carry):
        r = pl.program_id(2) * rows_per_step + rl
        rs = jnp.clip(r - NA_WIN_H // 2, 0, n_rows - NA_WIN_H)
        base = rs - r + (NA_WIN_H - 1)
        qoff = pl.multiple_of(rl * w, w)
        koff = pl.multiple_of(rs * w, w)
        q = q_ref[pl.ds(qoff, w), :]
        qs = jnp.concatenate([jnp.where(head0, q, zero), jnp.where(head0, zero, q)], axis=0)
        kw = k_ref[pl.ds(koff, NA_WIN_H * w), :]
        vw = v_ref[pl.ds(koff, NA_WIN_H * w), :]
        s = lax.dot_general(qs, kw, (((1,), (1,)), ((), ())), preferred_element_type=F32)
        tiles = [s[:, t * two:(t + 1) * two] + tab_ref[base + 2 * t] for t in range(NA_WIN_H // 2)]
        sm = lax.dot_general(qs, km, (((1,), (1,)), ((), ())), preferred_element_type=F32) + mb
        m = jnp.maximum(jnp.maximum(tiles[0], tiles[1]), jnp.maximum(tiles[2], tiles[3]))
        m = jnp.maximum(jnp.max(m, axis=-1, keepdims=True), jnp.max(sm, axis=-1, keepdims=True))
        ps = [jnp.exp(t - m) for t in tiles]
        pm = jnp.exp(sm - m)
        den = (jnp.sum(ps[0] + ps[1] + ps[2] + ps[3], axis=-1, keepdims=True)
               + jnp.sum(pm, axis=-1, keepdims=True))
        p = jnp.concatenate([t.astype(BF16) for t in ps], axis=1)
        o = _dot(p, vw) + _dot(pm.astype(BF16), vm)
        o = o * (1.0 / den)
        o = jnp.where(head0, o[:w], o[w:])
        gate = gate_ref[pl.ds(qoff, w), :].astype(F32)
        o_ref[pl.ds(qoff, w), :] = (o * _silu(gate)).astype(o_ref.dtype)
        return carry

    lax.fori_loop(0, rows_per_step, one_row, 0)


def _na_call(zqk, zmain, zqk_m, zmain_m, rpb_pad, meta_bias, *, batch, seq, rows_per_step):
    n_rows = seq // GRID_W
    assert n_rows >= NA_WIN_H and n_rows % rows_per_step == 0
    nsteps = n_rows // rows_per_step
    hp = NA_HEADS // 2
    lanes = 2 * NA_DH
    tq = rows_per_step * GRID_W
    n_idx = 2 * NA_WIN_H - 1
    in_specs = [
        pl.BlockSpec((tq, lanes), lambda b, p, i: (b * nsteps + i, p)),
        pl.BlockSpec((seq, lanes), lambda b, p, i: (b, NA_WIDTH // lanes + p)),
        pl.BlockSpec((seq, lanes), lambda b, p, i: (b, COL_NV // lanes + p)),
        pl.BlockSpec((tq, lanes), lambda b, p, i: (b * nsteps + i, COL_NG // lanes + p)),
        pl.BlockSpec((N_META, lanes), lambda b, p, i: (0, NA_WIDTH // lanes + p)),
        pl.BlockSpec((N_META, lanes), lambda b, p, i: (0, COL_NV // lanes + p)),
        pl.BlockSpec((2, n_idx, lanes), lambda b, p, i: (p, 0, 0)),
        pl.BlockSpec((2, 1, N_META), lambda b, p, i: (p, 0, 0)),
    ]
    return pl.pallas_call(
        functools.partial(_na_kernel, rows_per_step=rows_per_step, n_rows=n_rows),
        grid=(batch, hp, nsteps),
        in_specs=in_specs,
        out_specs=pl.BlockSpec((tq, lanes), lambda b, p, i: (b * nsteps + i, p)),
        out_shape=jax.ShapeDtypeStruct((batch * seq, NA_WIDTH), BF16),
        scratch_shapes=[pltpu.VMEM((n_idx - 1, lanes, lanes), F32)],
        compiler_params=pltpu.CompilerParams(
            dimension_semantics=("parallel", "parallel", "arbitrary"), vmem_limit_bytes=VMEM_LIMIT),
    )(zqk, zqk, zmain, zmain, zqk_m, zmain_m, rpb_pad, meta_bias)


def _out_kernel(x_ref, of_ref, ob_ref, gg_ref, na_ref, gn_ref, w_ref, o_ref):
    s = of_ref[...].astype(F32) + ob_ref[...].astype(F32)
    gate = gg_ref[...].astype(F32)
    gn = gn_ref[...]
    heads = []
    for h in range(GLA_HEADS):
        cols = slice(h * GLA_DV, (h + 1) * GLA_DV)
        sh = s[:, cols]
        ms = jnp.mean(sh * sh, axis=-1, keepdims=True)
        y = sh * lax.rsqrt(ms + RMS_EPS) * gn
        heads.append((y * _silu(gate[:, cols])).astype(BF16))
    o_gla = jnp.concatenate(heads, axis=1)
    acc = _dot(o_gla, w_ref[:GLA_WIDTH, :]) + _dot(na_ref[...], w_ref[GLA_WIDTH:, :])
    o_ref[...] = x_ref[...] + acc


def _out_call(x2d, o_f, o_b, zmain, o_na, gn, w_out, *, tm):
    m, d = x2d.shape
    assert m % tm == 0
    return pl.pallas_call(
        _out_kernel,
        grid=(m // tm,),
        in_specs=[
            pl.BlockSpec((tm, d), lambda i: (i, 0)),
            pl.BlockSpec((tm, GLA_WIDTH), lambda i: (i, 0)),
            pl.BlockSpec((tm, GLA_WIDTH), lambda i: (i, 0)),
            pl.BlockSpec((tm, GLA_WIDTH), lambda i: (i, COL_GG // GLA_WIDTH)),
            pl.BlockSpec((tm, NA_WIDTH), lambda i: (i, 0)),
            pl.BlockSpec((1, GLA_DV), lambda i: (0, 0)),
            pl.BlockSpec((GLA_WIDTH + NA_WIDTH, d), lambda i: (0, 0)),
        ],
        out_specs=pl.BlockSpec((tm, d), lambda i: (i, 0)),
        out_shape=jax.ShapeDtypeStruct((m, d), F32),
        compiler_params=pltpu.CompilerParams(
            dimension_semantics=("parallel",), vmem_limit_bytes=VMEM_LIMIT),
    )(x2d, o_f, o_b, zmain, o_na, gn, w_out)


def _pick(n, cands):
    for c in cands:
        if n % c == 0:
            return c
    raise ValueError(f"no tile for {n}")


def kernel(x, meta_tokens, norm_g, w_in, w_decay_fwd, b_decay_fwd, w_decay_bwd, b_decay_bwd,
           gla_out_norm_g, q_norm_g, k_norm_g, rpb, meta_bias, w_out):
    batch, seq, d = x.shape
    assert d == D_MODEL and seq % TILE == 0 and seq % GRID_W == 0
    depth = norm_g.shape[0]
    assert depth == 1
    l = 0
    x2d = x.reshape(batch * seq, d)
    g = norm_g[l].reshape(1, d)

    w = w_in[l]
    gq, gk, gv = w[:, 0:512], w[:, 512:1024], w[:, 1024:2048]
    rfw, rbw = w[:, 2048:2064], w[:, 2064:2080]
    gg, nq, nk, nv, ng = (w[:, 2080:3104], w[:, 3104:4128], w[:, 4128:5152],
                          w[:, 5152:6176], w[:, 6176:7200])
    w_main = jnp.concatenate([gv, gg, nv, ng, gq], axis=1).astype(BF16)
    cs_main = jnp.concatenate([jnp.ones((4 * 1024,), F32),
                               jnp.full((GLA_KEY_WIDTH,), GLA_DK ** -0.5, F32)]).reshape(1, MAIN_W)
    w_qk = jnp.concatenate([nq, nk], axis=1).astype(BF16)
    cs_qk = jnp.concatenate([jnp.tile(q_norm_g[l], NA_HEADS) * (NA_DH ** -0.5),
                             jnp.tile(k_norm_g[l], NA_HEADS)]).reshape(1, 2 * NA_WIDTH)
    w_r = jnp.concatenate([rfw, rbw], axis=1).astype(BF16)
    cs_r = jnp.ones((1, 2 * GLA_RANK), F32)
    w_t = jnp.concatenate([gk, rfw, rbw], axis=1).T.astype(BF16)

    wd = jnp.stack([w_decay_fwd[l], w_decay_bwd[l]])
    wd_hi = wd.astype(BF16)
    wd_lo = (wd - wd_hi.astype(F32)).astype(BF16)
    wd_s = jnp.stack([wd_hi, wd_lo], axis=1)
    wdt_s = jnp.swapaxes(wd_s, 2, 3)
    zeros_r = jnp.zeros_like(wd_s[0])
    wd_s = jnp.stack([jnp.concatenate([wd_s[0], zeros_r], axis=1),
                      jnp.concatenate([zeros_r, wd_s[1]], axis=1)])
    bias = jnp.stack([b_decay_fwd[l], b_decay_bwd[l]]).reshape(2, 1, GLA_KEY_WIDTH)
    biast = bias.reshape(2, GLA_KEY_WIDTH, 1)

    tm = _pick(batch * seq, (512, 256, 128))
    zmain = _proj_call(x2d, g, w_main, cs_main, mode="scale", out_dtype=BF16, tm=tm, tn=512)
    zqk = _proj_call(x2d, g, w_qk, cs_qk, mode="qknorm", out_dtype=BF16, tm=tm, tn=512)
    rtok = _proj_call(x2d, g, w_r, cs_r, mode="scale", out_dtype=F32, tm=tm, tn=2 * GLA_RANK)
    ztr = _proj_t_call(x2d, g, w_t, tm=tm)

    zmain_m = _proj_call(meta_tokens, g, w_main, cs_main, mode="scale", out_dtype=BF16, tm=N_META, tn=512)
    zqk_m = _proj_call(meta_tokens, g, w_qk, cs_qk, mode="qknorm", out_dtype=BF16, tm=N_META, tn=512)
    ztr_m = _proj_t_call(meta_tokens, g, w_t, tm=N_META)
    zmain_m_pad = jnp.pad(zmain_m, ((0, TILE - N_META), (0, 0)))
    ztr_m_pad = jnp.pad(ztr_m, ((0, 0), (0, TILE - N_META)))

    o_f, o_b = _gla_call(zmain, ztr, rtok, zmain_m_pad, ztr_m_pad, wd_s, wdt_s, bias, biast,
                         batch=batch, seq=seq, tokens_per_step=_pick(seq, (512, 256, 128)))

    rpb_pad = jnp.pad(rpb[l], ((0, 0), (0, 0), (0, 2 * NA_DH - rpb.shape[-1])))
    mb = meta_bias[l].reshape(NA_HEADS, 1, N_META)
    o_na = _na_call(zqk, zmain, zqk_m, zmain_m, rpb_pad, mb, batch=batch, seq=seq,
                    rows_per_step=_pick(seq // GRID_W, (8, 4, 2, 1)))

    out = _out_call(x2d, o_f, o_b, zmain, o_na, gla_out_norm_g[l].reshape(1, GLA_DV),
                    w_out[l].astype(BF16), tm=_pick(batch * seq, (512, 256, 128)))
    return out.reshape(batch, seq, d)
```

```python
import functools

import jax
import jax.numpy as jnp
import numpy as np
from jax import lax
from jax.experimental import pallas as pl
from jax.experimental.pallas import tpu as pltpu

F32 = jnp.float32
BF16 = jnp.bfloat16

D_MODEL = 1024
N_META = 16
GRID_W = 64
GLA_HEADS = 4
GLA_DK = 128
GLA_DV = 256
GLA_KEY_WIDTH = GLA_HEADS * GLA_DK
GLA_WIDTH = GLA_HEADS * GLA_DV
GLA_RANK = 16
GLA_TAU = 16.0
GLA_CHUNK = 64
NA_HEADS = 16
NA_DH = 64
NA_WIDTH = NA_HEADS * NA_DH
NA_WIN_H = 8
NA_WIN_W = 16
RMS_EPS = 1e-6
NEG_BIG = -1e30

MAIN_W = 4 * 1024 + GLA_KEY_WIDTH
COL_GV, COL_GG, COL_NV, COL_NG, COL_GQ = 0, 1024, 2048, 3072, 4096
TR_W = GLA_KEY_WIDTH + 2 * GLA_RANK

TILE = 2 * GLA_CHUNK
NA_ROW_GROUP = 8
VMEM_LIMIT = 56 * 1024 * 1024


def _split_bf16(x):
    hi = x.astype(BF16)
    lo = (x - hi.astype(F32)).astype(BF16)
    return hi, lo


def _dot(a, b):
    return jnp.dot(a, b, preferred_element_type=F32)


def _log_sigmoid(x):
    return jnp.minimum(x, 0.0) - jnp.log1p(jnp.exp(-jnp.abs(x)))


def _silu(x):
    return x * (1.0 / (1.0 + jnp.exp(-x)))


def _normed_bf16(x_ref, g_ref):
    x = x_ref[...]
    ms = jnp.mean(x * x, axis=-1, keepdims=True)
    return (x * lax.rsqrt(ms + RMS_EPS) * g_ref[...]).astype(BF16)


def _proj_kernel(x_ref, g_ref, w_ref, cs_ref, *rest, mode):
    if mode == "qknorm":
        bd_ref, o_ref, u_ref = rest
    else:
        o_ref, u_ref = rest

    @pl.when(pl.program_id(1) == 0)
    def _():
        u_ref[...] = _normed_bf16(x_ref, g_ref)

    acc = _dot(u_ref[...], w_ref[...])
    if mode == "qknorm":
        ms = _dot((acc * acc).astype(BF16), bd_ref[...])
        acc = acc * lax.rsqrt(ms + RMS_EPS)
    o_ref[...] = (acc * cs_ref[...]).astype(o_ref.dtype)


def _proj_call(x2d, g, w, colscale, *, mode, out_dtype, tm, tn, name):
    m, d = x2d.shape
    n = w.shape[1]
    assert m % tm == 0 and n % tn == 0
    in_specs = [
        pl.BlockSpec((tm, d), lambda i, j: (i, 0)),
        pl.BlockSpec((1, d), lambda i, j: (0, 0)),
        pl.BlockSpec((d, tn), lambda i, j: (0, j)),
        pl.BlockSpec((1, tn), lambda i, j: (0, j)),
    ]
    args = [x2d, g, w, colscale]
    if mode == "qknorm":
        grp = jnp.arange(tn) // NA_DH
        bd = jnp.where(grp[:, None] == grp[None, :], 1.0 / NA_DH, 0.0).astype(BF16)
        in_specs.append(pl.BlockSpec((tn, tn), lambda i, j: (0, 0)))
        args.append(bd)
    return pl.pallas_call(
        functools.partial(_proj_kernel, mode=mode),
        name=name,
        grid=(m // tm, n // tn),
        in_specs=in_specs,
        out_specs=pl.BlockSpec((tm, tn), lambda i, j: (i, j)),
        out_shape=jax.ShapeDtypeStruct((m, n), out_dtype),
        scratch_shapes=[pltpu.VMEM((tm, d), BF16)],
        compiler_params=pltpu.CompilerParams(
            dimension_semantics=("parallel", "arbitrary"), vmem_limit_bytes=VMEM_LIMIT),
    )(*args)


def _proj_t_kernel(x_ref, g_ref, wt_ref, o_ref):
    u = _normed_bf16(x_ref, g_ref)
    o_ref[...] = lax.dot_general(wt_ref[...], u, (((1,), (1,)), ((), ())),
                                 preferred_element_type=F32)


def _proj_t_call(x2d, g, wt, *, tm, name):
    m, d = x2d.shape
    n = wt.shape[0]
    assert m % tm == 0
    return pl.pallas_call(
        _proj_t_kernel,
        name=name,
        grid=(m // tm,),
        in_specs=[
            pl.BlockSpec((tm, d), lambda i: (i, 0)),
            pl.BlockSpec((1, d), lambda i: (0, 0)),
            pl.BlockSpec((n, d), lambda i: (0, 0)),
        ],
        out_specs=pl.BlockSpec((n, tm), lambda i: (0, i)),
        out_shape=jax.ShapeDtypeStruct((n, m), F32),
        compiler_params=pltpu.CompilerParams(
            dimension_semantics=("parallel",), vmem_limit_bytes=VMEM_LIMIT),
    )(x2d, g, wt)


def _decay_logits(r, rt, wd_ref, wdt_ref, b_ref, bt_ref):
    r_hi, r_lo = _split_bf16(r)
    rt_hi, rt_lo = _split_bf16(rt)
    w_hi, w_lo = wd_ref[0], wd_ref[1]
    wt_hi, wt_lo = wdt_ref[0], wdt_ref[1]
    x = _dot(r_hi, w_hi) + _dot(r_lo, w_hi) + _dot(r_hi, w_lo) + b_ref[...]
    xt = _dot(wt_hi, rt_hi) + _dot(wt_hi, rt_lo) + _dot(wt_lo, rt_hi) + bt_ref[...]
    return _log_sigmoid(x) * (1.0 / GLA_TAU), _log_sigmoid(xt) * (1.0 / GLA_TAU)


def _gla_tile(q, kt, v, lg, lgt, tm_ref, mm_ref, amask, s, *, reverse):
    hi, lo = _split_bf16(lg)
    g = _dot(tm_ref[...], jnp.concatenate([hi, lo], axis=0))
    hit, lot = _split_bf16(lgt)
    gd = _dot(jnp.concatenate([hit, lot], axis=1), mm_ref[...])
    gt, dt = gd[:, :TILE], gd[:, TILE:]
    q_in = (q.astype(F32) * jnp.exp(g)).astype(BF16)
    k_in_t = (kt * jnp.exp(-gt)).astype(BF16)
    k_st_t = kt * jnp.exp(dt)
    a_cols = jnp.exp(gt + dt)
    a_mat = _dot(q_in, k_in_t)
    a_mat = jnp.where(amask, a_mat, 0.0).astype(BF16)
    o_intra = _dot(a_mat, v)
    lane = lax.broadcasted_iota(jnp.int32, (GLA_DK, TILE), 1)
    outs = [None, None]
    for half in ((1, 0) if reverse else (0, 1)):
        rows = slice(half * GLA_CHUNK, (half + 1) * GLA_CHUNK)
        outs[half] = o_intra[rows] + _dot(q_in[rows], s.astype(BF16))
        in_half = (lane >= half * GLA_CHUNK) & (lane < (half + 1) * GLA_CHUNK)
        u = _dot(jnp.where(in_half, k_st_t, 0.0).astype(BF16), v)
        a = a_cols[:, half * GLA_CHUNK:half * GLA_CHUNK + 1]
        s = a * s + u
    return jnp.concatenate(outs, axis=0), s


def _gla_kernel(qf_ref, ktf_ref, vf_ref, rf_ref, rtf_ref,
                qb_ref, ktb_ref, vb_ref, rb_ref, rtb_ref,
                ktm_ref, vm_ref, rtm_ref,
                wdf_ref, wdtf_ref, bf_ref, btf_ref,
                wdb_ref, wdtb_ref, bb_ref, btb_ref,
                tmf_ref, mmf_ref, tmb_ref, mmb_ref, mex_ref,
                of_ref, ob_ref, sf_ref, sb_ref, *, n_tiles):
    row = lax.broadcasted_iota(jnp.int32, (TILE, TILE), 0)
    col = lax.broadcasted_iota(jnp.int32, (TILE, TILE), 1)
    same = (row // GLA_CHUNK) == (col // GLA_CHUNK)
    mask_f = same & (row >= col)
    mask_b = same & (row <= col)

    @pl.when(pl.program_id(2) == 0)
    def _():
        rtm = rtm_ref[...]
        rt_hi, rt_lo = _split_bf16(rtm)
        xt = (_dot(wdtf_ref[0], rt_hi) + _dot(wdtf_ref[0], rt_lo) + _dot(wdtf_ref[1], rt_hi)
              + btf_ref[...])
        lane = lax.broadcasted_iota(jnp.int32, (GLA_DK, TILE), 1)
        lgt = jnp.where(lane < N_META, _log_sigmoid(xt) * (1.0 / GLA_TAU), 0.0)
        hit, lot = _split_bf16(lgt)
        dt = _dot(jnp.concatenate([hit, lot], axis=1), mex_ref[...])
        k_st_t = (ktm_ref[...] * jnp.exp(dt)).astype(BF16)
        sf_ref[...] = _dot(k_st_t, vm_ref[...])
        sb_ref[...] = jnp.zeros_like(sb_ref)

    s_f = sf_ref[...]
    s_b = sb_ref[...]
    for t in range(n_tiles):
        tf = t
        tb = n_tiles - 1 - t
        rows_f = slice(tf * TILE, (tf + 1) * TILE)
        rows_b = slice(tb * TILE, (tb + 1) * TILE)
        lg, lgt = _decay_logits(rf_ref[rows_f, :], rtf_ref[:, rows_f], wdf_ref, wdtf_ref, bf_ref, btf_ref)
        o, s_f = _gla_tile(qf_ref[rows_f, :], ktf_ref[:, rows_f], vf_ref[rows_f, :], lg, lgt,
                           tmf_ref, mmf_ref, mask_f, s_f, reverse=False)
        of_ref[rows_f, :] = o.astype(of_ref.dtype)
        lg, lgt = _decay_logits(rb_ref[rows_b, :], rtb_ref[:, rows_b], wdb_ref, wdtb_ref, bb_ref, btb_ref)
        o, s_b = _gla_tile(qb_ref[rows_b, :], ktb_ref[:, rows_b], vb_ref[rows_b, :], lg, lgt,
                           tmb_ref, mmb_ref, mask_b, s_b, reverse=True)
        ob_ref[rows_b, :] = o.astype(ob_ref.dtype)
    sf_ref[...] = s_f
    sb_ref[...] = s_b


def _gla_constants():
    i = np.arange(TILE)
    same = (i[:, None] // GLA_CHUNK) == (i[None, :] // GLA_CHUNK)
    tm_f = same & (i[None, :] <= i[:, None])
    tm_b = same & (i[None, :] >= i[:, None])
    minc_f, mexc_f = tm_f.T, same & (i[:, None] > i[None, :])
    minc_b, mexc_b = tm_b.T, same & (i[:, None] < i[None, :])
    mex_meta = i[:, None] > i[None, :]

    def stack_tm(t):
        return jnp.asarray(np.concatenate([t, t], axis=1), BF16)

    def stack_mm(a, b):
        m = np.concatenate([a, b], axis=1)
        return jnp.asarray(np.concatenate([m, m], axis=0), BF16)

    mex = jnp.asarray(np.concatenate([mex_meta, mex_meta], axis=0), BF16)
    return stack_tm(tm_f), stack_mm(minc_f, mexc_f), stack_tm(tm_b), stack_mm(minc_b, mexc_b), mex


def _gla_call(zmain, ztr, rtok, zmain_m, ztr_m, wd, wdt, bias, biast, *, batch, seq, tokens_per_step):
    t_blk = tokens_per_step
    nb = seq // t_blk
    n_tiles = t_blk // TILE
    h = GLA_HEADS
    tm_f, mm_f, tm_b, mm_b, mex = _gla_constants()

    def tok(fwd):
        if fwd:
            return lambda b, hh, i: b * nb + i
        return lambda b, hh, i: b * nb + (nb - 1 - i)

    def dir_specs(fwd, d):
        t = tok(fwd)
        return [
            pl.BlockSpec((t_blk, GLA_DK), lambda b, hh, i: (t(b, hh, i), COL_GQ // GLA_DK + hh)),
            pl.BlockSpec((GLA_DK, t_blk), lambda b, hh, i: (hh, t(b, hh, i))),
            pl.BlockSpec((t_blk, GLA_DV), lambda b, hh, i: (t(b, hh, i), COL_GV // GLA_DV + hh)),
            pl.BlockSpec((t_blk, 2 * GLA_RANK), lambda b, hh, i: (t(b, hh, i), 0)),
            pl.BlockSpec((GLA_RANK, t_blk), lambda b, hh, i: (GLA_KEY_WIDTH // GLA_RANK + d, t(b, hh, i))),
        ]

    def w_specs(d):
        return [
            pl.BlockSpec((None, 2, 2 * GLA_RANK, GLA_DK), lambda b, hh, i: (d, 0, 0, hh)),
            pl.BlockSpec((None, 2, GLA_DK, GLA_RANK), lambda b, hh, i: (d, 0, hh, 0)),
            pl.BlockSpec((None, 1, GLA_DK), lambda b, hh, i: (d, 0, hh)),
            pl.BlockSpec((None, GLA_DK, 1), lambda b, hh, i: (d, hh, 0)),
        ]

    const = lambda shape: pl.BlockSpec(shape, lambda b, hh, i: (0,) * len(shape))
    in_specs = (
        dir_specs(True, 0) + dir_specs(False, 1)
        + [pl.BlockSpec((GLA_DK, TILE), lambda b, hh, i: (hh, 0)),
           pl.BlockSpec((TILE, GLA_DV), lambda b, hh, i: (0, COL_GV // GLA_DV + hh)),
           pl.BlockSpec((GLA_RANK, TILE), lambda b, hh, i: (GLA_KEY_WIDTH // GLA_RANK, 0))]
        + w_specs(0) + w_specs(1)
        + [const((TILE, 2 * TILE)), const((2 * TILE, 2 * TILE)),
           const((TILE, 2 * TILE)), const((2 * TILE, 2 * TILE)), const((2 * TILE, TILE))]
    )
    args = ([zmain, ztr, zmain, rtok, ztr] * 2 + [ztr_m, zmain_m, ztr_m]
            + [wd, wdt, bias, biast] * 2 + [tm_f, mm_f, tm_b, mm_b, mex])
    out_spec_f = pl.BlockSpec((t_blk, GLA_DV), lambda b, hh, i: (b * nb + i, hh))
    out_spec_b = pl.BlockSpec((t_blk, GLA_DV), lambda b, hh, i: (b * nb + (nb - 1 - i), hh))
    o_shape = jax.ShapeDtypeStruct((batch * seq, GLA_WIDTH), BF16)
    return pl.pallas_call(
        functools.partial(_gla_kernel, n_tiles=n_tiles),
        name="gla_scan",
        grid=(batch, h, nb),
        in_specs=in_specs,
        out_specs=[out_spec_f, out_spec_b],
        out_shape=[o_shape, o_shape],
        scratch_shapes=[pltpu.VMEM((GLA_DK, GLA_DV), F32), pltpu.VMEM((GLA_DK, GLA_DV), F32)],
        compiler_params=pltpu.CompilerParams(
            dimension_semantics=("parallel", "parallel", "arbitrary"), vmem_limit_bytes=VMEM_LIMIT),
    )(*args)


def _na_kernel(q_ref, k_ref, v_ref, gate_ref, km_ref, vm_ref, rpb_ref, mb_ref, o_ref, tab_ref,
               *, rows_per_step, n_rows):
    w = GRID_W
    two = 2 * w
    n_idx = 2 * NA_WIN_H - 1

    @pl.when(pl.program_id(2) == 0)
    def _():
        r_i = lax.broadcasted_iota(jnp.int32, (w, two), 0)
        l_i = lax.broadcasted_iota(jnp.int32, (w, two), 1)
        cp = l_i % w
        cs = jnp.clip(r_i - NA_WIN_W // 2, 0, w - NA_WIN_W)
        valid = (cp >= cs) & (cp < cs + NA_WIN_W)
        for hh in range(2):
            rolled = []
            for idx in range(n_idx):
                vrow = jnp.broadcast_to(rpb_ref[hh, idx:idx + 1, :], (w, two))
                lo = pltpu.roll(vrow, two - (NA_WIN_W - 1), 1, stride=1, stride_axis=0)
                hi_ = pltpu.roll(vrow, two - (NA_WIN_W - 1) + w, 1, stride=1, stride_axis=0)
                rolled.append((lo, hi_))
            for j in range(n_idx - 1):
                t = jnp.where(l_i < w, rolled[j][0], rolled[j + 1][1])
                tab_ref[j, hh * w:(hh + 1) * w, :] = jnp.where(valid, t, NEG_BIG)

    lane = lax.broadcasted_iota(jnp.int32, (w, two), 1)
    head0 = lane < w
    srow = lax.broadcasted_iota(jnp.int32, (two, N_META), 0)
    mb = jnp.where(srow < w, mb_ref[0], mb_ref[1])
    km = km_ref[...]
    vm = vm_ref[...]
    zero = jnp.zeros((), BF16)

    def row_group(gi, carry):
        rows = []
        for j in range(NA_ROW_GROUP):
            rl = gi * NA_ROW_GROUP + j
            r = pl.program_id(2) * rows_per_step + rl
            rs = jnp.clip(r - NA_WIN_H // 2, 0, n_rows - NA_WIN_H)
            base = rs - r + (NA_WIN_H - 1)
            qoff = pl.multiple_of(rl * w, w)
            koff = pl.multiple_of(rs * w, w)
            q = q_ref[pl.ds(qoff, w), :]
            qs = jnp.concatenate([jnp.where(head0, q, zero), jnp.where(head0, zero, q)], axis=0)
            kw = k_ref[pl.ds(koff, NA_WIN_H * w), :]
            s = lax.dot_general(qs, kw, (((1,), (1,)), ((), ())), preferred_element_type=F32)
            sm = lax.dot_general(qs, km, (((1,), (1,)), ((), ())), preferred_element_type=F32)
            rows.append((qoff, koff, base, s, sm))
        probs = []
        for qoff, koff, base, s, sm in rows:
            tiles = [s[:, t * two:(t + 1) * two] + tab_ref[base + 2 * t] for t in range(NA_WIN_H // 2)]
            sm = sm + mb
            m = jnp.maximum(jnp.maximum(tiles[0], tiles[1]), jnp.maximum(tiles[2], tiles[3]))
            m = jnp.maximum(jnp.max(m, axis=-1, keepdims=True), jnp.max(sm, axis=-1, keepdims=True))
            ps = [jnp.exp(t - m) for t in tiles]
            pm = jnp.exp(sm - m)
            den = (jnp.sum(ps[0] + ps[1] + ps[2] + ps[3], axis=-1, keepdims=True)
                   + jnp.sum(pm, axis=-1, keepdims=True))
            p = jnp.concatenate([t.astype(BF16) for t in ps], axis=1)
            probs.append((qoff, koff, p, pm.astype(BF16), den))
        for qoff, koff, p, pm, den in probs:
            vw = v_ref[pl.ds(koff, NA_WIN_H * w), :]
            o = _dot(p, vw) + _dot(pm, vm)
            o = o * (1.0 / den)
            o = jnp.where(head0, o[:w], o[w:])
            gate = gate_ref[pl.ds(qoff, w), :].astype(F32)
            o_ref[pl.ds(qoff, w), :] = (o * _silu(gate)).astype(o_ref.dtype)
        return carry

    lax.fori_loop(0, rows_per_step // NA_ROW_GROUP, row_group, 0)


def _na_call(zqk, zmain, zqk_m, zmain_m, rpb_pad, meta_bias, *, batch, seq, rows_per_step):
    n_rows = seq // GRID_W
    assert n_rows >= NA_WIN_H and n_rows % rows_per_step == 0
    nsteps = n_rows // rows_per_step
    hp = NA_HEADS // 2
    lanes = 2 * NA_DH
    tq = rows_per_step * GRID_W
    n_idx = 2 * NA_WIN_H - 1
    in_specs = [
        pl.BlockSpec((tq, lanes), lambda b, p, i: (b * nsteps + i, p)),
        pl.BlockSpec((seq, lanes), lambda b, p, i: (b, NA_WIDTH // lanes + p)),
        pl.BlockSpec((seq, lanes), lambda b, p, i: (b, COL_NV // lanes + p)),
        pl.BlockSpec((tq, lanes), lambda b, p, i: (b * nsteps + i, COL_NG // lanes + p)),
        pl.BlockSpec((N_META, lanes), lambda b, p, i: (0, NA_WIDTH // lanes + p)),
        pl.BlockSpec((N_META, lanes), lambda b, p, i: (0, COL_NV // lanes + p)),
        pl.BlockSpec((2, n_idx, lanes), lambda b, p, i: (p, 0, 0)),
        pl.BlockSpec((2, 1, N_META), lambda b, p, i: (p, 0, 0)),
    ]
    return pl.pallas_call(
        functools.partial(_na_kernel, rows_per_step=rows_per_step, n_rows=n_rows),
        name="na_attn",
        grid=(batch, hp, nsteps),
        in_specs=in_specs,
        out_specs=pl.BlockSpec((tq, lanes), lambda b, p, i: (b * nsteps + i, p)),
        out_shape=jax.ShapeDtypeStruct((batch * seq, NA_WIDTH), BF16),
        scratch_shapes=[pltpu.VMEM((n_idx - 1, lanes, lanes), F32)],
        compiler_params=pltpu.CompilerParams(
            dimension_semantics=("parallel", "parallel", "arbitrary"), vmem_limit_bytes=VMEM_LIMIT),
    )(zqk, zqk, zmain, zmain, zqk_m, zmain_m, rpb_pad, meta_bias)


def _out_kernel(x_ref, of_ref, ob_ref, gg_ref, na_ref, gn_ref, w_ref, o_ref):
    s = of_ref[...].astype(F32) + ob_ref[...].astype(F32)
    gate = gg_ref[...].astype(F32)
    gn = gn_ref[...]
    heads = []
    for h in range(GLA_HEADS):
        cols = slice(h * GLA_DV, (h + 1) * GLA_DV)
        sh = s[:, cols]
        ms = jnp.mean(sh * sh, axis=-1, keepdims=True)
        y = sh * lax.rsqrt(ms + RMS_EPS) * gn
        heads.append((y * _silu(gate[:, cols])).astype(BF16))
    o_gla = jnp.concatenate(heads, axis=1)
    acc = _dot(o_gla, w_ref[:GLA_WIDTH, :]) + _dot(na_ref[...], w_ref[GLA_WIDTH:, :])
    o_ref[...] = x_ref[...] + acc


def _out_call(x2d, o_f, o_b, zmain, o_na, gn, w_out, *, tm):
    m, d = x2d.shape
    assert m % tm == 0
    return pl.pallas_call(
        _out_kernel,
        name="out_proj",
        grid=(m // tm,),
        in_specs=[
            pl.BlockSpec((tm, d), lambda i: (i, 0)),
            pl.BlockSpec((tm, GLA_WIDTH), lambda i: (i, 0)),
            pl.BlockSpec((tm, GLA_WIDTH), lambda i: (i, 0)),
            pl.BlockSpec((tm, GLA_WIDTH), lambda i: (i, COL_GG // GLA_WIDTH)),
            pl.BlockSpec((tm, NA_WIDTH), lambda i: (i, 0)),
            pl.BlockSpec((1, GLA_DV), lambda i: (0, 0)),
            pl.BlockSpec((GLA_WIDTH + NA_WIDTH, d), lambda i: (0, 0)),
        ],
        out_specs=pl.BlockSpec((tm, d), lambda i: (i, 0)),
        out_shape=jax.ShapeDtypeStruct((m, d), F32),
        compiler_params=pltpu.CompilerParams(
            dimension_semantics=("parallel",), vmem_limit_bytes=VMEM_LIMIT),
    )(x2d, o_f, o_b, zmain, o_na, gn, w_out)


def _pick(n, cands):
    for c in cands:
        if n % c == 0:
            return c
    raise ValueError(f"no tile for {n}")


def kernel(x, meta_tokens, norm_g, w_in, w_decay_fwd, b_decay_fwd, w_decay_bwd, b_decay_bwd,
           gla_out_norm_g, q_norm_g, k_norm_g, rpb, meta_bias, w_out):
    batch, seq, d = x.shape
    assert d == D_MODEL and seq % TILE == 0 and seq % GRID_W == 0
    depth = norm_g.shape[0]
    assert depth == 1
    l = 0
    x2d = x.reshape(batch * seq, d)
    g = norm_g[l].reshape(1, d)

    w = w_in[l]
    gq, gk, gv = w[:, 0:512], w[:, 512:1024], w[:, 1024:2048]
    rfw, rbw = w[:, 2048:2064], w[:, 2064:2080]
    gg, nq, nk, nv, ng = (w[:, 2080:3104], w[:, 3104:4128], w[:, 4128:5152],
                          w[:, 5152:6176], w[:, 6176:7200])
    w_main = jnp.concatenate([gv, gg, nv, ng, gq], axis=1).astype(BF16)
    cs_main = jnp.concatenate([jnp.ones((4 * 1024,), F32),
                               jnp.full((GLA_KEY_WIDTH,), GLA_DK ** -0.5, F32)]).reshape(1, MAIN_W)
    w_qk = jnp.concatenate([nq, nk], axis=1).astype(BF16)
    cs_qk = jnp.concatenate([jnp.tile(q_norm_g[l], NA_HEADS) * (NA_DH ** -0.5),
                             jnp.tile(k_norm_g[l], NA_HEADS)]).reshape(1, 2 * NA_WIDTH)
    w_r = jnp.concatenate([rfw, rbw], axis=1).astype(BF16)
    cs_r = jnp.ones((1, 2 * GLA_RANK), F32)
    w_t = jnp.concatenate([gk, rfw, rbw], axis=1).T.astype(BF16)

    wd = jnp.stack([w_decay_fwd[l], w_decay_bwd[l]])
    wd_hi = wd.astype(BF16)
    wd_lo = (wd - wd_hi.astype(F32)).astype(BF16)
    wd_s = jnp.stack([wd_hi, wd_lo], axis=1)
    wdt_s = jnp.swapaxes(wd_s, 2, 3)
    zeros_r = jnp.zeros_like(wd_s[0])
    wd_s = jnp.stack([jnp.concatenate([wd_s[0], zeros_r], axis=1),
                      jnp.concatenate([zeros_r, wd_s[1]], axis=1)])
    bias = jnp.stack([b_decay_fwd[l], b_decay_bwd[l]]).reshape(2, 1, GLA_KEY_WIDTH)
    biast = bias.reshape(2, GLA_KEY_WIDTH, 1)

    tm = _pick(batch * seq, (512, 256, 128))
    zmain = _proj_call(x2d, g, w_main, cs_main, mode="scale", out_dtype=BF16, tm=tm, tn=512, name="proj_main")
    zqk = _proj_call(x2d, g, w_qk, cs_qk, mode="qknorm", out_dtype=BF16, tm=tm, tn=512, name="proj_qk")
    rtok = _proj_call(x2d, g, w_r, cs_r, mode="scale", out_dtype=F32, tm=tm, tn=2 * GLA_RANK, name="proj_r")
    ztr = _proj_t_call(x2d, g, w_t, tm=tm, name="proj_t")

    zmain_m = _proj_call(meta_tokens, g, w_main, cs_main, mode="scale", out_dtype=BF16, tm=N_META, tn=512,
                         name="meta_main")
    zqk_m = _proj_call(meta_tokens, g, w_qk, cs_qk, mode="qknorm", out_dtype=BF16, tm=N_META, tn=512,
                       name="meta_qk")
    ztr_m = _proj_t_call(meta_tokens, g, w_t, tm=N_META, name="meta_t")
    zmain_m_pad = jnp.pad(zmain_m, ((0, TILE - N_META), (0, 0)))
    ztr_m_pad = jnp.pad(ztr_m, ((0, 0), (0, TILE - N_META)))

    o_f, o_b = _gla_call(zmain, ztr, rtok, zmain_m_pad, ztr_m_pad, wd_s, wdt_s, bias, biast,
                         batch=batch, seq=seq, tokens_per_step=_pick(seq, (512, 256, 128)))

    rpb_pad = jnp.pad(rpb[l], ((0, 0), (0, 0), (0, 2 * NA_DH - rpb.shape[-1])))
    mb = meta_bias[l].reshape(NA_HEADS, 1, N_META)
    o_na = _na_call(zqk, zmain, zqk_m, zmain_m, rpb_pad, mb, batch=batch, seq=seq,
                    rows_per_step=_pick(seq // GRID_W, (8, 4, 2, 1)))

    out = _out_call(x2d, o_f, o_b, zmain, o_na, gla_out_norm_g[l].reshape(1, GLA_DV),
                    w_out[l].astype(BF16), tm=_pick(batch * seq, (512, 256, 128)))
    return out.reshape(batch, seq, d)
```

```python
import functools

import jax
import jax.numpy as jnp
import numpy as np
from jax import lax
from jax.experimental import pallas as pl
from jax.experimental.pallas import tpu as pltpu

F32 = jnp.float32
BF16 = jnp.bfloat16

D_MODEL = 1024
N_META = 16
GRID_W = 64
GLA_HEADS = 4
GLA_DK = 128
GLA_DV = 256
GLA_KEY_WIDTH = GLA_HEADS * GLA_DK
GLA_WIDTH = GLA_HEADS * GLA_DV
GLA_RANK = 16
GLA_TAU = 16.0
GLA_CHUNK = 64
NA_HEADS = 16
NA_DH = 64
NA_WIDTH = NA_HEADS * NA_DH
NA_WIN_H = 8
NA_WIN_W = 16
RMS_EPS = 1e-6
NEG_BIG = -1e30
LANES = 128

COL_GV, COL_GG, COL_NV, COL_NG, COL_GQ, COL_GK, COL_R = 0, 1024, 2048, 3072, 4096, 4608, 5120
MAIN_W = COL_R + LANES
QK_W = 2 * NA_WIDTH
PROJ_TN = 512

TILE = 2 * GLA_CHUNK
NA_ROW_GROUP = 8
VMEM_LIMIT = 58 * 1024 * 1024


def _split_bf16(x):
    hi = x.astype(BF16)
    lo = (x - hi.astype(F32)).astype(BF16)
    return hi, lo


def _dot(a, b):
    return jnp.dot(a, b, preferred_element_type=F32)


def _dot_nt(a, b):
    return lax.dot_general(a, b, (((1,), (1,)), ((), ())), preferred_element_type=F32)


def _dot_tn(a, b):
    return lax.dot_general(a, b, (((0,), (0,)), ((), ())), preferred_element_type=F32)


def _log_sigmoid(x):
    return jnp.minimum(x, 0.0) - jnp.log1p(jnp.exp(-jnp.abs(x)))


def _silu(x):
    return x * (1.0 / (1.0 + jnp.exp(-x)))


def _proj_kernel(x_ref, g_ref, wm_ref, wq_ref, gain_ref, bd_ref, zm_ref, zq_ref):
    x = x_ref[...]
    ms = jnp.mean(x * x, axis=-1, keepdims=True)
    u = (x * lax.rsqrt(ms + RMS_EPS) * g_ref[...]).astype(BF16)
    for c0 in range(0, MAIN_W, PROJ_TN):
        c1 = min(c0 + PROJ_TN, MAIN_W)
        acc = _dot(u, wm_ref[:, c0:c1])
        if COL_GQ <= c0 and c1 <= COL_GK:
            acc = acc * (GLA_DK ** -0.5)
        zm_ref[:, c0:c1] = acc.astype(zm_ref.dtype)
    half = PROJ_TN // 2
    for c0 in range(0, QK_W, PROJ_TN):
        acc = _dot(u, wq_ref[:, c0:c0 + PROJ_TN])
        sq = (acc * acc).astype(BF16)
        ms = jnp.concatenate([_dot(sq[:, :half], bd_ref[...]), _dot(sq[:, half:], bd_ref[...])], axis=1)
        y = acc * lax.rsqrt(ms + RMS_EPS) * gain_ref[:, c0:c0 + PROJ_TN]
        zq_ref[:, c0:c0 + PROJ_TN] = y.astype(zq_ref.dtype)


def _proj_call(x2d, g, w_main, w_qk, gain_qk, *, tm, name):
    m, d = x2d.shape
    assert m % tm == 0 and w_main.shape == (d, MAIN_W) and w_qk.shape == (d, QK_W)
    half = PROJ_TN // 2
    grp = jnp.arange(half) // NA_DH
    bd = jnp.where(grp[:, None] == grp[None, :], 1.0 / NA_DH, 0.0).astype(BF16)
    resident = lambda shape: pl.BlockSpec(shape, lambda i: (0, 0), pipeline_mode=pl.Buffered(1))
    return pl.pallas_call(
        _proj_kernel,
        name=name,
        grid=(m // tm,),
        in_specs=[
            pl.BlockSpec((tm, d), lambda i: (i, 0)),
            resident((1, d)),
            resident((d, MAIN_W)),
            resident((d, QK_W)),
            resident((1, QK_W)),
            resident((half, half)),
        ],
        out_specs=[pl.BlockSpec((tm, MAIN_W), lambda i: (i, 0)),
                   pl.BlockSpec((tm, QK_W), lambda i: (i, 0))],
        out_shape=[jax.ShapeDtypeStruct((m, MAIN_W), BF16), jax.ShapeDtypeStruct((m, QK_W), BF16)],
        compiler_params=pltpu.CompilerParams(
            dimension_semantics=("parallel",), vmem_limit_bytes=VMEM_LIMIT),
    )(x2d, g, w_main, w_qk, gain_qk, bd)


def _gla_kernel(qf_ref, kf_ref, vf_ref, rf_ref, qb_ref, kb_ref, vb_ref, rb_ref,
                km_ref, vm_ref, rm_ref,
                wdf_ref, bf_ref, wdb_ref, bb_ref, tgf_ref, tgb_ref, tgm_ref,
                of_ref, ob_ref, sf_ref, sb_ref, *, n_tiles):
    row = lax.broadcasted_iota(jnp.int32, (TILE, TILE), 0)
    col = lax.broadcasted_iota(jnp.int32, (TILE, TILE), 1)
    same = (row // GLA_CHUNK) == (col // GLA_CHUNK)
    masks = (same & (row >= col), same & (row <= col))

    def decay(r, wd_ref, b_ref):
        return _log_sigmoid(_dot(r, wd_ref[...]) + b_ref[...]) * (1.0 / GLA_TAU)

    def cums(lg, tg_ref):
        hi, lo = _split_bf16(lg)
        return _dot(tg_ref[...], jnp.concatenate([hi, lo], axis=0))

    @pl.when(pl.program_id(2) == 0)
    def _():
        rowm = lax.broadcasted_iota(jnp.int32, (TILE, GLA_DK), 0)
        lg = jnp.where(rowm < N_META, decay(rm_ref[...], wdf_ref, bf_ref), 0.0)
        d_m = cums(lg, tgm_ref)[TILE:]
        k_st = (km_ref[...].astype(F32) * jnp.exp(d_m)).astype(BF16)
        sf_ref[...] = _dot_tn(k_st, vm_ref[...])
        sb_ref[...] = jnp.zeros_like(sb_ref)

    items = []
    for t in range(n_tiles):
        items.append((0, t))
        items.append((1, n_tiles - 1 - t))
    refs = ((qf_ref, kf_ref, vf_ref, rf_ref, wdf_ref, bf_ref, tgf_ref, of_ref),
            (qb_ref, kb_ref, vb_ref, rb_ref, wdb_ref, bb_ref, tgb_ref, ob_ref))

    def rows_of(t):
        return slice(t * TILE, (t + 1) * TILE)

    lgs = [decay(refs[d][3][rows_of(t), :], refs[d][4], refs[d][5]) for d, t in items]
    gds = [cums(lg, refs[d][6]) for lg, (d, t) in zip(lgs, items)]
    prepped = []
    for gd, (d, t) in zip(gds, items):
        g, dd = gd[:TILE], gd[TILE:]
        q = refs[d][0][rows_of(t), :].astype(F32)
        k = refs[d][1][rows_of(t), :].astype(F32)
        q_in = (q * jnp.exp(g)).astype(BF16)
        k_in = (k * jnp.exp(-g)).astype(BF16)
        k_st = (k * jnp.exp(dd)).astype(BF16)
        a_t = jnp.exp(jnp.transpose(g + dd))
        prepped.append((q_in, k_in, k_st, a_t))
    amats = [jnp.where(masks[d], _dot_nt(q_in, k_in), 0.0).astype(BF16)
             for (q_in, k_in, _, _), (d, t) in zip(prepped, items)]
    us = []
    for (_, _, k_st, _), (d, t) in zip(prepped, items):
        v = refs[d][2][rows_of(t), :]
        us.append([_dot_tn(k_st[h * GLA_CHUNK:(h + 1) * GLA_CHUNK], v[h * GLA_CHUNK:(h + 1) * GLA_CHUNK])
                   for h in range(2)])
    intras = [_dot(a, refs[d][2][rows_of(t), :]) for a, (d, t) in zip(amats, items)]

    state = [sf_ref[...], sb_ref[...]]
    for (q_in, _, _, a_t), u, o_intra, (d, t) in zip(prepped, us, intras, items):
        s = state[d]
        outs = [None, None]
        for h in ((0, 1) if d == 0 else (1, 0)):
            rows = slice(h * GLA_CHUNK, (h + 1) * GLA_CHUNK)
            outs[h] = o_intra[rows] + _dot(q_in[rows], s.astype(BF16))
            s = a_t[:, h * GLA_CHUNK:h * GLA_CHUNK + 1] * s + u[h]
        state[d] = s
        refs[d][7][rows_of(t), :] = jnp.concatenate(outs, axis=0).astype(refs[d][7].dtype)
    sf_ref[...] = state[0]
    sb_ref[...] = state[1]


def _gla_constants():
    i = np.arange(TILE)
    same = (i[:, None] // GLA_CHUNK) == (i[None, :] // GLA_CHUNK)
    g_f, d_f = same & (i[None, :] <= i[:, None]), same & (i[None, :] > i[:, None])
    g_b, d_b = same & (i[None, :] >= i[:, None]), same & (i[None, :] < i[:, None])
    g_m, d_m = i[None, :] <= i[:, None], i[None, :] > i[:, None]

    def stack(g, d):
        m = np.concatenate([g, d], axis=0)
        return jnp.asarray(np.concatenate([m, m], axis=1), BF16)

    return stack(g_f, d_f), stack(g_b, d_b), stack(g_m, d_m)


def _gla_call(zmain, zmain_m, wd, bias, *, batch, seq, tokens_per_step):
    t_blk = tokens_per_step
    nb = seq // t_blk
    n_tiles = t_blk // TILE
    tg_f, tg_b, tg_m = _gla_constants()

    def tok(fwd):
        if fwd:
            return lambda b, hh, i: b * nb + i
        return lambda b, hh, i: b * nb + (nb - 1 - i)

    def dir_specs(fwd):
        t = tok(fwd)
        return [
            pl.BlockSpec((t_blk, GLA_DK), lambda b, hh, i: (t(b, hh, i), COL_GQ // GLA_DK + hh)),
            pl.BlockSpec((t_blk, GLA_DK), lambda b, hh, i: (t(b, hh, i), COL_GK // GLA_DK + hh)),
            pl.BlockSpec((t_blk, GLA_DV), lambda b, hh, i: (t(b, hh, i), COL_GV // GLA_DV + hh)),
            pl.BlockSpec((t_blk, LANES), lambda b, hh, i: (t(b, hh, i), COL_R // LANES)),
        ]

    def w_specs(d):
        return [pl.BlockSpec((None, LANES, GLA_DK), lambda b, hh, i: (d, 0, hh)),
                pl.BlockSpec((None, 1, GLA_DK), lambda b, hh, i: (d, 0, hh))]

    const = lambda shape: pl.BlockSpec(shape, lambda b, hh, i: (0,) * len(shape))
    in_specs = (
        dir_specs(True) + dir_specs(False)
        + [pl.BlockSpec((TILE, GLA_DK), lambda b, hh, i: (0, COL_GK // GLA_DK + hh)),
           pl.BlockSpec((TILE, GLA_DV), lambda b, hh, i: (0, COL_GV // GLA_DV + hh)),
           pl.BlockSpec((TILE, LANES), lambda b, hh, i: (0, COL_R // LANES))]
        + w_specs(0) + w_specs(1)
        + [const((2 * TILE, 2 * TILE))] * 3
    )
    args = [zmain] * 8 + [zmain_m] * 3 + [wd, bias, wd, bias, tg_f, tg_b, tg_m]
    out_spec_f = pl.BlockSpec((t_blk, GLA_DV), lambda b, hh, i: (b * nb + i, hh))
    out_spec_b = pl.BlockSpec((t_blk, GLA_DV), lambda b, hh, i: (b * nb + (nb - 1 - i), hh))
    o_shape = jax.ShapeDtypeStruct((batch * seq, GLA_WIDTH), BF16)
    return pl.pallas_call(
        functools.partial(_gla_kernel, n_tiles=n_tiles),
        name="gla_scan",
        grid=(batch, GLA_HEADS, nb),
        in_specs=in_specs,
        out_specs=[out_spec_f, out_spec_b],
        out_shape=[o_shape, o_shape],
        scratch_shapes=[pltpu.VMEM((GLA_DK, GLA_DV), F32), pltpu.VMEM((GLA_DK, GLA_DV), F32)],
        compiler_params=pltpu.CompilerParams(
            dimension_semantics=("parallel", "parallel", "arbitrary"), vmem_limit_bytes=VMEM_LIMIT),
    )(*args)


def _na_kernel(q_ref, k_ref, v_ref, gate_ref, km_ref, vm_ref, rpb_ref, mb_ref, o_ref, tab_ref,
               *, rows_per_step, n_rows):
    w = GRID_W
    two = 2 * w
    n_idx = 2 * NA_WIN_H - 1

    @pl.when(pl.program_id(2) == 0)
    def _():
        r_i = lax.broadcasted_iota(jnp.int32, (w, two), 0)
        l_i = lax.broadcasted_iota(jnp.int32, (w, two), 1)
        cp = l_i % w
        cs = jnp.clip(r_i - NA_WIN_W // 2, 0, w - NA_WIN_W)
        valid = (cp >= cs) & (cp < cs + NA_WIN_W)
        for hh in range(2):
            rolled = []
            for idx in range(n_idx):
                vrow = jnp.broadcast_to(rpb_ref[hh, idx:idx + 1, :], (w, two))
                lo = pltpu.roll(vrow, two - (NA_WIN_W - 1), 1, stride=1, stride_axis=0)
                hi_ = pltpu.roll(vrow, two - (NA_WIN_W - 1) + w, 1, stride=1, stride_axis=0)
                rolled.append((lo, hi_))
            for j in range(n_idx - 1):
                t = jnp.where(l_i < w, rolled[j][0], rolled[j + 1][1])
                tab_ref[j, hh * w:(hh + 1) * w, :] = jnp.where(valid, t, NEG_BIG)

    lane = lax.broadcasted_iota(jnp.int32, (w, two), 1)
    head0 = lane < w
    srow = lax.broadcasted_iota(jnp.int32, (two, N_META), 0)
    mb = jnp.where(srow < w, mb_ref[0], mb_ref[1])
    km = km_ref[...]
    vm = vm_ref[...]
    zero = jnp.zeros((), BF16)

    def row_group(gi, carry):
        rows = []
        for j in range(NA_ROW_GROUP):
            rl = gi * NA_ROW_GROUP + j
            r = pl.program_id(2) * rows_per_step + rl
            rs = jnp.clip(r - NA_WIN_H // 2, 0, n_rows - NA_WIN_H)
            base = rs - r + (NA_WIN_H - 1)
            qoff = pl.multiple_of(rl * w, w)
            koff = pl.multiple_of(rs * w, w)
            q = q_ref[pl.ds(qoff, w), :]
            qs = jnp.concatenate([jnp.where(head0, q, zero), jnp.where(head0, zero, q)], axis=0)
            kw = k_ref[pl.ds(koff, NA_WIN_H * w), :]
            s = _dot_nt(qs, kw)
            sm = _dot_nt(qs, km)
            rows.append((qoff, koff, base, s, sm))
        probs = []
        for qoff, koff, base, s, sm in rows:
            tiles = [s[:, t * two:(t + 1) * two] + tab_ref[base + 2 * t] for t in range(NA_WIN_H // 2)]
            sm = sm + mb
            m = jnp.maximum(jnp.maximum(tiles[0], tiles[1]), jnp.maximum(tiles[2], tiles[3]))
            m = jnp.maximum(jnp.max(m, axis=-1, keepdims=True), jnp.max(sm, axis=-1, keepdims=True))
            ps = [jnp.exp(t - m) for t in tiles]
            pm = jnp.exp(sm - m)
            den = (jnp.sum(ps[0] + ps[1] + ps[2] + ps[3], axis=-1, keepdims=True)
                   + jnp.sum(pm, axis=-1, keepdims=True))
            p = jnp.concatenate([t.astype(BF16) for t in ps], axis=1)
            probs.append((qoff, koff, p, pm.astype(BF16), den))
        for qoff, koff, p, pm, den in probs:
            vw = v_ref[pl.ds(koff, NA_WIN_H * w), :]
            o = _dot(p, vw) + _dot(pm, vm)
            o = o * (1.0 / den)
            o = jnp.where(head0, o[:w], o[w:])
            gate = gate_ref[pl.ds(qoff, w), :].astype(F32)
            o_ref[pl.ds(qoff, w), :] = (o * _silu(gate)).astype(o_ref.dtype)
        return carry

    lax.fori_loop(0, rows_per_step // NA_ROW_GROUP, row_group, 0)


def _na_call(zqk, zmain, zqk_m, zmain_m, rpb_pad, meta_bias, *, batch, seq, rows_per_step):
    n_rows = seq // GRID_W
    assert n_rows >= NA_WIN_H and n_rows % rows_per_step == 0 and rows_per_step % NA_ROW_GROUP == 0
    nsteps = n_rows // rows_per_step
    hp = NA_HEADS // 2
    lanes = 2 * NA_DH
    tq = rows_per_step * GRID_W
    n_idx = 2 * NA_WIN_H - 1
    in_specs = [
        pl.BlockSpec((tq, lanes), lambda b, p, i: (b * nsteps + i, p)),
        pl.BlockSpec((seq, lanes), lambda b, p, i: (b, NA_WIDTH // lanes + p)),
        pl.BlockSpec((seq, lanes), lambda b, p, i: (b, COL_NV // lanes + p)),
        pl.BlockSpec((tq, lanes), lambda b, p, i: (b * nsteps + i, COL_NG // lanes + p)),
        pl.BlockSpec((N_META, lanes), lambda b, p, i: (0, NA_WIDTH // lanes + p)),
        pl.BlockSpec((N_META, lanes), lambda b, p, i: (0, COL_NV // lanes + p)),
        pl.BlockSpec((2, n_idx, lanes), lambda b, p, i: (p, 0, 0)),
        pl.BlockSpec((2, 1, N_META), lambda b, p, i: (p, 0, 0)),
    ]
    return pl.pallas_call(
        functools.partial(_na_kernel, rows_per_step=rows_per_step, n_rows=n_rows),
        name="na_attn",
        grid=(batch, hp, nsteps),
        in_specs=in_specs,
        out_specs=pl.BlockSpec((tq, lanes), lambda b, p, i: (b * nsteps + i, p)),
        out_shape=jax.ShapeDtypeStruct((batch * seq, NA_WIDTH), BF16),
        scratch_shapes=[pltpu.VMEM((n_idx - 1, lanes, lanes), F32)],
        compiler_params=pltpu.CompilerParams(
            dimension_semantics=("parallel", "parallel", "arbitrary"), vmem_limit_bytes=VMEM_LIMIT),
    )(zqk, zqk, zmain, zmain, zqk_m, zmain_m, rpb_pad, meta_bias)


def _out_kernel(x_ref, of_ref, ob_ref, gg_ref, na_ref, gn_ref, w_ref, o_ref):
    s = of_ref[...].astype(F32) + ob_ref[...].astype(F32)
    gate = gg_ref[...].astype(F32)
    gn = gn_ref[...]
    heads = []
    for h in range(GLA_HEADS):
        cols = slice(h * GLA_DV, (h + 1) * GLA_DV)
        sh = s[:, cols]
        ms = jnp.mean(sh * sh, axis=-1, keepdims=True)
        y = sh * lax.rsqrt(ms + RMS_EPS) * gn
        heads.append((y * _silu(gate[:, cols])).astype(BF16))
    o_gla = jnp.concatenate(heads, axis=1)
    acc = _dot(o_gla, w_ref[:GLA_WIDTH, :]) + _dot(na_ref[...], w_ref[GLA_WIDTH:, :])
    o_ref[...] = x_ref[...] + acc


def _out_call(x2d, o_f, o_b, zmain, o_na, gn, w_out, *, tm):
    m, d = x2d.shape
    assert m % tm == 0
    return pl.pallas_call(
        _out_kernel,
        name="out_proj",
        grid=(m // tm,),
        in_specs=[
            pl.BlockSpec((tm, d), lambda i: (i, 0)),
            pl.BlockSpec((tm, GLA_WIDTH), lambda i: (i, 0)),
            pl.BlockSpec((tm, GLA_WIDTH), lambda i: (i, 0)),
            pl.BlockSpec((tm, GLA_WIDTH), lambda i: (i, COL_GG // GLA_WIDTH)),
            pl.BlockSpec((tm, NA_WIDTH), lambda i: (i, 0)),
            pl.BlockSpec((1, GLA_DV), lambda i: (0, 0)),
            pl.BlockSpec((GLA_WIDTH + NA_WIDTH, d), lambda i: (0, 0)),
        ],
        out_specs=pl.BlockSpec((tm, d), lambda i: (i, 0)),
        out_shape=jax.ShapeDtypeStruct((m, d), F32),
        compiler_params=pltpu.CompilerParams(
            dimension_semantics=("parallel",), vmem_limit_bytes=VMEM_LIMIT),
    )(x2d, o_f, o_b, zmain, o_na, gn, w_out)


def _pick(n, cands):
    for c in cands:
        if n % c == 0:
            return c
    raise ValueError(f"no tile for {n}")


def kernel(x, meta_tokens, norm_g, w_in, w_decay_fwd, b_decay_fwd, w_decay_bwd, b_decay_bwd,
           gla_out_norm_g, q_norm_g, k_norm_g, rpb, meta_bias, w_out):
    batch, seq, d = x.shape
    assert d == D_MODEL and seq % TILE == 0 and seq % GRID_W == 0
    depth = norm_g.shape[0]
    assert depth == 1
    l = 0
    x2d = x.reshape(batch * seq, d)
    g = norm_g[l].reshape(1, d)

    w = w_in[l]
    gq, gk, gv = w[:, 0:512], w[:, 512:1024], w[:, 1024:2048]
    r_w = w[:, 2048:2080]
    gg, nq, nk, nv, ng = (w[:, 2080:3104], w[:, 3104:4128], w[:, 4128:5152],
                          w[:, 5152:6176], w[:, 6176:7200])
    w_main = jnp.concatenate([gv, gg, nv, ng, gq, gk, r_w,
                              jnp.zeros((d, LANES - 2 * GLA_RANK), w.dtype)], axis=1).astype(BF16)
    w_qk = jnp.concatenate([nq, nk], axis=1).astype(BF16)
    gain_qk = jnp.concatenate([jnp.tile(q_norm_g[l], NA_HEADS) * (NA_DH ** -0.5),
                               jnp.tile(k_norm_g[l], NA_HEADS)]).reshape(1, QK_W)

    zr = jnp.zeros((GLA_RANK, GLA_KEY_WIDTH), F32)
    ztail = jnp.zeros((LANES - 2 * GLA_RANK, GLA_KEY_WIDTH), F32)
    wd = jnp.stack([jnp.concatenate([w_decay_fwd[l], zr, ztail], axis=0),
                    jnp.concatenate([zr, w_decay_bwd[l], ztail], axis=0)]).astype(BF16)
    bias = jnp.stack([b_decay_fwd[l], b_decay_bwd[l]]).reshape(2, 1, GLA_KEY_WIDTH)

    zmain, zqk = _proj_call(x2d, g, w_main, w_qk, gain_qk, tm=_pick(batch * seq, (512, 256, 128)),
                            name="proj_tokens")
    zmain_m, zqk_m = _proj_call(meta_tokens, g, w_main, w_qk, gain_qk, tm=N_META, name="proj_meta")
    zmain_m_pad = jnp.pad(zmain_m, ((0, TILE - N_META), (0, 0)))

    o_f, o_b = _gla_call(zmain, zmain_m_pad, wd, bias, batch=batch, seq=seq,
                         tokens_per_step=_pick(seq, (512, 256, 128)))

    rpb_pad = jnp.pad(rpb[l], ((0, 0), (0, 0), (0, 2 * NA_DH - rpb.shape[-1])))
    mb = meta_bias[l].reshape(NA_HEADS, 1, N_META)
    o_na = _na_call(zqk, zmain, zqk_m, zmain_m, rpb_pad, mb, batch=batch, seq=seq,
                    rows_per_step=_pick(seq // GRID_W, (8,)))

    out = _out_call(x2d, o_f, o_b, zmain, o_na, gla_out_norm_g[l].reshape(1, GLA_DV),
                    w_out[l].astype(BF16), tm=_pick(batch * seq, (512, 256, 128)))
    return out.reshape(batch, seq, d)
```

```python
import functools

import jax
import jax.numpy as jnp
import numpy as np
from jax import lax
from jax.experimental import pallas as pl
from jax.experimental.pallas import tpu as pltpu

F32 = jnp.float32
BF16 = jnp.bfloat16

D_MODEL = 1024
N_META = 16
GRID_W = 64
GLA_HEADS = 4
GLA_DK = 128
GLA_DV = 256
GLA_KEY_WIDTH = GLA_HEADS * GLA_DK
GLA_WIDTH = GLA_HEADS * GLA_DV
GLA_RANK = 16
GLA_TAU = 16.0
GLA_CHUNK = 64
NA_HEADS = 16
NA_DH = 64
NA_WIDTH = NA_HEADS * NA_DH
NA_WIN_H = 8
NA_WIN_W = 16
RMS_EPS = 1e-6
NEG_BIG = -1e30
LANES = 128

COL_GV, COL_GG, COL_NV, COL_NG, COL_GQ, COL_GK, COL_R = 0, 1024, 2048, 3072, 4096, 4608, 5120
MAIN_W = COL_R + LANES
IN_GQ, IN_GK, IN_GV, IN_R, IN_GG, IN_NQ, IN_NK, IN_NV, IN_NG = 0, 512, 1024, 2048, 2080, 3104, 4128, 5152, 6176
D_IN_PROJ = 7200
QK_W = 2 * NA_WIDTH
PROJ_TN = 512

TILE = 2 * GLA_CHUNK
NA_ROW_GROUP = 8
VMEM_LIMIT = 58 * 1024 * 1024


def _split_bf16(x):
    hi = x.astype(BF16)
    lo = (x - hi.astype(F32)).astype(BF16)
    return hi, lo


def _dot(a, b):
    return jnp.dot(a, b, preferred_element_type=F32)


def _dot_nt(a, b):
    return lax.dot_general(a, b, (((1,), (1,)), ((), ())), preferred_element_type=F32)


def _dot_tn(a, b):
    return lax.dot_general(a, b, (((0,), (0,)), ((), ())), preferred_element_type=F32)


def _log_sigmoid(x):
    return jnp.minimum(x, 0.0) - jnp.log1p(jnp.exp(-jnp.abs(x)))


def _silu(x):
    return x * (1.0 / (1.0 + jnp.exp(-x)))


MAIN_SEGMENTS = ((COL_GV, IN_GV, 1024), (COL_GG, IN_GG, 1024), (COL_NV, IN_NV, 1024), (COL_NG, IN_NG, 1024),
                 (COL_GQ, IN_GQ, 512), (COL_GK, IN_GK, 512),
                 (COL_R, IN_R, LANES))
QK_SEGMENTS = ((0, IN_NQ, NA_WIDTH), (NA_WIDTH, IN_NK, NA_WIDTH))


def _proj_kernel(x_ref, g_ref, wt_ref, gain_ref, bd_ref, zm_ref, zq_ref):
    x = x_ref[...]
    ms = jnp.mean(x * x, axis=-1, keepdims=True)
    u = (x * lax.rsqrt(ms + RMS_EPS) * g_ref[...]).astype(BF16)
    for dst, src, width in MAIN_SEGMENTS:
        for off in range(0, width, PROJ_TN):
            n = min(PROJ_TN, width - off)
            acc = _dot_nt(u, wt_ref[src + off:src + off + n, :])
            if dst == COL_GQ:
                acc = acc * (GLA_DK ** -0.5)
            zm_ref[:, dst + off:dst + off + n] = acc.astype(zm_ref.dtype)
    half = PROJ_TN // 2
    for dst, src, width in QK_SEGMENTS:
        for off in range(0, width, PROJ_TN):
            acc = _dot_nt(u, wt_ref[src + off:src + off + PROJ_TN, :])
            sq = (acc * acc).astype(BF16)
            ms = jnp.concatenate([_dot(sq[:, :half], bd_ref[...]), _dot(sq[:, half:], bd_ref[...])], axis=1)
            y = acc * lax.rsqrt(ms + RMS_EPS) * gain_ref[:, dst + off:dst + off + PROJ_TN]
            zq_ref[:, dst + off:dst + off + PROJ_TN] = y.astype(zq_ref.dtype)


def _proj_call(x2d, g, wt, gain_qk, *, tm, name):
    m, d = x2d.shape
    assert m % tm == 0 and wt.shape == (D_IN_PROJ, d)
    half = PROJ_TN // 2
    grp = jnp.arange(half) // NA_DH
    bd = jnp.where(grp[:, None] == grp[None, :], 1.0 / NA_DH, 0.0).astype(BF16)
    resident = lambda shape: pl.BlockSpec(shape, lambda i: (0, 0), pipeline_mode=pl.Buffered(1))
    return pl.pallas_call(
        _proj_kernel,
        name=name,
        grid=(m // tm,),
        in_specs=[
            pl.BlockSpec((tm, d), lambda i: (i, 0)),
            resident((1, d)),
            resident((D_IN_PROJ, d)),
            resident((1, QK_W)),
            resident((half, half)),
        ],
        out_specs=[pl.BlockSpec((tm, MAIN_W), lambda i: (i, 0)),
                   pl.BlockSpec((tm, QK_W), lambda i: (i, 0))],
        out_shape=[jax.ShapeDtypeStruct((m, MAIN_W), BF16), jax.ShapeDtypeStruct((m, QK_W), BF16)],
        compiler_params=pltpu.CompilerParams(
            dimension_semantics=("parallel",), vmem_limit_bytes=VMEM_LIMIT),
    )(x2d, g, wt, gain_qk, bd)


def _gla_kernel(qf_ref, kf_ref, vf_ref, rf_ref, qb_ref, kb_ref, vb_ref, rb_ref,
                km_ref, vm_ref, rm_ref,
                wdf_ref, bf_ref, wdb_ref, bb_ref, tgf_ref, tgb_ref, tgm_ref,
                of_ref, ob_ref, sf_ref, sb_ref, *, n_tiles):
    row = lax.broadcasted_iota(jnp.int32, (TILE, TILE), 0)
    col = lax.broadcasted_iota(jnp.int32, (TILE, TILE), 1)
    same = (row // GLA_CHUNK) == (col // GLA_CHUNK)
    masks = (same & (row >= col), same & (row <= col))

    def decay(r, wd_ref, b_ref):
        return _log_sigmoid(_dot(r, wd_ref[...]) + b_ref[...]) * (1.0 / GLA_TAU)

    def cums(lg, tg_ref):
        hi, lo = _split_bf16(lg)
        return _dot(tg_ref[...], jnp.concatenate([hi, lo], axis=0))

    @pl.when(pl.program_id(2) == 0)
    def _():
        rowm = lax.broadcasted_iota(jnp.int32, (TILE, GLA_DK), 0)
        lg = jnp.where(rowm < N_META, decay(rm_ref[...], wdf_ref, bf_ref), 0.0)
        d_m = cums(lg, tgm_ref)[TILE:]
        k_st = (km_ref[...].astype(F32) * jnp.exp(d_m)).astype(BF16)
        sf_ref[...] = _dot_tn(k_st, vm_ref[...])
        sb_ref[...] = jnp.zeros_like(sb_ref)

    items = []
    for t in range(n_tiles):
        items.append((0, t))
        items.append((1, n_tiles - 1 - t))
    refs = ((qf_ref, kf_ref, vf_ref, rf_ref, wdf_ref, bf_ref, tgf_ref, of_ref),
            (qb_ref, kb_ref, vb_ref, rb_ref, wdb_ref, bb_ref, tgb_ref, ob_ref))

    def rows_of(t):
        return slice(t * TILE, (t + 1) * TILE)

    lgs = [decay(refs[d][3][rows_of(t), :], refs[d][4], refs[d][5]) for d, t in items]
    gds = [cums(lg, refs[d][6]) for lg, (d, t) in zip(lgs, items)]
    prepped = []
    for gd, (d, t) in zip(gds, items):
        g, dd = gd[:TILE], gd[TILE:]
        q = refs[d][0][rows_of(t), :].astype(F32)
        k = refs[d][1][rows_of(t), :].astype(F32)
        q_in = (q * jnp.exp(g)).astype(BF16)
        k_in = (k * jnp.exp(-g)).astype(BF16)
        k_st = (k * jnp.exp(dd)).astype(BF16)
        a_t = jnp.exp(jnp.transpose(g + dd))
        prepped.append((q_in, k_in, k_st, a_t))
    amats = [jnp.where(masks[d], _dot_nt(q_in, k_in), 0.0).astype(BF16)
             for (q_in, k_in, _, _), (d, t) in zip(prepped, items)]
    us = []
    for (_, _, k_st, _), (d, t) in zip(prepped, items):
        v = refs[d][2][rows_of(t), :]
        us.append([_dot_tn(k_st[h * GLA_CHUNK:(h + 1) * GLA_CHUNK], v[h * GLA_CHUNK:(h + 1) * GLA_CHUNK])
                   for h in range(2)])
    intras = [_dot(a, refs[d][2][rows_of(t), :]) for a, (d, t) in zip(amats, items)]

    state = [sf_ref[...], sb_ref[...]]
    for (q_in, _, _, a_t), u, o_intra, (d, t) in zip(prepped, us, intras, items):
        s = state[d]
        outs = [None, None]
        for h in ((0, 1) if d == 0 else (1, 0)):
            rows = slice(h * GLA_CHUNK, (h + 1) * GLA_CHUNK)
            outs[h] = o_intra[rows] + _dot(q_in[rows], s.astype(BF16))
            s = a_t[:, h * GLA_CHUNK:h * GLA_CHUNK + 1] * s + u[h]
        state[d] = s
        refs[d][7][rows_of(t), :] = jnp.concatenate(outs, axis=0).astype(refs[d][7].dtype)
    sf_ref[...] = state[0]
    sb_ref[...] = state[1]


def _gla_constants():
    i = np.arange(TILE)
    same = (i[:, None] // GLA_CHUNK) == (i[None, :] // GLA_CHUNK)
    g_f, d_f = same & (i[None, :] <= i[:, None]), same & (i[None, :] > i[:, None])
    g_b, d_b = same & (i[None, :] >= i[:, None]), same & (i[None, :] < i[:, None])
    g_m, d_m = i[None, :] <= i[:, None], i[None, :] > i[:, None]

    def stack(g, d):
        m = np.concatenate([g, d], axis=0)
        return jnp.asarray(np.concatenate([m, m], axis=1), BF16)

    return stack(g_f, d_f), stack(g_b, d_b), stack(g_m, d_m)


def _gla_call(zmain, zmain_m, wd, bias, *, batch, seq, tokens_per_step):
    t_blk = tokens_per_step
    nb = seq // t_blk
    n_tiles = t_blk // TILE
    tg_f, tg_b, tg_m = _gla_constants()

    def tok(fwd):
        if fwd:
            return lambda b, hh, i: b * nb + i
        return lambda b, hh, i: b * nb + (nb - 1 - i)

    def dir_specs(fwd):
        t = tok(fwd)
        return [
            pl.BlockSpec((t_blk, GLA_DK), lambda b, hh, i: (t(b, hh, i), COL_GQ // GLA_DK + hh)),
            pl.BlockSpec((t_blk, GLA_DK), lambda b, hh, i: (t(b, hh, i), COL_GK // GLA_DK + hh)),
            pl.BlockSpec((t_blk, GLA_DV), lambda b, hh, i: (t(b, hh, i), COL_GV // GLA_DV + hh)),
            pl.BlockSpec((t_blk, LANES), lambda b, hh, i: (t(b, hh, i), COL_R // LANES)),
        ]

    def w_specs(d):
        return [pl.BlockSpec((None, LANES, GLA_DK), lambda b, hh, i: (d, 0, hh)),
                pl.BlockSpec((None, 1, GLA_DK), lambda b, hh, i: (d, 0, hh))]

    const = lambda shape: pl.BlockSpec(shape, lambda b, hh, i: (0,) * len(shape))
    in_specs = (
        dir_specs(True) + dir_specs(False)
        + [pl.BlockSpec((TILE, GLA_DK), lambda b, hh, i: (0, COL_GK // GLA_DK + hh)),
           pl.BlockSpec((TILE, GLA_DV), lambda b, hh, i: (0, COL_GV // GLA_DV + hh)),
           pl.BlockSpec((TILE, LANES), lambda b, hh, i: (0, COL_R // LANES))]
        + w_specs(0) + w_specs(1)
        + [const((2 * TILE, 2 * TILE))] * 3
    )
    args = [zmain] * 8 + [zmain_m] * 3 + [wd, bias, wd, bias, tg_f, tg_b, tg_m]
    out_spec_f = pl.BlockSpec((t_blk, GLA_DV), lambda b, hh, i: (b * nb + i, hh))
    out_spec_b = pl.BlockSpec((t_blk, GLA_DV), lambda b, hh, i: (b * nb + (nb - 1 - i), hh))
    o_shape = jax.ShapeDtypeStruct((batch * seq, GLA_WIDTH), BF16)
    return pl.pallas_call(
        functools.partial(_gla_kernel, n_tiles=n_tiles),
        name="gla_scan",
        grid=(batch, GLA_HEADS, nb),
        in_specs=in_specs,
        out_specs=[out_spec_f, out_spec_b],
        out_shape=[o_shape, o_shape],
        scratch_shapes=[pltpu.VMEM((GLA_DK, GLA_DV), F32), pltpu.VMEM((GLA_DK, GLA_DV), F32)],
        compiler_params=pltpu.CompilerParams(
            dimension_semantics=("parallel", "parallel", "arbitrary"), vmem_limit_bytes=VMEM_LIMIT),
    )(*args)


def _na_kernel(q_ref, k_ref, v_ref, gate_ref, km_ref, vm_ref, rpb_ref, mb_ref, o_ref, tab_ref,
               *, rows_per_step, n_rows):
    w = GRID_W
    two = 2 * w
    n_idx = 2 * NA_WIN_H - 1

    @pl.when(pl.program_id(2) == 0)
    def _():
        r_i = lax.broadcasted_iota(jnp.int32, (w, two), 0)
        l_i = lax.broadcasted_iota(jnp.int32, (w, two), 1)
        cp = l_i % w
        cs = jnp.clip(r_i - NA_WIN_W // 2, 0, w - NA_WIN_W)
        valid = (cp >= cs) & (cp < cs + NA_WIN_W)
        for hh in range(2):
            rolled = []
            for idx in range(n_idx):
                vrow = jnp.broadcast_to(rpb_ref[hh, idx:idx + 1, :], (w, two))
                lo = pltpu.roll(vrow, two - (NA_WIN_W - 1), 1, stride=1, stride_axis=0)
                hi_ = pltpu.roll(vrow, two - (NA_WIN_W - 1) + w, 1, stride=1, stride_axis=0)
                rolled.append((lo, hi_))
            for j in range(n_idx - 1):
                t = jnp.where(l_i < w, rolled[j][0], rolled[j + 1][1])
                tab_ref[j, hh * w:(hh + 1) * w, :] = jnp.where(valid, t, NEG_BIG)

    lane = lax.broadcasted_iota(jnp.int32, (w, two), 1)
    head0 = lane < w
    srow = lax.broadcasted_iota(jnp.int32, (two, N_META), 0)
    mb = jnp.where(srow < w, mb_ref[0], mb_ref[1])
    km = km_ref[...]
    vm = vm_ref[...]
    zero = jnp.zeros((), BF16)

    def row_group(gi, carry):
        rows = []
        for j in range(NA_ROW_GROUP):
            rl = gi * NA_ROW_GROUP + j
            r = pl.program_id(2) * rows_per_step + rl
            rs = jnp.clip(r - NA_WIN_H // 2, 0, n_rows - NA_WIN_H)
            base = rs - r + (NA_WIN_H - 1)
            qoff = pl.multiple_of(rl * w, w)
            koff = pl.multiple_of(rs * w, w)
            q = q_ref[pl.ds(qoff, w), :]
            qs = jnp.concatenate([jnp.where(head0, q, zero), jnp.where(head0, zero, q)], axis=0)
            kw = k_ref[pl.ds(koff, NA_WIN_H * w), :]
            s = _dot_nt(qs, kw)
            sm = _dot_nt(qs, km)
            rows.append((qoff, koff, base, s, sm))
        probs = []
        for qoff, koff, base, s, sm in rows:
            tiles = [s[:, t * two:(t + 1) * two] + tab_ref[base + 2 * t] for t in range(NA_WIN_H // 2)]
            sm = sm + mb
            m = jnp.maximum(jnp.maximum(tiles[0], tiles[1]), jnp.maximum(tiles[2], tiles[3]))
            m = jnp.maximum(jnp.max(m, axis=-1, keepdims=True), jnp.max(sm, axis=-1, keepdims=True))
            ps = [jnp.exp(t - m) for t in tiles]
            pm = jnp.exp(sm - m)
            den = (jnp.sum(ps[0] + ps[1] + ps[2] + ps[3], axis=-1, keepdims=True)
                   + jnp.sum(pm, axis=-1, keepdims=True))
            p = jnp.concatenate([t.astype(BF16) for t in ps], axis=1)
            probs.append((qoff, koff, p, pm.astype(BF16), den))
        for qoff, koff, p, pm, den in probs:
            vw = v_ref[pl.ds(koff, NA_WIN_H * w), :]
            o = _dot(p, vw) + _dot(pm, vm)
            o = o * (1.0 / den)
            o = jnp.where(head0, o[:w], o[w:])
            gate = gate_ref[pl.ds(qoff, w), :].astype(F32)
            o_ref[pl.ds(qoff, w), :] = (o * _silu(gate)).astype(o_ref.dtype)
        return carry

    lax.fori_loop(0, rows_per_step // NA_ROW_GROUP, row_group, 0)


def _na_call(zqk, zmain, zqk_m, zmain_m, rpb_pad, meta_bias, *, batch, seq, rows_per_step):
    n_rows = seq // GRID_W
    assert n_rows >= NA_WIN_H and n_rows % rows_per_step == 0 and rows_per_step % NA_ROW_GROUP == 0
    nsteps = n_rows // rows_per_step
    hp = NA_HEADS // 2
    lanes = 2 * NA_DH
    tq = rows_per_step * GRID_W
    n_idx = 2 * NA_WIN_H - 1
    in_specs = [
        pl.BlockSpec((tq, lanes), lambda b, p, i: (b * nsteps + i, p)),
        pl.BlockSpec((seq, lanes), lambda b, p, i: (b, NA_WIDTH // lanes + p)),
        pl.BlockSpec((seq, lanes), lambda b, p, i: (b, COL_NV // lanes + p)),
        pl.BlockSpec((tq, lanes), lambda b, p, i: (b * nsteps + i, COL_NG // lanes + p)),
        pl.BlockSpec((N_META, lanes), lambda b, p, i: (0, NA_WIDTH // lanes + p)),
        pl.BlockSpec((N_META, lanes), lambda b, p, i: (0, COL_NV // lanes + p)),
        pl.BlockSpec((2, n_idx, lanes), lambda b, p, i: (p, 0, 0)),
        pl.BlockSpec((2, 1, N_META), lambda b, p, i: (p, 0, 0)),
    ]
    return pl.pallas_call(
        functools.partial(_na_kernel, rows_per_step=rows_per_step, n_rows=n_rows),
        name="na_attn",
        grid=(batch, hp, nsteps),
        in_specs=in_specs,
        out_specs=pl.BlockSpec((tq, lanes), lambda b, p, i: (b * nsteps + i, p)),
        out_shape=jax.ShapeDtypeStruct((batch * seq, NA_WIDTH), BF16),
        scratch_shapes=[pltpu.VMEM((n_idx - 1, lanes, lanes), F32)],
        compiler_params=pltpu.CompilerParams(
            dimension_semantics=("parallel", "parallel", "arbitrary"), vmem_limit_bytes=VMEM_LIMIT),
    )(zqk, zqk, zmain, zmain, zqk_m, zmain_m, rpb_pad, meta_bias)


def _out_kernel(x_ref, of_ref, ob_ref, gg_ref, na_ref, gn_ref, w_ref, o_ref):
    s = of_ref[...].astype(F32) + ob_ref[...].astype(F32)
    gate = gg_ref[...].astype(F32)
    gn = gn_ref[...]
    heads = []
    for h in range(GLA_HEADS):
        cols = slice(h * GLA_DV, (h + 1) * GLA_DV)
        sh = s[:, cols]
        ms = jnp.mean(sh * sh, axis=-1, keepdims=True)
        y = sh * lax.rsqrt(ms + RMS_EPS) * gn
        heads.append((y * _silu(gate[:, cols])).astype(BF16))
    o_gla = jnp.concatenate(heads, axis=1)
    acc = _dot(o_gla, w_ref[:GLA_WIDTH, :]) + _dot(na_ref[...], w_ref[GLA_WIDTH:, :])
    o_ref[...] = x_ref[...] + acc


def _out_call(x2d, o_f, o_b, zmain, o_na, gn, w_out, *, tm):
    m, d = x2d.shape
    assert m % tm == 0
    return pl.pallas_call(
        _out_kernel,
        name="out_proj",
        grid=(m // tm,),
        in_specs=[
            pl.BlockSpec((tm, d), lambda i: (i, 0)),
            pl.BlockSpec((tm, GLA_WIDTH), lambda i: (i, 0)),
            pl.BlockSpec((tm, GLA_WIDTH), lambda i: (i, 0)),
            pl.BlockSpec((tm, GLA_WIDTH), lambda i: (i, COL_GG // GLA_WIDTH)),
            pl.BlockSpec((tm, NA_WIDTH), lambda i: (i, 0)),
            pl.BlockSpec((1, GLA_DV), lambda i: (0, 0)),
            pl.BlockSpec((GLA_WIDTH + NA_WIDTH, d), lambda i: (0, 0)),
        ],
        out_specs=pl.BlockSpec((tm, d), lambda i: (i, 0)),
        out_shape=jax.ShapeDtypeStruct((m, d), F32),
        compiler_params=pltpu.CompilerParams(
            dimension_semantics=("parallel",), vmem_limit_bytes=VMEM_LIMIT),
    )(x2d, o_f, o_b, zmain, o_na, gn, w_out)


def _pick(n, cands):
    for c in cands:
        if n % c == 0:
            return c
    raise ValueError(f"no tile for {n}")


def kernel(x, meta_tokens, norm_g, w_in, w_decay_fwd, b_decay_fwd, w_decay_bwd, b_decay_bwd,
           gla_out_norm_g, q_norm_g, k_norm_g, rpb, meta_bias, w_out):
    batch, seq, d = x.shape
    assert d == D_MODEL and seq % TILE == 0 and seq % GRID_W == 0
    depth = norm_g.shape[0]
    assert depth == 1
    l = 0
    x2d = x.reshape(batch * seq, d)
    g = norm_g[l].reshape(1, d)

    assert w_in.shape[1:] == (d, D_IN_PROJ)
    wt = jnp.swapaxes(w_in[l], 0, 1).astype(BF16)
    gain_qk = jnp.concatenate([jnp.tile(q_norm_g[l], NA_HEADS) * (NA_DH ** -0.5),
                               jnp.tile(k_norm_g[l], NA_HEADS)]).reshape(1, QK_W)

    zr = jnp.zeros((GLA_RANK, GLA_KEY_WIDTH), F32)
    ztail = jnp.zeros((LANES - 2 * GLA_RANK, GLA_KEY_WIDTH), F32)
    wd = jnp.stack([jnp.concatenate([w_decay_fwd[l], zr, ztail], axis=0),
                    jnp.concatenate([zr, w_decay_bwd[l], ztail], axis=0)]).astype(BF16)
    bias = jnp.stack([b_decay_fwd[l], b_decay_bwd[l]]).reshape(2, 1, GLA_KEY_WIDTH)

    zmain, zqk = _proj_call(x2d, g, wt, gain_qk, tm=_pick(batch * seq, (512, 256, 128)), name="proj_tokens")
    zmain_m, zqk_m = _proj_call(meta_tokens, g, wt, gain_qk, tm=N_META, name="proj_meta")
    zmain_m_pad = jnp.pad(zmain_m, ((0, TILE - N_META), (0, 0)))

    o_f, o_b = _gla_call(zmain, zmain_m_pad, wd, bias, batch=batch, seq=seq,
                         tokens_per_step=_pick(seq, (1024, 512, 256, 128)))

    rpb_pad = jnp.pad(rpb[l], ((0, 0), (0, 0), (0, 2 * NA_DH - rpb.shape[-1])))
    mb = meta_bias[l].reshape(NA_HEADS, 1, N_META)
    o_na = _na_call(zqk, zmain, zqk_m, zmain_m, rpb_pad, mb, batch=batch, seq=seq,
                    rows_per_step=_pick(seq // GRID_W, (32, 16, 8)))

    out = _out_call(x2d, o_f, o_b, zmain, o_na, gla_out_norm_g[l].reshape(1, GLA_DV),
                    w_out[l].astype(BF16), tm=_pick(batch * seq, (512, 256, 128)))
    return out.reshape(batch, seq, d)
```

```python
import functools

import jax
import jax.numpy as jnp
import numpy as np
from jax import lax
from jax.experimental import pallas as pl
from jax.experimental.pallas import tpu as pltpu

F32 = jnp.float32
BF16 = jnp.bfloat16

D_MODEL = 1024
N_META = 16
GRID_W = 64
GLA_HEADS = 4
GLA_DK = 128
GLA_DV = 256
GLA_KEY_WIDTH = GLA_HEADS * GLA_DK
GLA_WIDTH = GLA_HEADS * GLA_DV
GLA_RANK = 16
GLA_TAU = 16.0
GLA_CHUNK = 64
NA_HEADS = 16
NA_DH = 64
NA_WIDTH = NA_HEADS * NA_DH
NA_WIN_H = 8
NA_WIN_W = 16
RMS_EPS = 1e-6
NEG_BIG = -1e30
LANES = 128

COL_GV, COL_GG, COL_NG, COL_GQ, COL_GK, COL_R = 0, 1024, 2048, 3072, 3584, 4096
MAIN_W = COL_R + LANES
NA_CBW = 16
NA_CB = GRID_W // NA_CBW
IN_GQ, IN_GK, IN_GV, IN_R, IN_GG, IN_NQ, IN_NK, IN_NV, IN_NG = 0, 512, 1024, 2048, 2080, 3104, 4128, 5152, 6176
D_IN_PROJ = 7200
PROJ_TN = 512

TILE = 2 * GLA_CHUNK
NA_ROW_GROUP = 8
VMEM_LIMIT = 58 * 1024 * 1024


def _split_bf16(x):
    hi = x.astype(BF16)
    lo = (x - hi.astype(F32)).astype(BF16)
    return hi, lo


def _dot(a, b):
    return jnp.dot(a, b, preferred_element_type=F32)


def _dot_nt(a, b):
    return lax.dot_general(a, b, (((1,), (1,)), ((), ())), preferred_element_type=F32)


def _dot_tn(a, b):
    return lax.dot_general(a, b, (((0,), (0,)), ((), ())), preferred_element_type=F32)


def _log_sigmoid(x):
    return jnp.minimum(x, 0.0) - jnp.log1p(jnp.exp(-jnp.abs(x)))


def _silu(x):
    return x * (1.0 / (1.0 + jnp.exp(-x)))


MAIN_SEGMENTS = ((COL_GV, IN_GV, 1024), (COL_GG, IN_GG, 1024), (COL_NG, IN_NG, 1024),
                 (COL_GQ, IN_GQ, 512), (COL_GK, IN_GK, 512),
                 (COL_R, IN_R, LANES))


def _store_rows(o_ref, c0, y, permute):
    n = y.shape[1]
    if not permute:
        o_ref[:, c0:c0 + n] = y
        return
    for rl in range(y.shape[0] // GRID_W):
        for cb in range(NA_CB):
            src = rl * GRID_W + cb * NA_CBW
            o_ref[cb, rl * NA_CBW:(rl + 1) * NA_CBW, c0:c0 + n] = y[src:src + NA_CBW]


def _proj_kernel(x_ref, g_ref, wt_ref, gain_ref, kgain_ref, bd_ref, zm_ref, nq_ref, nk_ref, nv_ref, *, permute):
    x = x_ref[...]
    ms = jnp.mean(x * x, axis=-1, keepdims=True)
    u = (x * lax.rsqrt(ms + RMS_EPS) * g_ref[...]).astype(BF16)
    for dst, src, width in MAIN_SEGMENTS:
        for off in range(0, width, PROJ_TN):
            n = min(PROJ_TN, width - off)
            acc = _dot_nt(u, wt_ref[src + off:src + off + n, :])
            if dst == COL_GQ:
                acc = acc * (GLA_DK ** -0.5)
            zm_ref[:, dst + off:dst + off + n] = acc.astype(zm_ref.dtype)
    for off in range(0, NA_WIDTH, PROJ_TN):
        acc = _dot_nt(u, wt_ref[IN_NV + off:IN_NV + off + PROJ_TN, :])
        _store_rows(nv_ref, off, acc.astype(nv_ref.dtype), permute)
    half = PROJ_TN // 2

    def head_norm_rows(acc, goff):
        sq = (acc * acc).astype(BF16)
        ms = jnp.concatenate([_dot(sq[:, :half], bd_ref[...]), _dot(sq[:, half:], bd_ref[...])], axis=1)
        return (acc * lax.rsqrt(ms + RMS_EPS) * gain_ref[:, goff:goff + PROJ_TN]).astype(BF16)

    for off in range(0, NA_WIDTH, PROJ_TN):
        acc = _dot_nt(u, wt_ref[IN_NQ + off:IN_NQ + off + PROJ_TN, :])
        nq_ref[:, off:off + PROJ_TN] = head_norm_rows(acc, off)
    if not permute:
        for off in range(0, NA_WIDTH, PROJ_TN):
            acc = _dot_nt(u, wt_ref[IN_NK + off:IN_NK + off + PROJ_TN, :])
            nk_ref[:, off:off + PROJ_TN] = head_norm_rows(acc, NA_WIDTH + off)
        return
    tm = u.shape[0]
    u_cb = jnp.concatenate([u[rl * GRID_W + cb * NA_CBW:rl * GRID_W + (cb + 1) * NA_CBW]
                            for cb in range(NA_CB) for rl in range(tm // GRID_W)], axis=0)
    for off in range(0, NA_WIDTH, PROJ_TN):
        acc_t = _dot_nt(wt_ref[IN_NK + off:IN_NK + off + PROJ_TN, :], u_cb)
        for h in range(PROJ_TN // NA_DH):
            rows = slice(off + h * NA_DH, off + (h + 1) * NA_DH)
            blk = acc_t[h * NA_DH:(h + 1) * NA_DH]
            ms = jnp.sum(blk * blk, axis=0, keepdims=True) * (1.0 / NA_DH)
            y = (blk * lax.rsqrt(ms + RMS_EPS) * kgain_ref[rows, :]).astype(BF16)
            for cb in range(NA_CB):
                nk_ref[cb, rows, :] = y[:, cb * LANES:(cb + 1) * LANES]


def _proj_call(x2d, g, wt, gain_qk, kgain_col, *, tm, permute, name):
    m, d = x2d.shape
    assert m % tm == 0 and wt.shape == (D_IN_PROJ, d)
    half = PROJ_TN // 2
    grp = jnp.arange(half) // NA_DH
    bd = jnp.where(grp[:, None] == grp[None, :], 1.0 / NA_DH, 0.0).astype(BF16)
    resident = lambda shape: pl.BlockSpec(shape, lambda i: (0, 0), pipeline_mode=pl.Buffered(1))
    plain = pl.BlockSpec((tm, NA_WIDTH), lambda i: (i, 0))
    plain_shape = jax.ShapeDtypeStruct((m, NA_WIDTH), BF16)
    if permute:
        assert tm == NA_WIN_H * GRID_W and NA_CB * LANES == tm
        v_spec = pl.BlockSpec((NA_CB, tm // NA_CB, NA_WIDTH), lambda i: (0, i, 0))
        v_shape = jax.ShapeDtypeStruct((NA_CB, m // NA_CB, NA_WIDTH), BF16)
        k_spec = pl.BlockSpec((NA_CB, NA_WIDTH, LANES), lambda i: (i, 0, 0))
        k_shape = jax.ShapeDtypeStruct((m // LANES, NA_WIDTH, LANES), BF16)
    else:
        v_spec, v_shape, k_spec, k_shape = plain, plain_shape, plain, plain_shape
    return pl.pallas_call(
        functools.partial(_proj_kernel, permute=permute),
        name=name,
        grid=(m // tm,),
        in_specs=[
            pl.BlockSpec((tm, d), lambda i: (i, 0)),
            resident((1, d)),
            resident((D_IN_PROJ, d)),
            resident((1, 2 * NA_WIDTH)),
            resident((NA_WIDTH, 1)),
            resident((half, half)),
        ],
        out_specs=[pl.BlockSpec((tm, MAIN_W), lambda i: (i, 0)), plain, k_spec, v_spec],
        out_shape=[jax.ShapeDtypeStruct((m, MAIN_W), BF16), plain_shape, k_shape, v_shape],
        compiler_params=pltpu.CompilerParams(
            dimension_semantics=("parallel",), vmem_limit_bytes=VMEM_LIMIT),
    )(x2d, g, wt, gain_qk, kgain_col, bd)


def _gla_kernel(qf_ref, kf_ref, vf_ref, rf_ref, qb_ref, kb_ref, vb_ref, rb_ref,
                km_ref, vm_ref, rm_ref,
                wdf_ref, bf_ref, wdb_ref, bb_ref, tgf_ref, tgb_ref, tgm_ref,
                of_ref, ob_ref, sf_ref, sb_ref, *, n_tiles):
    row = lax.broadcasted_iota(jnp.int32, (TILE, TILE), 0)
    col = lax.broadcasted_iota(jnp.int32, (TILE, TILE), 1)
    same = (row // GLA_CHUNK) == (col // GLA_CHUNK)
    masks = (same & (row >= col), same & (row <= col))

    def decay(r, wd_ref, b_ref):
        return _log_sigmoid(_dot(r, wd_ref[...]) + b_ref[...]) * (1.0 / GLA_TAU)

    def cums(lg, tg_ref):
        hi, lo = _split_bf16(lg)
        return _dot(tg_ref[...], jnp.concatenate([hi, lo], axis=0))

    @pl.when(pl.program_id(2) == 0)
    def _():
        rowm = lax.broadcasted_iota(jnp.int32, (TILE, GLA_DK), 0)
        lg = jnp.where(rowm < N_META, decay(rm_ref[...], wdf_ref, bf_ref), 0.0)
        d_m = cums(lg, tgm_ref)[TILE:]
        k_st = (km_ref[...].astype(F32) * jnp.exp(d_m)).astype(BF16)
        sf_ref[...] = _dot_tn(k_st, vm_ref[...])
        sb_ref[...] = jnp.zeros_like(sb_ref)

    items = []
    for t in range(n_tiles):
        items.append((0, t))
        items.append((1, n_tiles - 1 - t))
    refs = ((qf_ref, kf_ref, vf_ref, rf_ref, wdf_ref, bf_ref, tgf_ref, of_ref),
            (qb_ref, kb_ref, vb_ref, rb_ref, wdb_ref, bb_ref, tgb_ref, ob_ref))

    def rows_of(t):
        return slice(t * TILE, (t + 1) * TILE)

    lgs = [decay(refs[d][3][rows_of(t), :], refs[d][4], refs[d][5]) for d, t in items]
    gds = [cums(lg, refs[d][6]) for lg, (d, t) in zip(lgs, items)]
    prepped = []
    for gd, (d, t) in zip(gds, items):
        g, dd = gd[:TILE], gd[TILE:]
        q = refs[d][0][rows_of(t), :].astype(F32)
        k = refs[d][1][rows_of(t), :].astype(F32)
        q_in = (q * jnp.exp(g)).astype(BF16)
        k_in = (k * jnp.exp(-g)).astype(BF16)
        k_st = (k * jnp.exp(dd)).astype(BF16)
        a_t = jnp.exp(jnp.transpose(g + dd))
        prepped.append((q_in, k_in, k_st, a_t))
    amats = [jnp.where(masks[d], _dot_nt(q_in, k_in), 0.0).astype(BF16)
             for (q_in, k_in, _, _), (d, t) in zip(prepped, items)]
    us = []
    for (_, _, k_st, _), (d, t) in zip(prepped, items):
        v = refs[d][2][rows_of(t), :]
        us.append([_dot_tn(k_st[h * GLA_CHUNK:(h + 1) * GLA_CHUNK], v[h * GLA_CHUNK:(h + 1) * GLA_CHUNK])
                   for h in range(2)])
    intras = [_dot(a, refs[d][2][rows_of(t), :]) for a, (d, t) in zip(amats, items)]

    state = [sf_ref[...], sb_ref[...]]
    for (q_in, _, _, a_t), u, o_intra, (d, t) in zip(prepped, us, intras, items):
        s = state[d]
        outs = [None, None]
        for h in ((0, 1) if d == 0 else (1, 0)):
            rows = slice(h * GLA_CHUNK, (h + 1) * GLA_CHUNK)
            outs[h] = o_intra[rows] + _dot(q_in[rows], s.astype(BF16))
            s = a_t[:, h * GLA_CHUNK:h * GLA_CHUNK + 1] * s + u[h]
        state[d] = s
        refs[d][7][rows_of(t), :] = jnp.concatenate(outs, axis=0).astype(refs[d][7].dtype)
    sf_ref[...] = state[0]
    sb_ref[...] = state[1]


def _gla_constants():
    i = np.arange(TILE)
    same = (i[:, None] // GLA_CHUNK) == (i[None, :] // GLA_CHUNK)
    g_f, d_f = same & (i[None, :] <= i[:, None]), same & (i[None, :] > i[:, None])
    g_b, d_b = same & (i[None, :] >= i[:, None]), same & (i[None, :] < i[:, None])
    g_m, d_m = i[None, :] <= i[:, None], i[None, :] > i[:, None]

    def stack(g, d):
        m = np.concatenate([g, d], axis=0)
        return jnp.asarray(np.concatenate([m, m], axis=1), BF16)

    return stack(g_f, d_f), stack(g_b, d_b), stack(g_m, d_m)


def _gla_call(zmain, zmain_m, wd, bias, *, batch, seq, tokens_per_step):
    t_blk = tokens_per_step
    nb = seq // t_blk
    n_tiles = t_blk // TILE
    tg_f, tg_b, tg_m = _gla_constants()

    def tok(fwd):
        if fwd:
            return lambda b, hh, i: b * nb + i
        return lambda b, hh, i: b * nb + (nb - 1 - i)

    def dir_specs(fwd):
        t = tok(fwd)
        return [
            pl.BlockSpec((t_blk, GLA_DK), lambda b, hh, i: (t(b, hh, i), COL_GQ // GLA_DK + hh)),
            pl.BlockSpec((t_blk, GLA_DK), lambda b, hh, i: (t(b, hh, i), COL_GK // GLA_DK + hh)),
            pl.BlockSpec((t_blk, GLA_DV), lambda b, hh, i: (t(b, hh, i), COL_GV // GLA_DV + hh)),
            pl.BlockSpec((t_blk, LANES), lambda b, hh, i: (t(b, hh, i), COL_R // LANES)),
        ]

    def w_specs(d):
        return [pl.BlockSpec((None, LANES, GLA_DK), lambda b, hh, i: (d, 0, hh)),
                pl.BlockSpec((None, 1, GLA_DK), lambda b, hh, i: (d, 0, hh))]

    const = lambda shape: pl.BlockSpec(shape, lambda b, hh, i: (0,) * len(shape))
    in_specs = (
        dir_specs(True) + dir_specs(False)
        + [pl.BlockSpec((TILE, GLA_DK), lambda b, hh, i: (0, COL_GK // GLA_DK + hh)),
           pl.BlockSpec((TILE, GLA_DV), lambda b, hh, i: (0, COL_GV // GLA_DV + hh)),
           pl.BlockSpec((TILE, LANES), lambda b, hh, i: (0, COL_R // LANES))]
        + w_specs(0) + w_specs(1)
        + [const((2 * TILE, 2 * TILE))] * 3
    )
    args = [zmain] * 8 + [zmain_m] * 3 + [wd, bias, wd, bias, tg_f, tg_b, tg_m]
    out_spec_f = pl.BlockSpec((t_blk, GLA_DV), lambda b, hh, i: (b * nb + i, hh))
    out_spec_b = pl.BlockSpec((t_blk, GLA_DV), lambda b, hh, i: (b * nb + (nb - 1 - i), hh))
    o_shape = jax.ShapeDtypeStruct((batch * seq, GLA_WIDTH), BF16)
    return pl.pallas_call(
        functools.partial(_gla_kernel, n_tiles=n_tiles),
        name="gla_scan",
        grid=(batch, GLA_HEADS, nb),
        in_specs=in_specs,
        out_specs=[out_spec_f, out_spec_b],
        out_shape=[o_shape, o_shape],
        scratch_shapes=[pltpu.VMEM((GLA_DK, GLA_DV), F32), pltpu.VMEM((GLA_DK, GLA_DV), F32)],
        compiler_params=pltpu.CompilerParams(
            dimension_semantics=("parallel", "parallel", "arbitrary"), vmem_limit_bytes=VMEM_LIMIT),
    )(*args)


NA_SEG_COLS = ((0, 24), (24, 40), (40, 64))
NA_SEG_ROW0 = (0, 48, 80)
NA_STACK = 2 * GRID_W


NA_BASE_MID = NA_WIN_H // 2 - 1
NA_VARIANTS = 2 * NA_WIN_H - 1


def _na_variant(base, a):
    return jnp.where(base == NA_BASE_MID, a, jnp.where(base < NA_BASE_MID, NA_WIN_H + base, NA_WIN_H - 1 + base))


def _na_build_bias(rpb_ref, tab_ref):
    def build(v, carry):
        a = jnp.where(v < NA_WIN_H, v, 0)
        base = jnp.where(v < NA_WIN_H, NA_BASE_MID,
                         jnp.where(v < NA_WIN_H + NA_BASE_MID, v - NA_WIN_H, v - (NA_WIN_H - 1)))
        for hh in range(2):
            bias_rows = [rpb_ref[hh, pl.ds(base + ((t - a + NA_WIN_H) & (NA_WIN_H - 1)), 1), :]
                         for t in range(NA_WIN_H)]
            for s, (c0, c1) in enumerate(NA_SEG_COLS):
                n = c1 - c0
                vrows = [jnp.broadcast_to(br, (n, LANES)) for br in bias_rows]
                row0 = NA_SEG_ROW0[s] + hh * n
                rho = lax.broadcasted_iota(jnp.int32, (n, LANES), 0)
                lane = lax.broadcasted_iota(jnp.int32, (n, LANES), 1)
                cs = jnp.clip(c0 + rho - NA_WIN_W // 2, 0, GRID_W - NA_WIN_W)
                for half in range(2):
                    cb = s + half
                    cp = NA_CBW * cb + lane % NA_CBW
                    valid = (cp >= cs) & (cp < cs + NA_WIN_W)
                    acc = None
                    for i in range(NA_WIN_H):
                        shift = (NA_CBW * i - NA_CBW * cb + c0 - (NA_WIN_W - 1)) % LANES
                        rolled = pltpu.roll(vrows[i], shift, 1, stride=1, stride_axis=0)
                        acc = rolled if acc is None else jnp.where(lane // NA_CBW == i, rolled, acc)
                    tab_ref[v, row0:row0 + n, half * LANES:(half + 1) * LANES] = jnp.where(valid, acc, NEG_BIG)
        return carry

    lax.fori_loop(0, NA_VARIANTS, build, 0)


def _na_kernel(q_ref, kt_ref, v_ref, gate_ref, km_ref, vm_ref, rpb_ref, mb_ref, o_ref, tab_ref, sc_ref, sm_ref,
               *, rows_per_step, n_rows):
    w = GRID_W

    @pl.when(pl.program_id(2) == 0)
    def _():
        _na_build_bias(rpb_ref, tab_ref)

    head0_q = lax.broadcasted_iota(jnp.int32, (w, LANES), 1) < NA_DH
    srow = lax.broadcasted_iota(jnp.int32, (NA_STACK, N_META), 0)
    is_h1 = jnp.zeros((NA_STACK, N_META), jnp.bool_)
    for s, (c0, c1) in enumerate(NA_SEG_COLS):
        lo = NA_SEG_ROW0[s] + (c1 - c0)
        is_h1 = is_h1 | ((srow >= lo) & (srow < lo + (c1 - c0)))
    mb = jnp.where(is_h1, mb_ref[1], mb_ref[0])
    km = km_ref[...]
    vm = vm_ref[...]
    seg_rows = [slice(NA_SEG_ROW0[s], NA_SEG_ROW0[s] + 2 * (c1 - c0)) for s, (c0, c1) in enumerate(NA_SEG_COLS)]
    blk_rows = [slice(seg_rows[max(cb - 1, 0)].start, seg_rows[min(cb, len(seg_rows) - 1)].stop)
                for cb in range(NA_CB)]

    n_tiles = n_rows // NA_WIN_H
    key_lane = lax.broadcasted_iota(jnp.int32, (LANES, LANES), 1)

    def window(rl):
        r = pl.program_id(2) * rows_per_step + rl
        rs = jnp.clip(r - NA_WIN_H // 2, 0, n_rows - NA_WIN_H)
        a = rs & (NA_WIN_H - 1)
        return rs // NA_WIN_H, a, _na_variant(rs - r + (NA_WIN_H - 1), a)

    def scores_store(slot, j, rl):
        m, a, var = window(rl)
        q = q_ref[pl.ds(pl.multiple_of(rl * w, w), w), :].astype(F32)
        q0 = jnp.where(head0_q, q, 0.0)
        q1 = q - q0
        qs = jnp.concatenate([piece[c0:c1] for c0, c1 in NA_SEG_COLS for piece in (q0, q1)],
                             axis=0).astype(BF16)
        from_m = key_lane >= a * NA_CBW
        m1 = jnp.minimum(m + 1, n_tiles - 1)
        blocks = [_dot(qs[blk_rows[cb]],
                       jnp.where(from_m, kt_ref[NA_CB * m + cb], kt_ref[NA_CB * m1 + cb]))
                  for cb in range(NA_CB)]
        for s in range(len(NA_SEG_COLS)):
            n2 = seg_rows[s].stop - seg_rows[s].start
            for half in range(2):
                cb = s + half
                a0 = seg_rows[s].start - blk_rows[cb].start
                lanes = slice(half * LANES, (half + 1) * LANES)
                sc_ref[slot, j, seg_rows[s], lanes] = blocks[cb][a0:a0 + n2] + tab_ref[var, seg_rows[s], lanes]
        sm_ref[slot, j] = _dot_nt(qs, km) + mb

    def value_block(cb, m, a):
        groups = []
        for t in range(NA_WIN_H):
            row = NA_WIN_H * m + t + jnp.where(t < a, NA_WIN_H, 0)
            groups.append(v_ref[cb, pl.ds(pl.multiple_of(row * NA_CBW, NA_CBW), NA_CBW), :])
        return jnp.concatenate(groups, axis=0)

    def softmax_values(slot, j, rl):
        m_tile, a, _ = window(rl)
        sc = sc_ref[slot, j]
        sm = sm_ref[slot, j]
        m = jnp.maximum(sc[:, :LANES], sc[:, LANES:])
        m = jnp.maximum(jnp.max(m, axis=-1, keepdims=True), jnp.max(sm, axis=-1, keepdims=True))
        p = jnp.exp(sc - m)
        pm = jnp.exp(sm - m)
        den = (jnp.sum(p[:, :LANES] + p[:, LANES:], axis=-1, keepdims=True)
               + jnp.sum(pm, axis=-1, keepdims=True))
        p = p.astype(BF16)
        p_blk = [p[seg_rows[0], :LANES],
                 jnp.concatenate([p[seg_rows[0], LANES:], p[seg_rows[1], :LANES]], axis=0),
                 jnp.concatenate([p[seg_rows[1], LANES:], p[seg_rows[2], :LANES]], axis=0),
                 p[seg_rows[2], LANES:]]
        parts = [_dot(p_blk[cb], value_block(cb, m_tile, a)) for cb in range(NA_CB)]
        o_meta = _dot(pm.astype(BF16), vm)
        inv = 1.0 / den
        outs = []
        for s, (c0, c1) in enumerate(NA_SEG_COLS):
            a0 = seg_rows[s].start - blk_rows[s].start
            b0 = seg_rows[s].start - blk_rows[s + 1].start
            n2 = seg_rows[s].stop - seg_rows[s].start
            o = (parts[s][a0:a0 + n2] + parts[s + 1][b0:b0 + n2] + o_meta[seg_rows[s]]) * inv[seg_rows[s]]
            n = c1 - c0
            head0 = lax.broadcasted_iota(jnp.int32, (n, LANES), 1) < NA_DH
            outs.append(jnp.where(head0, o[:n], o[n:]))
        o = jnp.concatenate(outs, axis=0)
        qoff = pl.multiple_of(rl * w, w)
        gate = gate_ref[pl.ds(qoff, w), :].astype(F32)
        o_ref[pl.ds(qoff, w), :] = (o * _silu(gate)).astype(o_ref.dtype)

    n_groups = rows_per_step // NA_ROW_GROUP
    for j in range(NA_ROW_GROUP):
        scores_store(0, j, j)

    def trip(g, carry):
        slot = g % 2
        for j in range(NA_ROW_GROUP):
            scores_store(slot, j, g * NA_ROW_GROUP + j)
        for j in range(NA_ROW_GROUP):
            softmax_values(1 - slot, j, (g - 1) * NA_ROW_GROUP + j)
        return carry

    lax.fori_loop(1, n_groups, trip, 0)
    for j in range(NA_ROW_GROUP):
        softmax_values((n_groups - 1) % 2, j, (n_groups - 1) * NA_ROW_GROUP + j)


def _na_call(nq, nk_t, nv_cb, zmain, nk_m, nv_m, rpb_pad, meta_bias, *, batch, seq, rows_per_step):
    n_rows = seq // GRID_W
    assert n_rows % NA_WIN_H == 0 and n_rows % rows_per_step == 0 and rows_per_step % NA_ROW_GROUP == 0
    nsteps = n_rows // rows_per_step
    hp = NA_HEADS // 2
    tq = rows_per_step * GRID_W
    n_idx = 2 * NA_WIN_H - 1
    in_specs = [
        pl.BlockSpec((tq, LANES), lambda b, p, i: (b * nsteps + i, p)),
        pl.BlockSpec((seq // LANES, LANES, LANES), lambda b, p, i: (b, p, 0)),
        pl.BlockSpec((NA_CB, seq // NA_CB, LANES), lambda b, p, i: (0, b, p)),
        pl.BlockSpec((tq, LANES), lambda b, p, i: (b * nsteps + i, COL_NG // LANES + p)),
        pl.BlockSpec((N_META, LANES), lambda b, p, i: (0, p)),
        pl.BlockSpec((N_META, LANES), lambda b, p, i: (0, p)),
        pl.BlockSpec((2, n_idx, LANES), lambda b, p, i: (p, 0, 0)),
        pl.BlockSpec((2, 1, N_META), lambda b, p, i: (p, 0, 0)),
    ]
    return pl.pallas_call(
        functools.partial(_na_kernel, rows_per_step=rows_per_step, n_rows=n_rows),
        name="na_attn",
        grid=(batch, hp, nsteps),
        in_specs=in_specs,
        out_specs=pl.BlockSpec((tq, LANES), lambda b, p, i: (b * nsteps + i, p)),
        out_shape=jax.ShapeDtypeStruct((batch * seq, NA_WIDTH), BF16),
        scratch_shapes=[pltpu.VMEM((NA_VARIANTS, NA_STACK, 2 * LANES), F32),
                        pltpu.VMEM((2, NA_ROW_GROUP, NA_STACK, 2 * LANES), F32),
                        pltpu.VMEM((2, NA_ROW_GROUP, NA_STACK, N_META), F32)],
        compiler_params=pltpu.CompilerParams(
            dimension_semantics=("parallel", "parallel", "arbitrary"), vmem_limit_bytes=VMEM_LIMIT),
    )(nq, nk_t, nv_cb, zmain, nk_m, nv_m, rpb_pad, meta_bias)


def _out_kernel(x_ref, of_ref, ob_ref, gg_ref, na_ref, gn_ref, w_ref, o_ref):
    s = of_ref[...].astype(F32) + ob_ref[...].astype(F32)
    gate = gg_ref[...].astype(F32)
    gn = gn_ref[...]
    heads = []
    for h in range(GLA_HEADS):
        cols = slice(h * GLA_DV, (h + 1) * GLA_DV)
        sh = s[:, cols]
        ms = jnp.mean(sh * sh, axis=-1, keepdims=True)
        y = sh * lax.rsqrt(ms + RMS_EPS) * gn
        heads.append((y * _silu(gate[:, cols])).astype(BF16))
    o_gla = jnp.concatenate(heads, axis=1)
    acc = _dot(o_gla, w_ref[:GLA_WIDTH, :]) + _dot(na_ref[...], w_ref[GLA_WIDTH:, :])
    o_ref[...] = x_ref[...] + acc


def _out_call(x2d, o_f, o_b, zmain, o_na, gn, w_out, *, tm):
    m, d = x2d.shape
    assert m % tm == 0
    return pl.pallas_call(
        _out_kernel,
        name="out_proj",
        grid=(m // tm,),
        in_specs=[
            pl.BlockSpec((tm, d), lambda i: (i, 0)),
            pl.BlockSpec((tm, GLA_WIDTH), lambda i: (i, 0)),
            pl.BlockSpec((tm, GLA_WIDTH), lambda i: (i, 0)),
            pl.BlockSpec((tm, GLA_WIDTH), lambda i: (i, COL_GG // GLA_WIDTH)),
            pl.BlockSpec((tm, NA_WIDTH), lambda i: (i, 0)),
            pl.BlockSpec((1, GLA_DV), lambda i: (0, 0)),
            pl.BlockSpec((GLA_WIDTH + NA_WIDTH, d), lambda i: (0, 0)),
        ],
        out_specs=pl.BlockSpec((tm, d), lambda i: (i, 0)),
        out_shape=jax.ShapeDtypeStruct((m, d), F32),
        compiler_params=pltpu.CompilerParams(
            dimension_semantics=("parallel",), vmem_limit_bytes=VMEM_LIMIT),
    )(x2d, o_f, o_b, zmain, o_na, gn, w_out)


def _pick(n, cands):
    for c in cands:
        if n % c == 0:
            return c
    raise ValueError(f"no tile for {n}")


def kernel(x, meta_tokens, norm_g, w_in, w_decay_fwd, b_decay_fwd, w_decay_bwd, b_decay_bwd,
           gla_out_norm_g, q_norm_g, k_norm_g, rpb, meta_bias, w_out):
    batch, seq, d = x.shape
    assert d == D_MODEL and seq % TILE == 0 and seq % GRID_W == 0
    depth = norm_g.shape[0]
    assert depth == 1
    l = 0
    x2d = x.reshape(batch * seq, d)
    g = norm_g[l].reshape(1, d)

    assert w_in.shape[1:] == (d, D_IN_PROJ)
    wt = jnp.swapaxes(w_in[l], 0, 1).astype(BF16)
    gain_qk = jnp.concatenate([jnp.tile(q_norm_g[l], NA_HEADS) * (NA_DH ** -0.5),
                               jnp.tile(k_norm_g[l], NA_HEADS)]).reshape(1, 2 * NA_WIDTH)

    zr = jnp.zeros((GLA_RANK, GLA_KEY_WIDTH), F32)
    ztail = jnp.zeros((LANES - 2 * GLA_RANK, GLA_KEY_WIDTH), F32)
    wd = jnp.stack([jnp.concatenate([w_decay_fwd[l], zr, ztail], axis=0),
                    jnp.concatenate([zr, w_decay_bwd[l], ztail], axis=0)]).astype(BF16)
    bias = jnp.stack([b_decay_fwd[l], b_decay_bwd[l]]).reshape(2, 1, GLA_KEY_WIDTH)

    assert seq % (NA_WIN_H * GRID_W) == 0
    kgain_col = jnp.tile(k_norm_g[l], NA_HEADS).reshape(NA_WIDTH, 1)
    zmain, nq, nk_t, nv_cb = _proj_call(x2d, g, wt, gain_qk, kgain_col, tm=NA_WIN_H * GRID_W,
                                        permute=True, name="proj_tokens")
    zmain_m, _, nk_m, nv_m = _proj_call(meta_tokens, g, wt, gain_qk, kgain_col, tm=N_META,
                                        permute=False, name="proj_meta")
    zmain_m_pad = jnp.pad(zmain_m, ((0, TILE - N_META), (0, 0)))

    o_f, o_b = _gla_call(zmain, zmain_m_pad, wd, bias, batch=batch, seq=seq,
                         tokens_per_step=_pick(seq, (1024, 512, 256, 128)))

    rpb_pad = jnp.pad(rpb[l], ((0, 0), (0, 0), (0, 2 * NA_DH - rpb.shape[-1])))
    mb = meta_bias[l].reshape(NA_HEADS, 1, N_META)
    o_na = _na_call(nq, nk_t, nv_cb, zmain, nk_m, nv_m, rpb_pad, mb, batch=batch, seq=seq,
                    rows_per_step=_pick(seq // GRID_W, (128, 64, 32, 16)))

    out = _out_call(x2d, o_f, o_b, zmain, o_na, gla_out_norm_g[l].reshape(1, GLA_DV),
                    w_out[l].astype(BF16), tm=_pick(batch * seq, (512, 256, 128)))
    return out.reshape(batch, seq, d)
```

```python
import functools

import jax
import jax.numpy as jnp
import numpy as np
from jax import lax
from jax.experimental import pallas as pl
from jax.experimental.pallas import tpu as pltpu

F32 = jnp.float32
BF16 = jnp.bfloat16

D_MODEL = 1024
N_META = 16
GRID_W = 64
GLA_HEADS = 4
GLA_DK = 128
GLA_DV = 256
GLA_KEY_WIDTH = GLA_HEADS * GLA_DK
GLA_WIDTH = GLA_HEADS * GLA_DV
GLA_RANK = 16
GLA_TAU = 16.0
GLA_CHUNK = 64
NA_HEADS = 16
NA_DH = 64
NA_WIDTH = NA_HEADS * NA_DH
NA_WIN_H = 8
NA_WIN_W = 16
RMS_EPS = 1e-6
NEG_BIG = -1e30
LANES = 128

COL_GV, COL_GG, COL_NG, COL_GQ, COL_GK, COL_R = 0, 1024, 2048, 3072, 3584, 4096
MAIN_W = COL_R + LANES
COL_DIR0, DIR_W = 3072, 3 * 512
MAIN_TOK_W = COL_DIR0 + 2 * DIR_W
NA_CBW = 16
NA_CB = GRID_W // NA_CBW
IN_GQ, IN_GK, IN_GV, IN_R, IN_GG, IN_NQ, IN_NK, IN_NV, IN_NG = 0, 512, 1024, 2048, 2080, 3104, 4128, 5152, 6176
D_IN_PROJ = 7200
PROJ_TN = 512

TILE = 2 * GLA_CHUNK
NA_ROW_GROUP = 8
VMEM_LIMIT = 58 * 1024 * 1024


def _split_bf16(x):
    hi = x.astype(BF16)
    lo = (x - hi.astype(F32)).astype(BF16)
    return hi, lo


def _dot(a, b):
    return jnp.dot(a, b, preferred_element_type=F32)


def _dot_nt(a, b):
    return lax.dot_general(a, b, (((1,), (1,)), ((), ())), preferred_element_type=F32)


def _dot_tn(a, b):
    return lax.dot_general(a, b, (((0,), (0,)), ((), ())), preferred_element_type=F32)


def _log_sigmoid(x):
    return jnp.minimum(x, 0.0) - jnp.log1p(jnp.exp(-jnp.abs(x)))


def _silu(x):
    return x * (1.0 / (1.0 + jnp.exp(-x)))


SHARED_SEGMENTS = ((COL_GV, IN_GV, 1024), (COL_GG, IN_GG, 1024), (COL_NG, IN_NG, 1024))
META_SEGMENTS = ((COL_GQ, IN_GQ, 512), (COL_GK, IN_GK, 512),
                 (COL_R, IN_R, LANES))


def _store_rows(o_ref, c0, y, permute):
    n = y.shape[1]
    if not permute:
        o_ref[:, c0:c0 + n] = y
        return
    for rl in range(y.shape[0] // GRID_W):
        for cb in range(NA_CB):
            src = rl * GRID_W + cb * NA_CBW
            o_ref[cb, rl * NA_CBW:(rl + 1) * NA_CBW, c0:c0 + n] = y[src:src + NA_CBW]


def _gla_decay_logits(u, wt_ref, wd_ref, bdec_ref):
    q = _dot_nt(u, wt_ref[IN_GQ:IN_GQ + GLA_KEY_WIDTH, :]) * (GLA_DK ** -0.5)
    k = _dot_nt(u, wt_ref[IN_GK:IN_GK + GLA_KEY_WIDTH, :])
    r = _dot_nt(u, wt_ref[IN_R:IN_R + LANES, :]).astype(BF16)
    logits = [_dot(r, wd_ref[d]) + bdec_ref[d] for d in range(2)]
    return q, k, logits


def _gla_log_decay_split(logits):
    return [_split_bf16(_log_sigmoid(x) * (1.0 / GLA_TAU)) for x in logits]


def _gla_operands_store(q, k, splits, tg_ref, zm_ref, gtot_ref):
    tm = q.shape[0]
    tiles = [(d, t) for d in range(2) for t in range(tm // TILE)]
    cums = [_dot(tg_ref[d], jnp.concatenate([splits[d][0][t * TILE:(t + 1) * TILE],
                                             splits[d][1][t * TILE:(t + 1) * TILE]], axis=0))
            for d, t in tiles]
    for g, (d, t) in zip(cums, tiles):
        rows = slice(t * TILE, (t + 1) * TILE)
        col = COL_DIR0 + d * DIR_W
        tots = []
        for c in range(TILE // GLA_CHUNK):
            last = c * GLA_CHUNK + (GLA_CHUNK - 1 if d == 0 else 0)
            tot = g[last:last + 1]
            gtot_ref[(t * TILE) // GLA_CHUNK + c:(t * TILE) // GLA_CHUNK + c + 1,
                     d * GLA_KEY_WIDTH:(d + 1) * GLA_KEY_WIDTH] = tot
            tots.append(jnp.broadcast_to(tot, (GLA_CHUNK, GLA_KEY_WIDTH)))
        dd = jnp.concatenate(tots, axis=0) - g
        zm_ref[rows, col:col + 512] = (q[rows] * jnp.exp(g)).astype(BF16)
        zm_ref[rows, col + 512:col + 1024] = (k[rows] * jnp.exp(-g)).astype(BF16)
        zm_ref[rows, col + 1024:col + 1536] = (k[rows] * jnp.exp(dd)).astype(BF16)


def _proj_kernel(x_ref, g_ref, wt_ref, gain_ref, kgain_ref, bd_ref, wd_ref, bdec_ref, tg_ref,
                 zm_ref, nq_ref, nk_ref, nv_ref, gtot_ref, *, tokens):
    permute = tokens
    x = x_ref[...]
    ms = jnp.mean(x * x, axis=-1, keepdims=True)
    u = (x * lax.rsqrt(ms + RMS_EPS) * g_ref[...]).astype(BF16)
    if tokens:
        gla_q, gla_k, logits = _gla_decay_logits(u, wt_ref, wd_ref, bdec_ref)
    else:
        gtot_ref[...] = jnp.zeros_like(gtot_ref)
    for dst, src, width in SHARED_SEGMENTS + (() if tokens else META_SEGMENTS):
        for off in range(0, width, PROJ_TN):
            n = min(PROJ_TN, width - off)
            acc = _dot_nt(u, wt_ref[src + off:src + off + n, :])
            if dst == COL_GQ:
                acc = acc * (GLA_DK ** -0.5)
            zm_ref[:, dst + off:dst + off + n] = acc.astype(zm_ref.dtype)
    if tokens:
        _gla_operands_store(gla_q, gla_k, _gla_log_decay_split(logits), tg_ref, zm_ref, gtot_ref)
    for off in range(0, NA_WIDTH, PROJ_TN):
        acc = _dot_nt(u, wt_ref[IN_NV + off:IN_NV + off + PROJ_TN, :])
        _store_rows(nv_ref, off, acc.astype(nv_ref.dtype), permute)
    half = PROJ_TN // 2

    def head_norm_rows(acc, goff):
        sq = (acc * acc).astype(BF16)
        ms = jnp.concatenate([_dot(sq[:, :half], bd_ref[...]), _dot(sq[:, half:], bd_ref[...])], axis=1)
        return (acc * lax.rsqrt(ms + RMS_EPS) * gain_ref[:, goff:goff + PROJ_TN]).astype(BF16)

    for off in range(0, NA_WIDTH, PROJ_TN):
        acc = _dot_nt(u, wt_ref[IN_NQ + off:IN_NQ + off + PROJ_TN, :])
        nq_ref[:, off:off + PROJ_TN] = head_norm_rows(acc, off)
    if not permute:
        for off in range(0, NA_WIDTH, PROJ_TN):
            acc = _dot_nt(u, wt_ref[IN_NK + off:IN_NK + off + PROJ_TN, :])
            nk_ref[:, off:off + PROJ_TN] = head_norm_rows(acc, NA_WIDTH + off)
        return
    tm = u.shape[0]
    u_cb = jnp.concatenate([u[rl * GRID_W + cb * NA_CBW:rl * GRID_W + (cb + 1) * NA_CBW]
                            for cb in range(NA_CB) for rl in range(tm // GRID_W)], axis=0)
    for off in range(0, NA_WIDTH, PROJ_TN):
        acc_t = _dot_nt(wt_ref[IN_NK + off:IN_NK + off + PROJ_TN, :], u_cb)
        for h in range(PROJ_TN // NA_DH):
            rows = slice(off + h * NA_DH, off + (h + 1) * NA_DH)
            blk = acc_t[h * NA_DH:(h + 1) * NA_DH]
            ms = jnp.sum(blk * blk, axis=0, keepdims=True) * (1.0 / NA_DH)
            y = (blk * lax.rsqrt(ms + RMS_EPS) * kgain_ref[rows, :]).astype(BF16)
            for cb in range(NA_CB):
                nk_ref[cb, rows, :] = y[:, cb * LANES:(cb + 1) * LANES]


def _proj_call(x2d, g, wt, gain_qk, kgain_col, wd, bdec, tg, *, tm, tokens, name):
    m, d = x2d.shape
    assert m % tm == 0 and wt.shape == (D_IN_PROJ, d)
    half = PROJ_TN // 2
    grp = jnp.arange(half) // NA_DH
    bd = jnp.where(grp[:, None] == grp[None, :], 1.0 / NA_DH, 0.0).astype(BF16)
    resident = lambda shape: pl.BlockSpec(shape, lambda i: (0,) * len(shape), pipeline_mode=pl.Buffered(1))
    plain = pl.BlockSpec((tm, NA_WIDTH), lambda i: (i, 0))
    plain_shape = jax.ShapeDtypeStruct((m, NA_WIDTH), BF16)
    if tokens:
        assert tm == NA_WIN_H * GRID_W and NA_CB * LANES == tm
        main_w = MAIN_TOK_W
        v_spec = pl.BlockSpec((NA_CB, tm // NA_CB, NA_WIDTH), lambda i: (0, i, 0))
        v_shape = jax.ShapeDtypeStruct((NA_CB, m // NA_CB, NA_WIDTH), BF16)
        k_spec = pl.BlockSpec((NA_CB, NA_WIDTH, LANES), lambda i: (i, 0, 0))
        k_shape = jax.ShapeDtypeStruct((m // LANES, NA_WIDTH, LANES), BF16)
        n_chunk_rows = tm // GLA_CHUNK
    else:
        main_w = MAIN_W
        v_spec, v_shape, k_spec, k_shape = plain, plain_shape, plain, plain_shape
        n_chunk_rows = 8
    gt_rows = (m // tm) * n_chunk_rows
    return pl.pallas_call(
        functools.partial(_proj_kernel, tokens=tokens),
        name=name,
        grid=(m // tm,),
        in_specs=[
            pl.BlockSpec((tm, d), lambda i: (i, 0)),
            resident((1, d)),
            resident((D_IN_PROJ, d)),
            resident((1, 2 * NA_WIDTH)),
            resident((NA_WIDTH, 1)),
            resident((half, half)),
            resident((2, LANES, GLA_KEY_WIDTH)),
            resident((2, 1, GLA_KEY_WIDTH)),
            resident((2, TILE, 2 * TILE)),
        ],
        out_specs=[pl.BlockSpec((tm, main_w), lambda i: (i, 0)), plain, k_spec, v_spec,
                   pl.BlockSpec((n_chunk_rows, 2 * GLA_KEY_WIDTH), lambda i: (i, 0))],
        out_shape=[jax.ShapeDtypeStruct((m, main_w), BF16), plain_shape, k_shape, v_shape,
                   jax.ShapeDtypeStruct((gt_rows, 2 * GLA_KEY_WIDTH), F32)],
        compiler_params=pltpu.CompilerParams(
            dimension_semantics=("parallel",), vmem_limit_bytes=VMEM_LIMIT),
    )(x2d, g, wt, gain_qk, kgain_col, bd, wd, bdec, tg)


def _gla_kernel(qf_ref, kif_ref, ksf_ref, vf_ref, gtf_ref, qb_ref, kib_ref, ksb_ref, vb_ref, gtb_ref,
                km_ref, vm_ref, rm_ref, wd_ref, b_ref, tgm_ref,
                of_ref, ob_ref, sf_ref, sb_ref, *, n_tiles):
    row = lax.broadcasted_iota(jnp.int32, (TILE, TILE), 0)
    col = lax.broadcasted_iota(jnp.int32, (TILE, TILE), 1)
    same = (row // GLA_CHUNK) == (col // GLA_CHUNK)
    masks = (same & (row >= col), same & (row <= col))

    @pl.when(pl.program_id(2) == 0)
    def _():
        rowm = lax.broadcasted_iota(jnp.int32, (TILE, GLA_DK), 0)
        lg = _log_sigmoid(_dot(rm_ref[...], wd_ref[...]) + b_ref[...]) * (1.0 / GLA_TAU)
        hi, lo = _split_bf16(jnp.where(rowm < N_META, lg, 0.0))
        d_m = _dot(tgm_ref[...], jnp.concatenate([hi, lo], axis=0))
        k_st = (km_ref[...].astype(F32) * jnp.exp(d_m)).astype(BF16)
        sf_ref[...] = _dot_tn(k_st, vm_ref[...])
        sb_ref[...] = jnp.zeros_like(sb_ref)

    items = []
    for t in range(n_tiles):
        items.append((0, t))
        items.append((1, n_tiles - 1 - t))
    refs = ((qf_ref, kif_ref, ksf_ref, vf_ref, of_ref), (qb_ref, kib_ref, ksb_ref, vb_ref, ob_ref))
    a_cols = (jnp.exp(jnp.transpose(gtf_ref[...])), jnp.exp(jnp.transpose(gtb_ref[...])))

    def rows_of(t):
        return slice(t * TILE, (t + 1) * TILE)

    amats = [jnp.where(masks[d], _dot_nt(refs[d][0][rows_of(t), :], refs[d][1][rows_of(t), :]), 0.0).astype(BF16)
             for d, t in items]

    def state_independent(idx):
        d, t = items[idx]
        v = refs[d][3][rows_of(t), :]
        k_st = refs[d][2][rows_of(t), :]
        u = [_dot_tn(k_st[h * GLA_CHUNK:(h + 1) * GLA_CHUNK], v[h * GLA_CHUNK:(h + 1) * GLA_CHUNK])
             for h in range(2)]
        return _dot(amats[idx], v), u

    state = [sf_ref[...], sb_ref[...]]
    ahead = state_independent(0)
    for idx, (d, t) in enumerate(items):
        o_intra, u = ahead
        if idx + 1 < len(items):
            ahead = state_independent(idx + 1)
        s = state[d]
        outs = [None, None]
        for h in ((0, 1) if d == 0 else (1, 0)):
            rows = slice(t * TILE + h * GLA_CHUNK, t * TILE + (h + 1) * GLA_CHUNK)
            outs[h] = o_intra[h * GLA_CHUNK:(h + 1) * GLA_CHUNK] + _dot(refs[d][0][rows, :], s.astype(BF16))
            chunk = 2 * t + h
            s = a_cols[d][:, chunk:chunk + 1] * s + u[h]
        state[d] = s
        refs[d][4][rows_of(t), :] = jnp.concatenate(outs, axis=0).astype(refs[d][4].dtype)
    sf_ref[...] = state[0]
    sb_ref[...] = state[1]


def _gla_constants():
    i = np.arange(TILE)
    same = (i[:, None] // GLA_CHUNK) == (i[None, :] // GLA_CHUNK)
    inc_f = same & (i[None, :] <= i[:, None])
    inc_b = same & (i[None, :] >= i[:, None])
    to_end_meta = i[None, :] > i[:, None]
    dup = lambda m: np.concatenate([m, m], axis=1)
    tg = jnp.asarray(np.stack([dup(inc_f), dup(inc_b)]), BF16)
    return tg, jnp.asarray(dup(to_end_meta), BF16)


def _gla_call(zmain, gtot, zmain_m, wd, bias, tg_m, *, batch, seq, tokens_per_step):
    t_blk = tokens_per_step
    nb = seq // t_blk
    n_tiles = t_blk // TILE
    n_chunks = t_blk // GLA_CHUNK

    def tok(fwd):
        if fwd:
            return lambda b, hh, i: b * nb + i
        return lambda b, hh, i: b * nb + (nb - 1 - i)

    def dir_specs(d):
        t = tok(d == 0)
        c0 = (COL_DIR0 + d * DIR_W) // GLA_DK
        return [
            pl.BlockSpec((t_blk, GLA_DK), lambda b, hh, i: (t(b, hh, i), c0 + hh)),
            pl.BlockSpec((t_blk, GLA_DK), lambda b, hh, i: (t(b, hh, i), c0 + GLA_HEADS + hh)),
            pl.BlockSpec((t_blk, GLA_DK), lambda b, hh, i: (t(b, hh, i), c0 + 2 * GLA_HEADS + hh)),
            pl.BlockSpec((t_blk, GLA_DV), lambda b, hh, i: (t(b, hh, i), COL_GV // GLA_DV + hh)),
            pl.BlockSpec((n_chunks, GLA_DK), lambda b, hh, i: (t(b, hh, i), d * GLA_HEADS + hh)),
        ]

    in_specs = (
        dir_specs(0) + dir_specs(1)
        + [pl.BlockSpec((TILE, GLA_DK), lambda b, hh, i: (0, COL_GK // GLA_DK + hh)),
           pl.BlockSpec((TILE, GLA_DV), lambda b, hh, i: (0, COL_GV // GLA_DV + hh)),
           pl.BlockSpec((TILE, LANES), lambda b, hh, i: (0, COL_R // LANES)),
           pl.BlockSpec((None, LANES, GLA_DK), lambda b, hh, i: (0, 0, hh)),
           pl.BlockSpec((None, 1, GLA_DK), lambda b, hh, i: (0, 0, hh)),
           pl.BlockSpec((TILE, 2 * TILE), lambda b, hh, i: (0, 0))]
    )
    args = [zmain] * 4 + [gtot] + [zmain] * 4 + [gtot] + [zmain_m] * 3 + [wd, bias, tg_m]
    out_spec_f = pl.BlockSpec((t_blk, GLA_DV), lambda b, hh, i: (b * nb + i, hh))
    out_spec_b = pl.BlockSpec((t_blk, GLA_DV), lambda b, hh, i: (b * nb + (nb - 1 - i), hh))
    o_shape = jax.ShapeDtypeStruct((batch * seq, GLA_WIDTH), BF16)
    return pl.pallas_call(
        functools.partial(_gla_kernel, n_tiles=n_tiles),
        name="gla_scan",
        grid=(batch, GLA_HEADS, nb),
        in_specs=in_specs,
        out_specs=[out_spec_f, out_spec_b],
        out_shape=[o_shape, o_shape],
        scratch_shapes=[pltpu.VMEM((GLA_DK, GLA_DV), F32), pltpu.VMEM((GLA_DK, GLA_DV), F32)],
        compiler_params=pltpu.CompilerParams(
            dimension_semantics=("parallel", "parallel", "arbitrary"), vmem_limit_bytes=VMEM_LIMIT),
    )(*args)


NA_SEG_COLS = ((0, 24), (24, 40), (40, 64))
NA_SEG_ROW0 = (0, 48, 80)
NA_STACK = 2 * GRID_W


NA_BASE_MID = NA_WIN_H // 2 - 1
NA_VARIANTS = 2 * NA_WIN_H - 1


def _na_variant(base, a):
    return jnp.where(base == NA_BASE_MID, a, jnp.where(base < NA_BASE_MID, NA_WIN_H + base, NA_WIN_H - 1 + base))


def _na_build_bias(rpb_ref, tab_ref):
    def build(v, carry):
        a = jnp.where(v < NA_WIN_H, v, 0)
        base = jnp.where(v < NA_WIN_H, NA_BASE_MID,
                         jnp.where(v < NA_WIN_H + NA_BASE_MID, v - NA_WIN_H, v - (NA_WIN_H - 1)))
        for hh in range(2):
            bias_rows = [rpb_ref[hh, pl.ds(base + ((t - a + NA_WIN_H) & (NA_WIN_H - 1)), 1), :]
                         for t in range(NA_WIN_H)]
            for s, (c0, c1) in enumerate(NA_SEG_COLS):
                n = c1 - c0
                vrows = [jnp.broadcast_to(br, (n, LANES)) for br in bias_rows]
                row0 = NA_SEG_ROW0[s] + hh * n
                rho = lax.broadcasted_iota(jnp.int32, (n, LANES), 0)
                lane = lax.broadcasted_iota(jnp.int32, (n, LANES), 1)
                cs = jnp.clip(c0 + rho - NA_WIN_W // 2, 0, GRID_W - NA_WIN_W)
                for half in range(2):
                    cb = s + half
                    cp = NA_CBW * cb + lane % NA_CBW
                    valid = (cp >= cs) & (cp < cs + NA_WIN_W)
                    acc = None
                    for i in range(NA_WIN_H):
                        shift = (NA_CBW * i - NA_CBW * cb + c0 - (NA_WIN_W - 1)) % LANES
                        rolled = pltpu.roll(vrows[i], shift, 1, stride=1, stride_axis=0)
                        acc = rolled if acc is None else jnp.where(lane // NA_CBW == i, rolled, acc)
                    tab_ref[v, row0:row0 + n, half * LANES:(half + 1) * LANES] = jnp.where(valid, acc, NEG_BIG)
        return carry

    lax.fori_loop(0, NA_VARIANTS, build, 0)


def _na_kernel(q_ref, kt_ref, v_ref, gate_ref, km_ref, vm_ref, rpb_ref, mb_ref, o_ref, tab_ref, sc_ref, sm_ref,
               *, rows_per_step, n_rows):
    w = GRID_W

    @pl.when(pl.program_id(2) == 0)
    def _():
        _na_build_bias(rpb_ref, tab_ref)

    head0_q = lax.broadcasted_iota(jnp.int32, (w, LANES), 1) < NA_DH
    srow = lax.broadcasted_iota(jnp.int32, (NA_STACK, N_META), 0)
    is_h1 = jnp.zeros((NA_STACK, N_META), jnp.bool_)
    for s, (c0, c1) in enumerate(NA_SEG_COLS):
        lo = NA_SEG_ROW0[s] + (c1 - c0)
        is_h1 = is_h1 | ((srow >= lo) & (srow < lo + (c1 - c0)))
    mb = jnp.where(is_h1, mb_ref[1], mb_ref[0])
    km = km_ref[...]
    vm = vm_ref[...]
    seg_rows = [slice(NA_SEG_ROW0[s], NA_SEG_ROW0[s] + 2 * (c1 - c0)) for s, (c0, c1) in enumerate(NA_SEG_COLS)]
    blk_rows = [slice(seg_rows[max(cb - 1, 0)].start, seg_rows[min(cb, len(seg_rows) - 1)].stop)
                for cb in range(NA_CB)]

    n_tiles = n_rows // NA_WIN_H
    key_lane = lax.broadcasted_iota(jnp.int32, (LANES, LANES), 1)

    def window(rl):
        r = pl.program_id(2) * rows_per_step + rl
        rs = jnp.clip(r - NA_WIN_H // 2, 0, n_rows - NA_WIN_H)
        a = rs & (NA_WIN_H - 1)
        return rs // NA_WIN_H, a, _na_variant(rs - r + (NA_WIN_H - 1), a)

    def scores_store(slot, j, rl):
        m, a, var = window(rl)
        q = q_ref[pl.ds(pl.multiple_of(rl * w, w), w), :].astype(F32)
        q0 = jnp.where(head0_q, q, 0.0)
        q1 = q - q0
        qs = jnp.concatenate([piece[c0:c1] for c0, c1 in NA_SEG_COLS for piece in (q0, q1)],
                             axis=0).astype(BF16)
        from_m = key_lane >= a * NA_CBW
        m1 = jnp.minimum(m + 1, n_tiles - 1)
        blocks = [_dot(qs[blk_rows[cb]],
                       jnp.where(from_m, kt_ref[NA_CB * m + cb], kt_ref[NA_CB * m1 + cb]))
                  for cb in range(NA_CB)]
        for s in range(len(NA_SEG_COLS)):
            n2 = seg_rows[s].stop - seg_rows[s].start
            for half in range(2):
                cb = s + half
                a0 = seg_rows[s].start - blk_rows[cb].start
                lanes = slice(half * LANES, (half + 1) * LANES)
                sc_ref[slot, j, seg_rows[s], lanes] = blocks[cb][a0:a0 + n2] + tab_ref[var, seg_rows[s], lanes]
        sm_ref[slot, j] = _dot_nt(qs, km) + mb

    def value_block(cb, m, a):
        groups = []
        for t in range(NA_WIN_H):
            row = NA_WIN_H * m + t + jnp.where(t < a, NA_WIN_H, 0)
            groups.append(v_ref[cb, pl.ds(pl.multiple_of(row * NA_CBW, NA_CBW), NA_CBW), :])
        return jnp.concatenate(groups, axis=0)

    def softmax_values(slot, j, rl):
        m_tile, a, _ = window(rl)
        sc = sc_ref[slot, j]
        sm = sm_ref[slot, j]
        m = jnp.maximum(sc[:, :LANES], sc[:, LANES:])
        m = jnp.maximum(jnp.max(m, axis=-1, keepdims=True), jnp.max(sm, axis=-1, keepdims=True))
        p = jnp.exp(sc - m)
        pm = jnp.exp(sm - m)
        den = (jnp.sum(p[:, :LANES] + p[:, LANES:], axis=-1, keepdims=True)
               + jnp.sum(pm, axis=-1, keepdims=True))
        p = p.astype(BF16)
        p_blk = [p[seg_rows[0], :LANES],
                 jnp.concatenate([p[seg_rows[0], LANES:], p[seg_rows[1], :LANES]], axis=0),
                 jnp.concatenate([p[seg_rows[1], LANES:], p[seg_rows[2], :LANES]], axis=0),
                 p[seg_rows[2], LANES:]]
        parts = [_dot(p_blk[cb], value_block(cb, m_tile, a)) for cb in range(NA_CB)]
        o_meta = _dot(pm.astype(BF16), vm)
        inv = 1.0 / den
        outs = []
        for s, (c0, c1) in enumerate(NA_SEG_COLS):
            a0 = seg_rows[s].start - blk_rows[s].start
            b0 = seg_rows[s].start - blk_rows[s + 1].start
            n2 = seg_rows[s].stop - seg_rows[s].start
            o = (parts[s][a0:a0 + n2] + parts[s + 1][b0:b0 + n2] + o_meta[seg_rows[s]]) * inv[seg_rows[s]]
            n = c1 - c0
            head0 = lax.broadcasted_iota(jnp.int32, (n, LANES), 1) < NA_DH
            outs.append(jnp.where(head0, o[:n], o[n:]))
        o = jnp.concatenate(outs, axis=0)
        qoff = pl.multiple_of(rl * w, w)
        gate = gate_ref[pl.ds(qoff, w), :].astype(F32)
        o_ref[pl.ds(qoff, w), :] = (o * _silu(gate)).astype(o_ref.dtype)

    n_groups = rows_per_step // NA_ROW_GROUP
    for j in range(NA_ROW_GROUP):
        scores_store(0, j, j)

    def trip(g, carry):
        slot = g % 2
        for j in range(NA_ROW_GROUP):
            scores_store(slot, j, g * NA_ROW_GROUP + j)
        for j in range(NA_ROW_GROUP):
            softmax_values(1 - slot, j, (g - 1) * NA_ROW_GROUP + j)
        return carry

    lax.fori_loop(1, n_groups, trip, 0)
    for j in range(NA_ROW_GROUP):
        softmax_values((n_groups - 1) % 2, j, (n_groups - 1) * NA_ROW_GROUP + j)


def _na_call(nq, nk_t, nv_cb, zmain, nk_m, nv_m, rpb_pad, meta_bias, *, batch, seq, rows_per_step):
    n_rows = seq // GRID_W
    assert n_rows % NA_WIN_H == 0 and n_rows % rows_per_step == 0 and rows_per_step % NA_ROW_GROUP == 0
    nsteps = n_rows // rows_per_step
    hp = NA_HEADS // 2
    tq = rows_per_step * GRID_W
    n_idx = 2 * NA_WIN_H - 1
    in_specs = [
        pl.BlockSpec((tq, LANES), lambda b, p, i: (b * nsteps + i, p)),
        pl.BlockSpec((seq // LANES, LANES, LANES), lambda b, p, i: (b, p, 0)),
        pl.BlockSpec((NA_CB, seq // NA_CB, LANES), lambda b, p, i: (0, b, p)),
        pl.BlockSpec((tq, LANES), lambda b, p, i: (b * nsteps + i, COL_NG // LANES + p)),
        pl.BlockSpec((N_META, LANES), lambda b, p, i: (0, p)),
        pl.BlockSpec((N_META, LANES), lambda b, p, i: (0, p)),
        pl.BlockSpec((2, n_idx, LANES), lambda b, p, i: (p, 0, 0)),
        pl.BlockSpec((2, 1, N_META), lambda b, p, i: (p, 0, 0)),
    ]
    return pl.pallas_call(
        functools.partial(_na_kernel, rows_per_step=rows_per_step, n_rows=n_rows),
        name="na_attn",
        grid=(batch, hp, nsteps),
        in_specs=in_specs,
        out_specs=pl.BlockSpec((tq, LANES), lambda b, p, i: (b * nsteps + i, p)),
        out_shape=jax.ShapeDtypeStruct((batch * seq, NA_WIDTH), BF16),
        scratch_shapes=[pltpu.VMEM((NA_VARIANTS, NA_STACK, 2 * LANES), F32),
                        pltpu.VMEM((2, NA_ROW_GROUP, NA_STACK, 2 * LANES), F32),
                        pltpu.VMEM((2, NA_ROW_GROUP, NA_STACK, N_META), F32)],
        compiler_params=pltpu.CompilerParams(
            dimension_semantics=("parallel", "parallel", "arbitrary"), vmem_limit_bytes=VMEM_LIMIT),
    )(nq, nk_t, nv_cb, zmain, nk_m, nv_m, rpb_pad, meta_bias)


def _out_kernel(x_ref, of_ref, ob_ref, gg_ref, na_ref, gn_ref, w_ref, o_ref):
    s = of_ref[...].astype(F32) + ob_ref[...].astype(F32)
    gate = gg_ref[...].astype(F32)
    gn = gn_ref[...]
    heads = []
    for h in range(GLA_HEADS):
        cols = slice(h * GLA_DV, (h + 1) * GLA_DV)
        sh = s[:, cols]
        ms = jnp.mean(sh * sh, axis=-1, keepdims=True)
        y = sh * lax.rsqrt(ms + RMS_EPS) * gn
        heads.append((y * _silu(gate[:, cols])).astype(BF16))
    o_gla = jnp.concatenate(heads, axis=1)
    acc = _dot(o_gla, w_ref[:GLA_WIDTH, :]) + _dot(na_ref[...], w_ref[GLA_WIDTH:, :])
    o_ref[...] = x_ref[...] + acc


def _out_call(x2d, o_f, o_b, zmain, o_na, gn, w_out, *, tm):
    m, d = x2d.shape
    assert m % tm == 0
    return pl.pallas_call(
        _out_kernel,
        name="out_proj",
        grid=(m // tm,),
        in_specs=[
            pl.BlockSpec((tm, d), lambda i: (i, 0)),
            pl.BlockSpec((tm, GLA_WIDTH), lambda i: (i, 0)),
            pl.BlockSpec((tm, GLA_WIDTH), lambda i: (i, 0)),
            pl.BlockSpec((tm, GLA_WIDTH), lambda i: (i, COL_GG // GLA_WIDTH)),
            pl.BlockSpec((tm, NA_WIDTH), lambda i: (i, 0)),
            pl.BlockSpec((1, GLA_DV), lambda i: (0, 0)),
            pl.BlockSpec((GLA_WIDTH + NA_WIDTH, d), lambda i: (0, 0)),
        ],
        out_specs=pl.BlockSpec((tm, d), lambda i: (i, 0)),
        out_shape=jax.ShapeDtypeStruct((m, d), F32),
        compiler_params=pltpu.CompilerParams(
            dimension_semantics=("parallel",), vmem_limit_bytes=VMEM_LIMIT),
    )(x2d, o_f, o_b, zmain, o_na, gn, w_out)


def _pick(n, cands):
    for c in cands:
        if n % c == 0:
            return c
    raise ValueError(f"no tile for {n}")


def kernel(x, meta_tokens, norm_g, w_in, w_decay_fwd, b_decay_fwd, w_decay_bwd, b_decay_bwd,
           gla_out_norm_g, q_norm_g, k_norm_g, rpb, meta_bias, w_out):
    batch, seq, d = x.shape
    assert d == D_MODEL and seq % TILE == 0 and seq % GRID_W == 0
    depth = norm_g.shape[0]
    assert depth == 1
    l = 0
    x2d = x.reshape(batch * seq, d)
    g = norm_g[l].reshape(1, d)

    assert w_in.shape[1:] == (d, D_IN_PROJ)
    wt = jnp.swapaxes(w_in[l], 0, 1).astype(BF16)
    gain_qk = jnp.concatenate([jnp.tile(q_norm_g[l], NA_HEADS) * (NA_DH ** -0.5),
                               jnp.tile(k_norm_g[l], NA_HEADS)]).reshape(1, 2 * NA_WIDTH)

    zr = jnp.zeros((GLA_RANK, GLA_KEY_WIDTH), F32)
    ztail = jnp.zeros((LANES - 2 * GLA_RANK, GLA_KEY_WIDTH), F32)
    wd = jnp.stack([jnp.concatenate([w_decay_fwd[l], zr, ztail], axis=0),
                    jnp.concatenate([zr, w_decay_bwd[l], ztail], axis=0)]).astype(BF16)
    bias = jnp.stack([b_decay_fwd[l], b_decay_bwd[l]]).reshape(2, 1, GLA_KEY_WIDTH)
    tg, tg_m = _gla_constants()

    assert seq % (NA_WIN_H * GRID_W) == 0
    kgain_col = jnp.tile(k_norm_g[l], NA_HEADS).reshape(NA_WIDTH, 1)
    zmain, nq, nk_t, nv_cb, gtot = _proj_call(x2d, g, wt, gain_qk, kgain_col, wd, bias, tg,
                                              tm=NA_WIN_H * GRID_W, tokens=True, name="proj_tokens")
    zmain_m, _, nk_m, nv_m, _ = _proj_call(meta_tokens, g, wt, gain_qk, kgain_col, wd, bias, tg,
                                           tm=N_META, tokens=False, name="proj_meta")
    zmain_m_pad = jnp.pad(zmain_m, ((0, TILE - N_META), (0, 0)))

    o_f, o_b = _gla_call(zmain, gtot, zmain_m_pad, wd, bias, tg_m, batch=batch, seq=seq,
                         tokens_per_step=_pick(seq, (2048, 1024, 512, 256, 128)))

    rpb_pad = jnp.pad(rpb[l], ((0, 0), (0, 0), (0, 2 * NA_DH - rpb.shape[-1])))
    mb = meta_bias[l].reshape(NA_HEADS, 1, N_META)
    o_na = _na_call(nq, nk_t, nv_cb, zmain, nk_m, nv_m, rpb_pad, mb, batch=batch, seq=seq,
                    rows_per_step=_pick(seq // GRID_W, (128, 64, 32, 16)))

    out = _out_call(x2d, o_f, o_b, zmain, o_na, gla_out_norm_g[l].reshape(1, GLA_DV),
                    w_out[l].astype(BF16), tm=_pick(batch * seq, (512, 256, 128)))
    return out.reshape(batch, seq, d)
```

```python
import functools

import jax
import jax.numpy as jnp
import numpy as np
from jax import lax
from jax.experimental import pallas as pl
from jax.experimental.pallas import tpu as pltpu

F32 = jnp.float32
BF16 = jnp.bfloat16

D_MODEL = 1024
N_META = 16
GRID_W = 64
GLA_HEADS = 4
GLA_DK = 128
GLA_DV = 256
GLA_KEY_WIDTH = GLA_HEADS * GLA_DK
GLA_WIDTH = GLA_HEADS * GLA_DV
GLA_RANK = 16
GLA_TAU = 16.0
GLA_CHUNK = 64
NA_HEADS = 16
NA_DH = 64
NA_WIDTH = NA_HEADS * NA_DH
NA_WIN_H = 8
NA_WIN_W = 16
RMS_EPS = 1e-6
NEG_BIG = -1e30
LANES = 128

COL_GV, COL_GG, COL_NG, COL_GQ, COL_GK, COL_R = 0, 1024, 2048, 3072, 3584, 4096
MAIN_W = COL_R + LANES
COL_DIR0, DIR_W = 3072, 3 * 512
MAIN_TOK_W = COL_DIR0 + 2 * DIR_W
NA_CBW = 16
NA_CB = GRID_W // NA_CBW
IN_GQ, IN_GK, IN_GV, IN_R, IN_GG, IN_NQ, IN_NK, IN_NV, IN_NG = 0, 512, 1024, 2048, 2080, 3104, 4128, 5152, 6176
D_IN_PROJ = 7200
PROJ_TN = 512

TILE = 2 * GLA_CHUNK
NA_ROW_GROUP = 8
VMEM_LIMIT = 58 * 1024 * 1024


def _split_bf16(x):
    hi = x.astype(BF16)
    lo = (x - hi.astype(F32)).astype(BF16)
    return hi, lo


def _dot(a, b):
    return jnp.dot(a, b, preferred_element_type=F32)


def _dot_nt(a, b):
    return lax.dot_general(a, b, (((1,), (1,)), ((), ())), preferred_element_type=F32)


def _dot_tn(a, b):
    return lax.dot_general(a, b, (((0,), (0,)), ((), ())), preferred_element_type=F32)


def _log_sigmoid(x):
    return jnp.minimum(x, 0.0) - jnp.log(1.0 + jnp.exp(-jnp.abs(x)))


def _silu(x):
    return x * (1.0 / (1.0 + jnp.exp(-x)))


SHARED_SEGMENTS = ((COL_GV, IN_GV, 1024), (COL_GG, IN_GG, 1024), (COL_NG, IN_NG, 1024))
META_SEGMENTS = ((COL_GQ, IN_GQ, 512), (COL_GK, IN_GK, 512),
                 (COL_R, IN_R, LANES))


def _store_rows(o_ref, c0, y, permute):
    n = y.shape[1]
    if not permute:
        o_ref[:, c0:c0 + n] = y
        return
    for rl in range(y.shape[0] // GRID_W):
        for cb in range(NA_CB):
            src = rl * GRID_W + cb * NA_CBW
            o_ref[cb, rl * NA_CBW:(rl + 1) * NA_CBW, c0:c0 + n] = y[src:src + NA_CBW]


def _gla_log_decay(r, wd_ref, bdec_ref, d):
    return (_log_sigmoid(_dot(r, wd_ref[d]) + bdec_ref[d]) * (1.0 / GLA_TAU)).astype(BF16)


def _gla_operands_store(q, k, log_decay, d, t, tg_ref, zm_ref, gtot_ref):
    rows = slice(t * TILE, (t + 1) * TILE)
    g = _dot(tg_ref[d], log_decay[rows])
    col = COL_DIR0 + d * DIR_W
    tots = []
    for c in range(TILE // GLA_CHUNK):
        last = c * GLA_CHUNK + (GLA_CHUNK - 1 if d == 0 else 0)
        tot = g[last:last + 1]
        gtot_ref[(t * TILE) // GLA_CHUNK + c:(t * TILE) // GLA_CHUNK + c + 1,
                 d * GLA_KEY_WIDTH:(d + 1) * GLA_KEY_WIDTH] = tot
        tots.append(jnp.broadcast_to(tot, (GLA_CHUNK, GLA_KEY_WIDTH)))
    dd = jnp.concatenate(tots, axis=0) - g
    zm_ref[rows, col:col + 512] = (q[rows] * jnp.exp(g)).astype(BF16)
    zm_ref[rows, col + 512:col + 1024] = (k[rows] * jnp.exp(-g)).astype(BF16)
    zm_ref[rows, col + 1024:col + 1536] = (k[rows] * jnp.exp(dd)).astype(BF16)


def _proj_kernel(x_ref, g_ref, wt_ref, gain_ref, kgain_ref, bd_ref, wd_ref, bdec_ref, tg_ref,
                 zm_ref, nq_ref, nk_ref, nv_ref, gtot_ref, *, tokens):
    x = x_ref[...]
    ms = jnp.mean(x * x, axis=-1, keepdims=True)
    u = (x * lax.rsqrt(ms + RMS_EPS) * g_ref[...]).astype(BF16)
    tm = u.shape[0]
    half = PROJ_TN // 2

    def main_tile(dst, src, off, n):
        acc = _dot_nt(u, wt_ref[src + off:src + off + n, :])
        if dst == COL_GQ:
            acc = acc * (GLA_DK ** -0.5)
        zm_ref[:, dst + off:dst + off + n] = acc.astype(zm_ref.dtype)

    def nv_tile(off):
        acc = _dot_nt(u, wt_ref[IN_NV + off:IN_NV + off + PROJ_TN, :])
        _store_rows(nv_ref, off, acc.astype(nv_ref.dtype), tokens)

    def head_norm_rows(acc, goff):
        sq = (acc * acc).astype(BF16)
        ms = jnp.concatenate([_dot(sq[:, :half], bd_ref[...]), _dot(sq[:, half:], bd_ref[...])], axis=1)
        return (acc * lax.rsqrt(ms + RMS_EPS) * gain_ref[:, goff:goff + PROJ_TN]).astype(BF16)

    def nq_tile(off):
        acc = _dot_nt(u, wt_ref[IN_NQ + off:IN_NQ + off + PROJ_TN, :])
        nq_ref[:, off:off + PROJ_TN] = head_norm_rows(acc, off)

    def nk_rows_tile(off):
        acc = _dot_nt(u, wt_ref[IN_NK + off:IN_NK + off + PROJ_TN, :])
        nk_ref[:, off:off + PROJ_TN] = head_norm_rows(acc, NA_WIDTH + off)

    def nk_transposed_tile(off):
        u_cb = jnp.concatenate([u[rl * GRID_W + cb * NA_CBW:rl * GRID_W + (cb + 1) * NA_CBW]
                                for cb in range(NA_CB) for rl in range(tm // GRID_W)], axis=0)
        acc_t = _dot_nt(wt_ref[IN_NK + off:IN_NK + off + PROJ_TN, :], u_cb)
        for h in range(PROJ_TN // NA_DH):
            rows = slice(off + h * NA_DH, off + (h + 1) * NA_DH)
            blk = acc_t[h * NA_DH:(h + 1) * NA_DH]
            ms = jnp.sum(blk * blk, axis=0, keepdims=True) * (1.0 / NA_DH)
            y = (blk * lax.rsqrt(ms + RMS_EPS) * kgain_ref[rows, :]).astype(BF16)
            for cb in range(NA_CB):
                nk_ref[cb, rows, :] = y[:, cb * LANES:(cb + 1) * LANES]

    tiles = []
    for dst, src, width in SHARED_SEGMENTS + (() if tokens else META_SEGMENTS):
        for off in range(0, width, PROJ_TN):
            tiles.append(functools.partial(main_tile, dst, src, off, min(PROJ_TN, width - off)))
    tiles += [functools.partial(nv_tile, off) for off in range(0, NA_WIDTH, PROJ_TN)]
    tiles += [functools.partial(nq_tile, off) for off in range(0, NA_WIDTH, PROJ_TN)]
    nk_tile = nk_transposed_tile if tokens else nk_rows_tile
    tiles += [functools.partial(nk_tile, off) for off in range(0, NA_WIDTH, PROJ_TN)]
    if not tokens:
        gtot_ref[...] = jnp.zeros_like(gtot_ref)
        for tile in tiles:
            tile()
        return

    r = _dot_nt(u, wt_ref[IN_R:IN_R + LANES, :]).astype(BF16)
    q = _dot_nt(u, wt_ref[IN_GQ:IN_GQ + GLA_KEY_WIDTH, :]) * (GLA_DK ** -0.5)
    k = _dot_nt(u, wt_ref[IN_GK:IN_GK + GLA_KEY_WIDTH, :])
    log_decay = [None, None]

    def decay_stage(d):
        log_decay[d] = _gla_log_decay(r, wd_ref, bdec_ref, d)

    stages = [functools.partial(decay_stage, 0), functools.partial(decay_stage, 1), None]
    stages += [functools.partial(lambda d, t: _gla_operands_store(q, k, log_decay[d], d, t, tg_ref, zm_ref, gtot_ref),
                                 d, t)
               for d in range(2) for t in range(tm // TILE)]
    for i, tile in enumerate(tiles):
        if i < len(stages) and stages[i] is not None:
            stages[i]()
        tile()
    assert len(stages) <= len(tiles)


def _proj_call(x2d, g, wt, gain_qk, kgain_col, wd, bdec, tg, *, tm, tokens, name):
    m, d = x2d.shape
    assert m % tm == 0 and wt.shape == (D_IN_PROJ, d)
    half = PROJ_TN // 2
    grp = jnp.arange(half) // NA_DH
    bd = jnp.where(grp[:, None] == grp[None, :], 1.0 / NA_DH, 0.0).astype(BF16)
    resident = lambda shape: pl.BlockSpec(shape, lambda i: (0,) * len(shape), pipeline_mode=pl.Buffered(1))
    plain = pl.BlockSpec((tm, NA_WIDTH), lambda i: (i, 0))
    plain_shape = jax.ShapeDtypeStruct((m, NA_WIDTH), BF16)
    if tokens:
        assert tm == NA_WIN_H * GRID_W and NA_CB * LANES == tm
        main_w = MAIN_TOK_W
        v_spec = pl.BlockSpec((NA_CB, tm // NA_CB, NA_WIDTH), lambda i: (0, i, 0))
        v_shape = jax.ShapeDtypeStruct((NA_CB, m // NA_CB, NA_WIDTH), BF16)
        k_spec = pl.BlockSpec((NA_CB, NA_WIDTH, LANES), lambda i: (i, 0, 0))
        k_shape = jax.ShapeDtypeStruct((m // LANES, NA_WIDTH, LANES), BF16)
        n_chunk_rows = tm // GLA_CHUNK
    else:
        main_w = MAIN_W
        v_spec, v_shape, k_spec, k_shape = plain, plain_shape, plain, plain_shape
        n_chunk_rows = 8
    gt_rows = (m // tm) * n_chunk_rows
    return pl.pallas_call(
        functools.partial(_proj_kernel, tokens=tokens),
        name=name,
        grid=(m // tm,),
        in_specs=[
            pl.BlockSpec((tm, d), lambda i: (i, 0)),
            resident((1, d)),
            resident((D_IN_PROJ, d)),
            resident((1, 2 * NA_WIDTH)),
            resident((NA_WIDTH, 1)),
            resident((half, half)),
            resident((2, LANES, GLA_KEY_WIDTH)),
            resident((2, 1, GLA_KEY_WIDTH)),
            resident((2, TILE, TILE)),
        ],
        out_specs=[pl.BlockSpec((tm, main_w), lambda i: (i, 0)), plain, k_spec, v_spec,
                   pl.BlockSpec((n_chunk_rows, 2 * GLA_KEY_WIDTH), lambda i: (i, 0))],
        out_shape=[jax.ShapeDtypeStruct((m, main_w), BF16), plain_shape, k_shape, v_shape,
                   jax.ShapeDtypeStruct((gt_rows, 2 * GLA_KEY_WIDTH), F32)],
        compiler_params=pltpu.CompilerParams(
            dimension_semantics=("parallel",), vmem_limit_bytes=VMEM_LIMIT),
    )(x2d, g, wt, gain_qk, kgain_col, bd, wd, bdec, tg)


def _gla_kernel(qf_ref, kif_ref, ksf_ref, vf_ref, gtf_ref, qb_ref, kib_ref, ksb_ref, vb_ref, gtb_ref,
                km_ref, vm_ref, rm_ref, wd_ref, b_ref, tgm_ref,
                of_ref, ob_ref, sf_ref, sb_ref, *, n_tiles):
    row = lax.broadcasted_iota(jnp.int32, (TILE, TILE), 0)
    col = lax.broadcasted_iota(jnp.int32, (TILE, TILE), 1)
    same = (row // GLA_CHUNK) == (col // GLA_CHUNK)
    masks = (same & (row >= col), same & (row <= col))

    @pl.when(pl.program_id(2) == 0)
    def _():
        rowm = lax.broadcasted_iota(jnp.int32, (TILE, GLA_DK), 0)
        lg = _log_sigmoid(_dot(rm_ref[...], wd_ref[...]) + b_ref[...]) * (1.0 / GLA_TAU)
        hi, lo = _split_bf16(jnp.where(rowm < N_META, lg, 0.0))
        d_m = _dot(tgm_ref[...], jnp.concatenate([hi, lo], axis=0))
        k_st = (km_ref[...].astype(F32) * jnp.exp(d_m)).astype(BF16)
        sf_ref[...] = _dot_tn(k_st, vm_ref[...])
        sb_ref[...] = jnp.zeros_like(sb_ref)

    items = []
    for t in range(n_tiles):
        items.append((0, t))
        items.append((1, n_tiles - 1 - t))
    refs = ((qf_ref, kif_ref, ksf_ref, vf_ref, of_ref), (qb_ref, kib_ref, ksb_ref, vb_ref, ob_ref))
    a_cols = (jnp.exp(jnp.transpose(gtf_ref[...])), jnp.exp(jnp.transpose(gtb_ref[...])))

    def rows_of(t):
        return slice(t * TILE, (t + 1) * TILE)

    amats = [jnp.where(masks[d], _dot_nt(refs[d][0][rows_of(t), :], refs[d][1][rows_of(t), :]), 0.0).astype(BF16)
             for d, t in items]

    def state_independent(idx):
        d, t = items[idx]
        v = refs[d][3][rows_of(t), :]
        k_st = refs[d][2][rows_of(t), :]
        u = [_dot_tn(k_st[h * GLA_CHUNK:(h + 1) * GLA_CHUNK], v[h * GLA_CHUNK:(h + 1) * GLA_CHUNK])
             for h in range(2)]
        return _dot(amats[idx], v), u

    state = [sf_ref[...], sb_ref[...]]
    ahead = state_independent(0)
    for idx, (d, t) in enumerate(items):
        o_intra, u = ahead
        if idx + 1 < len(items):
            ahead = state_independent(idx + 1)
        s = state[d]
        outs = [None, None]
        for h in ((0, 1) if d == 0 else (1, 0)):
            rows = slice(t * TILE + h * GLA_CHUNK, t * TILE + (h + 1) * GLA_CHUNK)
            outs[h] = o_intra[h * GLA_CHUNK:(h + 1) * GLA_CHUNK] + _dot(refs[d][0][rows, :], s.astype(BF16))
            chunk = 2 * t + h
            s = a_cols[d][:, chunk:chunk + 1] * s + u[h]
        state[d] = s
        refs[d][4][rows_of(t), :] = jnp.concatenate(outs, axis=0).astype(refs[d][4].dtype)
    sf_ref[...] = state[0]
    sb_ref[...] = state[1]


def _gla_constants():
    i = np.arange(TILE)
    same = (i[:, None] // GLA_CHUNK) == (i[None, :] // GLA_CHUNK)
    inc_f = same & (i[None, :] <= i[:, None])
    inc_b = same & (i[None, :] >= i[:, None])
    to_end_meta = i[None, :] > i[:, None]
    dup = lambda m: np.concatenate([m, m], axis=1)
    tg = jnp.asarray(np.stack([inc_f, inc_b]), BF16)
    return tg, jnp.asarray(dup(to_end_meta), BF16)


def _gla_call(zmain, gtot, zmain_m, wd, bias, tg_m, *, batch, seq, tokens_per_step):
    t_blk = tokens_per_step
    nb = seq // t_blk
    n_tiles = t_blk // TILE
    n_chunks = t_blk // GLA_CHUNK

    def tok(fwd):
        if fwd:
            return lambda b, hh, i: b * nb + i
        return lambda b, hh, i: b * nb + (nb - 1 - i)

    def dir_specs(d):
        t = tok(d == 0)
        c0 = (COL_DIR0 + d * DIR_W) // GLA_DK
        return [
            pl.BlockSpec((t_blk, GLA_DK), lambda b, hh, i: (t(b, hh, i), c0 + hh)),
            pl.BlockSpec((t_blk, GLA_DK), lambda b, hh, i: (t(b, hh, i), c0 + GLA_HEADS + hh)),
            pl.BlockSpec((t_blk, GLA_DK), lambda b, hh, i: (t(b, hh, i), c0 + 2 * GLA_HEADS + hh)),
            pl.BlockSpec((t_blk, GLA_DV), lambda b, hh, i: (t(b, hh, i), COL_GV // GLA_DV + hh)),
            pl.BlockSpec((n_chunks, GLA_DK), lambda b, hh, i: (t(b, hh, i), d * GLA_HEADS + hh)),
        ]

    in_specs = (
        dir_specs(0) + dir_specs(1)
        + [pl.BlockSpec((TILE, GLA_DK), lambda b, hh, i: (0, COL_GK // GLA_DK + hh)),
           pl.BlockSpec((TILE, GLA_DV), lambda b, hh, i: (0, COL_GV // GLA_DV + hh)),
           pl.BlockSpec((TILE, LANES), lambda b, hh, i: (0, COL_R // LANES)),
           pl.BlockSpec((None, LANES, GLA_DK), lambda b, hh, i: (0, 0, hh)),
           pl.BlockSpec((None, 1, GLA_DK), lambda b, hh, i: (0, 0, hh)),
           pl.BlockSpec((TILE, 2 * TILE), lambda b, hh, i: (0, 0))]
    )
    args = [zmain] * 4 + [gtot] + [zmain] * 4 + [gtot] + [zmain_m] * 3 + [wd, bias, tg_m]
    out_spec_f = pl.BlockSpec((t_blk, GLA_DV), lambda b, hh, i: (b * nb + i, hh))
    out_spec_b = pl.BlockSpec((t_blk, GLA_DV), lambda b, hh, i: (b * nb + (nb - 1 - i), hh))
    o_shape = jax.ShapeDtypeStruct((batch * seq, GLA_WIDTH), BF16)
    return pl.pallas_call(
        functools.partial(_gla_kernel, n_tiles=n_tiles),
        name="gla_scan",
        grid=(batch, GLA_HEADS, nb),
        in_specs=in_specs,
        out_specs=[out_spec_f, out_spec_b],
        out_shape=[o_shape, o_shape],
        scratch_shapes=[pltpu.VMEM((GLA_DK, GLA_DV), F32), pltpu.VMEM((GLA_DK, GLA_DV), F32)],
        compiler_params=pltpu.CompilerParams(
            dimension_semantics=("parallel", "parallel", "arbitrary"), vmem_limit_bytes=VMEM_LIMIT),
    )(*args)


NA_SEG_COLS = ((0, 24), (24, 40), (40, 64))
NA_SEG_ROW0 = (0, 48, 80)
NA_STACK = 2 * GRID_W


NA_BASE_MID = NA_WIN_H // 2 - 1
NA_VARIANTS = 2 * NA_WIN_H - 1


def _na_variant(base, a):
    return jnp.where(base == NA_BASE_MID, a, jnp.where(base < NA_BASE_MID, NA_WIN_H + base, NA_WIN_H - 1 + base))


def _na_build_bias(rpb_ref, tab_ref):
    def build(v, carry):
        a = jnp.where(v < NA_WIN_H, v, 0)
        base = jnp.where(v < NA_WIN_H, NA_BASE_MID,
                         jnp.where(v < NA_WIN_H + NA_BASE_MID, v - NA_WIN_H, v - (NA_WIN_H - 1)))
        for hh in range(2):
            bias_rows = [rpb_ref[hh, pl.ds(base + ((t - a + NA_WIN_H) & (NA_WIN_H - 1)), 1), :]
                         for t in range(NA_WIN_H)]
            for s, (c0, c1) in enumerate(NA_SEG_COLS):
                n = c1 - c0
                vrows = [jnp.broadcast_to(br, (n, LANES)) for br in bias_rows]
                row0 = NA_SEG_ROW0[s] + hh * n
                rho = lax.broadcasted_iota(jnp.int32, (n, LANES), 0)
                lane = lax.broadcasted_iota(jnp.int32, (n, LANES), 1)
                cs = jnp.clip(c0 + rho - NA_WIN_W // 2, 0, GRID_W - NA_WIN_W)
                for half in range(2):
                    cb = s + half
                    cp = NA_CBW * cb + lane % NA_CBW
                    valid = (cp >= cs) & (cp < cs + NA_WIN_W)
                    acc = None
                    for i in range(NA_WIN_H):
                        shift = (NA_CBW * i - NA_CBW * cb + c0 - (NA_WIN_W - 1)) % LANES
                        rolled = pltpu.roll(vrows[i], shift, 1, stride=1, stride_axis=0)
                        acc = rolled if acc is None else jnp.where(lane // NA_CBW == i, rolled, acc)
                    tab_ref[v, row0:row0 + n, half * LANES:(half + 1) * LANES] = jnp.where(valid, acc, NEG_BIG)
        return carry

    lax.fori_loop(0, NA_VARIANTS, build, 0)


def _na_kernel(q_ref, kt_ref, v_ref, gate_ref, km_ref, vm_ref, rpb_ref, mb_ref, o_ref, tab_ref, sc_ref, sm_ref,
               *, rows_per_step, n_rows):
    w = GRID_W

    @pl.when((pl.program_id(1) == 0) & (pl.program_id(2) == 0))
    def _():
        _na_build_bias(rpb_ref, tab_ref)

    head0_q = lax.broadcasted_iota(jnp.int32, (w, LANES), 1) < NA_DH
    srow = lax.broadcasted_iota(jnp.int32, (NA_STACK, N_META), 0)
    is_h1 = jnp.zeros((NA_STACK, N_META), jnp.bool_)
    for s, (c0, c1) in enumerate(NA_SEG_COLS):
        lo = NA_SEG_ROW0[s] + (c1 - c0)
        is_h1 = is_h1 | ((srow >= lo) & (srow < lo + (c1 - c0)))
    mb = jnp.where(is_h1, mb_ref[1], mb_ref[0])
    km = km_ref[...]
    vm = vm_ref[...]
    seg_rows = [slice(NA_SEG_ROW0[s], NA_SEG_ROW0[s] + 2 * (c1 - c0)) for s, (c0, c1) in enumerate(NA_SEG_COLS)]
    blk_rows = [slice(seg_rows[max(cb - 1, 0)].start, seg_rows[min(cb, len(seg_rows) - 1)].stop)
                for cb in range(NA_CB)]

    n_tiles = n_rows // NA_WIN_H
    key_lane = lax.broadcasted_iota(jnp.int32, (LANES, LANES), 1)

    def window(rl):
        r = pl.program_id(2) * rows_per_step + rl
        rs = jnp.clip(r - NA_WIN_H // 2, 0, n_rows - NA_WIN_H)
        a = rs & (NA_WIN_H - 1)
        return rs // NA_WIN_H, a, _na_variant(rs - r + (NA_WIN_H - 1), a)

    def scores_store(slot, j, rl):
        m, a, var = window(rl)
        q = q_ref[pl.ds(pl.multiple_of(rl * w, w), w), :].astype(F32)
        q0 = jnp.where(head0_q, q, 0.0)
        q1 = q - q0
        qs = jnp.concatenate([piece[c0:c1] for c0, c1 in NA_SEG_COLS for piece in (q0, q1)],
                             axis=0).astype(BF16)
        from_m = key_lane >= a * NA_CBW
        m1 = jnp.minimum(m + 1, n_tiles - 1)
        blocks = [_dot(qs[blk_rows[cb]],
                       jnp.where(from_m, kt_ref[NA_CB * m + cb], kt_ref[NA_CB * m1 + cb]))
                  for cb in range(NA_CB)]
        for s in range(len(NA_SEG_COLS)):
            n2 = seg_rows[s].stop - seg_rows[s].start
            for half in range(2):
                cb = s + half
                a0 = seg_rows[s].start - blk_rows[cb].start
                lanes = slice(half * LANES, (half + 1) * LANES)
                sc_ref[slot, j, seg_rows[s], lanes] = blocks[cb][a0:a0 + n2] + tab_ref[var, seg_rows[s], lanes]
        sm_ref[slot, j] = _dot_nt(qs, km) + mb

    def value_block(cb, m, a):
        groups = []
        for t in range(NA_WIN_H):
            row = NA_WIN_H * m + t + jnp.where(t < a, NA_WIN_H, 0)
            groups.append(v_ref[cb, pl.ds(pl.multiple_of(row * NA_CBW, NA_CBW), NA_CBW), :])
        return jnp.concatenate(groups, axis=0)

    def softmax_values(slot, j, rl):
        m_tile, a, _ = window(rl)
        sc = sc_ref[slot, j]
        sm = sm_ref[slot, j]
        m = jnp.maximum(sc[:, :LANES], sc[:, LANES:])
        m = jnp.maximum(jnp.max(m, axis=-1, keepdims=True), jnp.max(sm, axis=-1, keepdims=True))
        p = jnp.exp(sc - m)
        pm = jnp.exp(sm - m)
        den = (jnp.sum(p[:, :LANES] + p[:, LANES:], axis=-1, keepdims=True)
               + jnp.sum(pm, axis=-1, keepdims=True))
        p = p.astype(BF16)
        p_blk = [p[seg_rows[0], :LANES],
                 jnp.concatenate([p[seg_rows[0], LANES:], p[seg_rows[1], :LANES]], axis=0),
                 jnp.concatenate([p[seg_rows[1], LANES:], p[seg_rows[2], :LANES]], axis=0),
                 p[seg_rows[2], LANES:]]
        parts = [_dot(p_blk[cb], value_block(cb, m_tile, a)) for cb in range(NA_CB)]
        o_meta = _dot(pm.astype(BF16), vm)
        inv = 1.0 / den
        outs = []
        for s, (c0, c1) in enumerate(NA_SEG_COLS):
            a0 = seg_rows[s].start - blk_rows[s].start
            b0 = seg_rows[s].start - blk_rows[s + 1].start
            n2 = seg_rows[s].stop - seg_rows[s].start
            o = (parts[s][a0:a0 + n2] + parts[s + 1][b0:b0 + n2] + o_meta[seg_rows[s]]) * inv[seg_rows[s]]
            n = c1 - c0
            head0 = lax.broadcasted_iota(jnp.int32, (n, LANES), 1) < NA_DH
            outs.append(jnp.where(head0, o[:n], o[n:]))
        o = jnp.concatenate(outs, axis=0)
        qoff = pl.multiple_of(rl * w, w)
        gate = gate_ref[pl.ds(qoff, w), :].astype(F32)
        o_ref[pl.ds(qoff, w), :] = (o * _silu(gate)).astype(o_ref.dtype)

    n_groups = rows_per_step // NA_ROW_GROUP
    for j in range(NA_ROW_GROUP):
        scores_store(0, j, j)

    def trip(g, carry):
        slot = g % 2
        for j in range(NA_ROW_GROUP):
            scores_store(slot, j, g * NA_ROW_GROUP + j)
        for j in range(NA_ROW_GROUP):
            softmax_values(1 - slot, j, (g - 1) * NA_ROW_GROUP + j)
        return carry

    lax.fori_loop(1, n_groups, trip, 0)
    for j in range(NA_ROW_GROUP):
        softmax_values((n_groups - 1) % 2, j, (n_groups - 1) * NA_ROW_GROUP + j)


def _na_call(nq, nk_t, nv_cb, zmain, nk_m, nv_m, rpb_pad, meta_bias, *, batch, seq, rows_per_step):
    n_rows = seq // GRID_W
    assert n_rows % NA_WIN_H == 0 and n_rows % rows_per_step == 0 and rows_per_step % NA_ROW_GROUP == 0
    nsteps = n_rows // rows_per_step
    hp = NA_HEADS // 2
    tq = rows_per_step * GRID_W
    n_idx = 2 * NA_WIN_H - 1
    in_specs = [
        pl.BlockSpec((tq, LANES), lambda p, b, i: (b * nsteps + i, p)),
        pl.BlockSpec((seq // LANES, LANES, LANES), lambda p, b, i: (b, p, 0)),
        pl.BlockSpec((NA_CB, seq // NA_CB, LANES), lambda p, b, i: (0, b, p)),
        pl.BlockSpec((tq, LANES), lambda p, b, i: (b * nsteps + i, COL_NG // LANES + p)),
        pl.BlockSpec((N_META, LANES), lambda p, b, i: (0, p)),
        pl.BlockSpec((N_META, LANES), lambda p, b, i: (0, p)),
        pl.BlockSpec((2, n_idx, LANES), lambda p, b, i: (p, 0, 0)),
        pl.BlockSpec((2, 1, N_META), lambda p, b, i: (p, 0, 0)),
    ]
    return pl.pallas_call(
        functools.partial(_na_kernel, rows_per_step=rows_per_step, n_rows=n_rows),
        name="na_attn",
        grid=(hp, batch, nsteps),
        in_specs=in_specs,
        out_specs=pl.BlockSpec((tq, LANES), lambda p, b, i: (b * nsteps + i, p)),
        out_shape=jax.ShapeDtypeStruct((batch * seq, NA_WIDTH), BF16),
        scratch_shapes=[pltpu.VMEM((NA_VARIANTS, NA_STACK, 2 * LANES), F32),
                        pltpu.VMEM((2, NA_ROW_GROUP, NA_STACK, 2 * LANES), F32),
                        pltpu.VMEM((2, NA_ROW_GROUP, NA_STACK, N_META), F32)],
        compiler_params=pltpu.CompilerParams(
            dimension_semantics=("arbitrary", "arbitrary", "arbitrary"), vmem_limit_bytes=VMEM_LIMIT),
    )(nq, nk_t, nv_cb, zmain, nk_m, nv_m, rpb_pad, meta_bias)


def _out_kernel(x_ref, of_ref, ob_ref, gg_ref, na_ref, gn_ref, w_ref, o_ref):
    s = of_ref[...].astype(F32) + ob_ref[...].astype(F32)
    gate = gg_ref[...].astype(F32)
    gn = gn_ref[...]
    heads = []
    for h in range(GLA_HEADS):
        cols = slice(h * GLA_DV, (h + 1) * GLA_DV)
        sh = s[:, cols]
        ms = jnp.mean(sh * sh, axis=-1, keepdims=True)
        y = sh * lax.rsqrt(ms + RMS_EPS) * gn
        heads.append((y * _silu(gate[:, cols])).astype(BF16))
    o_gla = jnp.concatenate(heads, axis=1)
    acc = _dot(o_gla, w_ref[:GLA_WIDTH, :]) + _dot(na_ref[...], w_ref[GLA_WIDTH:, :])
    o_ref[...] = x_ref[...] + acc


def _out_call(x2d, o_f, o_b, zmain, o_na, gn, w_out, *, tm):
    m, d = x2d.shape
    assert m % tm == 0
    return pl.pallas_call(
        _out_kernel,
        name="out_proj",
        grid=(m // tm,),
        in_specs=[
            pl.BlockSpec((tm, d), lambda i: (i, 0)),
            pl.BlockSpec((tm, GLA_WIDTH), lambda i: (i, 0)),
            pl.BlockSpec((tm, GLA_WIDTH), lambda i: (i, 0)),
            pl.BlockSpec((tm, GLA_WIDTH), lambda i: (i, COL_GG // GLA_WIDTH)),
            pl.BlockSpec((tm, NA_WIDTH), lambda i: (i, 0)),
            pl.BlockSpec((1, GLA_DV), lambda i: (0, 0)),
            pl.BlockSpec((GLA_WIDTH + NA_WIDTH, d), lambda i: (0, 0)),
        ],
        out_specs=pl.BlockSpec((tm, d), lambda i: (i, 0)),
        out_shape=jax.ShapeDtypeStruct((m, d), F32),
        compiler_params=pltpu.CompilerParams(
            dimension_semantics=("parallel",), vmem_limit_bytes=VMEM_LIMIT),
    )(x2d, o_f, o_b, zmain, o_na, gn, w_out)


def _pick(n, cands):
    for c in cands:
        if n % c == 0:
            return c
    raise ValueError(f"no tile for {n}")


def kernel(x, meta_tokens, norm_g, w_in, w_decay_fwd, b_decay_fwd, w_decay_bwd, b_decay_bwd,
           gla_out_norm_g, q_norm_g, k_norm_g, rpb, meta_bias, w_out):
    batch, seq, d = x.shape
    assert d == D_MODEL and seq % TILE == 0 and seq % GRID_W == 0
    depth = norm_g.shape[0]
    assert depth == 1
    l = 0
    x2d = x.reshape(batch * seq, d)
    g = norm_g[l].reshape(1, d)

    assert w_in.shape[1:] == (d, D_IN_PROJ)
    wt = jnp.swapaxes(w_in[l], 0, 1).astype(BF16)
    gain_qk = jnp.concatenate([jnp.tile(q_norm_g[l], NA_HEADS) * (NA_DH ** -0.5),
                               jnp.tile(k_norm_g[l], NA_HEADS)]).reshape(1, 2 * NA_WIDTH)

    zr = jnp.zeros((GLA_RANK, GLA_KEY_WIDTH), F32)
    ztail = jnp.zeros((LANES - 2 * GLA_RANK, GLA_KEY_WIDTH), F32)
    wd = jnp.stack([jnp.concatenate([w_decay_fwd[l], zr, ztail], axis=0),
                    jnp.concatenate([zr, w_decay_bwd[l], ztail], axis=0)]).astype(BF16)
    bias = jnp.stack([b_decay_fwd[l], b_decay_bwd[l]]).reshape(2, 1, GLA_KEY_WIDTH)
    tg, tg_m = _gla_constants()

    assert seq % (NA_WIN_H * GRID_W) == 0
    kgain_col = jnp.tile(k_norm_g[l], NA_HEADS).reshape(NA_WIDTH, 1)
    zmain, nq, nk_t, nv_cb, gtot = _proj_call(x2d, g, wt, gain_qk, kgain_col, wd, bias, tg,
                                              tm=NA_WIN_H * GRID_W, tokens=True, name="proj_tokens")
    zmain_m, _, nk_m, nv_m, _ = _proj_call(meta_tokens, g, wt, gain_qk, kgain_col, wd, bias, tg,
                                           tm=N_META, tokens=False, name="proj_meta")
    zmain_m_pad = jnp.pad(zmain_m, ((0, TILE - N_META), (0, 0)))

    o_f, o_b = _gla_call(zmain, gtot, zmain_m_pad, wd, bias, tg_m, batch=batch, seq=seq,
                         tokens_per_step=_pick(seq, (2048, 1024, 512, 256, 128)))

    rpb_pad = jnp.pad(rpb[l], ((0, 0), (0, 0), (0, 2 * NA_DH - rpb.shape[-1])))
    mb = meta_bias[l].reshape(NA_HEADS, 1, N_META)
    o_na = _na_call(nq, nk_t, nv_cb, zmain, nk_m, nv_m, rpb_pad, mb, batch=batch, seq=seq,
                    rows_per_step=_pick(seq // GRID_W, (128, 64, 32, 16)))

    out = _out_call(x2d, o_f, o_b, zmain, o_na, gla_out_norm_g[l].reshape(1, GLA_DV),
                    w_out[l].astype(BF16), tm=_pick(batch * seq, (512, 256, 128)))
    return out.reshape(batch, seq, d)
```

```python
import functools

import jax
import jax.numpy as jnp
import numpy as np
from jax import lax
from jax.experimental import pallas as pl
from jax.experimental.pallas import tpu as pltpu

F32 = jnp.float32
BF16 = jnp.bfloat16

D_MODEL = 1024
N_META = 16
GRID_W = 64
GLA_HEADS = 4
GLA_DK = 128
GLA_DV = 256
GLA_KEY_WIDTH = GLA_HEADS * GLA_DK
GLA_WIDTH = GLA_HEADS * GLA_DV
GLA_RANK = 16
GLA_TAU = 16.0
GLA_CHUNK = 64
NA_HEADS = 16
NA_DH = 64
NA_WIDTH = NA_HEADS * NA_DH
NA_WIN_H = 8
NA_WIN_W = 16
RMS_EPS = 1e-6
NEG_BIG = -1e30
LANES = 128

COL_GV, COL_GG, COL_NG, COL_GQ, COL_GK, COL_R = 0, 1024, 2048, 3072, 3584, 4096
MAIN_W = COL_R + LANES
COL_DIR0, DIR_W = 3072, 3 * 512
MAIN_TOK_W = COL_DIR0 + 2 * DIR_W
NA_CBW = 16
NA_CB = GRID_W // NA_CBW
IN_GQ, IN_GK, IN_GV, IN_R, IN_GG, IN_NQ, IN_NK, IN_NV, IN_NG = 0, 512, 1024, 2048, 2080, 3104, 4128, 5152, 6176
D_IN_PROJ = 7200
PROJ_TN = 512

TILE = 2 * GLA_CHUNK
NA_ROW_GROUP = 16
VMEM_LIMIT = 58 * 1024 * 1024


def _split_bf16(x):
    hi = x.astype(BF16)
    lo = (x - hi.astype(F32)).astype(BF16)
    return hi, lo


def _dot(a, b):
    return jnp.dot(a, b, preferred_element_type=F32)


def _dot_nt(a, b):
    return lax.dot_general(a, b, (((1,), (1,)), ((), ())), preferred_element_type=F32)


def _dot_tn(a, b):
    return lax.dot_general(a, b, (((0,), (0,)), ((), ())), preferred_element_type=F32)


def _log_sigmoid(x):
    return jnp.minimum(x, 0.0) - jnp.log(1.0 + jnp.exp(-jnp.abs(x)))


def _silu(x):
    return x * (1.0 / (1.0 + jnp.exp(-x)))


SHARED_SEGMENTS = ((COL_GV, IN_GV, 1024), (COL_GG, IN_GG, 1024), (COL_NG, IN_NG, 1024))
META_SEGMENTS = ((COL_GQ, IN_GQ, 512), (COL_GK, IN_GK, 512),
                 (COL_R, IN_R, LANES))


def _store_rows(o_ref, c0, y, permute):
    n = y.shape[1]
    if not permute:
        o_ref[:, c0:c0 + n] = y
        return
    for rl in range(y.shape[0] // GRID_W):
        for cb in range(NA_CB):
            src = rl * GRID_W + cb * NA_CBW
            o_ref[cb, rl * NA_CBW:(rl + 1) * NA_CBW, c0:c0 + n] = y[src:src + NA_CBW]


def _gla_log_decay(r, wd_ref, bdec_ref, d):
    return (_log_sigmoid(_dot(r, wd_ref[d]) + bdec_ref[d]) * (1.0 / GLA_TAU)).astype(BF16)


def _gla_operands_store(q, k, log_decay, d, t, tg_ref, zm_ref, gtot_ref):
    rows = slice(t * TILE, (t + 1) * TILE)
    g = _dot(tg_ref[d], log_decay[rows])
    col = COL_DIR0 + d * DIR_W
    tots = []
    for c in range(TILE // GLA_CHUNK):
        last = c * GLA_CHUNK + (GLA_CHUNK - 1 if d == 0 else 0)
        tot = g[last:last + 1]
        gtot_ref[(t * TILE) // GLA_CHUNK + c:(t * TILE) // GLA_CHUNK + c + 1,
                 d * GLA_KEY_WIDTH:(d + 1) * GLA_KEY_WIDTH] = tot
        tots.append(jnp.broadcast_to(tot, (GLA_CHUNK, GLA_KEY_WIDTH)))
    dd = jnp.concatenate(tots, axis=0) - g
    zm_ref[rows, col:col + 512] = (q[rows] * jnp.exp(g)).astype(BF16)
    zm_ref[rows, col + 512:col + 1024] = (k[rows] * jnp.exp(-g)).astype(BF16)
    zm_ref[rows, col + 1024:col + 1536] = (k[rows] * jnp.exp(dd)).astype(BF16)


def _proj_kernel(x_ref, g_ref, wt_ref, gain_ref, kgain_ref, bd_ref, wd_ref, bdec_ref, tg_ref,
                 zm_ref, nq_ref, nk_ref, nv_ref, gtot_ref, *, tokens):
    x = x_ref[...]
    ms = jnp.mean(x * x, axis=-1, keepdims=True)
    u = (x * lax.rsqrt(ms + RMS_EPS) * g_ref[...]).astype(BF16)
    tm = u.shape[0]
    half = PROJ_TN // 2

    def main_tile(dst, src, off, n):
        acc = _dot_nt(u, wt_ref[src + off:src + off + n, :])
        if dst == COL_GQ:
            acc = acc * (GLA_DK ** -0.5)
        zm_ref[:, dst + off:dst + off + n] = acc.astype(zm_ref.dtype)

    def nv_tile(off):
        acc = _dot_nt(u, wt_ref[IN_NV + off:IN_NV + off + PROJ_TN, :])
        _store_rows(nv_ref, off, acc.astype(nv_ref.dtype), tokens)

    def head_norm_rows(acc, goff):
        sq = (acc * acc).astype(BF16)
        ms = jnp.concatenate([_dot(sq[:, :half], bd_ref[...]), _dot(sq[:, half:], bd_ref[...])], axis=1)
        return (acc * lax.rsqrt(ms + RMS_EPS) * gain_ref[:, goff:goff + PROJ_TN]).astype(BF16)

    def nq_tile(off):
        acc = _dot_nt(u, wt_ref[IN_NQ + off:IN_NQ + off + PROJ_TN, :])
        nq_ref[:, off:off + PROJ_TN] = head_norm_rows(acc, off)

    def nk_rows_tile(off):
        acc = _dot_nt(u, wt_ref[IN_NK + off:IN_NK + off + PROJ_TN, :])
        nk_ref[:, off:off + PROJ_TN] = head_norm_rows(acc, NA_WIDTH + off)

    def nk_transposed_tile(off):
        u_cb = jnp.concatenate([u[rl * GRID_W + cb * NA_CBW:rl * GRID_W + (cb + 1) * NA_CBW]
                                for cb in range(NA_CB) for rl in range(tm // GRID_W)], axis=0)
        acc_t = _dot_nt(wt_ref[IN_NK + off:IN_NK + off + PROJ_TN, :], u_cb)
        for h in range(PROJ_TN // NA_DH):
            rows = slice(off + h * NA_DH, off + (h + 1) * NA_DH)
            blk = acc_t[h * NA_DH:(h + 1) * NA_DH]
            ms = jnp.sum(blk * blk, axis=0, keepdims=True) * (1.0 / NA_DH)
            y = (blk * lax.rsqrt(ms + RMS_EPS) * kgain_ref[rows, :]).astype(BF16)
            for cb in range(NA_CB):
                nk_ref[cb, rows, :] = y[:, cb * LANES:(cb + 1) * LANES]

    nk_tile = nk_transposed_tile if tokens else nk_rows_tile
    tiles = [functools.partial(nk_tile, off) for off in range(0, NA_WIDTH, PROJ_TN)]
    tiles += [functools.partial(nq_tile, off) for off in range(0, NA_WIDTH, PROJ_TN)]
    tiles += [functools.partial(nv_tile, off) for off in range(0, NA_WIDTH, PROJ_TN)]
    for dst, src, width in SHARED_SEGMENTS + (() if tokens else META_SEGMENTS):
        for off in range(0, width, PROJ_TN):
            tiles.append(functools.partial(main_tile, dst, src, off, min(PROJ_TN, width - off)))
    if not tokens:
        gtot_ref[...] = jnp.zeros_like(gtot_ref)
        for tile in tiles:
            tile()
        return

    r = _dot_nt(u, wt_ref[IN_R:IN_R + LANES, :]).astype(BF16)
    q = _dot_nt(u, wt_ref[IN_GQ:IN_GQ + GLA_KEY_WIDTH, :]) * (GLA_DK ** -0.5)
    k = _dot_nt(u, wt_ref[IN_GK:IN_GK + GLA_KEY_WIDTH, :])
    log_decay = [None, None]

    def decay_stage(d):
        log_decay[d] = _gla_log_decay(r, wd_ref, bdec_ref, d)

    stages = [functools.partial(decay_stage, 0), functools.partial(decay_stage, 1), None]
    stages += [functools.partial(lambda d, t: _gla_operands_store(q, k, log_decay[d], d, t, tg_ref, zm_ref, gtot_ref),
                                 d, t)
               for d in range(2) for t in range(tm // TILE)]
    for i, tile in enumerate(tiles):
        if i < len(stages) and stages[i] is not None:
            stages[i]()
        tile()
    assert len(stages) <= len(tiles)


def _proj_call(x2d, g, wt, gain_qk, kgain_col, wd, bdec, tg, *, tm, tokens, name):
    m, d = x2d.shape
    assert m % tm == 0 and wt.shape == (D_IN_PROJ, d)
    half = PROJ_TN // 2
    grp = jnp.arange(half) // NA_DH
    bd = jnp.where(grp[:, None] == grp[None, :], 1.0 / NA_DH, 0.0).astype(BF16)
    resident = lambda shape: pl.BlockSpec(shape, lambda i: (0,) * len(shape), pipeline_mode=pl.Buffered(1))
    plain = pl.BlockSpec((tm, NA_WIDTH), lambda i: (i, 0))
    plain_shape = jax.ShapeDtypeStruct((m, NA_WIDTH), BF16)
    if tokens:
        assert tm == NA_WIN_H * GRID_W and NA_CB * LANES == tm
        main_w = MAIN_TOK_W
        v_spec = pl.BlockSpec((NA_CB, tm // NA_CB, NA_WIDTH), lambda i: (0, i, 0))
        v_shape = jax.ShapeDtypeStruct((NA_CB, m // NA_CB, NA_WIDTH), BF16)
        k_spec = pl.BlockSpec((NA_CB, NA_WIDTH, LANES), lambda i: (i, 0, 0))
        k_shape = jax.ShapeDtypeStruct((m // LANES, NA_WIDTH, LANES), BF16)
        n_chunk_rows = tm // GLA_CHUNK
    else:
        main_w = MAIN_W
        v_spec, v_shape, k_spec, k_shape = plain, plain_shape, plain, plain_shape
        n_chunk_rows = 8
    gt_rows = (m // tm) * n_chunk_rows
    return pl.pallas_call(
        functools.partial(_proj_kernel, tokens=tokens),
        name=name,
        grid=(m // tm,),
        in_specs=[
            pl.BlockSpec((tm, d), lambda i: (i, 0)),
            resident((1, d)),
            resident((D_IN_PROJ, d)),
            resident((1, 2 * NA_WIDTH)),
            resident((NA_WIDTH, 1)),
            resident((half, half)),
            resident((2, LANES, GLA_KEY_WIDTH)),
            resident((2, 1, GLA_KEY_WIDTH)),
            resident((2, TILE, TILE)),
        ],
        out_specs=[pl.BlockSpec((tm, main_w), lambda i: (i, 0)), plain, k_spec, v_spec,
                   pl.BlockSpec((n_chunk_rows, 2 * GLA_KEY_WIDTH), lambda i: (i, 0))],
        out_shape=[jax.ShapeDtypeStruct((m, main_w), BF16), plain_shape, k_shape, v_shape,
                   jax.ShapeDtypeStruct((gt_rows, 2 * GLA_KEY_WIDTH), F32)],
        compiler_params=pltpu.CompilerParams(
            dimension_semantics=("parallel",), vmem_limit_bytes=VMEM_LIMIT),
    )(x2d, g, wt, gain_qk, kgain_col, bd, wd, bdec, tg)


def _gla_kernel(qf_ref, kif_ref, ksf_ref, vf_ref, gtf_ref, qb_ref, kib_ref, ksb_ref, vb_ref, gtb_ref,
                km_ref, vm_ref, rm_ref, wd_ref, b_ref, tgm_ref,
                of_ref, ob_ref, sf_ref, sb_ref, *, n_tiles):
    row = lax.broadcasted_iota(jnp.int32, (TILE, TILE), 0)
    col = lax.broadcasted_iota(jnp.int32, (TILE, TILE), 1)
    same = (row // GLA_CHUNK) == (col // GLA_CHUNK)
    masks = (same & (row >= col), same & (row <= col))

    @pl.when(pl.program_id(2) == 0)
    def _():
        rowm = lax.broadcasted_iota(jnp.int32, (TILE, GLA_DK), 0)
        lg = _log_sigmoid(_dot(rm_ref[...], wd_ref[...]) + b_ref[...]) * (1.0 / GLA_TAU)
        hi, lo = _split_bf16(jnp.where(rowm < N_META, lg, 0.0))
        d_m = _dot(tgm_ref[...], jnp.concatenate([hi, lo], axis=0))
        k_st = (km_ref[...].astype(F32) * jnp.exp(d_m)).astype(BF16)
        sf_ref[...] = _dot_tn(k_st, vm_ref[...])
        sb_ref[...] = jnp.zeros_like(sb_ref)

    items = []
    for t in range(n_tiles):
        items.append((0, t))
        items.append((1, n_tiles - 1 - t))
    refs = ((qf_ref, kif_ref, ksf_ref, vf_ref, of_ref), (qb_ref, kib_ref, ksb_ref, vb_ref, ob_ref))
    a_cols = (jnp.exp(jnp.transpose(gtf_ref[...])), jnp.exp(jnp.transpose(gtb_ref[...])))

    def rows_of(t):
        return slice(t * TILE, (t + 1) * TILE)

    amats = [jnp.where(masks[d], _dot_nt(refs[d][0][rows_of(t), :], refs[d][1][rows_of(t), :]), 0.0).astype(BF16)
             for d, t in items]

    def state_independent(idx):
        d, t = items[idx]
        v = refs[d][3][rows_of(t), :]
        k_st = refs[d][2][rows_of(t), :]
        u = [_dot_tn(k_st[h * GLA_CHUNK:(h + 1) * GLA_CHUNK], v[h * GLA_CHUNK:(h + 1) * GLA_CHUNK])
             for h in range(2)]
        return _dot(amats[idx], v), u

    state = [sf_ref[...], sb_ref[...]]
    ahead = state_independent(0)
    for idx, (d, t) in enumerate(items):
        o_intra, u = ahead
        if idx + 1 < len(items):
            ahead = state_independent(idx + 1)
        s = state[d]
        outs = [None, None]
        for h in ((0, 1) if d == 0 else (1, 0)):
            rows = slice(t * TILE + h * GLA_CHUNK, t * TILE + (h + 1) * GLA_CHUNK)
            outs[h] = o_intra[h * GLA_CHUNK:(h + 1) * GLA_CHUNK] + _dot(refs[d][0][rows, :], s.astype(BF16))
            chunk = 2 * t + h
            s = a_cols[d][:, chunk:chunk + 1] * s + u[h]
        state[d] = s
        refs[d][4][rows_of(t), :] = jnp.concatenate(outs, axis=0).astype(refs[d][4].dtype)
    sf_ref[...] = state[0]
    sb_ref[...] = state[1]


def _gla_constants():
    i = np.arange(TILE)
    same = (i[:, None] // GLA_CHUNK) == (i[None, :] // GLA_CHUNK)
    inc_f = same & (i[None, :] <= i[:, None])
    inc_b = same & (i[None, :] >= i[:, None])
    to_end_meta = i[None, :] > i[:, None]
    dup = lambda m: np.concatenate([m, m], axis=1)
    tg = jnp.asarray(np.stack([inc_f, inc_b]), BF16)
    return tg, jnp.asarray(dup(to_end_meta), BF16)


def _gla_call(zmain, gtot, zmain_m, wd, bias, tg_m, *, batch, seq, tokens_per_step):
    t_blk = tokens_per_step
    nb = seq // t_blk
    n_tiles = t_blk // TILE
    n_chunks = t_blk // GLA_CHUNK

    def tok(fwd):
        if fwd:
            return lambda b, hh, i: b * nb + i
        return lambda b, hh, i: b * nb + (nb - 1 - i)

    def dir_specs(d):
        t = tok(d == 0)
        c0 = (COL_DIR0 + d * DIR_W) // GLA_DK
        return [
            pl.BlockSpec((t_blk, GLA_DK), lambda b, hh, i: (t(b, hh, i), c0 + hh)),
            pl.BlockSpec((t_blk, GLA_DK), lambda b, hh, i: (t(b, hh, i), c0 + GLA_HEADS + hh)),
            pl.BlockSpec((t_blk, GLA_DK), lambda b, hh, i: (t(b, hh, i), c0 + 2 * GLA_HEADS + hh)),
            pl.BlockSpec((t_blk, GLA_DV), lambda b, hh, i: (t(b, hh, i), COL_GV // GLA_DV + hh)),
            pl.BlockSpec((n_chunks, GLA_DK), lambda b, hh, i: (t(b, hh, i), d * GLA_HEADS + hh)),
        ]

    in_specs = (
        dir_specs(0) + dir_specs(1)
        + [pl.BlockSpec((TILE, GLA_DK), lambda b, hh, i: (0, COL_GK // GLA_DK + hh)),
           pl.BlockSpec((TILE, GLA_DV), lambda b, hh, i: (0, COL_GV // GLA_DV + hh)),
           pl.BlockSpec((TILE, LANES), lambda b, hh, i: (0, COL_R // LANES)),
           pl.BlockSpec((None, LANES, GLA_DK), lambda b, hh, i: (0, 0, hh)),
           pl.BlockSpec((None, 1, GLA_DK), lambda b, hh, i: (0, 0, hh)),
           pl.BlockSpec((TILE, 2 * TILE), lambda b, hh, i: (0, 0))]
    )
    args = [zmain] * 4 + [gtot] + [zmain] * 4 + [gtot] + [zmain_m] * 3 + [wd, bias, tg_m]
    out_spec_f = pl.BlockSpec((t_blk, GLA_DV), lambda b, hh, i: (b * nb + i, hh))
    out_spec_b = pl.BlockSpec((t_blk, GLA_DV), lambda b, hh, i: (b * nb + (nb - 1 - i), hh))
    o_shape = jax.ShapeDtypeStruct((batch * seq, GLA_WIDTH), BF16)
    return pl.pallas_call(
        functools.partial(_gla_kernel, n_tiles=n_tiles),
        name="gla_scan",
        grid=(batch, GLA_HEADS, nb),
        in_specs=in_specs,
        out_specs=[out_spec_f, out_spec_b],
        out_shape=[o_shape, o_shape],
        scratch_shapes=[pltpu.VMEM((GLA_DK, GLA_DV), F32), pltpu.VMEM((GLA_DK, GLA_DV), F32)],
        compiler_params=pltpu.CompilerParams(
            dimension_semantics=("parallel", "parallel", "arbitrary"), vmem_limit_bytes=VMEM_LIMIT),
    )(*args)


NA_SEG_COLS = ((0, 24), (24, 40), (40, 64))
NA_SEG_ROW0 = (0, 48, 80)
NA_STACK = 2 * GRID_W


NA_BASE_MID = NA_WIN_H // 2 - 1
NA_VARIANTS = 2 * NA_WIN_H - 1


def _na_variant(base, a):
    return jnp.where(base == NA_BASE_MID, a, jnp.where(base < NA_BASE_MID, NA_WIN_H + base, NA_WIN_H - 1 + base))


def _na_build_bias(rpb_ref, tab_ref):
    def build(v, carry):
        a = jnp.where(v < NA_WIN_H, v, 0)
        base = jnp.where(v < NA_WIN_H, NA_BASE_MID,
                         jnp.where(v < NA_WIN_H + NA_BASE_MID, v - NA_WIN_H, v - (NA_WIN_H - 1)))
        for hh in range(2):
            bias_rows = [rpb_ref[hh, pl.ds(base + ((t - a + NA_WIN_H) & (NA_WIN_H - 1)), 1), :]
                         for t in range(NA_WIN_H)]
            for s, (c0, c1) in enumerate(NA_SEG_COLS):
                n = c1 - c0
                vrows = [jnp.broadcast_to(br, (n, LANES)) for br in bias_rows]
                row0 = NA_SEG_ROW0[s] + hh * n
                rho = lax.broadcasted_iota(jnp.int32, (n, LANES), 0)
                lane = lax.broadcasted_iota(jnp.int32, (n, LANES), 1)
                cs = jnp.clip(c0 + rho - NA_WIN_W // 2, 0, GRID_W - NA_WIN_W)
                for half in range(2):
                    cb = s + half
                    cp = NA_CBW * cb + lane % NA_CBW
                    valid = (cp >= cs) & (cp < cs + NA_WIN_W)
                    acc = None
                    for i in range(NA_WIN_H):
                        shift = (NA_CBW * i - NA_CBW * cb + c0 - (NA_WIN_W - 1)) % LANES
                        rolled = pltpu.roll(vrows[i], shift, 1, stride=1, stride_axis=0)
                        acc = rolled if acc is None else jnp.where(lane // NA_CBW == i, rolled, acc)
                    tab_ref[v, row0:row0 + n, half * LANES:(half + 1) * LANES] = jnp.where(valid, acc, NEG_BIG)
        return carry

    lax.fori_loop(0, NA_VARIANTS, build, 0)


def _na_kernel(q_ref, kt_ref, v_ref, gate_ref, km_ref, vm_ref, rpb_ref, mb_ref, o_ref, tab_ref, sc_ref, sm_ref,
               *, rows_per_step, n_rows):
    w = GRID_W

    @pl.when((pl.program_id(1) == 0) & (pl.program_id(2) == 0))
    def _():
        _na_build_bias(rpb_ref, tab_ref)

    head0_q = lax.broadcasted_iota(jnp.int32, (w, LANES), 1) < NA_DH
    srow = lax.broadcasted_iota(jnp.int32, (NA_STACK, N_META), 0)
    is_h1 = jnp.zeros((NA_STACK, N_META), jnp.bool_)
    for s, (c0, c1) in enumerate(NA_SEG_COLS):
        lo = NA_SEG_ROW0[s] + (c1 - c0)
        is_h1 = is_h1 | ((srow >= lo) & (srow < lo + (c1 - c0)))
    mb = jnp.where(is_h1, mb_ref[1], mb_ref[0])
    km = km_ref[...]
    vm = vm_ref[...]
    seg_rows = [slice(NA_SEG_ROW0[s], NA_SEG_ROW0[s] + 2 * (c1 - c0)) for s, (c0, c1) in enumerate(NA_SEG_COLS)]
    blk_rows = [slice(seg_rows[max(cb - 1, 0)].start, seg_rows[min(cb, len(seg_rows) - 1)].stop)
                for cb in range(NA_CB)]

    n_tiles = n_rows // NA_WIN_H
    key_lane = lax.broadcasted_iota(jnp.int32, (LANES, LANES), 1)

    def window(rl):
        r = pl.program_id(2) * rows_per_step + rl
        rs = jnp.clip(r - NA_WIN_H // 2, 0, n_rows - NA_WIN_H)
        a = rs & (NA_WIN_H - 1)
        return rs // NA_WIN_H, a, _na_variant(rs - r + (NA_WIN_H - 1), a)

    def scores_store(slot, j, rl):
        m, a, var = window(rl)
        q = q_ref[pl.ds(pl.multiple_of(rl * w, w), w), :].astype(F32)
        q0 = jnp.where(head0_q, q, 0.0)
        q1 = q - q0
        qs = jnp.concatenate([piece[c0:c1] for c0, c1 in NA_SEG_COLS for piece in (q0, q1)],
                             axis=0).astype(BF16)
        from_m = key_lane >= a * NA_CBW
        m1 = jnp.minimum(m + 1, n_tiles - 1)
        blocks = [_dot(qs[blk_rows[cb]],
                       jnp.where(from_m, kt_ref[NA_CB * m + cb], kt_ref[NA_CB * m1 + cb]))
                  for cb in range(NA_CB)]
        for s in range(len(NA_SEG_COLS)):
            n2 = seg_rows[s].stop - seg_rows[s].start
            for half in range(2):
                cb = s + half
                a0 = seg_rows[s].start - blk_rows[cb].start
                lanes = slice(half * LANES, (half + 1) * LANES)
                sc_ref[slot, j, seg_rows[s], lanes] = blocks[cb][a0:a0 + n2] + tab_ref[var, seg_rows[s], lanes]
        sm_ref[slot, j] = _dot_nt(qs, km) + mb

    def value_block(cb, m, a):
        groups = []
        for t in range(NA_WIN_H):
            row = NA_WIN_H * m + t + jnp.where(t < a, NA_WIN_H, 0)
            groups.append(v_ref[cb, pl.ds(pl.multiple_of(row * NA_CBW, NA_CBW), NA_CBW), :])
        return jnp.concatenate(groups, axis=0)

    def softmax_values(slot, j, rl):
        m_tile, a, _ = window(rl)
        sc = sc_ref[slot, j]
        sm = sm_ref[slot, j]
        m = jnp.maximum(sc[:, :LANES], sc[:, LANES:])
        m = jnp.maximum(jnp.max(m, axis=-1, keepdims=True), jnp.max(sm, axis=-1, keepdims=True))
        p = jnp.exp(sc - m)
        pm = jnp.exp(sm - m)
        den = (jnp.sum(p[:, :LANES] + p[:, LANES:], axis=-1, keepdims=True)
               + jnp.sum(pm, axis=-1, keepdims=True))
        p = p.astype(BF16)
        p_blk = [p[seg_rows[0], :LANES],
                 jnp.concatenate([p[seg_rows[0], LANES:], p[seg_rows[1], :LANES]], axis=0),
                 jnp.concatenate([p[seg_rows[1], LANES:], p[seg_rows[2], :LANES]], axis=0),
                 p[seg_rows[2], LANES:]]
        parts = [_dot(p_blk[cb], value_block(cb, m_tile, a)) for cb in range(NA_CB)]
        o_meta = _dot(pm.astype(BF16), vm)
        inv = 1.0 / den
        outs = []
        for s, (c0, c1) in enumerate(NA_SEG_COLS):
            a0 = seg_rows[s].start - blk_rows[s].start
            b0 = seg_rows[s].start - blk_rows[s + 1].start
            n2 = seg_rows[s].stop - seg_rows[s].start
            o = (parts[s][a0:a0 + n2] + parts[s + 1][b0:b0 + n2] + o_meta[seg_rows[s]]) * inv[seg_rows[s]]
            n = c1 - c0
            head0 = lax.broadcasted_iota(jnp.int32, (n, LANES), 1) < NA_DH
            outs.append(jnp.where(head0, o[:n], o[n:]))
        o = jnp.concatenate(outs, axis=0)
        qoff = pl.multiple_of(rl * w, w)
        gate = gate_ref[pl.ds(qoff, w), :].astype(F32)
        o_ref[pl.ds(qoff, w), :] = (o * _silu(gate)).astype(o_ref.dtype)

    n_groups = rows_per_step // NA_ROW_GROUP
    for j in range(NA_ROW_GROUP):
        scores_store(0, j, j)

    def trip(g, carry):
        slot = g % 2
        for j in range(NA_ROW_GROUP):
            scores_store(slot, j, g * NA_ROW_GROUP + j)
        for j in range(NA_ROW_GROUP):
            softmax_values(1 - slot, j, (g - 1) * NA_ROW_GROUP + j)
        return carry

    lax.fori_loop(1, n_groups, trip, 0)
    for j in range(NA_ROW_GROUP):
        softmax_values((n_groups - 1) % 2, j, (n_groups - 1) * NA_ROW_GROUP + j)


def _na_call(nq, nk_t, nv_cb, zmain, nk_m, nv_m, rpb_pad, meta_bias, *, batch, seq, rows_per_step):
    n_rows = seq // GRID_W
    assert n_rows % NA_WIN_H == 0 and n_rows % rows_per_step == 0 and rows_per_step % NA_ROW_GROUP == 0
    nsteps = n_rows // rows_per_step
    hp = NA_HEADS // 2
    tq = rows_per_step * GRID_W
    n_idx = 2 * NA_WIN_H - 1
    in_specs = [
        pl.BlockSpec((tq, LANES), lambda p, b, i: (b * nsteps + i, p)),
        pl.BlockSpec((seq // LANES, LANES, LANES), lambda p, b, i: (b, p, 0)),
        pl.BlockSpec((NA_CB, seq // NA_CB, LANES), lambda p, b, i: (0, b, p)),
        pl.BlockSpec((tq, LANES), lambda p, b, i: (b * nsteps + i, COL_NG // LANES + p)),
        pl.BlockSpec((N_META, LANES), lambda p, b, i: (0, p)),
        pl.BlockSpec((N_META, LANES), lambda p, b, i: (0, p)),
        pl.BlockSpec((2, n_idx, LANES), lambda p, b, i: (p, 0, 0)),
        pl.BlockSpec((2, 1, N_META), lambda p, b, i: (p, 0, 0)),
    ]
    return pl.pallas_call(
        functools.partial(_na_kernel, rows_per_step=rows_per_step, n_rows=n_rows),
        name="na_attn",
        grid=(hp, batch, nsteps),
        in_specs=in_specs,
        out_specs=pl.BlockSpec((tq, LANES), lambda p, b, i: (b * nsteps + i, p)),
        out_shape=jax.ShapeDtypeStruct((batch * seq, NA_WIDTH), BF16),
        scratch_shapes=[pltpu.VMEM((NA_VARIANTS, NA_STACK, 2 * LANES), F32),
                        pltpu.VMEM((2, NA_ROW_GROUP, NA_STACK, 2 * LANES), F32),
                        pltpu.VMEM((2, NA_ROW_GROUP, NA_STACK, N_META), F32)],
        compiler_params=pltpu.CompilerParams(
            dimension_semantics=("arbitrary", "arbitrary", "arbitrary"), vmem_limit_bytes=VMEM_LIMIT),
    )(nq, nk_t, nv_cb, zmain, nk_m, nv_m, rpb_pad, meta_bias)


def _out_kernel(x_ref, of_ref, ob_ref, gg_ref, na_ref, gn_ref, w_ref, o_ref):
    s = of_ref[...].astype(F32) + ob_ref[...].astype(F32)
    gate = gg_ref[...].astype(F32)
    gn = gn_ref[...]
    heads = []
    for h in range(GLA_HEADS):
        cols = slice(h * GLA_DV, (h + 1) * GLA_DV)
        sh = s[:, cols]
        ms = jnp.mean(sh * sh, axis=-1, keepdims=True)
        y = sh * lax.rsqrt(ms + RMS_EPS) * gn
        heads.append((y * _silu(gate[:, cols])).astype(BF16))
    o_gla = jnp.concatenate(heads, axis=1)
    acc = _dot(o_gla, w_ref[:GLA_WIDTH, :]) + _dot(na_ref[...], w_ref[GLA_WIDTH:, :])
    o_ref[...] = x_ref[...] + acc


def _out_call(x2d, o_f, o_b, zmain, o_na, gn, w_out, *, tm):
    m, d = x2d.shape
    assert m % tm == 0
    return pl.pallas_call(
        _out_kernel,
        name="out_proj",
        grid=(m // tm,),
        in_specs=[
            pl.BlockSpec((tm, d), lambda i: (i, 0)),
            pl.BlockSpec((tm, GLA_WIDTH), lambda i: (i, 0)),
            pl.BlockSpec((tm, GLA_WIDTH), lambda i: (i, 0)),
            pl.BlockSpec((tm, GLA_WIDTH), lambda i: (i, COL_GG // GLA_WIDTH)),
            pl.BlockSpec((tm, NA_WIDTH), lambda i: (i, 0)),
            pl.BlockSpec((1, GLA_DV), lambda i: (0, 0), pipeline_mode=pl.Buffered(1)),
            pl.BlockSpec((GLA_WIDTH + NA_WIDTH, d), lambda i: (0, 0), pipeline_mode=pl.Buffered(1)),
        ],
        out_specs=pl.BlockSpec((tm, d), lambda i: (i, 0)),
        out_shape=jax.ShapeDtypeStruct((m, d), F32),
        compiler_params=pltpu.CompilerParams(
            dimension_semantics=("parallel",), vmem_limit_bytes=VMEM_LIMIT),
    )(x2d, o_f, o_b, zmain, o_na, gn, w_out)


def _pick(n, cands):
    for c in cands:
        if n % c == 0:
            return c
    raise ValueError(f"no tile for {n}")


def kernel(x, meta_tokens, norm_g, w_in, w_decay_fwd, b_decay_fwd, w_decay_bwd, b_decay_bwd,
           gla_out_norm_g, q_norm_g, k_norm_g, rpb, meta_bias, w_out):
    batch, seq, d = x.shape
    assert d == D_MODEL and seq % TILE == 0 and seq % GRID_W == 0
    depth = norm_g.shape[0]
    assert depth == 1
    l = 0
    x2d = x.reshape(batch * seq, d)
    g = norm_g[l].reshape(1, d)

    assert w_in.shape[1:] == (d, D_IN_PROJ)
    wt = jnp.swapaxes(w_in[l], 0, 1).astype(BF16)
    gain_qk = jnp.concatenate([jnp.tile(q_norm_g[l], NA_HEADS) * (NA_DH ** -0.5),
                               jnp.tile(k_norm_g[l], NA_HEADS)]).reshape(1, 2 * NA_WIDTH)

    zr = jnp.zeros((GLA_RANK, GLA_KEY_WIDTH), F32)
    ztail = jnp.zeros((LANES - 2 * GLA_RANK, GLA_KEY_WIDTH), F32)
    wd = jnp.stack([jnp.concatenate([w_decay_fwd[l], zr, ztail], axis=0),
                    jnp.concatenate([zr, w_decay_bwd[l], ztail], axis=0)]).astype(BF16)
    bias = jnp.stack([b_decay_fwd[l], b_decay_bwd[l]]).reshape(2, 1, GLA_KEY_WIDTH)
    tg, tg_m = _gla_constants()

    assert seq % (NA_WIN_H * GRID_W) == 0
    kgain_col = jnp.tile(k_norm_g[l], NA_HEADS).reshape(NA_WIDTH, 1)
    zmain, nq, nk_t, nv_cb, gtot = _proj_call(x2d, g, wt, gain_qk, kgain_col, wd, bias, tg,
                                              tm=NA_WIN_H * GRID_W, tokens=True, name="proj_tokens")
    zmain_m, _, nk_m, nv_m, _ = _proj_call(meta_tokens, g, wt, gain_qk, kgain_col, wd, bias, tg,
                                           tm=N_META, tokens=False, name="proj_meta")
    zmain_m_pad = jnp.pad(zmain_m, ((0, TILE - N_META), (0, 0)))

    o_f, o_b = _gla_call(zmain, gtot, zmain_m_pad, wd, bias, tg_m, batch=batch, seq=seq,
                         tokens_per_step=_pick(seq, (4096, 2048, 1024, 512, 256, 128)))

    rpb_pad = jnp.pad(rpb[l], ((0, 0), (0, 0), (0, 2 * NA_DH - rpb.shape[-1])))
    mb = meta_bias[l].reshape(NA_HEADS, 1, N_META)
    o_na = _na_call(nq, nk_t, nv_cb, zmain, nk_m, nv_m, rpb_pad, mb, batch=batch, seq=seq,
                    rows_per_step=_pick(seq // GRID_W, (128, 64, 32, 16)))

    out = _out_call(x2d, o_f, o_b, zmain, o_na, gla_out_norm_g[l].reshape(1, GLA_DV),
                    w_out[l].astype(BF16), tm=_pick(batch * seq, (1024, 512, 256, 128)))
    return out.reshape(batch, seq, d)
```

```python
import functools

import jax
import jax.numpy as jnp
import numpy as np
from jax import lax
from jax.experimental import pallas as pl
from jax.experimental.pallas import tpu as pltpu

F32 = jnp.float32
BF16 = jnp.bfloat16

D_MODEL = 1024
N_META = 16
GRID_W = 64
GLA_HEADS = 4
GLA_DK = 128
GLA_DV = 256
GLA_KEY_WIDTH = GLA_HEADS * GLA_DK
GLA_WIDTH = GLA_HEADS * GLA_DV
GLA_RANK = 16
GLA_TAU = 16.0
GLA_CHUNK = 64
NA_HEADS = 16
NA_DH = 64
NA_WIDTH = NA_HEADS * NA_DH
NA_WIN_H = 8
NA_WIN_W = 16
RMS_EPS = 1e-6
NEG_BIG = -1e30
LANES = 128

COL_GV, COL_GG, COL_NG, COL_GQ, COL_GK, COL_R = 0, 1024, 2048, 3072, 3584, 4096
MAIN_W = COL_R + LANES
COL_DIR0, DIR_W = 3072, 3 * 512
MAIN_TOK_W = COL_DIR0 + 2 * DIR_W
NA_CBW = 16
NA_CB = GRID_W // NA_CBW
IN_GQ, IN_GK, IN_GV, IN_R, IN_GG, IN_NQ, IN_NK, IN_NV, IN_NG = 0, 512, 1024, 2048, 2080, 3104, 4128, 5152, 6176
D_IN_PROJ = 7200
PROJ_TN = 512

TILE = 2 * GLA_CHUNK
NA_ROW_GROUP = 16
VMEM_LIMIT = 58 * 1024 * 1024


def _split_bf16(x):
    hi = x.astype(BF16)
    lo = (x - hi.astype(F32)).astype(BF16)
    return hi, lo


def _dot(a, b):
    return jnp.dot(a, b, preferred_element_type=F32)


def _dot_nt(a, b):
    return lax.dot_general(a, b, (((1,), (1,)), ((), ())), preferred_element_type=F32)


def _dot_tn(a, b):
    return lax.dot_general(a, b, (((0,), (0,)), ((), ())), preferred_element_type=F32)


def _log_sigmoid(x):
    return jnp.minimum(x, 0.0) - jnp.log(1.0 + jnp.exp(-jnp.abs(x)))


def _silu(x):
    return x * (1.0 / (1.0 + jnp.exp(-x)))


SHARED_SEGMENTS = ((COL_GV, IN_GV, 1024), (COL_GG, IN_GG, 1024), (COL_NG, IN_NG, 1024))
META_SEGMENTS = ((COL_GQ, IN_GQ, 512), (COL_GK, IN_GK, 512),
                 (COL_R, IN_R, LANES))


def _store_rows(o_ref, c0, y, permute):
    n = y.shape[1]
    if not permute:
        o_ref[:, c0:c0 + n] = y
        return
    for rl in range(y.shape[0] // GRID_W):
        for cb in range(NA_CB):
            src = rl * GRID_W + cb * NA_CBW
            o_ref[cb, rl * NA_CBW:(rl + 1) * NA_CBW, c0:c0 + n] = y[src:src + NA_CBW]


def _gla_log_decay(r, wd_ref, bdec_ref, d):
    return (_log_sigmoid(_dot(r, wd_ref[d]) + bdec_ref[d]) * (1.0 / GLA_TAU)).astype(BF16)


def _gla_operands_store(q, k, log_decay, d, t, tg_ref, zm_ref, gtot_ref):
    rows = slice(t * TILE, (t + 1) * TILE)
    g = _dot(tg_ref[d], log_decay[rows])
    col = COL_DIR0 + d * DIR_W
    tots = []
    for c in range(TILE // GLA_CHUNK):
        last = c * GLA_CHUNK + (GLA_CHUNK - 1 if d == 0 else 0)
        tot = g[last:last + 1]
        gtot_ref[(t * TILE) // GLA_CHUNK + c:(t * TILE) // GLA_CHUNK + c + 1,
                 d * GLA_KEY_WIDTH:(d + 1) * GLA_KEY_WIDTH] = tot
        tots.append(jnp.broadcast_to(tot, (GLA_CHUNK, GLA_KEY_WIDTH)))
    dd = jnp.concatenate(tots, axis=0) - g
    zm_ref[rows, col:col + 512] = (q[rows] * jnp.exp(g)).astype(BF16)
    zm_ref[rows, col + 512:col + 1024] = (k[rows] * jnp.exp(-g)).astype(BF16)
    zm_ref[rows, col + 1024:col + 1536] = (k[rows] * jnp.exp(dd)).astype(BF16)


def _proj_kernel(x0_ref, xn_ref, g_ref, wt_ref, gain_ref, kgain_ref, wd_ref, bdec_ref, tg_ref,
                 zm_ref, nq_ref, nk_ref, nv_ref, gtot_ref, u_scr, *, tokens):
    def normed(x_ref):
        x = x_ref[...]
        ms = jnp.mean(x * x, axis=-1, keepdims=True)
        return (x * lax.rsqrt(ms + RMS_EPS) * g_ref[...]).astype(BF16)

    step = pl.program_id(0)

    @pl.when(step == 0)
    def _():
        u_scr[0] = normed(x0_ref)

    slot = step % 2
    u = u_scr[slot]
    tm = u.shape[0]

    def prepare_next():
        u_scr[1 - slot] = normed(xn_ref)

    def main_tile(dst, src, off, n):
        acc = _dot_nt(u, wt_ref[src + off:src + off + n, :])
        if dst == COL_GQ:
            acc = acc * (GLA_DK ** -0.5)
        zm_ref[:, dst + off:dst + off + n] = acc.astype(zm_ref.dtype)

    def nv_tile(off):
        acc = _dot_nt(u, wt_ref[IN_NV + off:IN_NV + off + PROJ_TN, :])
        _store_rows(nv_ref, off, acc.astype(nv_ref.dtype), tokens)

    def head_norm_rows(acc, goff):
        first = lax.broadcasted_iota(jnp.int32, (acc.shape[0], LANES), 1) < NA_DH
        cols = []
        for c in range(0, acc.shape[1], LANES):
            a = acc[:, c:c + LANES]
            sq = a * a
            lo = jnp.sum(jnp.where(first, sq, 0.0), axis=-1, keepdims=True)
            hi = jnp.sum(jnp.where(first, 0.0, sq), axis=-1, keepdims=True)
            inv = jnp.where(first, lax.rsqrt(lo * (1.0 / NA_DH) + RMS_EPS), lax.rsqrt(hi * (1.0 / NA_DH) + RMS_EPS))
            cols.append(a * inv)
        return (jnp.concatenate(cols, axis=1) * gain_ref[:, goff:goff + PROJ_TN]).astype(BF16)

    def nq_tile(off):
        acc = _dot_nt(u, wt_ref[IN_NQ + off:IN_NQ + off + PROJ_TN, :])
        nq_ref[:, off:off + PROJ_TN] = head_norm_rows(acc, off)

    def nk_rows_tile(off):
        acc = _dot_nt(u, wt_ref[IN_NK + off:IN_NK + off + PROJ_TN, :])
        nk_ref[:, off:off + PROJ_TN] = head_norm_rows(acc, NA_WIDTH + off)

    def nk_transposed_tile(off):
        u_cb = jnp.concatenate([u[rl * GRID_W + cb * NA_CBW:rl * GRID_W + (cb + 1) * NA_CBW]
                                for cb in range(NA_CB) for rl in range(tm // GRID_W)], axis=0)
        acc_t = _dot_nt(wt_ref[IN_NK + off:IN_NK + off + PROJ_TN, :], u_cb)
        for h in range(PROJ_TN // NA_DH):
            rows = slice(off + h * NA_DH, off + (h + 1) * NA_DH)
            blk = acc_t[h * NA_DH:(h + 1) * NA_DH]
            ms = jnp.sum(blk * blk, axis=0, keepdims=True) * (1.0 / NA_DH)
            y = (blk * lax.rsqrt(ms + RMS_EPS) * kgain_ref[rows, :]).astype(BF16)
            for cb in range(NA_CB):
                nk_ref[cb, rows, :] = y[:, cb * LANES:(cb + 1) * LANES]

    nk_tile = nk_transposed_tile if tokens else nk_rows_tile
    tiles = [functools.partial(nk_tile, off) for off in range(0, NA_WIDTH, PROJ_TN)]
    tiles += [functools.partial(nq_tile, off) for off in range(0, NA_WIDTH, PROJ_TN)]
    tiles += [functools.partial(nv_tile, off) for off in range(0, NA_WIDTH, PROJ_TN)]
    for dst, src, width in SHARED_SEGMENTS + (() if tokens else META_SEGMENTS):
        for off in range(0, width, PROJ_TN):
            tiles.append(functools.partial(main_tile, dst, src, off, min(PROJ_TN, width - off)))
    if not tokens:
        gtot_ref[...] = jnp.zeros_like(gtot_ref)
        for tile in tiles:
            tile()
        prepare_next()
        return

    r = _dot_nt(u, wt_ref[IN_R:IN_R + LANES, :]).astype(BF16)
    q = _dot_nt(u, wt_ref[IN_GQ:IN_GQ + GLA_KEY_WIDTH, :]) * (GLA_DK ** -0.5)
    k = _dot_nt(u, wt_ref[IN_GK:IN_GK + GLA_KEY_WIDTH, :])
    log_decay = [None, None]

    def decay_stage(d):
        log_decay[d] = _gla_log_decay(r, wd_ref, bdec_ref, d)

    stages = [functools.partial(decay_stage, 0), functools.partial(decay_stage, 1), None]
    stages += [functools.partial(lambda d, t: _gla_operands_store(q, k, log_decay[d], d, t, tg_ref, zm_ref, gtot_ref),
                                 d, t)
               for d in range(2) for t in range(tm // TILE)]
    for i, tile in enumerate(tiles):
        if i < len(stages) and stages[i] is not None:
            stages[i]()
        tile()
    prepare_next()
    assert len(stages) <= len(tiles)


def _proj_call(x2d, g, wt, gain_qk, kgain_col, wd, bdec, tg, *, tm, tokens, name):
    m, d = x2d.shape
    assert m % tm == 0 and wt.shape == (D_IN_PROJ, d)
    resident = lambda shape: pl.BlockSpec(shape, lambda i: (0,) * len(shape), pipeline_mode=pl.Buffered(1))
    plain = pl.BlockSpec((tm, NA_WIDTH), lambda i: (i, 0))
    plain_shape = jax.ShapeDtypeStruct((m, NA_WIDTH), BF16)
    if tokens:
        assert tm == NA_WIN_H * GRID_W and NA_CB * LANES == tm
        main_w = MAIN_TOK_W
        v_spec = pl.BlockSpec((NA_CB, tm // NA_CB, NA_WIDTH), lambda i: (0, i, 0))
        v_shape = jax.ShapeDtypeStruct((NA_CB, m // NA_CB, NA_WIDTH), BF16)
        k_spec = pl.BlockSpec((NA_CB, NA_WIDTH, LANES), lambda i: (i, 0, 0))
        k_shape = jax.ShapeDtypeStruct((m // LANES, NA_WIDTH, LANES), BF16)
        n_chunk_rows = tm // GLA_CHUNK
    else:
        main_w = MAIN_W
        v_spec, v_shape, k_spec, k_shape = plain, plain_shape, plain, plain_shape
        n_chunk_rows = 8
    gt_rows = (m // tm) * n_chunk_rows
    return pl.pallas_call(
        functools.partial(_proj_kernel, tokens=tokens),
        name=name,
        grid=(m // tm,),
        in_specs=[
            resident((tm, d)),
            pl.BlockSpec((tm, d), lambda i: (jnp.minimum(i + 1, m // tm - 1), 0)),
            resident((1, d)),
            resident((D_IN_PROJ, d)),
            resident((1, 2 * NA_WIDTH)),
            resident((NA_WIDTH, 1)),
            resident((2, LANES, GLA_KEY_WIDTH)),
            resident((2, 1, GLA_KEY_WIDTH)),
            resident((2, TILE, TILE)),
        ],
        out_specs=[pl.BlockSpec((tm, main_w), lambda i: (i, 0)), plain, k_spec, v_spec,
                   pl.BlockSpec((n_chunk_rows, 2 * GLA_KEY_WIDTH), lambda i: (i, 0))],
        out_shape=[jax.ShapeDtypeStruct((m, main_w), BF16), plain_shape, k_shape, v_shape,
                   jax.ShapeDtypeStruct((gt_rows, 2 * GLA_KEY_WIDTH), F32)],
        scratch_shapes=[pltpu.VMEM((2, tm, d), BF16)],
        compiler_params=pltpu.CompilerParams(
            dimension_semantics=("arbitrary",), vmem_limit_bytes=VMEM_LIMIT),
    )(x2d, x2d, g, wt, gain_qk, kgain_col, wd, bdec, tg)


def _gla_kernel(qf_ref, kif_ref, ksf_ref, vf_ref, gtf_ref, qb_ref, kib_ref, ksb_ref, vb_ref, gtb_ref,
                km_ref, vm_ref, rm_ref, wd_ref, b_ref, tgm_ref,
                of_ref, ob_ref, sf_ref, sb_ref, *, n_tiles):
    row = lax.broadcasted_iota(jnp.int32, (TILE, TILE), 0)
    col = lax.broadcasted_iota(jnp.int32, (TILE, TILE), 1)
    same = (row // GLA_CHUNK) == (col // GLA_CHUNK)
    masks = (same & (row >= col), same & (row <= col))

    @pl.when(pl.program_id(2) == 0)
    def _():
        rowm = lax.broadcasted_iota(jnp.int32, (TILE, GLA_DK), 0)
        lg = _log_sigmoid(_dot(rm_ref[...], wd_ref[...]) + b_ref[...]) * (1.0 / GLA_TAU)
        hi, lo = _split_bf16(jnp.where(rowm < N_META, lg, 0.0))
        d_m = _dot(tgm_ref[...], jnp.concatenate([hi, lo], axis=0))
        k_st = (km_ref[...].astype(F32) * jnp.exp(d_m)).astype(BF16)
        sf_ref[...] = _dot_tn(k_st, vm_ref[...])
        sb_ref[...] = jnp.zeros_like(sb_ref)

    items = []
    for t in range(n_tiles):
        items.append((0, t))
        items.append((1, n_tiles - 1 - t))
    refs = ((qf_ref, kif_ref, ksf_ref, vf_ref, of_ref), (qb_ref, kib_ref, ksb_ref, vb_ref, ob_ref))
    a_cols = (jnp.exp(jnp.transpose(gtf_ref[...])), jnp.exp(jnp.transpose(gtb_ref[...])))

    def rows_of(t):
        return slice(t * TILE, (t + 1) * TILE)

    amats = [jnp.where(masks[d], _dot_nt(refs[d][0][rows_of(t), :], refs[d][1][rows_of(t), :]), 0.0).astype(BF16)
             for d, t in items]

    def state_independent(idx):
        d, t = items[idx]
        v = refs[d][3][rows_of(t), :]
        k_st = refs[d][2][rows_of(t), :]
        u = [_dot_tn(k_st[h * GLA_CHUNK:(h + 1) * GLA_CHUNK], v[h * GLA_CHUNK:(h + 1) * GLA_CHUNK])
             for h in range(2)]
        return _dot(amats[idx], v), u

    state = [sf_ref[...], sb_ref[...]]
    ahead = state_independent(0)
    for idx, (d, t) in enumerate(items):
        o_intra, u = ahead
        if idx + 1 < len(items):
            ahead = state_independent(idx + 1)
        s = state[d]
        outs = [None, None]
        for h in ((0, 1) if d == 0 else (1, 0)):
            rows = slice(t * TILE + h * GLA_CHUNK, t * TILE + (h + 1) * GLA_CHUNK)
            outs[h] = o_intra[h * GLA_CHUNK:(h + 1) * GLA_CHUNK] + _dot(refs[d][0][rows, :], s.astype(BF16))
            chunk = 2 * t + h
            s = a_cols[d][:, chunk:chunk + 1] * s + u[h]
        state[d] = s
        refs[d][4][rows_of(t), :] = jnp.concatenate(outs, axis=0).astype(refs[d][4].dtype)
    sf_ref[...] = state[0]
    sb_ref[...] = state[1]


def _gla_constants():
    i = np.arange(TILE)
    same = (i[:, None] // GLA_CHUNK) == (i[None, :] // GLA_CHUNK)
    inc_f = same & (i[None, :] <= i[:, None])
    inc_b = same & (i[None, :] >= i[:, None])
    to_end_meta = i[None, :] > i[:, None]
    dup = lambda m: np.concatenate([m, m], axis=1)
    tg = jnp.asarray(np.stack([inc_f, inc_b]), BF16)
    return tg, jnp.asarray(dup(to_end_meta), BF16)


def _gla_call(zmain, gtot, zmain_m, wd, bias, tg_m, *, batch, seq, tokens_per_step):
    t_blk = tokens_per_step
    nb = seq // t_blk
    n_tiles = t_blk // TILE
    n_chunks = t_blk // GLA_CHUNK

    def tok(fwd):
        if fwd:
            return lambda b, hh, i: b * nb + i
        return lambda b, hh, i: b * nb + (nb - 1 - i)

    def dir_specs(d):
        t = tok(d == 0)
        c0 = (COL_DIR0 + d * DIR_W) // GLA_DK
        return [
            pl.BlockSpec((t_blk, GLA_DK), lambda b, hh, i: (t(b, hh, i), c0 + hh)),
            pl.BlockSpec((t_blk, GLA_DK), lambda b, hh, i: (t(b, hh, i), c0 + GLA_HEADS + hh)),
            pl.BlockSpec((t_blk, GLA_DK), lambda b, hh, i: (t(b, hh, i), c0 + 2 * GLA_HEADS + hh)),
            pl.BlockSpec((t_blk, GLA_DV), lambda b, hh, i: (t(b, hh, i), COL_GV // GLA_DV + hh)),
            pl.BlockSpec((n_chunks, GLA_DK), lambda b, hh, i: (t(b, hh, i), d * GLA_HEADS + hh)),
        ]

    in_specs = (
        dir_specs(0) + dir_specs(1)
        + [pl.BlockSpec((TILE, GLA_DK), lambda b, hh, i: (0, COL_GK // GLA_DK + hh)),
           pl.BlockSpec((TILE, GLA_DV), lambda b, hh, i: (0, COL_GV // GLA_DV + hh)),
           pl.BlockSpec((TILE, LANES), lambda b, hh, i: (0, COL_R // LANES)),
           pl.BlockSpec((None, LANES, GLA_DK), lambda b, hh, i: (0, 0, hh)),
           pl.BlockSpec((None, 1, GLA_DK), lambda b, hh, i: (0, 0, hh)),
           pl.BlockSpec((TILE, 2 * TILE), lambda b, hh, i: (0, 0))]
    )
    args = [zmain] * 4 + [gtot] + [zmain] * 4 + [gtot] + [zmain_m] * 3 + [wd, bias, tg_m]
    out_spec_f = pl.BlockSpec((t_blk, GLA_DV), lambda b, hh, i: (b * nb + i, hh))
    out_spec_b = pl.BlockSpec((t_blk, GLA_DV), lambda b, hh, i: (b * nb + (nb - 1 - i), hh))
    o_shape = jax.ShapeDtypeStruct((batch * seq, GLA_WIDTH), BF16)
    return pl.pallas_call(
        functools.partial(_gla_kernel, n_tiles=n_tiles),
        name="gla_scan",
        grid=(batch, GLA_HEADS, nb),
        in_specs=in_specs,
        out_specs=[out_spec_f, out_spec_b],
        out_shape=[o_shape, o_shape],
        scratch_shapes=[pltpu.VMEM((GLA_DK, GLA_DV), F32), pltpu.VMEM((GLA_DK, GLA_DV), F32)],
        compiler_params=pltpu.CompilerParams(
            dimension_semantics=("parallel", "parallel", "arbitrary"), vmem_limit_bytes=VMEM_LIMIT),
    )(*args)


NA_SEG_COLS = ((0, 24), (24, 40), (40, 64))
NA_SEG_ROW0 = (0, 48, 80)
NA_STACK = 2 * GRID_W


NA_BASE_MID = NA_WIN_H // 2 - 1
NA_VARIANTS = 2 * NA_WIN_H - 1


def _na_variant(base, a):
    return jnp.where(base == NA_BASE_MID, a, jnp.where(base < NA_BASE_MID, NA_WIN_H + base, NA_WIN_H - 1 + base))


def _na_build_bias(rpb_ref, tab_ref):
    def build(v, carry):
        a = jnp.where(v < NA_WIN_H, v, 0)
        base = jnp.where(v < NA_WIN_H, NA_BASE_MID,
                         jnp.where(v < NA_WIN_H + NA_BASE_MID, v - NA_WIN_H, v - (NA_WIN_H - 1)))
        for hh in range(2):
            bias_rows = [rpb_ref[hh, pl.ds(base + ((t - a + NA_WIN_H) & (NA_WIN_H - 1)), 1), :]
                         for t in range(NA_WIN_H)]
            for s, (c0, c1) in enumerate(NA_SEG_COLS):
                n = c1 - c0
                vrows = [jnp.broadcast_to(br, (n, LANES)) for br in bias_rows]
                row0 = NA_SEG_ROW0[s] + hh * n
                rho = lax.broadcasted_iota(jnp.int32, (n, LANES), 0)
                lane = lax.broadcasted_iota(jnp.int32, (n, LANES), 1)
                cs = jnp.clip(c0 + rho - NA_WIN_W // 2, 0, GRID_W - NA_WIN_W)
                for half in range(2):
                    cb = s + half
                    cp = NA_CBW * cb + lane % NA_CBW
                    valid = (cp >= cs) & (cp < cs + NA_WIN_W)
                    acc = None
                    for i in range(NA_WIN_H):
                        shift = (NA_CBW * i - NA_CBW * cb + c0 - (NA_WIN_W - 1)) % LANES
                        rolled = pltpu.roll(vrows[i], shift, 1, stride=1, stride_axis=0)
                        acc = rolled if acc is None else jnp.where(lane // NA_CBW == i, rolled, acc)
                    tab_ref[v, row0:row0 + n, half * LANES:(half + 1) * LANES] = jnp.where(valid, acc, NEG_BIG)
        return carry

    lax.fori_loop(0, NA_VARIANTS, build, 0)


def _na_kernel(q_ref, kt_ref, v_ref, gate_ref, km_ref, vm_ref, rpb_ref, mb_ref, o_ref, tab_ref, sc_ref, sm_ref,
               *, rows_per_step, n_rows):
    w = GRID_W

    @pl.when((pl.program_id(1) == 0) & (pl.program_id(2) == 0))
    def _():
        _na_build_bias(rpb_ref, tab_ref)

    head0_q = lax.broadcasted_iota(jnp.int32, (w, LANES), 1) < NA_DH
    srow = lax.broadcasted_iota(jnp.int32, (NA_STACK, N_META), 0)
    is_h1 = jnp.zeros((NA_STACK, N_META), jnp.bool_)
    for s, (c0, c1) in enumerate(NA_SEG_COLS):
        lo = NA_SEG_ROW0[s] + (c1 - c0)
        is_h1 = is_h1 | ((srow >= lo) & (srow < lo + (c1 - c0)))
    mb = jnp.where(is_h1, mb_ref[1], mb_ref[0])
    km = km_ref[...]
    vm = vm_ref[...]
    seg_rows = [slice(NA_SEG_ROW0[s], NA_SEG_ROW0[s] + 2 * (c1 - c0)) for s, (c0, c1) in enumerate(NA_SEG_COLS)]
    blk_rows = [slice(seg_rows[max(cb - 1, 0)].start, seg_rows[min(cb, len(seg_rows) - 1)].stop)
                for cb in range(NA_CB)]

    n_tiles = n_rows // NA_WIN_H
    key_lane = lax.broadcasted_iota(jnp.int32, (LANES, LANES), 1)

    def window(rl):
        r = pl.program_id(2) * rows_per_step + rl
        rs = jnp.clip(r - NA_WIN_H // 2, 0, n_rows - NA_WIN_H)
        a = rs & (NA_WIN_H - 1)
        return rs // NA_WIN_H, a, _na_variant(rs - r + (NA_WIN_H - 1), a)

    def scores_store(slot, j, rl):
        m, a, var = window(rl)
        q = q_ref[pl.ds(pl.multiple_of(rl * w, w), w), :].astype(F32)
        q0 = jnp.where(head0_q, q, 0.0)
        q1 = q - q0
        qs = jnp.concatenate([piece[c0:c1] for c0, c1 in NA_SEG_COLS for piece in (q0, q1)],
                             axis=0).astype(BF16)
        from_m = key_lane >= a * NA_CBW
        m1 = jnp.minimum(m + 1, n_tiles - 1)
        blocks = [_dot(qs[blk_rows[cb]],
                       jnp.where(from_m, kt_ref[NA_CB * m + cb], kt_ref[NA_CB * m1 + cb]))
                  for cb in range(NA_CB)]
        for s in range(len(NA_SEG_COLS)):
            n2 = seg_rows[s].stop - seg_rows[s].start
            for half in range(2):
                cb = s + half
                a0 = seg_rows[s].start - blk_rows[cb].start
                lanes = slice(half * LANES, (half + 1) * LANES)
                sc_ref[slot, j, seg_rows[s], lanes] = blocks[cb][a0:a0 + n2] + tab_ref[var, seg_rows[s], lanes]
        sm_ref[slot, j] = _dot_nt(qs, km) + mb

    def value_block(cb, m, a):
        groups = []
        for t in range(NA_WIN_H):
            row = NA_WIN_H * m + t + jnp.where(t < a, NA_WIN_H, 0)
            groups.append(v_ref[cb, pl.ds(pl.multiple_of(row * NA_CBW, NA_CBW), NA_CBW), :])
        return jnp.concatenate(groups, axis=0)

    def softmax_values(slot, j, rl):
        m_tile, a, _ = window(rl)
        sc = sc_ref[slot, j]
        sm = sm_ref[slot, j]
        m = jnp.maximum(sc[:, :LANES], sc[:, LANES:])
        m = jnp.maximum(jnp.max(m, axis=-1, keepdims=True), jnp.max(sm, axis=-1, keepdims=True))
        p = jnp.exp(sc - m)
        pm = jnp.exp(sm - m)
        den = (jnp.sum(p[:, :LANES] + p[:, LANES:], axis=-1, keepdims=True)
               + jnp.sum(pm, axis=-1, keepdims=True))
        p = p.astype(BF16)
        vals = [value_block(cb, m_tile, a) for cb in range(NA_CB)]
        o_meta = _dot(pm.astype(BF16), vm)
        inv = 1.0 / den
        outs = []
        for s, (c0, c1) in enumerate(NA_SEG_COLS):
            o_loc = _dot(p[seg_rows[s]], jnp.concatenate([vals[s], vals[s + 1]], axis=0))
            o = (o_loc + o_meta[seg_rows[s]]) * inv[seg_rows[s]]
            n = c1 - c0
            head0 = lax.broadcasted_iota(jnp.int32, (n, LANES), 1) < NA_DH
            outs.append(jnp.where(head0, o[:n], o[n:]))
        o = jnp.concatenate(outs, axis=0)
        qoff = pl.multiple_of(rl * w, w)
        gate = gate_ref[pl.ds(qoff, w), :].astype(F32)
        o_ref[pl.ds(qoff, w), :] = (o * _silu(gate)).astype(o_ref.dtype)

    n_groups = rows_per_step // NA_ROW_GROUP
    for j in range(NA_ROW_GROUP):
        scores_store(0, j, j)

    def trip(g, carry):
        slot = g % 2
        for j in range(NA_ROW_GROUP):
            scores_store(slot, j, g * NA_ROW_GROUP + j)
        for j in range(NA_ROW_GROUP):
            softmax_values(1 - slot, j, (g - 1) * NA_ROW_GROUP + j)
        return carry

    lax.fori_loop(1, n_groups, trip, 0)
    for j in range(NA_ROW_GROUP):
        softmax_values((n_groups - 1) % 2, j, (n_groups - 1) * NA_ROW_GROUP + j)


def _na_call(nq, nk_t, nv_cb, zmain, nk_m, nv_m, rpb_pad, meta_bias, *, batch, seq, rows_per_step):
    n_rows = seq // GRID_W
    assert n_rows % NA_WIN_H == 0 and n_rows % rows_per_step == 0 and rows_per_step % NA_ROW_GROUP == 0
    nsteps = n_rows // rows_per_step
    hp = NA_HEADS // 2
    tq = rows_per_step * GRID_W
    n_idx = 2 * NA_WIN_H - 1
    in_specs = [
        pl.BlockSpec((tq, LANES), lambda p, b, i: (b * nsteps + i, p)),
        pl.BlockSpec((seq // LANES, LANES, LANES), lambda p, b, i: (b, p, 0)),
        pl.BlockSpec((NA_CB, seq // NA_CB, LANES), lambda p, b, i: (0, b, p)),
        pl.BlockSpec((tq, LANES), lambda p, b, i: (b * nsteps + i, COL_NG // LANES + p)),
        pl.BlockSpec((N_META, LANES), lambda p, b, i: (0, p)),
        pl.BlockSpec((N_META, LANES), lambda p, b, i: (0, p)),
        pl.BlockSpec((2, n_idx, LANES), lambda p, b, i: (p, 0, 0)),
        pl.BlockSpec((2, 1, N_META), lambda p, b, i: (p, 0, 0)),
    ]
    return pl.pallas_call(
        functools.partial(_na_kernel, rows_per_step=rows_per_step, n_rows=n_rows),
        name="na_attn",
        grid=(hp, batch, nsteps),
        in_specs=in_specs,
        out_specs=pl.BlockSpec((tq, LANES), lambda p, b, i: (b * nsteps + i, p)),
        out_shape=jax.ShapeDtypeStruct((batch * seq, NA_WIDTH), BF16),
        scratch_shapes=[pltpu.VMEM((NA_VARIANTS, NA_STACK, 2 * LANES), F32),
                        pltpu.VMEM((2, NA_ROW_GROUP, NA_STACK, 2 * LANES), F32),
                        pltpu.VMEM((2, NA_ROW_GROUP, NA_STACK, N_META), F32)],
        compiler_params=pltpu.CompilerParams(
            dimension_semantics=("arbitrary", "arbitrary", "arbitrary"), vmem_limit_bytes=VMEM_LIMIT),
    )(nq, nk_t, nv_cb, zmain, nk_m, nv_m, rpb_pad, meta_bias)


def _out_kernel(x_ref, of_ref, ob_ref, gg_ref, na_ref, gn_ref, w_ref, o_ref):
    s = of_ref[...].astype(F32) + ob_ref[...].astype(F32)
    gate = gg_ref[...].astype(F32)
    gn = gn_ref[...]
    heads = []
    for h in range(GLA_HEADS):
        cols = slice(h * GLA_DV, (h + 1) * GLA_DV)
        sh = s[:, cols]
        ms = jnp.mean(sh * sh, axis=-1, keepdims=True)
        y = sh * lax.rsqrt(ms + RMS_EPS) * gn
        heads.append((y * _silu(gate[:, cols])).astype(BF16))
    o_gla = jnp.concatenate(heads, axis=1)
    acc = _dot(o_gla, w_ref[:GLA_WIDTH, :]) + _dot(na_ref[...], w_ref[GLA_WIDTH:, :])
    o_ref[...] = x_ref[...] + acc


def _out_call(x2d, o_f, o_b, zmain, o_na, gn, w_out, *, tm):
    m, d = x2d.shape
    assert m % tm == 0
    return pl.pallas_call(
        _out_kernel,
        name="out_proj",
        grid=(m // tm,),
        in_specs=[
            pl.BlockSpec((tm, d), lambda i: (i, 0)),
            pl.BlockSpec((tm, GLA_WIDTH), lambda i: (i, 0)),
            pl.BlockSpec((tm, GLA_WIDTH), lambda i: (i, 0)),
            pl.BlockSpec((tm, GLA_WIDTH), lambda i: (i, COL_GG // GLA_WIDTH)),
            pl.BlockSpec((tm, NA_WIDTH), lambda i: (i, 0)),
            pl.BlockSpec((1, GLA_DV), lambda i: (0, 0), pipeline_mode=pl.Buffered(1)),
            pl.BlockSpec((GLA_WIDTH + NA_WIDTH, d), lambda i: (0, 0), pipeline_mode=pl.Buffered(1)),
        ],
        out_specs=pl.BlockSpec((tm, d), lambda i: (i, 0)),
        out_shape=jax.ShapeDtypeStruct((m, d), F32),
        compiler_params=pltpu.CompilerParams(
            dimension_semantics=("parallel",), vmem_limit_bytes=VMEM_LIMIT),
    )(x2d, o_f, o_b, zmain, o_na, gn, w_out)


def _pick(n, cands):
    for c in cands:
        if n % c == 0:
            return c
    raise ValueError(f"no tile for {n}")


def kernel(x, meta_tokens, norm_g, w_in, w_decay_fwd, b_decay_fwd, w_decay_bwd, b_decay_bwd,
           gla_out_norm_g, q_norm_g, k_norm_g, rpb, meta_bias, w_out):
    batch, seq, d = x.shape
    assert d == D_MODEL and seq % TILE == 0 and seq % GRID_W == 0
    depth = norm_g.shape[0]
    assert depth == 1
    l = 0
    x2d = x.reshape(batch * seq, d)
    g = norm_g[l].reshape(1, d)

    assert w_in.shape[1:] == (d, D_IN_PROJ)
    wt = jnp.swapaxes(w_in[l], 0, 1).astype(BF16)
    gain_qk = jnp.concatenate([jnp.tile(q_norm_g[l], NA_HEADS) * (NA_DH ** -0.5),
                               jnp.tile(k_norm_g[l], NA_HEADS)]).reshape(1, 2 * NA_WIDTH)

    zr = jnp.zeros((GLA_RANK, GLA_KEY_WIDTH), F32)
    ztail = jnp.zeros((LANES - 2 * GLA_RANK, GLA_KEY_WIDTH), F32)
    wd = jnp.stack([jnp.concatenate([w_decay_fwd[l], zr, ztail], axis=0),
                    jnp.concatenate([zr, w_decay_bwd[l], ztail], axis=0)]).astype(BF16)
    bias = jnp.stack([b_decay_fwd[l], b_decay_bwd[l]]).reshape(2, 1, GLA_KEY_WIDTH)
    tg, tg_m = _gla_constants()

    assert seq % (NA_WIN_H * GRID_W) == 0
    kgain_col = jnp.tile(k_norm_g[l], NA_HEADS).reshape(NA_WIDTH, 1)
    zmain, nq, nk_t, nv_cb, gtot = _proj_call(x2d, g, wt, gain_qk, kgain_col, wd, bias, tg,
                                              tm=NA_WIN_H * GRID_W, tokens=True, name="proj_tokens")
    zmain_m, _, nk_m, nv_m, _ = _proj_call(meta_tokens, g, wt, gain_qk, kgain_col, wd, bias, tg,
                                           tm=N_META, tokens=False, name="proj_meta")
    zmain_m_pad = jnp.pad(zmain_m, ((0, TILE - N_META), (0, 0)))

    o_f, o_b = _gla_call(zmain, gtot, zmain_m_pad, wd, bias, tg_m, batch=batch, seq=seq,
                         tokens_per_step=_pick(seq, (4096, 2048, 1024, 512, 256, 128)))

    rpb_pad = jnp.pad(rpb[l], ((0, 0), (0, 0), (0, 2 * NA_DH - rpb.shape[-1])))
    mb = meta_bias[l].reshape(NA_HEADS, 1, N_META)
    o_na = _na_call(nq, nk_t, nv_cb, zmain, nk_m, nv_m, rpb_pad, mb, batch=batch, seq=seq,
                    rows_per_step=_pick(seq // GRID_W, (128, 64, 32, 16)))

    out = _out_call(x2d, o_f, o_b, zmain, o_na, gla_out_norm_g[l].reshape(1, GLA_DV),
                    w_out[l].astype(BF16), tm=_pick(batch * seq, (1024, 512, 256, 128)))
    return out.reshape(batch, seq, d)
```

```python
import functools

import jax
import jax.numpy as jnp
import numpy as np
from jax import lax
from jax.experimental import pallas as pl
from jax.experimental.pallas import tpu as pltpu

F32 = jnp.float32
BF16 = jnp.bfloat16

D_MODEL = 1024
N_META = 16
GRID_W = 64
GLA_HEADS = 4
GLA_DK = 128
GLA_DV = 256
GLA_KEY_WIDTH = GLA_HEADS * GLA_DK
GLA_WIDTH = GLA_HEADS * GLA_DV
GLA_RANK = 16
GLA_TAU = 16.0
GLA_CHUNK = 64
NA_HEADS = 16
NA_DH = 64
NA_WIDTH = NA_HEADS * NA_DH
NA_WIN_H = 8
NA_WIN_W = 16
RMS_EPS = 1e-6
NEG_BIG = -1e30
LANES = 128

COL_GV, COL_GG, COL_NG, COL_GQ, COL_GK, COL_R = 0, 1024, 2048, 3072, 3584, 4096
MAIN_W = COL_R + LANES
COL_DIR0, DIR_W = 3072, 3 * 512
MAIN_TOK_W = COL_DIR0 + 2 * DIR_W
NA_CBW = 16
NA_CB = GRID_W // NA_CBW
IN_GQ, IN_GK, IN_GV, IN_R, IN_GG, IN_NQ, IN_NK, IN_NV, IN_NG = 0, 512, 1024, 2048, 2080, 3104, 4128, 5152, 6176
D_IN_PROJ = 7200
PROJ_TN = 512

TILE = 2 * GLA_CHUNK
NA_ROW_GROUP = 16
VMEM_LIMIT = 58 * 1024 * 1024


def _split_bf16(x):
    hi = x.astype(BF16)
    lo = (x - hi.astype(F32)).astype(BF16)
    return hi, lo


def _dot(a, b):
    return jnp.dot(a, b, preferred_element_type=F32)


def _dot_nt(a, b):
    return lax.dot_general(a, b, (((1,), (1,)), ((), ())), preferred_element_type=F32)


def _dot_tn(a, b):
    return lax.dot_general(a, b, (((0,), (0,)), ((), ())), preferred_element_type=F32)


def _log_sigmoid(x):
    return jnp.minimum(x, 0.0) - jnp.log(1.0 + jnp.exp(-jnp.abs(x)))


def _silu(x):
    return x * (1.0 / (1.0 + jnp.exp(-x)))


SHARED_SEGMENTS = ((COL_GV, IN_GV, 1024), (COL_GG, IN_GG, 1024), (COL_NG, IN_NG, 1024))
META_SEGMENTS = ((COL_GQ, IN_GQ, 512), (COL_GK, IN_GK, 512),
                 (COL_R, IN_R, LANES))


def _store_rows(o_ref, c0, y, permute):
    n = y.shape[1]
    if not permute:
        o_ref[:, c0:c0 + n] = y
        return
    for rl in range(y.shape[0] // GRID_W):
        for cb in range(NA_CB):
            src = rl * GRID_W + cb * NA_CBW
            o_ref[cb, rl * NA_CBW:(rl + 1) * NA_CBW, c0:c0 + n] = y[src:src + NA_CBW]


def _gla_log_decay(r, wd_ref, bdec_ref, d):
    return (_log_sigmoid(_dot(r, wd_ref[d]) + bdec_ref[d]) * (1.0 / GLA_TAU)).astype(BF16)


def _gla_operands_store(q, k, log_decay, d, t, tg_ref, zm_ref, gtot_ref):
    rows = slice(t * TILE, (t + 1) * TILE)
    g = _dot(tg_ref[d], log_decay[rows])
    col = COL_DIR0 + d * DIR_W
    tots = []
    for c in range(TILE // GLA_CHUNK):
        last = c * GLA_CHUNK + (GLA_CHUNK - 1 if d == 0 else 0)
        tot = g[last:last + 1]
        gtot_ref[(t * TILE) // GLA_CHUNK + c:(t * TILE) // GLA_CHUNK + c + 1,
                 d * GLA_KEY_WIDTH:(d + 1) * GLA_KEY_WIDTH] = tot
        tots.append(jnp.broadcast_to(tot, (GLA_CHUNK, GLA_KEY_WIDTH)))
    dd = jnp.concatenate(tots, axis=0) - g
    zm_ref[rows, col:col + 512] = (q[rows] * jnp.exp(g)).astype(BF16)
    zm_ref[rows, col + 512:col + 1024] = (k[rows] * jnp.exp(-g)).astype(BF16)
    zm_ref[rows, col + 1024:col + 1536] = (k[rows] * jnp.exp(dd)).astype(BF16)


def _proj_kernel(x0_ref, xn_ref, g_ref, wt_ref, gain_ref, kgain_ref, wd_ref, bdec_ref, tg_ref,
                 zm_ref, nq_ref, nk_ref, nv_ref, gtot_ref, u_scr, *, tokens):
    def normed(x_ref):
        x = x_ref[...]
        ms = jnp.mean(x * x, axis=-1, keepdims=True)
        return (x * lax.rsqrt(ms + RMS_EPS) * g_ref[...]).astype(BF16)

    step = pl.program_id(0)

    @pl.when(step == 0)
    def _():
        u_scr[0] = normed(x0_ref)

    slot = step % 2
    u = u_scr[slot]
    tm = u.shape[0]

    def prepare_next():
        u_scr[1 - slot] = normed(xn_ref)

    def main_tile(dst, src, off, n):
        acc = _dot_nt(u, wt_ref[src + off:src + off + n, :])
        if dst == COL_GQ:
            acc = acc * (GLA_DK ** -0.5)
        zm_ref[:, dst + off:dst + off + n] = acc.astype(zm_ref.dtype)

    def nv_tile(off):
        acc = _dot_nt(u, wt_ref[IN_NV + off:IN_NV + off + PROJ_TN, :])
        _store_rows(nv_ref, off, acc.astype(nv_ref.dtype), tokens)

    def head_norm_rows(acc, goff):
        first = lax.broadcasted_iota(jnp.int32, (acc.shape[0], LANES), 1) < NA_DH
        cols = []
        for c in range(0, acc.shape[1], LANES):
            a = acc[:, c:c + LANES]
            sq = a * a
            lo = jnp.sum(jnp.where(first, sq, 0.0), axis=-1, keepdims=True)
            hi = jnp.sum(jnp.where(first, 0.0, sq), axis=-1, keepdims=True)
            inv = jnp.where(first, lax.rsqrt(lo * (1.0 / NA_DH) + RMS_EPS), lax.rsqrt(hi * (1.0 / NA_DH) + RMS_EPS))
            cols.append(a * inv)
        return (jnp.concatenate(cols, axis=1) * gain_ref[:, goff:goff + PROJ_TN]).astype(BF16)

    def nq_tile(off):
        acc = _dot_nt(u, wt_ref[IN_NQ + off:IN_NQ + off + PROJ_TN, :])
        nq_ref[:, off:off + PROJ_TN] = head_norm_rows(acc, off)

    def nk_rows_tile(off):
        acc = _dot_nt(u, wt_ref[IN_NK + off:IN_NK + off + PROJ_TN, :])
        nk_ref[:, off:off + PROJ_TN] = head_norm_rows(acc, NA_WIDTH + off)

    def nk_transposed_tile(off):
        u_cb = jnp.concatenate([u[rl * GRID_W + cb * NA_CBW:rl * GRID_W + (cb + 1) * NA_CBW]
                                for cb in range(NA_CB) for rl in range(tm // GRID_W)], axis=0)
        acc_t = _dot_nt(wt_ref[IN_NK + off:IN_NK + off + PROJ_TN, :], u_cb)
        for h in range(PROJ_TN // NA_DH):
            rows = slice(off + h * NA_DH, off + (h + 1) * NA_DH)
            blk = acc_t[h * NA_DH:(h + 1) * NA_DH]
            ms = jnp.sum(blk * blk, axis=0, keepdims=True) * (1.0 / NA_DH)
            y = (blk * lax.rsqrt(ms + RMS_EPS) * kgain_ref[rows, :]).astype(BF16)
            for cb in range(NA_CB):
                nk_ref[cb, rows, :] = y[:, cb * LANES:(cb + 1) * LANES]

    nk_tile = nk_transposed_tile if tokens else nk_rows_tile
    tiles = [functools.partial(nk_tile, off) for off in range(0, NA_WIDTH, PROJ_TN)]
    tiles += [functools.partial(nq_tile, off) for off in range(0, NA_WIDTH, PROJ_TN)]
    tiles += [functools.partial(nv_tile, off) for off in range(0, NA_WIDTH, PROJ_TN)]
    for dst, src, width in SHARED_SEGMENTS + (() if tokens else META_SEGMENTS):
        for off in range(0, width, PROJ_TN):
            tiles.append(functools.partial(main_tile, dst, src, off, min(PROJ_TN, width - off)))
    if not tokens:
        gtot_ref[...] = jnp.zeros_like(gtot_ref)
        for tile in tiles:
            tile()
        prepare_next()
        return

    r = _dot_nt(u, wt_ref[IN_R:IN_R + LANES, :]).astype(BF16)
    q = _dot_nt(u, wt_ref[IN_GQ:IN_GQ + GLA_KEY_WIDTH, :]) * (GLA_DK ** -0.5)
    k = _dot_nt(u, wt_ref[IN_GK:IN_GK + GLA_KEY_WIDTH, :])
    log_decay = [None, None]

    def decay_stage(d):
        log_decay[d] = _gla_log_decay(r, wd_ref, bdec_ref, d)

    stages = [functools.partial(decay_stage, 0), functools.partial(decay_stage, 1), None]
    stages += [functools.partial(lambda d, t: _gla_operands_store(q, k, log_decay[d], d, t, tg_ref, zm_ref, gtot_ref),
                                 d, t)
               for d in range(2) for t in range(tm // TILE)]
    for i, tile in enumerate(tiles):
        if i < len(stages) and stages[i] is not None:
            stages[i]()
        tile()
    prepare_next()
    assert len(stages) <= len(tiles)


def _proj_call(x2d, g, wt, gain_qk, kgain_col, wd, bdec, tg, *, tm, tokens, name):
    m, d = x2d.shape
    assert m % tm == 0 and wt.shape == (D_IN_PROJ, d)
    resident = lambda shape: pl.BlockSpec(shape, lambda i: (0,) * len(shape), pipeline_mode=pl.Buffered(1))
    plain = pl.BlockSpec((tm, NA_WIDTH), lambda i: (i, 0))
    plain_shape = jax.ShapeDtypeStruct((m, NA_WIDTH), BF16)
    if tokens:
        assert tm == NA_WIN_H * GRID_W and NA_CB * LANES == tm
        main_w = MAIN_TOK_W
        v_spec = pl.BlockSpec((NA_CB, tm // NA_CB, NA_WIDTH), lambda i: (0, i, 0))
        v_shape = jax.ShapeDtypeStruct((NA_CB, m // NA_CB, NA_WIDTH), BF16)
        k_spec = pl.BlockSpec((NA_CB, NA_WIDTH, LANES), lambda i: (i, 0, 0))
        k_shape = jax.ShapeDtypeStruct((m // LANES, NA_WIDTH, LANES), BF16)
        n_chunk_rows = tm // GLA_CHUNK
    else:
        main_w = MAIN_W
        v_spec, v_shape, k_spec, k_shape = plain, plain_shape, plain, plain_shape
        n_chunk_rows = 8
    gt_rows = (m // tm) * n_chunk_rows
    return pl.pallas_call(
        functools.partial(_proj_kernel, tokens=tokens),
        name=name,
        grid=(m // tm,),
        in_specs=[
            resident((tm, d)),
            pl.BlockSpec((tm, d), lambda i: (jnp.minimum(i + 1, m // tm - 1), 0)),
            resident((1, d)),
            resident((D_IN_PROJ, d)),
            resident((1, 2 * NA_WIDTH)),
            resident((NA_WIDTH, 1)),
            resident((2, LANES, GLA_KEY_WIDTH)),
            resident((2, 1, GLA_KEY_WIDTH)),
            resident((2, TILE, TILE)),
        ],
        out_specs=[pl.BlockSpec((tm, main_w), lambda i: (i, 0)), plain, k_spec, v_spec,
                   pl.BlockSpec((n_chunk_rows, 2 * GLA_KEY_WIDTH), lambda i: (i, 0))],
        out_shape=[jax.ShapeDtypeStruct((m, main_w), BF16), plain_shape, k_shape, v_shape,
                   jax.ShapeDtypeStruct((gt_rows, 2 * GLA_KEY_WIDTH), F32)],
        scratch_shapes=[pltpu.VMEM((2, tm, d), BF16)],
        compiler_params=pltpu.CompilerParams(
            dimension_semantics=("arbitrary",), vmem_limit_bytes=VMEM_LIMIT),
    )(x2d, x2d, g, wt, gain_qk, kgain_col, wd, bdec, tg)


def _gla_kernel(qf_ref, kif_ref, ksf_ref, vf_ref, gtf_ref, qb_ref, kib_ref, ksb_ref, vb_ref, gtb_ref,
                km_ref, vm_ref, rm_ref, wd_ref, b_ref, tgm_ref,
                of_ref, ob_ref, sf_ref, sb_ref, *, n_tiles):
    row = lax.broadcasted_iota(jnp.int32, (TILE, TILE), 0)
    col = lax.broadcasted_iota(jnp.int32, (TILE, TILE), 1)
    same = (row // GLA_CHUNK) == (col // GLA_CHUNK)
    masks = (same & (row >= col), same & (row <= col))

    @pl.when(pl.program_id(2) == 0)
    def _():
        rowm = lax.broadcasted_iota(jnp.int32, (TILE, GLA_DK), 0)
        lg = _log_sigmoid(_dot(rm_ref[...], wd_ref[...]) + b_ref[...]) * (1.0 / GLA_TAU)
        hi, lo = _split_bf16(jnp.where(rowm < N_META, lg, 0.0))
        d_m = _dot(tgm_ref[...], jnp.concatenate([hi, lo], axis=0))
        k_st = (km_ref[...].astype(F32) * jnp.exp(d_m)).astype(BF16)
        sf_ref[...] = _dot_tn(k_st, vm_ref[...])
        sb_ref[...] = jnp.zeros_like(sb_ref)

    items = []
    for t in range(n_tiles):
        items.append((0, t))
        items.append((1, n_tiles - 1 - t))
    refs = ((qf_ref, kif_ref, ksf_ref, vf_ref, of_ref), (qb_ref, kib_ref, ksb_ref, vb_ref, ob_ref))
    a_cols = (jnp.exp(jnp.transpose(gtf_ref[...])), jnp.exp(jnp.transpose(gtb_ref[...])))

    def rows_of(t):
        return slice(t * TILE, (t + 1) * TILE)

    amats = [jnp.where(masks[d], _dot_nt(refs[d][0][rows_of(t), :], refs[d][1][rows_of(t), :]), 0.0).astype(BF16)
             for d, t in items]

    def state_independent(idx):
        d, t = items[idx]
        v = refs[d][3][rows_of(t), :]
        k_st = refs[d][2][rows_of(t), :]
        u = [_dot_tn(k_st[h * GLA_CHUNK:(h + 1) * GLA_CHUNK], v[h * GLA_CHUNK:(h + 1) * GLA_CHUNK])
             for h in range(2)]
        return _dot(amats[idx], v), u

    state = [sf_ref[...], sb_ref[...]]
    ahead = state_independent(0)
    for idx, (d, t) in enumerate(items):
        o_intra, u = ahead
        if idx + 1 < len(items):
            ahead = state_independent(idx + 1)
        s = state[d]
        outs = [None, None]
        for h in ((0, 1) if d == 0 else (1, 0)):
            rows = slice(t * TILE + h * GLA_CHUNK, t * TILE + (h + 1) * GLA_CHUNK)
            outs[h] = o_intra[h * GLA_CHUNK:(h + 1) * GLA_CHUNK] + _dot(refs[d][0][rows, :], s.astype(BF16))
            chunk = 2 * t + h
            s = a_cols[d][:, chunk:chunk + 1] * s + u[h]
        state[d] = s
        refs[d][4][rows_of(t), :] = jnp.concatenate(outs, axis=0).astype(refs[d][4].dtype)
    sf_ref[...] = state[0]
    sb_ref[...] = state[1]


def _gla_constants():
    i = np.arange(TILE)
    same = (i[:, None] // GLA_CHUNK) == (i[None, :] // GLA_CHUNK)
    inc_f = same & (i[None, :] <= i[:, None])
    inc_b = same & (i[None, :] >= i[:, None])
    to_end_meta = i[None, :] > i[:, None]
    dup = lambda m: np.concatenate([m, m], axis=1)
    tg = jnp.asarray(np.stack([inc_f, inc_b]), BF16)
    return tg, jnp.asarray(dup(to_end_meta), BF16)


def _gla_call(zmain, gtot, zmain_m, wd, bias, tg_m, *, batch, seq, tokens_per_step):
    t_blk = tokens_per_step
    nb = seq // t_blk
    n_tiles = t_blk // TILE
    n_chunks = t_blk // GLA_CHUNK

    def tok(fwd):
        if fwd:
            return lambda b, hh, i: b * nb + i
        return lambda b, hh, i: b * nb + (nb - 1 - i)

    def dir_specs(d):
        t = tok(d == 0)
        c0 = (COL_DIR0 + d * DIR_W) // GLA_DK
        return [
            pl.BlockSpec((t_blk, GLA_DK), lambda b, hh, i: (t(b, hh, i), c0 + hh)),
            pl.BlockSpec((t_blk, GLA_DK), lambda b, hh, i: (t(b, hh, i), c0 + GLA_HEADS + hh)),
            pl.BlockSpec((t_blk, GLA_DK), lambda b, hh, i: (t(b, hh, i), c0 + 2 * GLA_HEADS + hh)),
            pl.BlockSpec((t_blk, GLA_DV), lambda b, hh, i: (t(b, hh, i), COL_GV // GLA_DV + hh)),
            pl.BlockSpec((n_chunks, GLA_DK), lambda b, hh, i: (t(b, hh, i), d * GLA_HEADS + hh)),
        ]

    in_specs = (
        dir_specs(0) + dir_specs(1)
        + [pl.BlockSpec((TILE, GLA_DK), lambda b, hh, i: (0, COL_GK // GLA_DK + hh)),
           pl.BlockSpec((TILE, GLA_DV), lambda b, hh, i: (0, COL_GV // GLA_DV + hh)),
           pl.BlockSpec((TILE, LANES), lambda b, hh, i: (0, COL_R // LANES)),
           pl.BlockSpec((None, LANES, GLA_DK), lambda b, hh, i: (0, 0, hh)),
           pl.BlockSpec((None, 1, GLA_DK), lambda b, hh, i: (0, 0, hh)),
           pl.BlockSpec((TILE, 2 * TILE), lambda b, hh, i: (0, 0))]
    )
    args = [zmain] * 4 + [gtot] + [zmain] * 4 + [gtot] + [zmain_m] * 3 + [wd, bias, tg_m]
    out_spec_f = pl.BlockSpec((t_blk, GLA_DV), lambda b, hh, i: (b * nb + i, hh))
    out_spec_b = pl.BlockSpec((t_blk, GLA_DV), lambda b, hh, i: (b * nb + (nb - 1 - i), hh))
    o_shape = jax.ShapeDtypeStruct((batch * seq, GLA_WIDTH), BF16)
    return pl.pallas_call(
        functools.partial(_gla_kernel, n_tiles=n_tiles),
        name="gla_scan",
        grid=(batch, GLA_HEADS, nb),
        in_specs=in_specs,
        out_specs=[out_spec_f, out_spec_b],
        out_shape=[o_shape, o_shape],
        scratch_shapes=[pltpu.VMEM((GLA_DK, GLA_DV), F32), pltpu.VMEM((GLA_DK, GLA_DV), F32)],
        compiler_params=pltpu.CompilerParams(
            dimension_semantics=("parallel", "parallel", "arbitrary"), vmem_limit_bytes=VMEM_LIMIT),
    )(*args)


NA_SEG_COLS = ((0, 24), (24, 40), (40, 64))
NA_SEG_ROW0 = (0, 48, 80)
NA_STACK = 2 * GRID_W


NA_BASE_MID = NA_WIN_H // 2 - 1
NA_VARIANTS = 2 * NA_WIN_H - 1


def _na_variant(base, a):
    return jnp.where(base == NA_BASE_MID, a, jnp.where(base < NA_BASE_MID, NA_WIN_H + base, NA_WIN_H - 1 + base))


def _na_build_bias(rpb_ref, tab_ref):
    def build(v, carry):
        a = jnp.where(v < NA_WIN_H, v, 0)
        base = jnp.where(v < NA_WIN_H, NA_BASE_MID,
                         jnp.where(v < NA_WIN_H + NA_BASE_MID, v - NA_WIN_H, v - (NA_WIN_H - 1)))
        for hh in range(2):
            bias_rows = [rpb_ref[hh, pl.ds(base + ((t - a + NA_WIN_H) & (NA_WIN_H - 1)), 1), :]
                         for t in range(NA_WIN_H)]
            for s, (c0, c1) in enumerate(NA_SEG_COLS):
                n = c1 - c0
                vrows = [jnp.broadcast_to(br, (n, LANES)) for br in bias_rows]
                row0 = NA_SEG_ROW0[s] + hh * n
                rho = lax.broadcasted_iota(jnp.int32, (n, LANES), 0)
                lane = lax.broadcasted_iota(jnp.int32, (n, LANES), 1)
                cs = jnp.clip(c0 + rho - NA_WIN_W // 2, 0, GRID_W - NA_WIN_W)
                for half in range(2):
                    cb = s + half
                    cp = NA_CBW * cb + lane % NA_CBW
                    valid = (cp >= cs) & (cp < cs + NA_WIN_W)
                    acc = None
                    for i in range(NA_WIN_H):
                        shift = (NA_CBW * i - NA_CBW * cb + c0 - (NA_WIN_W - 1)) % LANES
                        rolled = pltpu.roll(vrows[i], shift, 1, stride=1, stride_axis=0)
                        acc = rolled if acc is None else jnp.where(lane // NA_CBW == i, rolled, acc)
                    tab_ref[v, row0:row0 + n, half * LANES:(half + 1) * LANES] = jnp.where(valid, acc, NEG_BIG)
        return carry

    lax.fori_loop(0, NA_VARIANTS, build, 0)


def _na_kernel(q_ref, kt_ref, v_ref, gate_ref, km_ref, vm_ref, rpb_ref, mb_ref, o_ref, tab_ref, sc_ref, sm_ref,
               *, rows_per_step, n_rows):
    w = GRID_W

    @pl.when(pl.program_id(1) == 0)
    def _():
        _na_build_bias(rpb_ref, tab_ref)

    head0_q = lax.broadcasted_iota(jnp.int32, (w, LANES), 1) < NA_DH
    srow = lax.broadcasted_iota(jnp.int32, (NA_STACK, N_META), 0)
    is_h1 = jnp.zeros((NA_STACK, N_META), jnp.bool_)
    for s, (c0, c1) in enumerate(NA_SEG_COLS):
        lo = NA_SEG_ROW0[s] + (c1 - c0)
        is_h1 = is_h1 | ((srow >= lo) & (srow < lo + (c1 - c0)))
    mb = jnp.where(is_h1, mb_ref[1], mb_ref[0])
    km = km_ref[...]
    vm = vm_ref[...]
    seg_rows = [slice(NA_SEG_ROW0[s], NA_SEG_ROW0[s] + 2 * (c1 - c0)) for s, (c0, c1) in enumerate(NA_SEG_COLS)]
    blk_rows = [slice(seg_rows[max(cb - 1, 0)].start, seg_rows[min(cb, len(seg_rows) - 1)].stop)
                for cb in range(NA_CB)]

    n_tiles = n_rows // NA_WIN_H
    key_lane = lax.broadcasted_iota(jnp.int32, (LANES, LANES), 1)

    def window(rl):
        row = pl.program_id(1) * rows_per_step + rl
        b = row // n_rows
        r = row - b * n_rows
        rs = jnp.clip(r - NA_WIN_H // 2, 0, n_rows - NA_WIN_H)
        a = rs & (NA_WIN_H - 1)
        return (b * n_tiles + rs // NA_WIN_H, a, (b + 1) * n_tiles - 1,
                _na_variant(rs - r + (NA_WIN_H - 1), a))

    def scores_store(slot, j, rl):
        m, a, m_last, var = window(rl)
        q = q_ref[pl.ds(pl.multiple_of(rl * w, w), w), :].astype(F32)
        q0 = jnp.where(head0_q, q, 0.0)
        q1 = q - q0
        qs = jnp.concatenate([piece[c0:c1] for c0, c1 in NA_SEG_COLS for piece in (q0, q1)],
                             axis=0).astype(BF16)
        from_m = key_lane >= a * NA_CBW
        m1 = jnp.minimum(m + 1, m_last)
        blocks = [_dot(qs[blk_rows[cb]],
                       jnp.where(from_m, kt_ref[NA_CB * m + cb], kt_ref[NA_CB * m1 + cb]))
                  for cb in range(NA_CB)]
        for s in range(len(NA_SEG_COLS)):
            n2 = seg_rows[s].stop - seg_rows[s].start
            for half in range(2):
                cb = s + half
                a0 = seg_rows[s].start - blk_rows[cb].start
                lanes = slice(half * LANES, (half + 1) * LANES)
                sc_ref[slot, j, seg_rows[s], lanes] = blocks[cb][a0:a0 + n2] + tab_ref[var, seg_rows[s], lanes]
        sm_ref[slot, j] = _dot_nt(qs, km) + mb

    def value_block(cb, m, a):
        groups = []
        for t in range(NA_WIN_H):
            row = NA_WIN_H * m + t + jnp.where(t < a, NA_WIN_H, 0)
            groups.append(v_ref[cb, pl.ds(pl.multiple_of(row * NA_CBW, NA_CBW), NA_CBW), :])
        return jnp.concatenate(groups, axis=0)

    def softmax_values(slot, j, rl):
        m_tile, a, _, _ = window(rl)
        sc = sc_ref[slot, j]
        sm = sm_ref[slot, j]
        m = jnp.maximum(sc[:, :LANES], sc[:, LANES:])
        m = jnp.maximum(jnp.max(m, axis=-1, keepdims=True), jnp.max(sm, axis=-1, keepdims=True))
        p = jnp.exp(sc - m)
        pm = jnp.exp(sm - m)
        den = (jnp.sum(p[:, :LANES] + p[:, LANES:], axis=-1, keepdims=True)
               + jnp.sum(pm, axis=-1, keepdims=True))
        p = p.astype(BF16)
        vals = [value_block(cb, m_tile, a) for cb in range(NA_CB)]
        o_meta = _dot(pm.astype(BF16), vm)
        inv = 1.0 / den
        outs = []
        for s, (c0, c1) in enumerate(NA_SEG_COLS):
            o_loc = _dot(p[seg_rows[s]], jnp.concatenate([vals[s], vals[s + 1]], axis=0))
            o = (o_loc + o_meta[seg_rows[s]]) * inv[seg_rows[s]]
            n = c1 - c0
            head0 = lax.broadcasted_iota(jnp.int32, (n, LANES), 1) < NA_DH
            outs.append(jnp.where(head0, o[:n], o[n:]))
        o = jnp.concatenate(outs, axis=0)
        qoff = pl.multiple_of(rl * w, w)
        gate = gate_ref[pl.ds(qoff, w), :].astype(F32)
        o_ref[pl.ds(qoff, w), :] = (o * _silu(gate)).astype(o_ref.dtype)

    n_groups = rows_per_step // NA_ROW_GROUP
    for j in range(NA_ROW_GROUP):
        scores_store(0, j, j)

    def trip(g, carry):
        slot = g % 2
        for j in range(NA_ROW_GROUP):
            scores_store(slot, j, g * NA_ROW_GROUP + j)
        for j in range(NA_ROW_GROUP):
            softmax_values(1 - slot, j, (g - 1) * NA_ROW_GROUP + j)
        return carry

    lax.fori_loop(1, n_groups, trip, 0)
    for j in range(NA_ROW_GROUP):
        softmax_values((n_groups - 1) % 2, j, (n_groups - 1) * NA_ROW_GROUP + j)


def _na_call(nq, nk_t, nv_cb, zmain, nk_m, nv_m, rpb_pad, meta_bias, *, batch, seq, rows_per_step):
    n_rows = seq // GRID_W
    total_rows = batch * n_rows
    assert n_rows % NA_WIN_H == 0 and total_rows % rows_per_step == 0 and rows_per_step % NA_ROW_GROUP == 0
    nsteps = total_rows // rows_per_step
    hp = NA_HEADS // 2
    tq = rows_per_step * GRID_W
    n_idx = 2 * NA_WIN_H - 1
    in_specs = [
        pl.BlockSpec((tq, LANES), lambda p, i: (i, p)),
        pl.BlockSpec((batch * seq // LANES, LANES, LANES), lambda p, i: (0, p, 0)),
        pl.BlockSpec((NA_CB, batch * seq // NA_CB, LANES), lambda p, i: (0, 0, p)),
        pl.BlockSpec((tq, LANES), lambda p, i: (i, COL_NG // LANES + p)),
        pl.BlockSpec((N_META, LANES), lambda p, i: (0, p)),
        pl.BlockSpec((N_META, LANES), lambda p, i: (0, p)),
        pl.BlockSpec((2, n_idx, LANES), lambda p, i: (p, 0, 0)),
        pl.BlockSpec((2, 1, N_META), lambda p, i: (p, 0, 0)),
    ]
    return pl.pallas_call(
        functools.partial(_na_kernel, rows_per_step=rows_per_step, n_rows=n_rows),
        name="na_attn",
        grid=(hp, nsteps),
        in_specs=in_specs,
        out_specs=pl.BlockSpec((tq, LANES), lambda p, i: (i, p)),
        out_shape=jax.ShapeDtypeStruct((batch * seq, NA_WIDTH), BF16),
        scratch_shapes=[pltpu.VMEM((NA_VARIANTS, NA_STACK, 2 * LANES), F32),
                        pltpu.VMEM((2, NA_ROW_GROUP, NA_STACK, 2 * LANES), F32),
                        pltpu.VMEM((2, NA_ROW_GROUP, NA_STACK, N_META), F32)],
        compiler_params=pltpu.CompilerParams(
            dimension_semantics=("arbitrary", "arbitrary"), vmem_limit_bytes=VMEM_LIMIT),
    )(nq, nk_t, nv_cb, zmain, nk_m, nv_m, rpb_pad, meta_bias)


def _out_kernel(x_ref, of_ref, ob_ref, gg_ref, na_ref, gn_ref, w_ref, o_ref):
    s = of_ref[...].astype(F32) + ob_ref[...].astype(F32)
    gate = gg_ref[...].astype(F32)
    gn = gn_ref[...]
    heads = []
    for h in range(GLA_HEADS):
        cols = slice(h * GLA_DV, (h + 1) * GLA_DV)
        sh = s[:, cols]
        ms = jnp.mean(sh * sh, axis=-1, keepdims=True)
        y = sh * lax.rsqrt(ms + RMS_EPS) * gn
        heads.append((y * _silu(gate[:, cols])).astype(BF16))
    o_gla = jnp.concatenate(heads, axis=1)
    acc = _dot(o_gla, w_ref[:GLA_WIDTH, :]) + _dot(na_ref[...], w_ref[GLA_WIDTH:, :])
    o_ref[...] = x_ref[...] + acc


def _out_call(x2d, o_f, o_b, zmain, o_na, gn, w_out, *, tm):
    m, d = x2d.shape
    assert m % tm == 0
    return pl.pallas_call(
        _out_kernel,
        name="out_proj",
        grid=(m // tm,),
        in_specs=[
            pl.BlockSpec((tm, d), lambda i: (i, 0)),
            pl.BlockSpec((tm, GLA_WIDTH), lambda i: (i, 0)),
            pl.BlockSpec((tm, GLA_WIDTH), lambda i: (i, 0)),
            pl.BlockSpec((tm, GLA_WIDTH), lambda i: (i, COL_GG // GLA_WIDTH)),
            pl.BlockSpec((tm, NA_WIDTH), lambda i: (i, 0)),
            pl.BlockSpec((1, GLA_DV), lambda i: (0, 0), pipeline_mode=pl.Buffered(1)),
            pl.BlockSpec((GLA_WIDTH + NA_WIDTH, d), lambda i: (0, 0), pipeline_mode=pl.Buffered(1)),
        ],
        out_specs=pl.BlockSpec((tm, d), lambda i: (i, 0)),
        out_shape=jax.ShapeDtypeStruct((m, d), F32),
        compiler_params=pltpu.CompilerParams(
            dimension_semantics=("parallel",), vmem_limit_bytes=VMEM_LIMIT),
    )(x2d, o_f, o_b, zmain, o_na, gn, w_out)


def _pick(n, cands):
    for c in cands:
        if n % c == 0:
            return c
    raise ValueError(f"no tile for {n}")


def kernel(x, meta_tokens, norm_g, w_in, w_decay_fwd, b_decay_fwd, w_decay_bwd, b_decay_bwd,
           gla_out_norm_g, q_norm_g, k_norm_g, rpb, meta_bias, w_out):
    batch, seq, d = x.shape
    assert d == D_MODEL and seq % TILE == 0 and seq % GRID_W == 0
    depth = norm_g.shape[0]
    assert depth == 1
    l = 0
    x2d = x.reshape(batch * seq, d)
    g = norm_g[l].reshape(1, d)

    assert w_in.shape[1:] == (d, D_IN_PROJ)
    wt = jnp.swapaxes(w_in[l], 0, 1).astype(BF16)
    gain_qk = jnp.concatenate([jnp.tile(q_norm_g[l], NA_HEADS) * (NA_DH ** -0.5),
                               jnp.tile(k_norm_g[l], NA_HEADS)]).reshape(1, 2 * NA_WIDTH)

    zr = jnp.zeros((GLA_RANK, GLA_KEY_WIDTH), F32)
    ztail = jnp.zeros((LANES - 2 * GLA_RANK, GLA_KEY_WIDTH), F32)
    wd = jnp.stack([jnp.concatenate([w_decay_fwd[l], zr, ztail], axis=0),
                    jnp.concatenate([zr, w_decay_bwd[l], ztail], axis=0)]).astype(BF16)
    bias = jnp.stack([b_decay_fwd[l], b_decay_bwd[l]]).reshape(2, 1, GLA_KEY_WIDTH)
    tg, tg_m = _gla_constants()

    assert seq % (NA_WIN_H * GRID_W) == 0
    kgain_col = jnp.tile(k_norm_g[l], NA_HEADS).reshape(NA_WIDTH, 1)
    zmain, nq, nk_t, nv_cb, gtot = _proj_call(x2d, g, wt, gain_qk, kgain_col, wd, bias, tg,
                                              tm=NA_WIN_H * GRID_W, tokens=True, name="proj_tokens")
    zmain_m, _, nk_m, nv_m, _ = _proj_call(meta_tokens, g, wt, gain_qk, kgain_col, wd, bias, tg,
                                           tm=N_META, tokens=False, name="proj_meta")
    zmain_m_pad = jnp.pad(zmain_m, ((0, TILE - N_META), (0, 0)))

    o_f, o_b = _gla_call(zmain, gtot, zmain_m_pad, wd, bias, tg_m, batch=batch, seq=seq,
                         tokens_per_step=_pick(seq, (4096, 2048, 1024, 512, 256, 128)))

    rpb_pad = jnp.pad(rpb[l], ((0, 0), (0, 0), (0, 2 * NA_DH - rpb.shape[-1])))
    mb = meta_bias[l].reshape(NA_HEADS, 1, N_META)
    o_na = _na_call(nq, nk_t, nv_cb, zmain, nk_m, nv_m, rpb_pad, mb, batch=batch, seq=seq,
                    rows_per_step=_pick(batch * seq // GRID_W, (256, 128, 64, 32, 16)))

    out = _out_call(x2d, o_f, o_b, zmain, o_na, gla_out_norm_g[l].reshape(1, GLA_DV),
                    w_out[l].astype(BF16), tm=_pick(batch * seq, (1024, 512, 256, 128)))
    return out.reshape(batch, seq, d)
```

```python
import functools

import jax
import jax.numpy as jnp
import numpy as np
from jax import lax
from jax.experimental import pallas as pl
from jax.experimental.pallas import tpu as pltpu

F32 = jnp.float32
BF16 = jnp.bfloat16

D_MODEL = 1024
N_META = 16
GRID_W = 64
GLA_HEADS = 4
GLA_DK = 128
GLA_DV = 256
GLA_KEY_WIDTH = GLA_HEADS * GLA_DK
GLA_WIDTH = GLA_HEADS * GLA_DV
GLA_RANK = 16
GLA_TAU = 16.0
GLA_CHUNK = 64
NA_HEADS = 16
NA_DH = 64
NA_WIDTH = NA_HEADS * NA_DH
NA_WIN_H = 8
NA_WIN_W = 16
RMS_EPS = 1e-6
NEG_BIG = -1e30
LOG2_E = 1.4426950408889634
LANES = 128

COL_GV, COL_GG, COL_NG, COL_GQ, COL_GK, COL_R = 0, 1024, 2048, 3072, 3584, 4096
MAIN_W = COL_R + LANES
COL_DIR0, DIR_W = 3072, 3 * 512
MAIN_TOK_W = COL_DIR0 + 2 * DIR_W
NA_CBW = 16
NA_CB = GRID_W // NA_CBW
IN_GQ, IN_GK, IN_GV, IN_R, IN_GG, IN_NQ, IN_NK, IN_NV, IN_NG = 0, 512, 1024, 2048, 2080, 3104, 4128, 5152, 6176
D_IN_PROJ = 7200
PROJ_TN = 512

TILE = 2 * GLA_CHUNK
NA_ROW_GROUP = 16
VMEM_LIMIT = 58 * 1024 * 1024


def _split_bf16(x):
    hi = x.astype(BF16)
    lo = (x - hi.astype(F32)).astype(BF16)
    return hi, lo


def _dot(a, b):
    return jnp.dot(a, b, preferred_element_type=F32)


def _dot_nt(a, b):
    return lax.dot_general(a, b, (((1,), (1,)), ((), ())), preferred_element_type=F32)


def _dot_tn(a, b):
    return lax.dot_general(a, b, (((0,), (0,)), ((), ())), preferred_element_type=F32)


def _log_sigmoid(x):
    return jnp.minimum(x, 0.0) - jnp.log(1.0 + jnp.exp(-jnp.abs(x)))


def _silu(x):
    return x * (1.0 / (1.0 + jnp.exp(-x)))


SHARED_SEGMENTS = ((COL_GV, IN_GV, 1024), (COL_GG, IN_GG, 1024), (COL_NG, IN_NG, 1024))
META_SEGMENTS = ((COL_GQ, IN_GQ, 512), (COL_GK, IN_GK, 512),
                 (COL_R, IN_R, LANES))


def _store_rows(o_ref, c0, y, permute):
    n = y.shape[1]
    if not permute:
        o_ref[:, c0:c0 + n] = y
        return
    for rl in range(y.shape[0] // GRID_W):
        for cb in range(NA_CB):
            src = rl * GRID_W + cb * NA_CBW
            o_ref[cb, rl * NA_CBW:(rl + 1) * NA_CBW, c0:c0 + n] = y[src:src + NA_CBW]


def _gla_log_decay(r, wd_ref, bdec_ref, d):
    return (_log_sigmoid(_dot(r, wd_ref[d]) + bdec_ref[d]) * (1.0 / GLA_TAU)).astype(BF16)


def _gla_operands_store(q, k, log_decay, d, t, tg_ref, zm_ref, gtot_ref):
    rows = slice(t * TILE, (t + 1) * TILE)
    g = _dot(tg_ref[d], log_decay[rows])
    col = COL_DIR0 + d * DIR_W
    tots = []
    for c in range(TILE // GLA_CHUNK):
        last = c * GLA_CHUNK + (GLA_CHUNK - 1 if d == 0 else 0)
        tot = g[last:last + 1]
        gtot_ref[(t * TILE) // GLA_CHUNK + c:(t * TILE) // GLA_CHUNK + c + 1,
                 d * GLA_KEY_WIDTH:(d + 1) * GLA_KEY_WIDTH] = tot
        tots.append(jnp.broadcast_to(tot, (GLA_CHUNK, GLA_KEY_WIDTH)))
    dd = jnp.concatenate(tots, axis=0) - g
    zm_ref[rows, col:col + 512] = (q[rows] * jnp.exp(g)).astype(BF16)
    zm_ref[rows, col + 512:col + 1024] = (k[rows] * jnp.exp(-g)).astype(BF16)
    zm_ref[rows, col + 1024:col + 1536] = (k[rows] * jnp.exp(dd)).astype(BF16)


def _proj_kernel(x0_ref, xn_ref, g_ref, wt_ref, gain_ref, kgain_ref, wd_ref, bdec_ref, tg_ref,
                 zm_ref, nq_ref, nk_ref, nv_ref, gtot_ref, u_scr, *, tokens):
    def normed(x_ref):
        x = x_ref[...]
        ms = jnp.mean(x * x, axis=-1, keepdims=True)
        return (x * lax.rsqrt(ms + RMS_EPS) * g_ref[...]).astype(BF16)

    step = pl.program_id(0)

    @pl.when(step == 0)
    def _():
        u_scr[0] = normed(x0_ref)

    slot = step % 2
    u = u_scr[slot]
    tm = u.shape[0]

    def prepare_next():
        u_scr[1 - slot] = normed(xn_ref)

    def main_tile(dst, src, off, n):
        acc = _dot_nt(u, wt_ref[src + off:src + off + n, :])
        if dst == COL_GQ:
            acc = acc * (GLA_DK ** -0.5)
        if dst == COL_NG:
            acc = _silu(acc)
        zm_ref[:, dst + off:dst + off + n] = acc.astype(zm_ref.dtype)

    def nv_tile(off):
        acc = _dot_nt(u, wt_ref[IN_NV + off:IN_NV + off + PROJ_TN, :])
        _store_rows(nv_ref, off, acc.astype(nv_ref.dtype), tokens)

    def head_norm_rows(acc, goff):
        first = lax.broadcasted_iota(jnp.int32, (acc.shape[0], LANES), 1) < NA_DH
        cols = []
        for c in range(0, acc.shape[1], LANES):
            a = acc[:, c:c + LANES]
            sq = a * a
            lo = jnp.sum(jnp.where(first, sq, 0.0), axis=-1, keepdims=True)
            hi = jnp.sum(jnp.where(first, 0.0, sq), axis=-1, keepdims=True)
            inv = jnp.where(first, lax.rsqrt(lo * (1.0 / NA_DH) + RMS_EPS), lax.rsqrt(hi * (1.0 / NA_DH) + RMS_EPS))
            cols.append(a * inv)
        return (jnp.concatenate(cols, axis=1) * gain_ref[:, goff:goff + PROJ_TN]).astype(BF16)

    def nq_tile(off):
        acc = _dot_nt(u, wt_ref[IN_NQ + off:IN_NQ + off + PROJ_TN, :])
        nq_ref[:, off:off + PROJ_TN] = head_norm_rows(acc, off)

    def nk_rows_tile(off):
        acc = _dot_nt(u, wt_ref[IN_NK + off:IN_NK + off + PROJ_TN, :])
        nk_ref[:, off:off + PROJ_TN] = head_norm_rows(acc, NA_WIDTH + off)

    def nk_transposed_tile(off):
        u_cb = jnp.concatenate([u[rl * GRID_W + cb * NA_CBW:rl * GRID_W + (cb + 1) * NA_CBW]
                                for cb in range(NA_CB) for rl in range(tm // GRID_W)], axis=0)
        acc_t = _dot_nt(wt_ref[IN_NK + off:IN_NK + off + PROJ_TN, :], u_cb)
        for h in range(PROJ_TN // NA_DH):
            rows = slice(off + h * NA_DH, off + (h + 1) * NA_DH)
            blk = acc_t[h * NA_DH:(h + 1) * NA_DH]
            ms = jnp.sum(blk * blk, axis=0, keepdims=True) * (1.0 / NA_DH)
            y = (blk * lax.rsqrt(ms + RMS_EPS) * kgain_ref[rows, :]).astype(BF16)
            for cb in range(NA_CB):
                nk_ref[cb, rows, :] = y[:, cb * LANES:(cb + 1) * LANES]

    nk_tile = nk_transposed_tile if tokens else nk_rows_tile
    tiles = [functools.partial(nk_tile, off) for off in range(0, NA_WIDTH, PROJ_TN)]
    tiles += [functools.partial(nq_tile, off) for off in range(0, NA_WIDTH, PROJ_TN)]
    tiles += [functools.partial(nv_tile, off) for off in range(0, NA_WIDTH, PROJ_TN)]
    for dst, src, width in SHARED_SEGMENTS + (() if tokens else META_SEGMENTS):
        for off in range(0, width, PROJ_TN):
            tiles.append(functools.partial(main_tile, dst, src, off, min(PROJ_TN, width - off)))
    if not tokens:
        gtot_ref[...] = jnp.zeros_like(gtot_ref)
        for tile in tiles:
            tile()
        prepare_next()
        return

    r = _dot_nt(u, wt_ref[IN_R:IN_R + LANES, :]).astype(BF16)
    q = _dot_nt(u, wt_ref[IN_GQ:IN_GQ + GLA_KEY_WIDTH, :]) * (GLA_DK ** -0.5)
    k = _dot_nt(u, wt_ref[IN_GK:IN_GK + GLA_KEY_WIDTH, :])
    log_decay = [None, None]

    def decay_stage(d):
        log_decay[d] = _gla_log_decay(r, wd_ref, bdec_ref, d)

    stages = [functools.partial(decay_stage, 0), functools.partial(decay_stage, 1), None]
    stages += [functools.partial(lambda d, t: _gla_operands_store(q, k, log_decay[d], d, t, tg_ref, zm_ref, gtot_ref),
                                 d, t)
               for d in range(2) for t in range(tm // TILE)]
    for i, tile in enumerate(tiles):
        if i < len(stages) and stages[i] is not None:
            stages[i]()
        tile()
    prepare_next()
    assert len(stages) <= len(tiles)


def _proj_call(x2d, g, wt, gain_qk, kgain_col, wd, bdec, tg, *, tm, tokens, name):
    m, d = x2d.shape
    assert m % tm == 0 and wt.shape == (D_IN_PROJ, d)
    resident = lambda shape: pl.BlockSpec(shape, lambda i: (0,) * len(shape), pipeline_mode=pl.Buffered(1))
    plain = pl.BlockSpec((tm, NA_WIDTH), lambda i: (i, 0))
    plain_shape = jax.ShapeDtypeStruct((m, NA_WIDTH), BF16)
    if tokens:
        assert tm == NA_WIN_H * GRID_W and NA_CB * LANES == tm
        main_w = MAIN_TOK_W
        v_spec = pl.BlockSpec((NA_CB, tm // NA_CB, NA_WIDTH), lambda i: (0, i, 0))
        v_shape = jax.ShapeDtypeStruct((NA_CB, m // NA_CB, NA_WIDTH), BF16)
        k_spec = pl.BlockSpec((NA_CB, NA_WIDTH, LANES), lambda i: (i, 0, 0))
        k_shape = jax.ShapeDtypeStruct((m // LANES, NA_WIDTH, LANES), BF16)
        n_chunk_rows = tm // GLA_CHUNK
    else:
        main_w = MAIN_W
        v_spec, v_shape, k_spec, k_shape = plain, plain_shape, plain, plain_shape
        n_chunk_rows = 8
    gt_rows = (m // tm) * n_chunk_rows
    return pl.pallas_call(
        functools.partial(_proj_kernel, tokens=tokens),
        name=name,
        grid=(m // tm,),
        in_specs=[
            resident((tm, d)),
            pl.BlockSpec((tm, d), lambda i: (jnp.minimum(i + 1, m // tm - 1), 0)),
            resident((1, d)),
            resident((D_IN_PROJ, d)),
            resident((1, 2 * NA_WIDTH)),
            resident((NA_WIDTH, 1)),
            resident((2, LANES, GLA_KEY_WIDTH)),
            resident((2, 1, GLA_KEY_WIDTH)),
            resident((2, TILE, TILE)),
        ],
        out_specs=[pl.BlockSpec((tm, main_w), lambda i: (i, 0)), plain, k_spec, v_spec,
                   pl.BlockSpec((n_chunk_rows, 2 * GLA_KEY_WIDTH), lambda i: (i, 0))],
        out_shape=[jax.ShapeDtypeStruct((m, main_w), BF16), plain_shape, k_shape, v_shape,
                   jax.ShapeDtypeStruct((gt_rows, 2 * GLA_KEY_WIDTH), F32)],
        scratch_shapes=[pltpu.VMEM((2, tm, d), BF16)],
        compiler_params=pltpu.CompilerParams(
            dimension_semantics=("arbitrary",), vmem_limit_bytes=VMEM_LIMIT),
    )(x2d, x2d, g, wt, gain_qk, kgain_col, wd, bdec, tg)


def _gla_kernel(qf_ref, kif_ref, ksf_ref, vf_ref, gtf_ref, qb_ref, kib_ref, ksb_ref, vb_ref, gtb_ref,
                km_ref, vm_ref, rm_ref, wd_ref, b_ref, tgm_ref,
                of_ref, ob_ref, sf_ref, sb_ref, *, n_tiles):
    row = lax.broadcasted_iota(jnp.int32, (TILE, TILE), 0)
    col = lax.broadcasted_iota(jnp.int32, (TILE, TILE), 1)
    same = (row // GLA_CHUNK) == (col // GLA_CHUNK)
    masks = (same & (row >= col), same & (row <= col))

    @pl.when(pl.program_id(2) == 0)
    def _():
        rowm = lax.broadcasted_iota(jnp.int32, (TILE, GLA_DK), 0)
        lg = _log_sigmoid(_dot(rm_ref[...], wd_ref[...]) + b_ref[...]) * (1.0 / GLA_TAU)
        hi, lo = _split_bf16(jnp.where(rowm < N_META, lg, 0.0))
        d_m = _dot(tgm_ref[...], jnp.concatenate([hi, lo], axis=0))
        k_st = (km_ref[...].astype(F32) * jnp.exp(d_m)).astype(BF16)
        sf_ref[...] = _dot_tn(k_st, vm_ref[...])
        sb_ref[...] = jnp.zeros_like(sb_ref)

    items = []
    for t in range(n_tiles):
        items.append((0, t))
        items.append((1, n_tiles - 1 - t))
    refs = ((qf_ref, kif_ref, ksf_ref, vf_ref, of_ref), (qb_ref, kib_ref, ksb_ref, vb_ref, ob_ref))
    a_cols = (jnp.exp(jnp.transpose(gtf_ref[...])), jnp.exp(jnp.transpose(gtb_ref[...])))

    def rows_of(t):
        return slice(t * TILE, (t + 1) * TILE)

    amats = [jnp.where(masks[d], _dot_nt(refs[d][0][rows_of(t), :], refs[d][1][rows_of(t), :]), 0.0).astype(BF16)
             for d, t in items]

    def state_independent(idx):
        d, t = items[idx]
        v = refs[d][3][rows_of(t), :]
        k_st = refs[d][2][rows_of(t), :]
        u = [_dot_tn(k_st[h * GLA_CHUNK:(h + 1) * GLA_CHUNK], v[h * GLA_CHUNK:(h + 1) * GLA_CHUNK])
             for h in range(2)]
        return _dot(amats[idx], v), u

    state = [sf_ref[...], sb_ref[...]]
    ahead = state_independent(0)
    for idx, (d, t) in enumerate(items):
        o_intra, u = ahead
        if idx + 1 < len(items):
            ahead = state_independent(idx + 1)
        s = state[d]
        outs = [None, None]
        for h in ((0, 1) if d == 0 else (1, 0)):
            rows = slice(t * TILE + h * GLA_CHUNK, t * TILE + (h + 1) * GLA_CHUNK)
            outs[h] = o_intra[h * GLA_CHUNK:(h + 1) * GLA_CHUNK] + _dot(refs[d][0][rows, :], s.astype(BF16))
            chunk = 2 * t + h
            s = a_cols[d][:, chunk:chunk + 1] * s + u[h]
        state[d] = s
        refs[d][4][rows_of(t), :] = jnp.concatenate(outs, axis=0).astype(refs[d][4].dtype)
    sf_ref[...] = state[0]
    sb_ref[...] = state[1]


def _gla_constants():
    i = np.arange(TILE)
    same = (i[:, None] // GLA_CHUNK) == (i[None, :] // GLA_CHUNK)
    inc_f = same & (i[None, :] <= i[:, None])
    inc_b = same & (i[None, :] >= i[:, None])
    to_end_meta = i[None, :] > i[:, None]
    dup = lambda m: np.concatenate([m, m], axis=1)
    tg = jnp.asarray(np.stack([inc_f, inc_b]), BF16)
    return tg, jnp.asarray(dup(to_end_meta), BF16)


def _gla_call(zmain, gtot, zmain_m, wd, bias, tg_m, *, batch, seq, tokens_per_step):
    t_blk = tokens_per_step
    nb = seq // t_blk
    n_tiles = t_blk // TILE
    n_chunks = t_blk // GLA_CHUNK

    def tok(fwd):
        if fwd:
            return lambda b, hh, i: b * nb + i
        return lambda b, hh, i: b * nb + (nb - 1 - i)

    def dir_specs(d):
        t = tok(d == 0)
        c0 = (COL_DIR0 + d * DIR_W) // GLA_DK
        return [
            pl.BlockSpec((t_blk, GLA_DK), lambda b, hh, i: (t(b, hh, i), c0 + hh)),
            pl.BlockSpec((t_blk, GLA_DK), lambda b, hh, i: (t(b, hh, i), c0 + GLA_HEADS + hh)),
            pl.BlockSpec((t_blk, GLA_DK), lambda b, hh, i: (t(b, hh, i), c0 + 2 * GLA_HEADS + hh)),
            pl.BlockSpec((t_blk, GLA_DV), lambda b, hh, i: (t(b, hh, i), COL_GV // GLA_DV + hh)),
            pl.BlockSpec((n_chunks, GLA_DK), lambda b, hh, i: (t(b, hh, i), d * GLA_HEADS + hh)),
        ]

    in_specs = (
        dir_specs(0) + dir_specs(1)
        + [pl.BlockSpec((TILE, GLA_DK), lambda b, hh, i: (0, COL_GK // GLA_DK + hh)),
           pl.BlockSpec((TILE, GLA_DV), lambda b, hh, i: (0, COL_GV // GLA_DV + hh)),
           pl.BlockSpec((TILE, LANES), lambda b, hh, i: (0, COL_R // LANES)),
           pl.BlockSpec((None, LANES, GLA_DK), lambda b, hh, i: (0, 0, hh)),
           pl.BlockSpec((None, 1, GLA_DK), lambda b, hh, i: (0, 0, hh)),
           pl.BlockSpec((TILE, 2 * TILE), lambda b, hh, i: (0, 0))]
    )
    args = [zmain] * 4 + [gtot] + [zmain] * 4 + [gtot] + [zmain_m] * 3 + [wd, bias, tg_m]
    out_spec_f = pl.BlockSpec((t_blk, GLA_DV), lambda b, hh, i: (b * nb + i, hh))
    out_spec_b = pl.BlockSpec((t_blk, GLA_DV), lambda b, hh, i: (b * nb + (nb - 1 - i), hh))
    o_shape = jax.ShapeDtypeStruct((batch * seq, GLA_WIDTH), BF16)
    return pl.pallas_call(
        functools.partial(_gla_kernel, n_tiles=n_tiles),
        name="gla_scan",
        grid=(batch, GLA_HEADS, nb),
        in_specs=in_specs,
        out_specs=[out_spec_f, out_spec_b],
        out_shape=[o_shape, o_shape],
        scratch_shapes=[pltpu.VMEM((GLA_DK, GLA_DV), F32), pltpu.VMEM((GLA_DK, GLA_DV), F32)],
        compiler_params=pltpu.CompilerParams(
            dimension_semantics=("parallel", "parallel", "arbitrary"), vmem_limit_bytes=VMEM_LIMIT),
    )(*args)


NA_SEG_COLS = ((0, 24), (24, 40), (40, 64))
NA_SEG_ROW0 = (0, 48, 80)
NA_STACK = 2 * GRID_W


NA_BASE_MID = NA_WIN_H // 2 - 1
NA_VARIANTS = 2 * NA_WIN_H - 1


def _na_variant(base, a):
    return jnp.where(base == NA_BASE_MID, a, jnp.where(base < NA_BASE_MID, NA_WIN_H + base, NA_WIN_H - 1 + base))


def _na_build_bias(rpb_ref, tab_ref):
    def build(v, carry):
        a = jnp.where(v < NA_WIN_H, v, 0)
        base = jnp.where(v < NA_WIN_H, NA_BASE_MID,
                         jnp.where(v < NA_WIN_H + NA_BASE_MID, v - NA_WIN_H, v - (NA_WIN_H - 1)))
        for hh in range(2):
            bias_rows = [rpb_ref[hh, pl.ds(base + ((t - a + NA_WIN_H) & (NA_WIN_H - 1)), 1), :]
                         for t in range(NA_WIN_H)]
            for s, (c0, c1) in enumerate(NA_SEG_COLS):
                n = c1 - c0
                vrows = [jnp.broadcast_to(br, (n, LANES)) for br in bias_rows]
                row0 = NA_SEG_ROW0[s] + hh * n
                rho = lax.broadcasted_iota(jnp.int32, (n, LANES), 0)
                lane = lax.broadcasted_iota(jnp.int32, (n, LANES), 1)
                cs = jnp.clip(c0 + rho - NA_WIN_W // 2, 0, GRID_W - NA_WIN_W)
                for half in range(2):
                    cb = s + half
                    cp = NA_CBW * cb + lane % NA_CBW
                    valid = (cp >= cs) & (cp < cs + NA_WIN_W)
                    acc = None
                    for i in range(NA_WIN_H):
                        shift = (NA_CBW * i - NA_CBW * cb + c0 - (NA_WIN_W - 1)) % LANES
                        rolled = pltpu.roll(vrows[i], shift, 1, stride=1, stride_axis=0)
                        acc = rolled if acc is None else jnp.where(lane // NA_CBW == i, rolled, acc)
                    tab_ref[v, row0:row0 + n, half * LANES:(half + 1) * LANES] = jnp.where(valid, acc, NEG_BIG)
        return carry

    lax.fori_loop(0, NA_VARIANTS, build, 0)


def _na_kernel(q_ref, kt_ref, v_ref, gate_ref, km_ref, vm_ref, rpb_ref, mb_ref, o_ref, tab_ref, sc_ref, sm_ref,
               *, rows_per_step, n_rows):
    w = GRID_W

    @pl.when(pl.program_id(1) == 0)
    def _():
        _na_build_bias(rpb_ref, tab_ref)

    head0_q = lax.broadcasted_iota(jnp.int32, (w, LANES), 1) < NA_DH
    srow = lax.broadcasted_iota(jnp.int32, (NA_STACK, N_META), 0)
    is_h1 = jnp.zeros((NA_STACK, N_META), jnp.bool_)
    for s, (c0, c1) in enumerate(NA_SEG_COLS):
        lo = NA_SEG_ROW0[s] + (c1 - c0)
        is_h1 = is_h1 | ((srow >= lo) & (srow < lo + (c1 - c0)))
    mb = jnp.where(is_h1, mb_ref[1], mb_ref[0])
    km = km_ref[...]
    vm = vm_ref[...]
    seg_rows = [slice(NA_SEG_ROW0[s], NA_SEG_ROW0[s] + 2 * (c1 - c0)) for s, (c0, c1) in enumerate(NA_SEG_COLS)]
    blk_rows = [slice(seg_rows[max(cb - 1, 0)].start, seg_rows[min(cb, len(seg_rows) - 1)].stop)
                for cb in range(NA_CB)]

    n_tiles = n_rows // NA_WIN_H
    key_lane = lax.broadcasted_iota(jnp.int32, (LANES, LANES), 1)

    def window(rl):
        row = pl.program_id(1) * rows_per_step + rl
        b = row // n_rows
        r = row - b * n_rows
        rs = jnp.clip(r - NA_WIN_H // 2, 0, n_rows - NA_WIN_H)
        a = rs & (NA_WIN_H - 1)
        return (b * n_tiles + rs // NA_WIN_H, a, (b + 1) * n_tiles - 1,
                _na_variant(rs - r + (NA_WIN_H - 1), a))

    def scores_store(slot, j, rl):
        m, a, m_last, var = window(rl)
        q = q_ref[pl.ds(pl.multiple_of(rl * w, w), w), :].astype(F32)
        q0 = jnp.where(head0_q, q, 0.0)
        q1 = q - q0
        qs = jnp.concatenate([piece[c0:c1] for c0, c1 in NA_SEG_COLS for piece in (q0, q1)],
                             axis=0).astype(BF16)
        from_m = key_lane >= a * NA_CBW
        m1 = jnp.minimum(m + 1, m_last)
        blocks = [_dot(qs[blk_rows[cb]],
                       jnp.where(from_m, kt_ref[NA_CB * m + cb], kt_ref[NA_CB * m1 + cb]))
                  for cb in range(NA_CB)]
        for s in range(len(NA_SEG_COLS)):
            n2 = seg_rows[s].stop - seg_rows[s].start
            for half in range(2):
                cb = s + half
                a0 = seg_rows[s].start - blk_rows[cb].start
                lanes = slice(half * LANES, (half + 1) * LANES)
                sc_ref[slot, j, seg_rows[s], lanes] = blocks[cb][a0:a0 + n2] + tab_ref[var, seg_rows[s], lanes]
        sm_ref[slot, j] = _dot_nt(qs, km) + mb

    def value_block(cb, m, a):
        groups = []
        for t in range(NA_WIN_H):
            row = NA_WIN_H * m + t + jnp.where(t < a, NA_WIN_H, 0)
            groups.append(v_ref[cb, pl.ds(pl.multiple_of(row * NA_CBW, NA_CBW), NA_CBW), :])
        return jnp.concatenate(groups, axis=0)

    def softmax_values(slot, j, rl):
        m_tile, a, _, _ = window(rl)
        sc = sc_ref[slot, j]
        sm = sm_ref[slot, j]
        m = jnp.maximum(sc[:, :LANES], sc[:, LANES:])
        m = jnp.maximum(jnp.max(m, axis=-1, keepdims=True), jnp.max(sm, axis=-1, keepdims=True))
        p = jnp.exp2(sc - m)
        pm = jnp.exp2(sm - m)
        den = (jnp.sum(p[:, :LANES] + p[:, LANES:], axis=-1, keepdims=True)
               + jnp.sum(pm, axis=-1, keepdims=True))
        p = p.astype(BF16)
        vals = [value_block(cb, m_tile, a) for cb in range(NA_CB)]
        o_meta = _dot(pm.astype(BF16), vm)
        inv = 1.0 / den
        outs = []
        for s, (c0, c1) in enumerate(NA_SEG_COLS):
            o_loc = _dot(p[seg_rows[s]], jnp.concatenate([vals[s], vals[s + 1]], axis=0))
            o = (o_loc + o_meta[seg_rows[s]]) * inv[seg_rows[s]]
            n = c1 - c0
            head0 = lax.broadcasted_iota(jnp.int32, (n, LANES), 1) < NA_DH
            outs.append(jnp.where(head0, o[:n], o[n:]))
        o = jnp.concatenate(outs, axis=0)
        qoff = pl.multiple_of(rl * w, w)
        gate = gate_ref[pl.ds(qoff, w), :].astype(F32)
        o_ref[pl.ds(qoff, w), :] = (o * gate).astype(o_ref.dtype)

    n_groups = rows_per_step // NA_ROW_GROUP
    for j in range(NA_ROW_GROUP):
        scores_store(0, j, j)

    def trip(g, carry):
        slot = g % 2
        for j in range(NA_ROW_GROUP):
            scores_store(slot, j, g * NA_ROW_GROUP + j)
        for j in range(NA_ROW_GROUP):
            softmax_values(1 - slot, j, (g - 1) * NA_ROW_GROUP + j)
        return carry

    lax.fori_loop(1, n_groups, trip, 0)
    for j in range(NA_ROW_GROUP):
        softmax_values((n_groups - 1) % 2, j, (n_groups - 1) * NA_ROW_GROUP + j)


def _na_call(nq, nk_t, nv_cb, zmain, nk_m, nv_m, rpb_pad, meta_bias, *, batch, seq, rows_per_step):
    n_rows = seq // GRID_W
    total_rows = batch * n_rows
    assert n_rows % NA_WIN_H == 0 and total_rows % rows_per_step == 0 and rows_per_step % NA_ROW_GROUP == 0
    nsteps = total_rows // rows_per_step
    hp = NA_HEADS // 2
    tq = rows_per_step * GRID_W
    n_idx = 2 * NA_WIN_H - 1
    in_specs = [
        pl.BlockSpec((tq, LANES), lambda p, i: (i, p)),
        pl.BlockSpec((batch * seq // LANES, LANES, LANES), lambda p, i: (0, p, 0)),
        pl.BlockSpec((NA_CB, batch * seq // NA_CB, LANES), lambda p, i: (0, 0, p)),
        pl.BlockSpec((tq, LANES), lambda p, i: (i, COL_NG // LANES + p)),
        pl.BlockSpec((N_META, LANES), lambda p, i: (0, p)),
        pl.BlockSpec((N_META, LANES), lambda p, i: (0, p)),
        pl.BlockSpec((2, n_idx, LANES), lambda p, i: (p, 0, 0)),
        pl.BlockSpec((2, 1, N_META), lambda p, i: (p, 0, 0)),
    ]
    return pl.pallas_call(
        functools.partial(_na_kernel, rows_per_step=rows_per_step, n_rows=n_rows),
        name="na_attn",
        grid=(hp, nsteps),
        in_specs=in_specs,
        out_specs=pl.BlockSpec((tq, LANES), lambda p, i: (i, p)),
        out_shape=jax.ShapeDtypeStruct((batch * seq, NA_WIDTH), BF16),
        scratch_shapes=[pltpu.VMEM((NA_VARIANTS, NA_STACK, 2 * LANES), F32),
                        pltpu.VMEM((2, NA_ROW_GROUP, NA_STACK, 2 * LANES), F32),
                        pltpu.VMEM((2, NA_ROW_GROUP, NA_STACK, N_META), F32)],
        compiler_params=pltpu.CompilerParams(
            dimension_semantics=("arbitrary", "arbitrary"), vmem_limit_bytes=VMEM_LIMIT),
    )(nq, nk_t, nv_cb, zmain, nk_m, nv_m, rpb_pad, meta_bias)


def _out_kernel(x_ref, of_ref, ob_ref, gg_ref, na_ref, gn_ref, w_ref, o_ref):
    s = of_ref[...].astype(F32) + ob_ref[...].astype(F32)
    gate = gg_ref[...].astype(F32)
    gn = gn_ref[...]
    heads = []
    for h in range(GLA_HEADS):
        cols = slice(h * GLA_DV, (h + 1) * GLA_DV)
        sh = s[:, cols]
        ms = jnp.mean(sh * sh, axis=-1, keepdims=True)
        y = sh * lax.rsqrt(ms + RMS_EPS) * gn
        heads.append((y * _silu(gate[:, cols])).astype(BF16))
    o_gla = jnp.concatenate(heads, axis=1)
    acc = _dot(o_gla, w_ref[:GLA_WIDTH, :]) + _dot(na_ref[...], w_ref[GLA_WIDTH:, :])
    o_ref[...] = x_ref[...] + acc


def _out_call(x2d, o_f, o_b, zmain, o_na, gn, w_out, *, tm):
    m, d = x2d.shape
    assert m % tm == 0
    return pl.pallas_call(
        _out_kernel,
        name="out_proj",
        grid=(m // tm,),
        in_specs=[
            pl.BlockSpec((tm, d), lambda i: (i, 0)),
            pl.BlockSpec((tm, GLA_WIDTH), lambda i: (i, 0)),
            pl.BlockSpec((tm, GLA_WIDTH), lambda i: (i, 0)),
            pl.BlockSpec((tm, GLA_WIDTH), lambda i: (i, COL_GG // GLA_WIDTH)),
            pl.BlockSpec((tm, NA_WIDTH), lambda i: (i, 0)),
            pl.BlockSpec((1, GLA_DV), lambda i: (0, 0), pipeline_mode=pl.Buffered(1)),
            pl.BlockSpec((GLA_WIDTH + NA_WIDTH, d), lambda i: (0, 0), pipeline_mode=pl.Buffered(1)),
        ],
        out_specs=pl.BlockSpec((tm, d), lambda i: (i, 0)),
        out_shape=jax.ShapeDtypeStruct((m, d), F32),
        compiler_params=pltpu.CompilerParams(
            dimension_semantics=("parallel",), vmem_limit_bytes=VMEM_LIMIT),
    )(x2d, o_f, o_b, zmain, o_na, gn, w_out)


def _pick(n, cands):
    for c in cands:
        if n % c == 0:
            return c
    raise ValueError(f"no tile for {n}")


def kernel(x, meta_tokens, norm_g, w_in, w_decay_fwd, b_decay_fwd, w_decay_bwd, b_decay_bwd,
           gla_out_norm_g, q_norm_g, k_norm_g, rpb, meta_bias, w_out):
    batch, seq, d = x.shape
    assert d == D_MODEL and seq % TILE == 0 and seq % GRID_W == 0
    depth = norm_g.shape[0]
    assert depth == 1
    l = 0
    x2d = x.reshape(batch * seq, d)
    g = norm_g[l].reshape(1, d)

    assert w_in.shape[1:] == (d, D_IN_PROJ)
    wt = jnp.swapaxes(w_in[l], 0, 1).astype(BF16)
    gain_qk = jnp.concatenate([jnp.tile(q_norm_g[l], NA_HEADS) * (NA_DH ** -0.5 * LOG2_E),
                               jnp.tile(k_norm_g[l], NA_HEADS)]).reshape(1, 2 * NA_WIDTH)

    zr = jnp.zeros((GLA_RANK, GLA_KEY_WIDTH), F32)
    ztail = jnp.zeros((LANES - 2 * GLA_RANK, GLA_KEY_WIDTH), F32)
    wd = jnp.stack([jnp.concatenate([w_decay_fwd[l], zr, ztail], axis=0),
                    jnp.concatenate([zr, w_decay_bwd[l], ztail], axis=0)]).astype(BF16)
    bias = jnp.stack([b_decay_fwd[l], b_decay_bwd[l]]).reshape(2, 1, GLA_KEY_WIDTH)
    tg, tg_m = _gla_constants()

    assert seq % (NA_WIN_H * GRID_W) == 0
    kgain_col = jnp.tile(k_norm_g[l], NA_HEADS).reshape(NA_WIDTH, 1)
    zmain, nq, nk_t, nv_cb, gtot = _proj_call(x2d, g, wt, gain_qk, kgain_col, wd, bias, tg,
                                              tm=NA_WIN_H * GRID_W, tokens=True, name="proj_tokens")
    zmain_m, _, nk_m, nv_m, _ = _proj_call(meta_tokens, g, wt, gain_qk, kgain_col, wd, bias, tg,
                                           tm=N_META, tokens=False, name="proj_meta")
    zmain_m_pad = jnp.pad(zmain_m, ((0, TILE - N_META), (0, 0)))

    o_f, o_b = _gla_call(zmain, gtot, zmain_m_pad, wd, bias, tg_m, batch=batch, seq=seq,
                         tokens_per_step=_pick(seq, (4096, 2048, 1024, 512, 256, 128)))

    rpb_pad = jnp.pad(rpb[l] * LOG2_E, ((0, 0), (0, 0), (0, 2 * NA_DH - rpb.shape[-1])))
    mb = (meta_bias[l] * LOG2_E).reshape(NA_HEADS, 1, N_META)
    o_na = _na_call(nq, nk_t, nv_cb, zmain, nk_m, nv_m, rpb_pad, mb, batch=batch, seq=seq,
                    rows_per_step=_pick(batch * seq // GRID_W, (256, 128, 64, 32, 16)))

    out = _out_call(x2d, o_f, o_b, zmain, o_na, gla_out_norm_g[l].reshape(1, GLA_DV),
                    w_out[l].astype(BF16), tm=_pick(batch * seq, (1024, 512, 256, 128)))
    return out.reshape(batch, seq, d)
```

```python
import functools

import jax
import jax.numpy as jnp
import numpy as np
from jax import lax
from jax.experimental import pallas as pl
from jax.experimental.pallas import tpu as pltpu

F32 = jnp.float32
BF16 = jnp.bfloat16

D_MODEL = 1024
N_META = 16
GRID_W = 64
GLA_HEADS = 4
GLA_DK = 128
GLA_DV = 256
GLA_KEY_WIDTH = GLA_HEADS * GLA_DK
GLA_WIDTH = GLA_HEADS * GLA_DV
GLA_RANK = 16
GLA_TAU = 16.0
GLA_CHUNK = 64
NA_HEADS = 16
NA_DH = 64
NA_WIDTH = NA_HEADS * NA_DH
NA_WIN_H = 8
NA_WIN_W = 16
RMS_EPS = 1e-6
NEG_BIG = -1e30
LOG2_E = 1.4426950408889634
LANES = 128

COL_GV, COL_GG, COL_NG, COL_GQ, COL_GK, COL_R = 0, 1024, 2048, 3072, 3584, 4096
MAIN_W = COL_R + LANES
COL_DIR0, DIR_W = 3072, 3 * 512
MAIN_TOK_W = COL_DIR0 + 2 * DIR_W
NA_CBW = 16
NA_CB = GRID_W // NA_CBW
IN_GQ, IN_GK, IN_GV, IN_R, IN_GG, IN_NQ, IN_NK, IN_NV, IN_NG = 0, 512, 1024, 2048, 2080, 3104, 4128, 5152, 6176
D_IN_PROJ = 7200
PROJ_TN = 512

TILE = 2 * GLA_CHUNK
NA_ROW_GROUP = 16
VMEM_LIMIT = 58 * 1024 * 1024


def _split_bf16(x):
    hi = x.astype(BF16)
    lo = (x - hi.astype(F32)).astype(BF16)
    return hi, lo


def _dot(a, b):
    return jnp.dot(a, b, preferred_element_type=F32)


def _dot_nt(a, b):
    return lax.dot_general(a, b, (((1,), (1,)), ((), ())), preferred_element_type=F32)


def _dot_tn(a, b):
    return lax.dot_general(a, b, (((0,), (0,)), ((), ())), preferred_element_type=F32)


def _log_sigmoid(x):
    return jnp.minimum(x, 0.0) - jnp.log(1.0 + jnp.exp(-jnp.abs(x)))


def _silu(x):
    return x * (1.0 / (1.0 + jnp.exp(-x)))


SHARED_SEGMENTS = ((COL_GV, IN_GV, 1024), (COL_GG, IN_GG, 1024), (COL_NG, IN_NG, 1024))
META_SEGMENTS = ((COL_GQ, IN_GQ, 512), (COL_GK, IN_GK, 512),
                 (COL_R, IN_R, LANES))


def _store_rows(o_ref, c0, y, permute):
    n = y.shape[1]
    if not permute:
        o_ref[:, c0:c0 + n] = y
        return
    for rl in range(y.shape[0] // GRID_W):
        for cb in range(NA_CB):
            src = rl * GRID_W + cb * NA_CBW
            o_ref[cb, rl * NA_CBW:(rl + 1) * NA_CBW, c0:c0 + n] = y[src:src + NA_CBW]


def _gla_log_decay(r, wd_ref, bdec_ref, d):
    return (_log_sigmoid(_dot(r, wd_ref[d]) + bdec_ref[d]) * (1.0 / GLA_TAU)).astype(BF16)


def _gla_operands_store(q, k, log_decay, d, t, tg_ref, zm_ref, gtot_ref):
    rows = slice(t * TILE, (t + 1) * TILE)
    g = _dot(tg_ref[d], log_decay[rows])
    col = COL_DIR0 + d * DIR_W
    tots = []
    for c in range(TILE // GLA_CHUNK):
        last = c * GLA_CHUNK + (GLA_CHUNK - 1 if d == 0 else 0)
        tot = g[last:last + 1]
        gtot_ref[(t * TILE) // GLA_CHUNK + c:(t * TILE) // GLA_CHUNK + c + 1,
                 d * GLA_KEY_WIDTH:(d + 1) * GLA_KEY_WIDTH] = tot
        tots.append(jnp.broadcast_to(tot, (GLA_CHUNK, GLA_KEY_WIDTH)))
    dd = jnp.concatenate(tots, axis=0) - g
    zm_ref[rows, col:col + 512] = (q[rows] * jnp.exp(g)).astype(BF16)
    zm_ref[rows, col + 512:col + 1024] = (k[rows] * jnp.exp(-g)).astype(BF16)
    zm_ref[rows, col + 1024:col + 1536] = (k[rows] * jnp.exp(dd)).astype(BF16)


def _proj_kernel(x0_ref, xn_ref, g_ref, wt_ref, gain_ref, kgain_ref, wd_ref, bdec_ref, tg_ref,
                 zm_ref, nq_ref, nk_ref, nv_ref, gtot_ref, u_scr, *, tokens):
    def normed(x_ref):
        x = x_ref[...]
        ms = jnp.mean(x * x, axis=-1, keepdims=True)
        return (x * lax.rsqrt(ms + RMS_EPS) * g_ref[...]).astype(BF16)

    step = pl.program_id(0)

    @pl.when(step == 0)
    def _():
        u_scr[0] = normed(x0_ref)

    slot = step % 2
    u = u_scr[slot]
    tm = u.shape[0]

    def prepare_next():
        u_scr[1 - slot] = normed(xn_ref)

    def main_tile(dst, src, off, n):
        acc = _dot_nt(u, wt_ref[src + off:src + off + n, :])
        if dst == COL_GQ:
            acc = acc * (GLA_DK ** -0.5)
        zm_ref[:, dst + off:dst + off + n] = acc.astype(zm_ref.dtype)

    def nv_tile(off):
        acc = _dot_nt(u, wt_ref[IN_NV + off:IN_NV + off + PROJ_TN, :])
        _store_rows(nv_ref, off, acc.astype(nv_ref.dtype), tokens)

    def head_norm_rows(acc, goff):
        first = lax.broadcasted_iota(jnp.int32, (acc.shape[0], LANES), 1) < NA_DH
        cols = []
        for c in range(0, acc.shape[1], LANES):
            a = acc[:, c:c + LANES]
            sq = a * a
            lo = jnp.sum(jnp.where(first, sq, 0.0), axis=-1, keepdims=True)
            hi = jnp.sum(jnp.where(first, 0.0, sq), axis=-1, keepdims=True)
            inv = jnp.where(first, lax.rsqrt(lo * (1.0 / NA_DH) + RMS_EPS), lax.rsqrt(hi * (1.0 / NA_DH) + RMS_EPS))
            cols.append(a * inv)
        return (jnp.concatenate(cols, axis=1) * gain_ref[:, goff:goff + PROJ_TN]).astype(BF16)

    def nq_tile(off):
        acc = _dot_nt(u, wt_ref[IN_NQ + off:IN_NQ + off + PROJ_TN, :])
        nq_ref[:, off:off + PROJ_TN] = head_norm_rows(acc, off)

    def nk_rows_tile(off):
        acc = _dot_nt(u, wt_ref[IN_NK + off:IN_NK + off + PROJ_TN, :])
        nk_ref[:, off:off + PROJ_TN] = head_norm_rows(acc, NA_WIDTH + off)

    def nk_transposed_tile(off):
        u_cb = jnp.concatenate([u[rl * GRID_W + cb * NA_CBW:rl * GRID_W + (cb + 1) * NA_CBW]
                                for cb in range(NA_CB) for rl in range(tm // GRID_W)], axis=0)
        acc_t = _dot_nt(wt_ref[IN_NK + off:IN_NK + off + PROJ_TN, :], u_cb)
        for h in range(PROJ_TN // NA_DH):
            rows = slice(off + h * NA_DH, off + (h + 1) * NA_DH)
            blk = acc_t[h * NA_DH:(h + 1) * NA_DH]
            ms = jnp.sum(blk * blk, axis=0, keepdims=True) * (1.0 / NA_DH)
            y = (blk * lax.rsqrt(ms + RMS_EPS) * kgain_ref[rows, :]).astype(BF16)
            for cb in range(NA_CB):
                nk_ref[cb, rows, :] = y[:, cb * LANES:(cb + 1) * LANES]

    nk_tile = nk_transposed_tile if tokens else nk_rows_tile
    tiles = [functools.partial(nk_tile, off) for off in range(0, NA_WIDTH, PROJ_TN)]
    tiles += [functools.partial(nq_tile, off) for off in range(0, NA_WIDTH, PROJ_TN)]
    tiles += [functools.partial(nv_tile, off) for off in range(0, NA_WIDTH, PROJ_TN)]
    for dst, src, width in SHARED_SEGMENTS + (() if tokens else META_SEGMENTS):
        for off in range(0, width, PROJ_TN):
            tiles.append(functools.partial(main_tile, dst, src, off, min(PROJ_TN, width - off)))
    if not tokens:
        gtot_ref[...] = jnp.zeros_like(gtot_ref)
        for tile in tiles:
            tile()
        prepare_next()
        return

    r = _dot_nt(u, wt_ref[IN_R:IN_R + LANES, :]).astype(BF16)
    q = _dot_nt(u, wt_ref[IN_GQ:IN_GQ + GLA_KEY_WIDTH, :]) * (GLA_DK ** -0.5)
    k = _dot_nt(u, wt_ref[IN_GK:IN_GK + GLA_KEY_WIDTH, :])
    log_decay = [None, None]

    def decay_stage(d):
        log_decay[d] = _gla_log_decay(r, wd_ref, bdec_ref, d)

    stages = [functools.partial(decay_stage, 0), functools.partial(decay_stage, 1), None]
    stages += [functools.partial(lambda d, t: _gla_operands_store(q, k, log_decay[d], d, t, tg_ref, zm_ref, gtot_ref),
                                 d, t)
               for d in range(2) for t in range(tm // TILE)]
    for i, tile in enumerate(tiles):
        if i < len(stages) and stages[i] is not None:
            stages[i]()
        tile()
    prepare_next()
    assert len(stages) <= len(tiles)


def _proj_call(x2d, g, wt, gain_qk, kgain_col, wd, bdec, tg, *, tm, tokens, name):
    m, d = x2d.shape
    assert m % tm == 0 and wt.shape == (D_IN_PROJ, d)
    resident = lambda shape: pl.BlockSpec(shape, lambda i: (0,) * len(shape), pipeline_mode=pl.Buffered(1))
    plain = pl.BlockSpec((tm, NA_WIDTH), lambda i: (i, 0))
    plain_shape = jax.ShapeDtypeStruct((m, NA_WIDTH), BF16)
    if tokens:
        assert tm == NA_WIN_H * GRID_W and NA_CB * LANES == tm
        main_w = MAIN_TOK_W
        v_spec = pl.BlockSpec((NA_CB, tm // NA_CB, NA_WIDTH), lambda i: (0, i, 0))
        v_shape = jax.ShapeDtypeStruct((NA_CB, m // NA_CB, NA_WIDTH), BF16)
        k_spec = pl.BlockSpec((NA_CB, NA_WIDTH, LANES), lambda i: (i, 0, 0))
        k_shape = jax.ShapeDtypeStruct((m // LANES, NA_WIDTH, LANES), BF16)
        n_chunk_rows = tm // GLA_CHUNK
    else:
        main_w = MAIN_W
        v_spec, v_shape, k_spec, k_shape = plain, plain_shape, plain, plain_shape
        n_chunk_rows = 8
    gt_rows = (m // tm) * n_chunk_rows
    return pl.pallas_call(
        functools.partial(_proj_kernel, tokens=tokens),
        name=name,
        grid=(m // tm,),
        in_specs=[
            resident((tm, d)),
            pl.BlockSpec((tm, d), lambda i: (jnp.minimum(i + 1, m // tm - 1), 0)),
            resident((1, d)),
            resident((D_IN_PROJ, d)),
            resident((1, 2 * NA_WIDTH)),
            resident((NA_WIDTH, 1)),
            resident((2, LANES, GLA_KEY_WIDTH)),
            resident((2, 1, GLA_KEY_WIDTH)),
            resident((2, TILE, TILE)),
        ],
        out_specs=[pl.BlockSpec((tm, main_w), lambda i: (i, 0)), plain, k_spec, v_spec,
                   pl.BlockSpec((n_chunk_rows, 2 * GLA_KEY_WIDTH), lambda i: (i, 0))],
        out_shape=[jax.ShapeDtypeStruct((m, main_w), BF16), plain_shape, k_shape, v_shape,
                   jax.ShapeDtypeStruct((gt_rows, 2 * GLA_KEY_WIDTH), F32)],
        scratch_shapes=[pltpu.VMEM((2, tm, d), BF16)],
        compiler_params=pltpu.CompilerParams(
            dimension_semantics=("arbitrary",), vmem_limit_bytes=VMEM_LIMIT),
    )(x2d, x2d, g, wt, gain_qk, kgain_col, wd, bdec, tg)


def _gla_kernel(qf_ref, kif_ref, ksf_ref, vf_ref, gtf_ref, qb_ref, kib_ref, ksb_ref, vb_ref, gtb_ref,
                km_ref, vm_ref, rm_ref, wd_ref, b_ref, tgm_ref,
                of_ref, ob_ref, sf_ref, sb_ref, *, n_tiles):
    row = lax.broadcasted_iota(jnp.int32, (TILE, TILE), 0)
    col = lax.broadcasted_iota(jnp.int32, (TILE, TILE), 1)
    same = (row // GLA_CHUNK) == (col // GLA_CHUNK)
    masks = (same & (row >= col), same & (row <= col))

    @pl.when(pl.program_id(2) == 0)
    def _():
        rowm = lax.broadcasted_iota(jnp.int32, (TILE, GLA_DK), 0)
        lg = _log_sigmoid(_dot(rm_ref[...], wd_ref[...]) + b_ref[...]) * (1.0 / GLA_TAU)
        hi, lo = _split_bf16(jnp.where(rowm < N_META, lg, 0.0))
        d_m = _dot(tgm_ref[...], jnp.concatenate([hi, lo], axis=0))
        k_st = (km_ref[...].astype(F32) * jnp.exp(d_m)).astype(BF16)
        sf_ref[...] = _dot_tn(k_st, vm_ref[...])
        sb_ref[...] = jnp.zeros_like(sb_ref)

    items = []
    for t in range(n_tiles):
        items.append((0, t))
        items.append((1, n_tiles - 1 - t))
    refs = ((qf_ref, kif_ref, ksf_ref, vf_ref, of_ref), (qb_ref, kib_ref, ksb_ref, vb_ref, ob_ref))
    a_cols = (jnp.exp(jnp.transpose(gtf_ref[...])), jnp.exp(jnp.transpose(gtb_ref[...])))

    def rows_of(t):
        return slice(t * TILE, (t + 1) * TILE)

    amats = [jnp.where(masks[d], _dot_nt(refs[d][0][rows_of(t), :], refs[d][1][rows_of(t), :]), 0.0).astype(BF16)
             for d, t in items]

    def state_independent(idx):
        d, t = items[idx]
        v = refs[d][3][rows_of(t), :]
        k_st = refs[d][2][rows_of(t), :]
        u = [_dot_tn(k_st[h * GLA_CHUNK:(h + 1) * GLA_CHUNK], v[h * GLA_CHUNK:(h + 1) * GLA_CHUNK])
             for h in range(2)]
        return _dot(amats[idx], v), u

    state = [sf_ref[...], sb_ref[...]]
    ahead = state_independent(0)
    for idx, (d, t) in enumerate(items):
        o_intra, u = ahead
        if idx + 1 < len(items):
            ahead = state_independent(idx + 1)
        s = state[d]
        outs = [None, None]
        for h in ((0, 1) if d == 0 else (1, 0)):
            rows = slice(t * TILE + h * GLA_CHUNK, t * TILE + (h + 1) * GLA_CHUNK)
            outs[h] = o_intra[h * GLA_CHUNK:(h + 1) * GLA_CHUNK] + _dot(refs[d][0][rows, :], s.astype(BF16))
            chunk = 2 * t + h
            s = a_cols[d][:, chunk:chunk + 1] * s + u[h]
        state[d] = s
        refs[d][4][rows_of(t), :] = jnp.concatenate(outs, axis=0).astype(refs[d][4].dtype)
    sf_ref[...] = state[0]
    sb_ref[...] = state[1]


def _gla_constants():
    i = np.arange(TILE)
    same = (i[:, None] // GLA_CHUNK) == (i[None, :] // GLA_CHUNK)
    inc_f = same & (i[None, :] <= i[:, None])
    inc_b = same & (i[None, :] >= i[:, None])
    to_end_meta = i[None, :] > i[:, None]
    dup = lambda m: np.concatenate([m, m], axis=1)
    tg = jnp.asarray(np.stack([inc_f, inc_b]), BF16)
    return tg, jnp.asarray(dup(to_end_meta), BF16)


def _gla_call(zmain, gtot, zmain_m, wd, bias, tg_m, *, batch, seq, tokens_per_step):
    t_blk = tokens_per_step
    nb = seq // t_blk
    n_tiles = t_blk // TILE
    n_chunks = t_blk // GLA_CHUNK

    def tok(fwd):
        if fwd:
            return lambda b, hh, i: b * nb + i
        return lambda b, hh, i: b * nb + (nb - 1 - i)

    def dir_specs(d):
        t = tok(d == 0)
        c0 = (COL_DIR0 + d * DIR_W) // GLA_DK
        return [
            pl.BlockSpec((t_blk, GLA_DK), lambda b, hh, i: (t(b, hh, i), c0 + hh)),
            pl.BlockSpec((t_blk, GLA_DK), lambda b, hh, i: (t(b, hh, i), c0 + GLA_HEADS + hh)),
            pl.BlockSpec((t_blk, GLA_DK), lambda b, hh, i: (t(b, hh, i), c0 + 2 * GLA_HEADS + hh)),
            pl.BlockSpec((t_blk, GLA_DV), lambda b, hh, i: (t(b, hh, i), COL_GV // GLA_DV + hh)),
            pl.BlockSpec((n_chunks, GLA_DK), lambda b, hh, i: (t(b, hh, i), d * GLA_HEADS + hh)),
        ]

    in_specs = (
        dir_specs(0) + dir_specs(1)
        + [pl.BlockSpec((TILE, GLA_DK), lambda b, hh, i: (0, COL_GK // GLA_DK + hh)),
           pl.BlockSpec((TILE, GLA_DV), lambda b, hh, i: (0, COL_GV // GLA_DV + hh)),
           pl.BlockSpec((TILE, LANES), lambda b, hh, i: (0, COL_R // LANES)),
           pl.BlockSpec((None, LANES, GLA_DK), lambda b, hh, i: (0, 0, hh)),
           pl.BlockSpec((None, 1, GLA_DK), lambda b, hh, i: (0, 0, hh)),
           pl.BlockSpec((TILE, 2 * TILE), lambda b, hh, i: (0, 0))]
    )
    args = [zmain] * 4 + [gtot] + [zmain] * 4 + [gtot] + [zmain_m] * 3 + [wd, bias, tg_m]
    out_spec_f = pl.BlockSpec((t_blk, GLA_DV), lambda b, hh, i: (b * nb + i, hh))
    out_spec_b = pl.BlockSpec((t_blk, GLA_DV), lambda b, hh, i: (b * nb + (nb - 1 - i), hh))
    o_shape = jax.ShapeDtypeStruct((batch * seq, GLA_WIDTH), BF16)
    return pl.pallas_call(
        functools.partial(_gla_kernel, n_tiles=n_tiles),
        name="gla_scan",
        grid=(batch, GLA_HEADS, nb),
        in_specs=in_specs,
        out_specs=[out_spec_f, out_spec_b],
        out_shape=[o_shape, o_shape],
        scratch_shapes=[pltpu.VMEM((GLA_DK, GLA_DV), F32), pltpu.VMEM((GLA_DK, GLA_DV), F32)],
        compiler_params=pltpu.CompilerParams(
            dimension_semantics=("parallel", "parallel", "arbitrary"), vmem_limit_bytes=VMEM_LIMIT),
    )(*args)


NA_SEG_COLS = ((0, 24), (24, 40), (40, 64))
NA_SEG_ROW0 = (0, 48, 80)
NA_STACK = 2 * GRID_W


NA_BASE_MID = NA_WIN_H // 2 - 1
NA_VARIANTS = 2 * NA_WIN_H - 1


def _na_variant(base, a):
    return jnp.where(base == NA_BASE_MID, a, jnp.where(base < NA_BASE_MID, NA_WIN_H + base, NA_WIN_H - 1 + base))


def _na_build_bias(rpb_ref, tab_ref):
    def build(v, carry):
        a = jnp.where(v < NA_WIN_H, v, 0)
        base = jnp.where(v < NA_WIN_H, NA_BASE_MID,
                         jnp.where(v < NA_WIN_H + NA_BASE_MID, v - NA_WIN_H, v - (NA_WIN_H - 1)))
        for hh in range(2):
            bias_rows = [rpb_ref[hh, pl.ds(base + ((t - a + NA_WIN_H) & (NA_WIN_H - 1)), 1), :]
                         for t in range(NA_WIN_H)]
            for s, (c0, c1) in enumerate(NA_SEG_COLS):
                n = c1 - c0
                vrows = [jnp.broadcast_to(br, (n, LANES)) for br in bias_rows]
                row0 = NA_SEG_ROW0[s] + hh * n
                rho = lax.broadcasted_iota(jnp.int32, (n, LANES), 0)
                lane = lax.broadcasted_iota(jnp.int32, (n, LANES), 1)
                cs = jnp.clip(c0 + rho - NA_WIN_W // 2, 0, GRID_W - NA_WIN_W)
                for half in range(2):
                    cb = s + half
                    cp = NA_CBW * cb + lane % NA_CBW
                    valid = (cp >= cs) & (cp < cs + NA_WIN_W)
                    acc = None
                    for i in range(NA_WIN_H):
                        shift = (NA_CBW * i - NA_CBW * cb + c0 - (NA_WIN_W - 1)) % LANES
                        rolled = pltpu.roll(vrows[i], shift, 1, stride=1, stride_axis=0)
                        acc = rolled if acc is None else jnp.where(lane // NA_CBW == i, rolled, acc)
                    tab_ref[v, row0:row0 + n, half * LANES:(half + 1) * LANES] = jnp.where(valid, acc, NEG_BIG)
        return carry

    lax.fori_loop(0, NA_VARIANTS, build, 0)


def _na_kernel(q_ref, kt_ref, v_ref, gate_ref, km_ref, vm_ref, rpb_ref, mb_ref, o_ref, tab_ref, sc_ref, sm_ref,
               *, rows_per_step, n_rows):
    w = GRID_W

    @pl.when(pl.program_id(1) == 0)
    def _():
        _na_build_bias(rpb_ref, tab_ref)

    head0_q = lax.broadcasted_iota(jnp.int32, (w, LANES), 1) < NA_DH
    srow = lax.broadcasted_iota(jnp.int32, (NA_STACK, N_META), 0)
    is_h1 = jnp.zeros((NA_STACK, N_META), jnp.bool_)
    for s, (c0, c1) in enumerate(NA_SEG_COLS):
        lo = NA_SEG_ROW0[s] + (c1 - c0)
        is_h1 = is_h1 | ((srow >= lo) & (srow < lo + (c1 - c0)))
    mb = jnp.where(is_h1, mb_ref[1], mb_ref[0])
    km = km_ref[...]
    vm = vm_ref[...]
    seg_rows = [slice(NA_SEG_ROW0[s], NA_SEG_ROW0[s] + 2 * (c1 - c0)) for s, (c0, c1) in enumerate(NA_SEG_COLS)]
    blk_rows = [slice(seg_rows[max(cb - 1, 0)].start, seg_rows[min(cb, len(seg_rows) - 1)].stop)
                for cb in range(NA_CB)]

    n_tiles = n_rows // NA_WIN_H
    key_lane = lax.broadcasted_iota(jnp.int32, (LANES, LANES), 1)

    def window(rl):
        row = pl.program_id(1) * rows_per_step + rl
        b = row // n_rows
        r = row - b * n_rows
        rs = jnp.clip(r - NA_WIN_H // 2, 0, n_rows - NA_WIN_H)
        a = rs & (NA_WIN_H - 1)
        return (b * n_tiles + rs // NA_WIN_H, a, (b + 1) * n_tiles - 1,
                _na_variant(rs - r + (NA_WIN_H - 1), a))

    def scores_store(slot, j, rl):
        m, a, m_last, var = window(rl)
        q = q_ref[pl.ds(pl.multiple_of(rl * w, w), w), :].astype(F32)
        q0 = jnp.where(head0_q, q, 0.0)
        q1 = q - q0
        qs = jnp.concatenate([piece[c0:c1] for c0, c1 in NA_SEG_COLS for piece in (q0, q1)],
                             axis=0).astype(BF16)
        from_m = key_lane >= a * NA_CBW
        m1 = jnp.minimum(m + 1, m_last)
        blocks = [_dot(qs[blk_rows[cb]],
                       jnp.where(from_m, kt_ref[NA_CB * m + cb], kt_ref[NA_CB * m1 + cb]))
                  for cb in range(NA_CB)]
        for s in range(len(NA_SEG_COLS)):
            n2 = seg_rows[s].stop - seg_rows[s].start
            for half in range(2):
                cb = s + half
                a0 = seg_rows[s].start - blk_rows[cb].start
                lanes = slice(half * LANES, (half + 1) * LANES)
                sc_ref[slot, j, seg_rows[s], lanes] = blocks[cb][a0:a0 + n2] + tab_ref[var, seg_rows[s], lanes]
        sm_ref[slot, j] = _dot_nt(qs, km) + mb

    def value_block(cb, m, a):
        groups = []
        for t in range(NA_WIN_H):
            row = NA_WIN_H * m + t + jnp.where(t < a, NA_WIN_H, 0)
            groups.append(v_ref[cb, pl.ds(pl.multiple_of(row * NA_CBW, NA_CBW), NA_CBW), :])
        return jnp.concatenate(groups, axis=0)

    def softmax_values(slot, j, rl):
        m_tile, a, _, _ = window(rl)
        sc = sc_ref[slot, j]
        sm = sm_ref[slot, j]
        m = jnp.maximum(sc[:, :LANES], sc[:, LANES:])
        m = jnp.maximum(jnp.max(m, axis=-1, keepdims=True), jnp.max(sm, axis=-1, keepdims=True))
        p = jnp.exp2(sc - m)
        pm = jnp.exp2(sm - m)
        den = (jnp.sum(p[:, :LANES] + p[:, LANES:], axis=-1, keepdims=True)
               + jnp.sum(pm, axis=-1, keepdims=True))
        p = p.astype(BF16)
        vals = [value_block(cb, m_tile, a) for cb in range(NA_CB)]
        o_meta = _dot(pm.astype(BF16), vm)
        inv = 1.0 / den
        outs = []
        for s, (c0, c1) in enumerate(NA_SEG_COLS):
            o_loc = _dot(p[seg_rows[s]], jnp.concatenate([vals[s], vals[s + 1]], axis=0))
            o = (o_loc + o_meta[seg_rows[s]]) * inv[seg_rows[s]]
            n = c1 - c0
            head0 = lax.broadcasted_iota(jnp.int32, (n, LANES), 1) < NA_DH
            outs.append(jnp.where(head0, o[:n], o[n:]))
        o = jnp.concatenate(outs, axis=0)
        qoff = pl.multiple_of(rl * w, w)
        gate = gate_ref[pl.ds(qoff, w), :].astype(F32)
        o_ref[pl.ds(qoff, w), :] = (o * _silu(gate)).astype(o_ref.dtype)

    n_groups = rows_per_step // NA_ROW_GROUP
    for j in range(NA_ROW_GROUP):
        scores_store(0, j, j)

    def trip(g, carry):
        slot = g % 2
        for j in range(NA_ROW_GROUP):
            scores_store(slot, j, g * NA_ROW_GROUP + j)
        for j in range(NA_ROW_GROUP):
            softmax_values(1 - slot, j, (g - 1) * NA_ROW_GROUP + j)
        return carry

    lax.fori_loop(1, n_groups, trip, 0)
    for j in range(NA_ROW_GROUP):
        softmax_values((n_groups - 1) % 2, j, (n_groups - 1) * NA_ROW_GROUP + j)


def _na_call(nq, nk_t, nv_cb, zmain, nk_m, nv_m, rpb_pad, meta_bias, *, batch, seq, rows_per_step):
    n_rows = seq // GRID_W
    total_rows = batch * n_rows
    assert n_rows % NA_WIN_H == 0 and total_rows % rows_per_step == 0 and rows_per_step % NA_ROW_GROUP == 0
    nsteps = total_rows // rows_per_step
    hp = NA_HEADS // 2
    tq = rows_per_step * GRID_W
    n_idx = 2 * NA_WIN_H - 1
    in_specs = [
        pl.BlockSpec((tq, LANES), lambda p, i: (i, p)),
        pl.BlockSpec((batch * seq // LANES, LANES, LANES), lambda p, i: (0, p, 0)),
        pl.BlockSpec((NA_CB, batch * seq // NA_CB, LANES), lambda p, i: (0, 0, p)),
        pl.BlockSpec((tq, LANES), lambda p, i: (i, COL_NG // LANES + p)),
        pl.BlockSpec((N_META, LANES), lambda p, i: (0, p)),
        pl.BlockSpec((N_META, LANES), lambda p, i: (0, p)),
        pl.BlockSpec((2, n_idx, LANES), lambda p, i: (p, 0, 0)),
        pl.BlockSpec((2, 1, N_META), lambda p, i: (p, 0, 0)),
    ]
    return pl.pallas_call(
        functools.partial(_na_kernel, rows_per_step=rows_per_step, n_rows=n_rows),
        name="na_attn",
        grid=(hp, nsteps),
        in_specs=in_specs,
        out_specs=pl.BlockSpec((tq, LANES), lambda p, i: (i, p)),
        out_shape=jax.ShapeDtypeStruct((batch * seq, NA_WIDTH), BF16),
        scratch_shapes=[pltpu.VMEM((NA_VARIANTS, NA_STACK, 2 * LANES), F32),
                        pltpu.VMEM((2, NA_ROW_GROUP, NA_STACK, 2 * LANES), F32),
                        pltpu.VMEM((2, NA_ROW_GROUP, NA_STACK, N_META), F32)],
        compiler_params=pltpu.CompilerParams(
            dimension_semantics=("arbitrary", "arbitrary"), vmem_limit_bytes=VMEM_LIMIT),
    )(nq, nk_t, nv_cb, zmain, nk_m, nv_m, rpb_pad, meta_bias)


def _out_kernel(x_ref, of_ref, ob_ref, gg_ref, na_ref, gn_ref, w_ref, o_ref):
    s = of_ref[...].astype(F32) + ob_ref[...].astype(F32)
    gate = gg_ref[...].astype(F32)
    gn = gn_ref[...]
    heads = []
    for h in range(GLA_HEADS):
        cols = slice(h * GLA_DV, (h + 1) * GLA_DV)
        sh = s[:, cols]
        ms = jnp.mean(sh * sh, axis=-1, keepdims=True)
        y = sh * lax.rsqrt(ms + RMS_EPS) * gn
        heads.append((y * _silu(gate[:, cols])).astype(BF16))
    o_gla = jnp.concatenate(heads, axis=1)
    acc = _dot(o_gla, w_ref[:GLA_WIDTH, :]) + _dot(na_ref[...], w_ref[GLA_WIDTH:, :])
    o_ref[...] = x_ref[...] + acc


def _out_call(x2d, o_f, o_b, zmain, o_na, gn, w_out, *, tm):
    m, d = x2d.shape
    assert m % tm == 0
    return pl.pallas_call(
        _out_kernel,
        name="out_proj",
        grid=(m // tm,),
        in_specs=[
            pl.BlockSpec((tm, d), lambda i: (i, 0)),
            pl.BlockSpec((tm, GLA_WIDTH), lambda i: (i, 0)),
            pl.BlockSpec((tm, GLA_WIDTH), lambda i: (i, 0)),
            pl.BlockSpec((tm, GLA_WIDTH), lambda i: (i, COL_GG // GLA_WIDTH)),
            pl.BlockSpec((tm, NA_WIDTH), lambda i: (i, 0)),
            pl.BlockSpec((1, GLA_DV), lambda i: (0, 0), pipeline_mode=pl.Buffered(1)),
            pl.BlockSpec((GLA_WIDTH + NA_WIDTH, d), lambda i: (0, 0), pipeline_mode=pl.Buffered(1)),
        ],
        out_specs=pl.BlockSpec((tm, d), lambda i: (i, 0)),
        out_shape=jax.ShapeDtypeStruct((m, d), F32),
        compiler_params=pltpu.CompilerParams(
            dimension_semantics=("parallel",), vmem_limit_bytes=VMEM_LIMIT),
    )(x2d, o_f, o_b, zmain, o_na, gn, w_out)


def _pick(n, cands):
    for c in cands:
        if n % c == 0:
            return c
    raise ValueError(f"no tile for {n}")


def kernel(x, meta_tokens, norm_g, w_in, w_decay_fwd, b_decay_fwd, w_decay_bwd, b_decay_bwd,
           gla_out_norm_g, q_norm_g, k_norm_g, rpb, meta_bias, w_out):
    batch, seq, d = x.shape
    assert d == D_MODEL and seq % TILE == 0 and seq % GRID_W == 0
    depth = norm_g.shape[0]
    assert depth == 1
    l = 0
    x2d = x.reshape(batch * seq, d)
    g = norm_g[l].reshape(1, d)

    assert w_in.shape[1:] == (d, D_IN_PROJ)
    wt = jnp.swapaxes(w_in[l], 0, 1).astype(BF16)
    gain_qk = jnp.concatenate([jnp.tile(q_norm_g[l], NA_HEADS) * (NA_DH ** -0.5 * LOG2_E),
                               jnp.tile(k_norm_g[l], NA_HEADS)]).reshape(1, 2 * NA_WIDTH)

    zr = jnp.zeros((GLA_RANK, GLA_KEY_WIDTH), F32)
    ztail = jnp.zeros((LANES - 2 * GLA_RANK, GLA_KEY_WIDTH), F32)
    wd = jnp.stack([jnp.concatenate([w_decay_fwd[l], zr, ztail], axis=0),
                    jnp.concatenate([zr, w_decay_bwd[l], ztail], axis=0)]).astype(BF16)
    bias = jnp.stack([b_decay_fwd[l], b_decay_bwd[l]]).reshape(2, 1, GLA_KEY_WIDTH)
    tg, tg_m = _gla_constants()

    assert seq % (NA_WIN_H * GRID_W) == 0
    kgain_col = jnp.tile(k_norm_g[l], NA_HEADS).reshape(NA_WIDTH, 1)
    zmain, nq, nk_t, nv_cb, gtot = _proj_call(x2d, g, wt, gain_qk, kgain_col, wd, bias, tg,
                                              tm=NA_WIN_H * GRID_W, tokens=True, name="proj_tokens")
    zmain_m, _, nk_m, nv_m, _ = _proj_call(meta_tokens, g, wt, gain_qk, kgain_col, wd, bias, tg,
                                           tm=N_META, tokens=False, name="proj_meta")
    zmain_m_pad = jnp.pad(zmain_m, ((0, TILE - N_META), (0, 0)))

    o_f, o_b = _gla_call(zmain, gtot, zmain_m_pad, wd, bias, tg_m, batch=batch, seq=seq,
                         tokens_per_step=_pick(seq, (4096, 2048, 1024, 512, 256, 128)))

    rpb_pad = jnp.pad(rpb[l] * LOG2_E, ((0, 0), (0, 0), (0, 2 * NA_DH - rpb.shape[-1])))
    mb = (meta_bias[l] * LOG2_E).reshape(NA_HEADS, 1, N_META)
    o_na = _na_call(nq, nk_t, nv_cb, zmain, nk_m, nv_m, rpb_pad, mb, batch=batch, seq=seq,
                    rows_per_step=_pick(batch * seq // GRID_W, (256, 128, 64, 32, 16)))

    out = _out_call(x2d, o_f, o_b, zmain, o_na, gla_out_norm_g[l].reshape(1, GLA_DV),
                    w_out[l].astype(BF16), tm=_pick(batch * seq, (1024, 512, 256, 128)))
    return out.reshape(batch, seq, d)
```

```python
import functools

import jax
import jax.numpy as jnp
import numpy as np
from jax import lax
from jax.experimental import pallas as pl
from jax.experimental.pallas import tpu as pltpu

F32 = jnp.float32
BF16 = jnp.bfloat16

D_MODEL = 1024
N_META = 16
GRID_W = 64
GLA_HEADS = 4
GLA_DK = 128
GLA_DV = 256
GLA_KEY_WIDTH = GLA_HEADS * GLA_DK
GLA_WIDTH = GLA_HEADS * GLA_DV
GLA_RANK = 16
GLA_TAU = 16.0
GLA_CHUNK = 64
NA_HEADS = 16
NA_DH = 64
NA_WIDTH = NA_HEADS * NA_DH
NA_WIN_H = 8
NA_WIN_W = 16
RMS_EPS = 1e-6
NEG_BIG = -1e30
LOG2_E = 1.4426950408889634
LANES = 128

COL_GV, COL_GG, COL_NG, COL_GQ, COL_GK, COL_R = 0, 1024, 2048, 3072, 3584, 4096
MAIN_W = COL_R + LANES
COL_DIR0, DIR_W = 3072, 3 * 512
MAIN_TOK_W = COL_DIR0 + 2 * DIR_W
NA_CBW = 16
NA_CB = GRID_W // NA_CBW
IN_GQ, IN_GK, IN_GV, IN_R, IN_GG, IN_NQ, IN_NK, IN_NV, IN_NG = 0, 512, 1024, 2048, 2080, 3104, 4128, 5152, 6176
D_IN_PROJ = 7200
PROJ_TN = 512

TILE = 2 * GLA_CHUNK
NA_ROW_GROUP = 16
VMEM_LIMIT = 58 * 1024 * 1024


def _split_bf16(x):
    hi = x.astype(BF16)
    lo = (x - hi.astype(F32)).astype(BF16)
    return hi, lo


def _dot(a, b):
    return jnp.dot(a, b, preferred_element_type=F32)


def _dot_nt(a, b):
    return lax.dot_general(a, b, (((1,), (1,)), ((), ())), preferred_element_type=F32)


def _dot_tn(a, b):
    return lax.dot_general(a, b, (((0,), (0,)), ((), ())), preferred_element_type=F32)


def _log_sigmoid(x):
    return jnp.minimum(x, 0.0) - jnp.log(1.0 + jnp.exp(-jnp.abs(x)))


def _silu(x):
    return x * (1.0 / (1.0 + jnp.exp(-x)))


SHARED_SEGMENTS = ((COL_GV, IN_GV, 1024), (COL_GG, IN_GG, 1024), (COL_NG, IN_NG, 1024))
META_SEGMENTS = ((COL_GQ, IN_GQ, 512), (COL_GK, IN_GK, 512),
                 (COL_R, IN_R, LANES))


def _store_rows(o_ref, c0, y, permute):
    n = y.shape[1]
    if not permute:
        o_ref[:, c0:c0 + n] = y
        return
    for rl in range(y.shape[0] // GRID_W):
        for cb in range(NA_CB):
            src = rl * GRID_W + cb * NA_CBW
            o_ref[cb, rl * NA_CBW:(rl + 1) * NA_CBW, c0:c0 + n] = y[src:src + NA_CBW]


def _gla_log_decay(r, wd_ref, bdec_ref, d):
    return (_log_sigmoid(_dot(r, wd_ref[d]) + bdec_ref[d]) * (1.0 / GLA_TAU)).astype(BF16)


def _gla_operands_store(q, k, log_decay, d, t, tg_ref, zm_ref, gtot_ref):
    rows = slice(t * TILE, (t + 1) * TILE)
    g = _dot(tg_ref[d], log_decay[rows])
    col = COL_DIR0 + d * DIR_W
    tots = []
    for c in range(TILE // GLA_CHUNK):
        last = c * GLA_CHUNK + (GLA_CHUNK - 1 if d == 0 else 0)
        tot = g[last:last + 1]
        gtot_ref[(t * TILE) // GLA_CHUNK + c:(t * TILE) // GLA_CHUNK + c + 1,
                 d * GLA_KEY_WIDTH:(d + 1) * GLA_KEY_WIDTH] = tot
        tots.append(jnp.broadcast_to(tot, (GLA_CHUNK, GLA_KEY_WIDTH)))
    dd = jnp.concatenate(tots, axis=0) - g
    zm_ref[rows, col:col + 512] = (q[rows] * jnp.exp(g)).astype(BF16)
    zm_ref[rows, col + 512:col + 1024] = (k[rows] * jnp.exp(-g)).astype(BF16)
    zm_ref[rows, col + 1024:col + 1536] = (k[rows] * jnp.exp(dd)).astype(BF16)


def _proj_kernel(x0_ref, xn_ref, g_ref, wt_ref, gain_ref, kgain_ref, wd_ref, bdec_ref, tg_ref,
                 zm_ref, nq_ref, nk_ref, nv_ref, gtot_ref, u_scr, *, tokens):
    def normed(x_ref):
        x = x_ref[...]
        ms = jnp.mean(x * x, axis=-1, keepdims=True)
        return (x * lax.rsqrt(ms + RMS_EPS) * g_ref[...]).astype(BF16)

    step = pl.program_id(0)

    @pl.when(step == 0)
    def _():
        u_scr[0] = normed(x0_ref)

    slot = step % 2
    u = u_scr[slot]
    tm = u.shape[0]

    def prepare_next():
        u_scr[1 - slot] = normed(xn_ref)

    def main_tile(dst, src, off, n):
        acc = _dot_nt(u, wt_ref[src + off:src + off + n, :])
        if dst == COL_GQ:
            acc = acc * (GLA_DK ** -0.5)
        zm_ref[:, dst + off:dst + off + n] = acc.astype(zm_ref.dtype)

    def nv_tile(off):
        acc = _dot_nt(u, wt_ref[IN_NV + off:IN_NV + off + PROJ_TN, :])
        _store_rows(nv_ref, off, acc.astype(nv_ref.dtype), tokens)

    def head_norm_rows(acc, goff):
        first = lax.broadcasted_iota(jnp.int32, (acc.shape[0], LANES), 1) < NA_DH
        cols = []
        for c in range(0, acc.shape[1], LANES):
            a = acc[:, c:c + LANES]
            sq = a * a
            lo = jnp.sum(jnp.where(first, sq, 0.0), axis=-1, keepdims=True)
            hi = jnp.sum(jnp.where(first, 0.0, sq), axis=-1, keepdims=True)
            inv = jnp.where(first, lax.rsqrt(lo * (1.0 / NA_DH) + RMS_EPS), lax.rsqrt(hi * (1.0 / NA_DH) + RMS_EPS))
            cols.append(a * inv)
        return (jnp.concatenate(cols, axis=1) * gain_ref[:, goff:goff + PROJ_TN]).astype(BF16)

    def nq_tile(off):
        acc = _dot_nt(u, wt_ref[IN_NQ + off:IN_NQ + off + PROJ_TN, :])
        nq_ref[:, off:off + PROJ_TN] = head_norm_rows(acc, off)

    def nk_rows_tile(off):
        acc = _dot_nt(u, wt_ref[IN_NK + off:IN_NK + off + PROJ_TN, :])
        nk_ref[:, off:off + PROJ_TN] = head_norm_rows(acc, NA_WIDTH + off)

    def nk_transposed_tile(off):
        u_cb = jnp.concatenate([u[rl * GRID_W + cb * NA_CBW:rl * GRID_W + (cb + 1) * NA_CBW]
                                for cb in range(NA_CB) for rl in range(tm // GRID_W)], axis=0)
        acc_t = _dot_nt(wt_ref[IN_NK + off:IN_NK + off + PROJ_TN, :], u_cb)
        for h in range(PROJ_TN // NA_DH):
            rows = slice(off + h * NA_DH, off + (h + 1) * NA_DH)
            blk = acc_t[h * NA_DH:(h + 1) * NA_DH]
            ms = jnp.sum(blk * blk, axis=0, keepdims=True) * (1.0 / NA_DH)
            y = (blk * lax.rsqrt(ms + RMS_EPS) * kgain_ref[rows, :]).astype(BF16)
            for cb in range(NA_CB):
                nk_ref[cb, rows, :] = y[:, cb * LANES:(cb + 1) * LANES]

    nk_tile = nk_transposed_tile if tokens else nk_rows_tile
    tiles = [functools.partial(nk_tile, off) for off in range(0, NA_WIDTH, PROJ_TN)]
    tiles += [functools.partial(nq_tile, off) for off in range(0, NA_WIDTH, PROJ_TN)]
    tiles += [functools.partial(nv_tile, off) for off in range(0, NA_WIDTH, PROJ_TN)]
    for dst, src, width in SHARED_SEGMENTS + (() if tokens else META_SEGMENTS):
        for off in range(0, width, PROJ_TN):
            tiles.append(functools.partial(main_tile, dst, src, off, min(PROJ_TN, width - off)))
    if not tokens:
        gtot_ref[...] = jnp.zeros_like(gtot_ref)
        for tile in tiles:
            tile()
        prepare_next()
        return

    r = _dot_nt(u, wt_ref[IN_R:IN_R + LANES, :]).astype(BF16)
    q = _dot_nt(u, wt_ref[IN_GQ:IN_GQ + GLA_KEY_WIDTH, :]) * (GLA_DK ** -0.5)
    k = _dot_nt(u, wt_ref[IN_GK:IN_GK + GLA_KEY_WIDTH, :])
    log_decay = [None, None]

    def decay_stage(d):
        log_decay[d] = _gla_log_decay(r, wd_ref, bdec_ref, d)

    stages = [functools.partial(decay_stage, 0), functools.partial(decay_stage, 1), None]
    stages += [functools.partial(lambda d, t: _gla_operands_store(q, k, log_decay[d], d, t, tg_ref, zm_ref, gtot_ref),
                                 d, t)
               for d in range(2) for t in range(tm // TILE)]
    for i, tile in enumerate(tiles):
        if i < len(stages) and stages[i] is not None:
            stages[i]()
        tile()
    prepare_next()
    assert len(stages) <= len(tiles)


def _proj_call(x2d, g, wt, gain_qk, kgain_col, wd, bdec, tg, *, tm, tokens, name):
    m, d = x2d.shape
    assert m % tm == 0 and wt.shape == (D_IN_PROJ, d)
    resident = lambda shape: pl.BlockSpec(shape, lambda i: (0,) * len(shape), pipeline_mode=pl.Buffered(1))
    plain = pl.BlockSpec((tm, NA_WIDTH), lambda i: (i, 0))
    plain_shape = jax.ShapeDtypeStruct((m, NA_WIDTH), BF16)
    if tokens:
        assert tm == NA_WIN_H * GRID_W and NA_CB * LANES == tm
        main_w = MAIN_TOK_W
        v_spec = pl.BlockSpec((NA_CB, tm // NA_CB, NA_WIDTH), lambda i: (0, i, 0))
        v_shape = jax.ShapeDtypeStruct((NA_CB, m // NA_CB, NA_WIDTH), BF16)
        k_spec = pl.BlockSpec((NA_CB, NA_WIDTH, LANES), lambda i: (i, 0, 0))
        k_shape = jax.ShapeDtypeStruct((m // LANES, NA_WIDTH, LANES), BF16)
        n_chunk_rows = tm // GLA_CHUNK
    else:
        main_w = MAIN_W
        v_spec, v_shape, k_spec, k_shape = plain, plain_shape, plain, plain_shape
        n_chunk_rows = 8
    gt_rows = (m // tm) * n_chunk_rows
    return pl.pallas_call(
        functools.partial(_proj_kernel, tokens=tokens),
        name=name,
        grid=(m // tm,),
        in_specs=[
            resident((tm, d)),
            pl.BlockSpec((tm, d), lambda i: (jnp.minimum(i + 1, m // tm - 1), 0)),
            resident((1, d)),
            resident((D_IN_PROJ, d)),
            resident((1, 2 * NA_WIDTH)),
            resident((NA_WIDTH, 1)),
            resident((2, LANES, GLA_KEY_WIDTH)),
            resident((2, 1, GLA_KEY_WIDTH)),
            resident((2, TILE, TILE)),
        ],
        out_specs=[pl.BlockSpec((tm, main_w), lambda i: (i, 0)), plain, k_spec, v_spec,
                   pl.BlockSpec((n_chunk_rows, 2 * GLA_KEY_WIDTH), lambda i: (i, 0))],
        out_shape=[jax.ShapeDtypeStruct((m, main_w), BF16), plain_shape, k_shape, v_shape,
                   jax.ShapeDtypeStruct((gt_rows, 2 * GLA_KEY_WIDTH), F32)],
        scratch_shapes=[pltpu.VMEM((2, tm, d), BF16)],
        compiler_params=pltpu.CompilerParams(
            dimension_semantics=("arbitrary",), vmem_limit_bytes=VMEM_LIMIT),
    )(x2d, x2d, g, wt, gain_qk, kgain_col, wd, bdec, tg)


def _gla_kernel(qf_ref, kif_ref, ksf_ref, vf_ref, gtf_ref, qb_ref, kib_ref, ksb_ref, vb_ref, gtb_ref,
                km_ref, vm_ref, rm_ref, wd_ref, b_ref, tgm_ref,
                o_ref, sf_ref, sb_ref, *, n_tiles):
    row = lax.broadcasted_iota(jnp.int32, (TILE, TILE), 0)
    col = lax.broadcasted_iota(jnp.int32, (TILE, TILE), 1)
    same = (row // GLA_CHUNK) == (col // GLA_CHUNK)
    masks = (same & (row >= col), same & (row <= col))

    @pl.when(pl.program_id(2) == 0)
    def _():
        rowm = lax.broadcasted_iota(jnp.int32, (TILE, GLA_DK), 0)
        lg = _log_sigmoid(_dot(rm_ref[...], wd_ref[...]) + b_ref[...]) * (1.0 / GLA_TAU)
        hi, lo = _split_bf16(jnp.where(rowm < N_META, lg, 0.0))
        d_m = _dot(tgm_ref[...], jnp.concatenate([hi, lo], axis=0))
        k_st = (km_ref[...].astype(F32) * jnp.exp(d_m)).astype(BF16)
        sf_ref[...] = _dot_tn(k_st, vm_ref[...])
        sb_ref[...] = jnp.zeros_like(sb_ref)

    def scan(step):
        items = []
        for t in range(n_tiles):
            items.append((0, t))
            items.append((1, n_tiles - 1 - t))
        refs = ((qf_ref, kif_ref, ksf_ref, vf_ref), (qb_ref, kib_ref, ksb_ref, vb_ref))
        out_half = (step, 1 - step)
        a_cols = (jnp.exp(jnp.transpose(gtf_ref[...])), jnp.exp(jnp.transpose(gtb_ref[...])))

        def rows_of(t):
            return slice(t * TILE, (t + 1) * TILE)

        amats = [jnp.where(masks[d], _dot_nt(refs[d][0][rows_of(t), :], refs[d][1][rows_of(t), :]), 0.0).astype(BF16)
                 for d, t in items]

        def state_independent(idx):
            d, t = items[idx]
            v = refs[d][3][rows_of(t), :]
            k_st = refs[d][2][rows_of(t), :]
            u = [_dot_tn(k_st[h * GLA_CHUNK:(h + 1) * GLA_CHUNK], v[h * GLA_CHUNK:(h + 1) * GLA_CHUNK])
                 for h in range(2)]
            return _dot(amats[idx], v), u

        state = [sf_ref[...], sb_ref[...]]
        ahead = state_independent(0)
        for idx, (d, t) in enumerate(items):
            o_intra, u = ahead
            if idx + 1 < len(items):
                ahead = state_independent(idx + 1)
            s = state[d]
            outs = [None, None]
            for h in ((0, 1) if d == 0 else (1, 0)):
                rows = slice(t * TILE + h * GLA_CHUNK, t * TILE + (h + 1) * GLA_CHUNK)
                outs[h] = o_intra[h * GLA_CHUNK:(h + 1) * GLA_CHUNK] + _dot(refs[d][0][rows, :], s.astype(BF16))
                chunk = 2 * t + h
                s = a_cols[d][:, chunk:chunk + 1] * s + u[h]
            state[d] = s
            o = jnp.concatenate(outs, axis=0)
            out_rows = rows_of(out_half[d] * n_tiles + t)
            if step == 1:
                o = o + o_ref[out_rows, :].astype(F32)
            o_ref[out_rows, :] = o.astype(o_ref.dtype)
        sf_ref[...] = state[0]
        sb_ref[...] = state[1]

    for step in range(2):
        pl.when(pl.program_id(2) == step)(functools.partial(scan, step))


def _gla_constants():
    i = np.arange(TILE)
    same = (i[:, None] // GLA_CHUNK) == (i[None, :] // GLA_CHUNK)
    inc_f = same & (i[None, :] <= i[:, None])
    inc_b = same & (i[None, :] >= i[:, None])
    to_end_meta = i[None, :] > i[:, None]
    dup = lambda m: np.concatenate([m, m], axis=1)
    tg = jnp.asarray(np.stack([inc_f, inc_b]), BF16)
    return tg, jnp.asarray(dup(to_end_meta), BF16)


def _gla_call(zmain, gtot, zmain_m, wd, bias, tg_m, *, batch, seq):
    nb = 2
    assert seq % (nb * TILE) == 0
    t_blk = seq // nb
    n_tiles = t_blk // TILE
    n_chunks = t_blk // GLA_CHUNK

    def tok(fwd):
        if fwd:
            return lambda b, hh, i: b * nb + i
        return lambda b, hh, i: b * nb + (nb - 1 - i)

    def dir_specs(d):
        t = tok(d == 0)
        c0 = (COL_DIR0 + d * DIR_W) // GLA_DK
        return [
            pl.BlockSpec((t_blk, GLA_DK), lambda b, hh, i: (t(b, hh, i), c0 + hh)),
            pl.BlockSpec((t_blk, GLA_DK), lambda b, hh, i: (t(b, hh, i), c0 + GLA_HEADS + hh)),
            pl.BlockSpec((t_blk, GLA_DK), lambda b, hh, i: (t(b, hh, i), c0 + 2 * GLA_HEADS + hh)),
            pl.BlockSpec((t_blk, GLA_DV), lambda b, hh, i: (t(b, hh, i), COL_GV // GLA_DV + hh)),
            pl.BlockSpec((n_chunks, GLA_DK), lambda b, hh, i: (t(b, hh, i), d * GLA_HEADS + hh)),
        ]

    in_specs = (
        dir_specs(0) + dir_specs(1)
        + [pl.BlockSpec((TILE, GLA_DK), lambda b, hh, i: (0, COL_GK // GLA_DK + hh)),
           pl.BlockSpec((TILE, GLA_DV), lambda b, hh, i: (0, COL_GV // GLA_DV + hh)),
           pl.BlockSpec((TILE, LANES), lambda b, hh, i: (0, COL_R // LANES)),
           pl.BlockSpec((None, LANES, GLA_DK), lambda b, hh, i: (0, 0, hh)),
           pl.BlockSpec((None, 1, GLA_DK), lambda b, hh, i: (0, 0, hh)),
           pl.BlockSpec((TILE, 2 * TILE), lambda b, hh, i: (0, 0))]
    )
    args = [zmain] * 4 + [gtot] + [zmain] * 4 + [gtot] + [zmain_m] * 3 + [wd, bias, tg_m]
    return pl.pallas_call(
        functools.partial(_gla_kernel, n_tiles=n_tiles),
        name="gla_scan",
        grid=(batch, GLA_HEADS, nb),
        in_specs=in_specs,
        out_specs=pl.BlockSpec((seq, GLA_DV), lambda b, hh, i: (b, hh)),
        out_shape=jax.ShapeDtypeStruct((batch * seq, GLA_WIDTH), BF16),
        scratch_shapes=[pltpu.VMEM((GLA_DK, GLA_DV), F32), pltpu.VMEM((GLA_DK, GLA_DV), F32)],
        compiler_params=pltpu.CompilerParams(
            dimension_semantics=("parallel", "parallel", "arbitrary"), vmem_limit_bytes=VMEM_LIMIT),
    )(*args)


NA_SEG_COLS = ((0, 24), (24, 40), (40, 64))
NA_SEG_ROW0 = (0, 48, 80)
NA_STACK = 2 * GRID_W


NA_BASE_MID = NA_WIN_H // 2 - 1
NA_VARIANTS = 2 * NA_WIN_H - 1


def _na_variant(base, a):
    return jnp.where(base == NA_BASE_MID, a, jnp.where(base < NA_BASE_MID, NA_WIN_H + base, NA_WIN_H - 1 + base))


def _na_build_bias(rpb_ref, tab_ref):
    def build(v, carry):
        a = jnp.where(v < NA_WIN_H, v, 0)
        base = jnp.where(v < NA_WIN_H, NA_BASE_MID,
                         jnp.where(v < NA_WIN_H + NA_BASE_MID, v - NA_WIN_H, v - (NA_WIN_H - 1)))
        for hh in range(2):
            bias_rows = [rpb_ref[hh, pl.ds(base + ((t - a + NA_WIN_H) & (NA_WIN_H - 1)), 1), :]
                         for t in range(NA_WIN_H)]
            for s, (c0, c1) in enumerate(NA_SEG_COLS):
                n = c1 - c0
                vrows = [jnp.broadcast_to(br, (n, LANES)) for br in bias_rows]
                row0 = NA_SEG_ROW0[s] + hh * n
                rho = lax.broadcasted_iota(jnp.int32, (n, LANES), 0)
                lane = lax.broadcasted_iota(jnp.int32, (n, LANES), 1)
                cs = jnp.clip(c0 + rho - NA_WIN_W // 2, 0, GRID_W - NA_WIN_W)
                for half in range(2):
                    cb = s + half
                    cp = NA_CBW * cb + lane % NA_CBW
                    valid = (cp >= cs) & (cp < cs + NA_WIN_W)
                    acc = None
                    for i in range(NA_WIN_H):
                        shift = (NA_CBW * i - NA_CBW * cb + c0 - (NA_WIN_W - 1)) % LANES
                        rolled = pltpu.roll(vrows[i], shift, 1, stride=1, stride_axis=0)
                        acc = rolled if acc is None else jnp.where(lane // NA_CBW == i, rolled, acc)
                    tab_ref[v, row0:row0 + n, half * LANES:(half + 1) * LANES] = jnp.where(valid, acc, NEG_BIG)
        return carry

    lax.fori_loop(0, NA_VARIANTS, build, 0)


def _na_kernel(q_ref, kt_ref, v_ref, gate_ref, km_ref, vm_ref, rpb_ref, mb_ref, o_ref, tab_ref, sc_ref, sm_ref,
               *, rows_per_step, n_rows):
    w = GRID_W

    @pl.when(pl.program_id(1) == 0)
    def _():
        _na_build_bias(rpb_ref, tab_ref)

    head0_q = lax.broadcasted_iota(jnp.int32, (w, LANES), 1) < NA_DH
    srow = lax.broadcasted_iota(jnp.int32, (NA_STACK, N_META), 0)
    is_h1 = jnp.zeros((NA_STACK, N_META), jnp.bool_)
    for s, (c0, c1) in enumerate(NA_SEG_COLS):
        lo = NA_SEG_ROW0[s] + (c1 - c0)
        is_h1 = is_h1 | ((srow >= lo) & (srow < lo + (c1 - c0)))
    mb = jnp.where(is_h1, mb_ref[1], mb_ref[0])
    km = km_ref[...]
    vm = vm_ref[...]
    seg_rows = [slice(NA_SEG_ROW0[s], NA_SEG_ROW0[s] + 2 * (c1 - c0)) for s, (c0, c1) in enumerate(NA_SEG_COLS)]
    blk_rows = [slice(seg_rows[max(cb - 1, 0)].start, seg_rows[min(cb, len(seg_rows) - 1)].stop)
                for cb in range(NA_CB)]

    n_tiles = n_rows // NA_WIN_H
    key_lane = lax.broadcasted_iota(jnp.int32, (LANES, LANES), 1)

    def window(rl):
        row = pl.program_id(1) * rows_per_step + rl
        b = row // n_rows
        r = row - b * n_rows
        rs = jnp.clip(r - NA_WIN_H // 2, 0, n_rows - NA_WIN_H)
        a = rs & (NA_WIN_H - 1)
        return (b * n_tiles + rs // NA_WIN_H, a, (b + 1) * n_tiles - 1,
                _na_variant(rs - r + (NA_WIN_H - 1), a))

    def scores_store(slot, j, rl):
        m, a, m_last, var = window(rl)
        q = q_ref[pl.ds(pl.multiple_of(rl * w, w), w), :].astype(F32)
        q0 = jnp.where(head0_q, q, 0.0)
        q1 = q - q0
        qs = jnp.concatenate([piece[c0:c1] for c0, c1 in NA_SEG_COLS for piece in (q0, q1)],
                             axis=0).astype(BF16)
        from_m = key_lane >= a * NA_CBW
        m1 = jnp.minimum(m + 1, m_last)
        blocks = [_dot(qs[blk_rows[cb]],
                       jnp.where(from_m, kt_ref[NA_CB * m + cb], kt_ref[NA_CB * m1 + cb]))
                  for cb in range(NA_CB)]
        for s in range(len(NA_SEG_COLS)):
            n2 = seg_rows[s].stop - seg_rows[s].start
            for half in range(2):
                cb = s + half
                a0 = seg_rows[s].start - blk_rows[cb].start
                lanes = slice(half * LANES, (half + 1) * LANES)
                sc_ref[slot, j, seg_rows[s], lanes] = blocks[cb][a0:a0 + n2] + tab_ref[var, seg_rows[s], lanes]
        sm_ref[slot, j] = _dot_nt(qs, km) + mb

    def value_block(cb, m, a):
        groups = []
        for t in range(NA_WIN_H):
            row = NA_WIN_H * m + t + jnp.where(t < a, NA_WIN_H, 0)
            groups.append(v_ref[cb, pl.ds(pl.multiple_of(row * NA_CBW, NA_CBW), NA_CBW), :])
        return jnp.concatenate(groups, axis=0)

    def softmax_values(slot, j, rl):
        m_tile, a, _, _ = window(rl)
        sc = sc_ref[slot, j]
        sm = sm_ref[slot, j]
        m = jnp.maximum(sc[:, :LANES], sc[:, LANES:])
        m = jnp.maximum(jnp.max(m, axis=-1, keepdims=True), jnp.max(sm, axis=-1, keepdims=True))
        p = jnp.exp2(sc - m)
        pm = jnp.exp2(sm - m)
        den = (jnp.sum(p[:, :LANES] + p[:, LANES:], axis=-1, keepdims=True)
               + jnp.sum(pm, axis=-1, keepdims=True))
        p = p.astype(BF16)
        vals = [value_block(cb, m_tile, a) for cb in range(NA_CB)]
        o_meta = _dot(pm.astype(BF16), vm)
        inv = 1.0 / den
        outs = []
        for s, (c0, c1) in enumerate(NA_SEG_COLS):
            o_loc = _dot(p[seg_rows[s]], jnp.concatenate([vals[s], vals[s + 1]], axis=0))
            o = (o_loc + o_meta[seg_rows[s]]) * inv[seg_rows[s]]
            n = c1 - c0
            head0 = lax.broadcasted_iota(jnp.int32, (n, LANES), 1) < NA_DH
            outs.append(jnp.where(head0, o[:n], o[n:]))
        o = jnp.concatenate(outs, axis=0)
        qoff = pl.multiple_of(rl * w, w)
        gate = gate_ref[pl.ds(qoff, w), :].astype(F32)
        o_ref[pl.ds(qoff, w), :] = (o * _silu(gate)).astype(o_ref.dtype)

    n_groups = rows_per_step // NA_ROW_GROUP
    for j in range(NA_ROW_GROUP):
        scores_store(0, j, j)

    def trip(g, carry):
        slot = g % 2
        for j in range(NA_ROW_GROUP):
            scores_store(slot, j, g * NA_ROW_GROUP + j)
        for j in range(NA_ROW_GROUP):
            softmax_values(1 - slot, j, (g - 1) * NA_ROW_GROUP + j)
        return carry

    lax.fori_loop(1, n_groups, trip, 0)
    for j in range(NA_ROW_GROUP):
        softmax_values((n_groups - 1) % 2, j, (n_groups - 1) * NA_ROW_GROUP + j)


def _na_call(nq, nk_t, nv_cb, zmain, nk_m, nv_m, rpb_pad, meta_bias, *, batch, seq, rows_per_step):
    n_rows = seq // GRID_W
    total_rows = batch * n_rows
    assert n_rows % NA_WIN_H == 0 and total_rows % rows_per_step == 0 and rows_per_step % NA_ROW_GROUP == 0
    nsteps = total_rows // rows_per_step
    hp = NA_HEADS // 2
    tq = rows_per_step * GRID_W
    n_idx = 2 * NA_WIN_H - 1
    in_specs = [
        pl.BlockSpec((tq, LANES), lambda p, i: (i, p)),
        pl.BlockSpec((batch * seq // LANES, LANES, LANES), lambda p, i: (0, p, 0)),
        pl.BlockSpec((NA_CB, batch * seq // NA_CB, LANES), lambda p, i: (0, 0, p)),
        pl.BlockSpec((tq, LANES), lambda p, i: (i, COL_NG // LANES + p)),
        pl.BlockSpec((N_META, LANES), lambda p, i: (0, p)),
        pl.BlockSpec((N_META, LANES), lambda p, i: (0, p)),
        pl.BlockSpec((2, n_idx, LANES), lambda p, i: (p, 0, 0)),
        pl.BlockSpec((2, 1, N_META), lambda p, i: (p, 0, 0)),
    ]
    return pl.pallas_call(
        functools.partial(_na_kernel, rows_per_step=rows_per_step, n_rows=n_rows),
        name="na_attn",
        grid=(hp, nsteps),
        in_specs=in_specs,
        out_specs=pl.BlockSpec((tq, LANES), lambda p, i: (i, p)),
        out_shape=jax.ShapeDtypeStruct((batch * seq, NA_WIDTH), BF16),
        scratch_shapes=[pltpu.VMEM((NA_VARIANTS, NA_STACK, 2 * LANES), F32),
                        pltpu.VMEM((2, NA_ROW_GROUP, NA_STACK, 2 * LANES), F32),
                        pltpu.VMEM((2, NA_ROW_GROUP, NA_STACK, N_META), F32)],
        compiler_params=pltpu.CompilerParams(
            dimension_semantics=("arbitrary", "arbitrary"), vmem_limit_bytes=VMEM_LIMIT),
    )(nq, nk_t, nv_cb, zmain, nk_m, nv_m, rpb_pad, meta_bias)


def _out_kernel(x_ref, os_ref, gg_ref, na_ref, gn_ref, w_ref, o_ref):
    s = os_ref[...].astype(F32)
    gate = gg_ref[...].astype(F32)
    gn = gn_ref[...]
    heads = []
    for h in range(GLA_HEADS):
        cols = slice(h * GLA_DV, (h + 1) * GLA_DV)
        sh = s[:, cols]
        ms = jnp.mean(sh * sh, axis=-1, keepdims=True)
        y = sh * lax.rsqrt(ms + RMS_EPS) * gn
        heads.append((y * _silu(gate[:, cols])).astype(BF16))
    o_gla = jnp.concatenate(heads, axis=1)
    acc = _dot(o_gla, w_ref[:GLA_WIDTH, :]) + _dot(na_ref[...], w_ref[GLA_WIDTH:, :])
    o_ref[...] = x_ref[...] + acc


def _out_call(x2d, o_sum, zmain, o_na, gn, w_out, *, tm):
    m, d = x2d.shape
    assert m % tm == 0
    return pl.pallas_call(
        _out_kernel,
        name="out_proj",
        grid=(m // tm,),
        in_specs=[
            pl.BlockSpec((tm, d), lambda i: (i, 0)),
            pl.BlockSpec((tm, GLA_WIDTH), lambda i: (i, 0)),
            pl.BlockSpec((tm, GLA_WIDTH), lambda i: (i, COL_GG // GLA_WIDTH)),
            pl.BlockSpec((tm, NA_WIDTH), lambda i: (i, 0)),
            pl.BlockSpec((1, GLA_DV), lambda i: (0, 0), pipeline_mode=pl.Buffered(1)),
            pl.BlockSpec((GLA_WIDTH + NA_WIDTH, d), lambda i: (0, 0), pipeline_mode=pl.Buffered(1)),
        ],
        out_specs=pl.BlockSpec((tm, d), lambda i: (i, 0)),
        out_shape=jax.ShapeDtypeStruct((m, d), F32),
        compiler_params=pltpu.CompilerParams(
            dimension_semantics=("parallel",), vmem_limit_bytes=VMEM_LIMIT),
    )(x2d, o_sum, zmain, o_na, gn, w_out)


def _pick(n, cands):
    for c in cands:
        if n % c == 0:
            return c
    raise ValueError(f"no tile for {n}")


def kernel(x, meta_tokens, norm_g, w_in, w_decay_fwd, b_decay_fwd, w_decay_bwd, b_decay_bwd,
           gla_out_norm_g, q_norm_g, k_norm_g, rpb, meta_bias, w_out):
    batch, seq, d = x.shape
    assert d == D_MODEL and seq % TILE == 0 and seq % GRID_W == 0
    depth = norm_g.shape[0]
    assert depth == 1
    l = 0
    x2d = x.reshape(batch * seq, d)
    g = norm_g[l].reshape(1, d)

    assert w_in.shape[1:] == (d, D_IN_PROJ)
    wt = jnp.swapaxes(w_in[l], 0, 1).astype(BF16)
    gain_qk = jnp.concatenate([jnp.tile(q_norm_g[l], NA_HEADS) * (NA_DH ** -0.5 * LOG2_E),
                               jnp.tile(k_norm_g[l], NA_HEADS)]).reshape(1, 2 * NA_WIDTH)

    zr = jnp.zeros((GLA_RANK, GLA_KEY_WIDTH), F32)
    ztail = jnp.zeros((LANES - 2 * GLA_RANK, GLA_KEY_WIDTH), F32)
    wd = jnp.stack([jnp.concatenate([w_decay_fwd[l], zr, ztail], axis=0),
                    jnp.concatenate([zr, w_decay_bwd[l], ztail], axis=0)]).astype(BF16)
    bias = jnp.stack([b_decay_fwd[l], b_decay_bwd[l]]).reshape(2, 1, GLA_KEY_WIDTH)
    tg, tg_m = _gla_constants()

    assert seq % (NA_WIN_H * GRID_W) == 0
    kgain_col = jnp.tile(k_norm_g[l], NA_HEADS).reshape(NA_WIDTH, 1)
    zmain, nq, nk_t, nv_cb, gtot = _proj_call(x2d, g, wt, gain_qk, kgain_col, wd, bias, tg,
                                              tm=NA_WIN_H * GRID_W, tokens=True, name="proj_tokens")
    zmain_m, _, nk_m, nv_m, _ = _proj_call(meta_tokens, g, wt, gain_qk, kgain_col, wd, bias, tg,
                                           tm=N_META, tokens=False, name="proj_meta")
    zmain_m_pad = jnp.pad(zmain_m, ((0, TILE - N_META), (0, 0)))

    o_sum = _gla_call(zmain, gtot, zmain_m_pad, wd, bias, tg_m, batch=batch, seq=seq)

    rpb_pad = jnp.pad(rpb[l] * LOG2_E, ((0, 0), (0, 0), (0, 2 * NA_DH - rpb.shape[-1])))
    mb = (meta_bias[l] * LOG2_E).reshape(NA_HEADS, 1, N_META)
    o_na = _na_call(nq, nk_t, nv_cb, zmain, nk_m, nv_m, rpb_pad, mb, batch=batch, seq=seq,
                    rows_per_step=_pick(batch * seq // GRID_W, (256, 128, 64, 32, 16)))

    out = _out_call(x2d, o_sum, zmain, o_na, gla_out_norm_g[l].reshape(1, GLA_DV),
                    w_out[l].astype(BF16), tm=_pick(batch * seq, (1024, 512, 256, 128)))
    return out.reshape(batch, seq, d)
```

```python
import functools

import jax
import jax.numpy as jnp
import numpy as np
from jax import lax
from jax.experimental import pallas as pl
from jax.experimental.pallas import tpu as pltpu

F32 = jnp.float32
BF16 = jnp.bfloat16

D_MODEL = 1024
N_META = 16
GRID_W = 64
GLA_HEADS = 4
GLA_DK = 128
GLA_DV = 256
GLA_KEY_WIDTH = GLA_HEADS * GLA_DK
GLA_WIDTH = GLA_HEADS * GLA_DV
GLA_RANK = 16
GLA_TAU = 16.0
GLA_CHUNK = 64
NA_HEADS = 16
NA_DH = 64
NA_WIDTH = NA_HEADS * NA_DH
NA_WIN_H = 8
NA_WIN_W = 16
RMS_EPS = 1e-6
NEG_BIG = -1e30
LOG2_E = 1.4426950408889634
LANES = 128

COL_GV, COL_GG, COL_NG, COL_GQ, COL_GK, COL_R = 0, 1024, 2048, 3072, 3584, 4096
MAIN_W = COL_R + LANES
COL_DIR0, DIR_W = 3072, 3 * 512
MAIN_TOK_W = COL_DIR0 + 2 * DIR_W
NA_CBW = 16
NA_CB = GRID_W // NA_CBW
IN_GQ, IN_GK, IN_GV, IN_R, IN_GG, IN_NQ, IN_NK, IN_NV, IN_NG = 0, 512, 1024, 2048, 2080, 3104, 4128, 5152, 6176
D_IN_PROJ = 7200
PROJ_TN = 512
OUT_TN = 256

TILE = 2 * GLA_CHUNK
NA_ROW_GROUP = 16
VMEM_LIMIT = 58 * 1024 * 1024


def _split_bf16(x):
    hi = x.astype(BF16)
    lo = (x - hi.astype(F32)).astype(BF16)
    return hi, lo


def _dot(a, b):
    return jnp.dot(a, b, preferred_element_type=F32)


def _dot_nt(a, b):
    return lax.dot_general(a, b, (((1,), (1,)), ((), ())), preferred_element_type=F32)


def _dot_tn(a, b):
    return lax.dot_general(a, b, (((0,), (0,)), ((), ())), preferred_element_type=F32)


def _log_sigmoid(x):
    return jnp.minimum(x, 0.0) - jnp.log(1.0 + jnp.exp(-jnp.abs(x)))


def _silu(x):
    h = 0.5 * x
    return h + h * jnp.tanh(h)


SHARED_SEGMENTS = ((COL_GV, IN_GV, 1024), (COL_GG, IN_GG, 1024), (COL_NG, IN_NG, 1024))
META_SEGMENTS = ((COL_GQ, IN_GQ, 512), (COL_GK, IN_GK, 512),
                 (COL_R, IN_R, LANES))


def _store_rows(o_ref, c0, y, permute):
    n = y.shape[1]
    if not permute:
        o_ref[:, c0:c0 + n] = y
        return
    for rl in range(y.shape[0] // GRID_W):
        for cb in range(NA_CB):
            src = rl * GRID_W + cb * NA_CBW
            o_ref[cb, rl * NA_CBW:(rl + 1) * NA_CBW, c0:c0 + n] = y[src:src + NA_CBW]


def _gla_log_decay(r, wd_ref, bdec_ref, d):
    return (_log_sigmoid(_dot(r, wd_ref[d]) + bdec_ref[d]) * (1.0 / GLA_TAU)).astype(BF16)


def _gla_operands_store(q, k, log_decay, d, t, tg_ref, zm_ref, gtot_ref):
    rows = slice(t * TILE, (t + 1) * TILE)
    g = _dot(tg_ref[d], log_decay[rows])
    col = COL_DIR0 + d * DIR_W
    tots = []
    for c in range(TILE // GLA_CHUNK):
        last = c * GLA_CHUNK + (GLA_CHUNK - 1 if d == 0 else 0)
        tot = g[last:last + 1]
        gtot_ref[(t * TILE) // GLA_CHUNK + c:(t * TILE) // GLA_CHUNK + c + 1,
                 d * GLA_KEY_WIDTH:(d + 1) * GLA_KEY_WIDTH] = tot
        tots.append(jnp.broadcast_to(tot, (GLA_CHUNK, GLA_KEY_WIDTH)))
    dd = jnp.concatenate(tots, axis=0) - g
    zm_ref[rows, col:col + 512] = (q[rows] * jnp.exp(g)).astype(BF16)
    zm_ref[rows, col + 512:col + 1024] = (k[rows] * jnp.exp(-g)).astype(BF16)
    zm_ref[rows, col + 1024:col + 1536] = (k[rows] * jnp.exp(dd)).astype(BF16)


def _proj_kernel(x0_ref, xn_ref, g_ref, wt_ref, gain_ref, kgain_ref, wd_ref, bdec_ref, tg_ref,
                 zm_ref, nq_ref, nk_ref, nv_ref, gtot_ref, u_scr, *, tokens):
    def normed(x_ref):
        x = x_ref[...]
        ms = jnp.mean(x * x, axis=-1, keepdims=True)
        return (x * lax.rsqrt(ms + RMS_EPS) * g_ref[...]).astype(BF16)

    step = pl.program_id(0)

    @pl.when(step == 0)
    def _():
        u_scr[0] = normed(x0_ref)

    slot = step % 2
    u = u_scr[slot]
    tm = u.shape[0]

    def prepare_next():
        u_scr[1 - slot] = normed(xn_ref)

    def main_tile(dst, src, off, n):
        acc = _dot_nt(u, wt_ref[src + off:src + off + n, :])
        if dst == COL_GQ:
            acc = acc * (GLA_DK ** -0.5)
        zm_ref[:, dst + off:dst + off + n] = acc.astype(zm_ref.dtype)

    def nv_tile(off):
        acc = _dot_nt(u, wt_ref[IN_NV + off:IN_NV + off + PROJ_TN, :])
        _store_rows(nv_ref, off, acc.astype(nv_ref.dtype), tokens)

    def head_norm_rows(acc, goff):
        first = lax.broadcasted_iota(jnp.int32, (acc.shape[0], LANES), 1) < NA_DH
        cols = []
        for c in range(0, acc.shape[1], LANES):
            a = acc[:, c:c + LANES]
            sq = a * a
            lo = jnp.sum(jnp.where(first, sq, 0.0), axis=-1, keepdims=True)
            hi = jnp.sum(jnp.where(first, 0.0, sq), axis=-1, keepdims=True)
            inv = jnp.where(first, lax.rsqrt(lo * (1.0 / NA_DH) + RMS_EPS), lax.rsqrt(hi * (1.0 / NA_DH) + RMS_EPS))
            cols.append(a * inv)
        return (jnp.concatenate(cols, axis=1) * gain_ref[:, goff:goff + PROJ_TN]).astype(BF16)

    def nq_tile(off):
        acc = _dot_nt(u, wt_ref[IN_NQ + off:IN_NQ + off + PROJ_TN, :])
        nq_ref[:, off:off + PROJ_TN] = head_norm_rows(acc, off)

    def nk_rows_tile(off):
        acc = _dot_nt(u, wt_ref[IN_NK + off:IN_NK + off + PROJ_TN, :])
        nk_ref[:, off:off + PROJ_TN] = head_norm_rows(acc, NA_WIDTH + off)

    def nk_transposed_tile(off):
        u_cb = jnp.concatenate([u[rl * GRID_W + cb * NA_CBW:rl * GRID_W + (cb + 1) * NA_CBW]
                                for cb in range(NA_CB) for rl in range(tm // GRID_W)], axis=0)
        acc_t = _dot_nt(wt_ref[IN_NK + off:IN_NK + off + PROJ_TN, :], u_cb)
        for h in range(PROJ_TN // NA_DH):
            rows = slice(off + h * NA_DH, off + (h + 1) * NA_DH)
            blk = acc_t[h * NA_DH:(h + 1) * NA_DH]
            ms = jnp.sum(blk * blk, axis=0, keepdims=True) * (1.0 / NA_DH)
            y = (blk * lax.rsqrt(ms + RMS_EPS) * kgain_ref[rows, :]).astype(BF16)
            for cb in range(NA_CB):
                nk_ref[cb, rows, :] = y[:, cb * LANES:(cb + 1) * LANES]

    nk_tile = nk_transposed_tile if tokens else nk_rows_tile
    tiles = [functools.partial(nk_tile, off) for off in range(0, NA_WIDTH, PROJ_TN)]
    tiles += [functools.partial(nq_tile, off) for off in range(0, NA_WIDTH, PROJ_TN)]
    tiles += [functools.partial(nv_tile, off) for off in range(0, NA_WIDTH, PROJ_TN)]
    for dst, src, width in SHARED_SEGMENTS + (() if tokens else META_SEGMENTS):
        for off in range(0, width, PROJ_TN):
            tiles.append(functools.partial(main_tile, dst, src, off, min(PROJ_TN, width - off)))
    if not tokens:
        gtot_ref[...] = jnp.zeros_like(gtot_ref)
        for tile in tiles:
            tile()
        prepare_next()
        return

    r = _dot_nt(u, wt_ref[IN_R:IN_R + LANES, :]).astype(BF16)
    q = _dot_nt(u, wt_ref[IN_GQ:IN_GQ + GLA_KEY_WIDTH, :]) * (GLA_DK ** -0.5)
    k = _dot_nt(u, wt_ref[IN_GK:IN_GK + GLA_KEY_WIDTH, :])
    log_decay = [None, None]

    def decay_stage(d):
        log_decay[d] = _gla_log_decay(r, wd_ref, bdec_ref, d)

    stages = [functools.partial(decay_stage, 0), functools.partial(decay_stage, 1), None]
    stages += [functools.partial(lambda d, t: _gla_operands_store(q, k, log_decay[d], d, t, tg_ref, zm_ref, gtot_ref),
                                 d, t)
               for d in range(2) for t in range(tm // TILE)]
    for i, tile in enumerate(tiles):
        if i < len(stages) and stages[i] is not None:
            stages[i]()
        tile()
    prepare_next()
    assert len(stages) <= len(tiles)


def _proj_call(x2d, g, wt, gain_qk, kgain_col, wd, bdec, tg, *, tm, tokens, name):
    m, d = x2d.shape
    assert m % tm == 0 and wt.shape == (D_IN_PROJ, d)
    resident = lambda shape: pl.BlockSpec(shape, lambda i: (0,) * len(shape), pipeline_mode=pl.Buffered(1))
    plain = pl.BlockSpec((tm, NA_WIDTH), lambda i: (i, 0))
    plain_shape = jax.ShapeDtypeStruct((m, NA_WIDTH), BF16)
    if tokens:
        assert tm == NA_WIN_H * GRID_W and NA_CB * LANES == tm
        main_w = MAIN_TOK_W
        v_spec = pl.BlockSpec((NA_CB, tm // NA_CB, NA_WIDTH), lambda i: (0, i, 0))
        v_shape = jax.ShapeDtypeStruct((NA_CB, m // NA_CB, NA_WIDTH), BF16)
        k_spec = pl.BlockSpec((NA_CB, NA_WIDTH, LANES), lambda i: (i, 0, 0))
        k_shape = jax.ShapeDtypeStruct((m // LANES, NA_WIDTH, LANES), BF16)
        n_chunk_rows = tm // GLA_CHUNK
    else:
        main_w = MAIN_W
        v_spec, v_shape, k_spec, k_shape = plain, plain_shape, plain, plain_shape
        n_chunk_rows = 8
    gt_rows = (m // tm) * n_chunk_rows
    return pl.pallas_call(
        functools.partial(_proj_kernel, tokens=tokens),
        name=name,
        grid=(m // tm,),
        in_specs=[
            resident((tm, d)),
            pl.BlockSpec((tm, d), lambda i: (jnp.minimum(i + 1, m // tm - 1), 0)),
            resident((1, d)),
            resident((D_IN_PROJ, d)),
            resident((1, 2 * NA_WIDTH)),
            resident((NA_WIDTH, 1)),
            resident((2, LANES, GLA_KEY_WIDTH)),
            resident((2, 1, GLA_KEY_WIDTH)),
            resident((2, TILE, TILE)),
        ],
        out_specs=[pl.BlockSpec((tm, main_w), lambda i: (i, 0)), plain, k_spec, v_spec,
                   pl.BlockSpec((n_chunk_rows, 2 * GLA_KEY_WIDTH), lambda i: (i, 0))],
        out_shape=[jax.ShapeDtypeStruct((m, main_w), BF16), plain_shape, k_shape, v_shape,
                   jax.ShapeDtypeStruct((gt_rows, 2 * GLA_KEY_WIDTH), F32)],
        scratch_shapes=[pltpu.VMEM((2, tm, d), BF16)],
        compiler_params=pltpu.CompilerParams(
            dimension_semantics=("arbitrary",), vmem_limit_bytes=VMEM_LIMIT),
    )(x2d, x2d, g, wt, gain_qk, kgain_col, wd, bdec, tg)


def _gla_kernel(qf_ref, kif_ref, ksf_ref, vf_ref, gtf_ref, qb_ref, kib_ref, ksb_ref, vb_ref, gtb_ref,
                km_ref, vm_ref, rm_ref, wd_ref, b_ref, tgm_ref,
                o_ref, sf_ref, sb_ref, *, n_tiles):
    row = lax.broadcasted_iota(jnp.int32, (TILE, TILE), 0)
    col = lax.broadcasted_iota(jnp.int32, (TILE, TILE), 1)
    same = (row // GLA_CHUNK) == (col // GLA_CHUNK)
    masks = (same & (row >= col), same & (row <= col))

    @pl.when(pl.program_id(2) == 0)
    def _():
        rowm = lax.broadcasted_iota(jnp.int32, (TILE, GLA_DK), 0)
        lg = _log_sigmoid(_dot(rm_ref[...], wd_ref[...]) + b_ref[...]) * (1.0 / GLA_TAU)
        hi, lo = _split_bf16(jnp.where(rowm < N_META, lg, 0.0))
        d_m = _dot(tgm_ref[...], jnp.concatenate([hi, lo], axis=0))
        k_st = (km_ref[...].astype(F32) * jnp.exp(d_m)).astype(BF16)
        sf_ref[...] = _dot_tn(k_st, vm_ref[...])
        sb_ref[...] = jnp.zeros_like(sb_ref)

    def scan(step):
        items = []
        for t in range(n_tiles):
            items.append((0, t))
            items.append((1, n_tiles - 1 - t))
        refs = ((qf_ref, kif_ref, ksf_ref, vf_ref), (qb_ref, kib_ref, ksb_ref, vb_ref))
        out_half = (step, 1 - step)
        a_cols = (jnp.exp(jnp.transpose(gtf_ref[...])), jnp.exp(jnp.transpose(gtb_ref[...])))

        def rows_of(t):
            return slice(t * TILE, (t + 1) * TILE)

        amats = [jnp.where(masks[d], _dot_nt(refs[d][0][rows_of(t), :], refs[d][1][rows_of(t), :]), 0.0).astype(BF16)
                 for d, t in items]

        def state_independent(idx):
            d, t = items[idx]
            v = refs[d][3][rows_of(t), :]
            k_st = refs[d][2][rows_of(t), :]
            u = [_dot_tn(k_st[h * GLA_CHUNK:(h + 1) * GLA_CHUNK], v[h * GLA_CHUNK:(h + 1) * GLA_CHUNK])
                 for h in range(2)]
            return _dot(amats[idx], v), u

        state = [sf_ref[...], sb_ref[...]]
        ahead = state_independent(0)
        for idx, (d, t) in enumerate(items):
            o_intra, u = ahead
            if idx + 1 < len(items):
                ahead = state_independent(idx + 1)
            s = state[d]
            outs = [None, None]
            for h in ((0, 1) if d == 0 else (1, 0)):
                rows = slice(t * TILE + h * GLA_CHUNK, t * TILE + (h + 1) * GLA_CHUNK)
                outs[h] = o_intra[h * GLA_CHUNK:(h + 1) * GLA_CHUNK] + _dot(refs[d][0][rows, :], s.astype(BF16))
                chunk = 2 * t + h
                s = a_cols[d][:, chunk:chunk + 1] * s + u[h]
            state[d] = s
            o = jnp.concatenate(outs, axis=0)
            out_rows = rows_of(out_half[d] * n_tiles + t)
            if step == 1:
                o = o + o_ref[out_rows, :].astype(F32)
            o_ref[out_rows, :] = o.astype(o_ref.dtype)
        sf_ref[...] = state[0]
        sb_ref[...] = state[1]

    for step in range(2):
        pl.when(pl.program_id(2) == step)(functools.partial(scan, step))


def _gla_constants():
    i = np.arange(TILE)
    same = (i[:, None] // GLA_CHUNK) == (i[None, :] // GLA_CHUNK)
    inc_f = same & (i[None, :] <= i[:, None])
    inc_b = same & (i[None, :] >= i[:, None])
    to_end_meta = i[None, :] > i[:, None]
    dup = lambda m: np.concatenate([m, m], axis=1)
    tg = jnp.asarray(np.stack([inc_f, inc_b]), BF16)
    return tg, jnp.asarray(dup(to_end_meta), BF16)


def _gla_call(zmain, gtot, zmain_m, wd, bias, tg_m, *, batch, seq):
    nb = 2
    assert seq % (nb * TILE) == 0
    t_blk = seq // nb
    n_tiles = t_blk // TILE
    n_chunks = t_blk // GLA_CHUNK

    def tok(fwd):
        if fwd:
            return lambda b, hh, i: b * nb + i
        return lambda b, hh, i: b * nb + (nb - 1 - i)

    def dir_specs(d):
        t = tok(d == 0)
        c0 = (COL_DIR0 + d * DIR_W) // GLA_DK
        return [
            pl.BlockSpec((t_blk, GLA_DK), lambda b, hh, i: (t(b, hh, i), c0 + hh)),
            pl.BlockSpec((t_blk, GLA_DK), lambda b, hh, i: (t(b, hh, i), c0 + GLA_HEADS + hh)),
            pl.BlockSpec((t_blk, GLA_DK), lambda b, hh, i: (t(b, hh, i), c0 + 2 * GLA_HEADS + hh)),
            pl.BlockSpec((t_blk, GLA_DV), lambda b, hh, i: (t(b, hh, i), COL_GV // GLA_DV + hh)),
            pl.BlockSpec((n_chunks, GLA_DK), lambda b, hh, i: (t(b, hh, i), d * GLA_HEADS + hh)),
        ]

    in_specs = (
        dir_specs(0) + dir_specs(1)
        + [pl.BlockSpec((TILE, GLA_DK), lambda b, hh, i: (0, COL_GK // GLA_DK + hh)),
           pl.BlockSpec((TILE, GLA_DV), lambda b, hh, i: (0, COL_GV // GLA_DV + hh)),
           pl.BlockSpec((TILE, LANES), lambda b, hh, i: (0, COL_R // LANES)),
           pl.BlockSpec((None, LANES, GLA_DK), lambda b, hh, i: (0, 0, hh)),
           pl.BlockSpec((None, 1, GLA_DK), lambda b, hh, i: (0, 0, hh)),
           pl.BlockSpec((TILE, 2 * TILE), lambda b, hh, i: (0, 0))]
    )
    args = [zmain] * 4 + [gtot] + [zmain] * 4 + [gtot] + [zmain_m] * 3 + [wd, bias, tg_m]
    return pl.pallas_call(
        functools.partial(_gla_kernel, n_tiles=n_tiles),
        name="gla_scan",
        grid=(batch, GLA_HEADS, nb),
        in_specs=in_specs,
        out_specs=pl.BlockSpec((seq, GLA_DV), lambda b, hh, i: (b, hh)),
        out_shape=jax.ShapeDtypeStruct((batch * seq, GLA_WIDTH), BF16),
        scratch_shapes=[pltpu.VMEM((GLA_DK, GLA_DV), F32), pltpu.VMEM((GLA_DK, GLA_DV), F32)],
        compiler_params=pltpu.CompilerParams(
            dimension_semantics=("parallel", "parallel", "arbitrary"), vmem_limit_bytes=VMEM_LIMIT),
    )(*args)


NA_SEG_COLS = ((0, 24), (24, 40), (40, 64))
NA_SEG_ROW0 = (0, 48, 80)
NA_STACK = 2 * GRID_W


NA_BASE_MID = NA_WIN_H // 2 - 1
NA_VARIANTS = 2 * NA_WIN_H - 1


def _na_variant(base, a):
    return jnp.where(base == NA_BASE_MID, a, jnp.where(base < NA_BASE_MID, NA_WIN_H + base, NA_WIN_H - 1 + base))


def _na_build_bias(rpb_ref, tab_ref):
    def build(v, carry):
        a = jnp.where(v < NA_WIN_H, v, 0)
        base = jnp.where(v < NA_WIN_H, NA_BASE_MID,
                         jnp.where(v < NA_WIN_H + NA_BASE_MID, v - NA_WIN_H, v - (NA_WIN_H - 1)))
        for hh in range(2):
            bias_rows = [rpb_ref[hh, pl.ds(base + ((t - a + NA_WIN_H) & (NA_WIN_H - 1)), 1), :]
                         for t in range(NA_WIN_H)]
            for s, (c0, c1) in enumerate(NA_SEG_COLS):
                n = c1 - c0
                vrows = [jnp.broadcast_to(br, (n, LANES)) for br in bias_rows]
                row0 = NA_SEG_ROW0[s] + hh * n
                rho = lax.broadcasted_iota(jnp.int32, (n, LANES), 0)
                lane = lax.broadcasted_iota(jnp.int32, (n, LANES), 1)
                cs = jnp.clip(c0 + rho - NA_WIN_W // 2, 0, GRID_W - NA_WIN_W)
                for half in range(2):
                    cb = s + half
                    cp = NA_CBW * cb + lane % NA_CBW
                    valid = (cp >= cs) & (cp < cs + NA_WIN_W)
                    acc = None
                    for i in range(NA_WIN_H):
                        shift = (NA_CBW * i - NA_CBW * cb + c0 - (NA_WIN_W - 1)) % LANES
                        rolled = pltpu.roll(vrows[i], shift, 1, stride=1, stride_axis=0)
                        acc = rolled if acc is None else jnp.where(lane // NA_CBW == i, rolled, acc)
                    tab_ref[v, row0:row0 + n, half * LANES:(half + 1) * LANES] = jnp.where(valid, acc, NEG_BIG)
        return carry

    lax.fori_loop(0, NA_VARIANTS, build, 0)


def _na_kernel(q_ref, kt_ref, v_ref, gate_ref, km_ref, vm_ref, rpb_ref, mb_ref, o_ref, tab_ref, sc_ref, sm_ref,
               *, rows_per_step, n_rows):
    w = GRID_W

    @pl.when(pl.program_id(1) == 0)
    def _():
        _na_build_bias(rpb_ref, tab_ref)

    head0_q = lax.broadcasted_iota(jnp.int32, (w, LANES), 1) < NA_DH
    srow = lax.broadcasted_iota(jnp.int32, (NA_STACK, N_META), 0)
    is_h1 = jnp.zeros((NA_STACK, N_META), jnp.bool_)
    for s, (c0, c1) in enumerate(NA_SEG_COLS):
        lo = NA_SEG_ROW0[s] + (c1 - c0)
        is_h1 = is_h1 | ((srow >= lo) & (srow < lo + (c1 - c0)))
    mb = jnp.where(is_h1, mb_ref[1], mb_ref[0])
    km = km_ref[...]
    vm = vm_ref[...]
    seg_rows = [slice(NA_SEG_ROW0[s], NA_SEG_ROW0[s] + 2 * (c1 - c0)) for s, (c0, c1) in enumerate(NA_SEG_COLS)]
    blk_rows = [slice(seg_rows[max(cb - 1, 0)].start, seg_rows[min(cb, len(seg_rows) - 1)].stop)
                for cb in range(NA_CB)]

    n_tiles = n_rows // NA_WIN_H
    key_lane = lax.broadcasted_iota(jnp.int32, (LANES, LANES), 1)

    def window(rl):
        row = pl.program_id(1) * rows_per_step + rl
        b = row // n_rows
        r = row - b * n_rows
        rs = jnp.clip(r - NA_WIN_H // 2, 0, n_rows - NA_WIN_H)
        a = rs & (NA_WIN_H - 1)
        return (b * n_tiles + rs // NA_WIN_H, a, (b + 1) * n_tiles - 1,
                _na_variant(rs - r + (NA_WIN_H - 1), a))

    def scores_store(slot, j, rl):
        m, a, m_last, var = window(rl)
        q = q_ref[pl.ds(pl.multiple_of(rl * w, w), w), :].astype(F32)
        q0 = jnp.where(head0_q, q, 0.0)
        q1 = q - q0
        qs = jnp.concatenate([piece[c0:c1] for c0, c1 in NA_SEG_COLS for piece in (q0, q1)],
                             axis=0).astype(BF16)
        from_m = key_lane >= a * NA_CBW
        m1 = jnp.minimum(m + 1, m_last)
        blocks = [_dot(qs[blk_rows[cb]],
                       jnp.where(from_m, kt_ref[NA_CB * m + cb], kt_ref[NA_CB * m1 + cb]))
                  for cb in range(NA_CB)]
        for s in range(len(NA_SEG_COLS)):
            n2 = seg_rows[s].stop - seg_rows[s].start
            for half in range(2):
                cb = s + half
                a0 = seg_rows[s].start - blk_rows[cb].start
                lanes = slice(half * LANES, (half + 1) * LANES)
                sc_ref[slot, j, seg_rows[s], lanes] = blocks[cb][a0:a0 + n2] + tab_ref[var, seg_rows[s], lanes]
        sm_ref[slot, j] = _dot_nt(qs, km) + mb

    def value_block(cb, m, a):
        groups = []
        for t in range(NA_WIN_H):
            row = NA_WIN_H * m + t + jnp.where(t < a, NA_WIN_H, 0)
            groups.append(v_ref[cb, pl.ds(pl.multiple_of(row * NA_CBW, NA_CBW), NA_CBW), :])
        return jnp.concatenate(groups, axis=0)

    def softmax_values(slot, j, rl):
        m_tile, a, _, _ = window(rl)
        sc = sc_ref[slot, j]
        sm = sm_ref[slot, j]
        m = jnp.maximum(sc[:, :LANES], sc[:, LANES:])
        m = jnp.maximum(jnp.max(m, axis=-1, keepdims=True), jnp.max(sm, axis=-1, keepdims=True))
        p = jnp.exp2(sc - m)
        pm = jnp.exp2(sm - m)
        den = (jnp.sum(p[:, :LANES] + p[:, LANES:], axis=-1, keepdims=True)
               + jnp.sum(pm, axis=-1, keepdims=True))
        p = p.astype(BF16)
        vals = [value_block(cb, m_tile, a) for cb in range(NA_CB)]
        o_meta = _dot(pm.astype(BF16), vm)
        inv = 1.0 / den
        outs = []
        for s, (c0, c1) in enumerate(NA_SEG_COLS):
            o_loc = _dot(p[seg_rows[s]], jnp.concatenate([vals[s], vals[s + 1]], axis=0))
            o = (o_loc + o_meta[seg_rows[s]]) * inv[seg_rows[s]]
            n = c1 - c0
            head0 = lax.broadcasted_iota(jnp.int32, (n, LANES), 1) < NA_DH
            outs.append(jnp.where(head0, o[:n], o[n:]))
        o = jnp.concatenate(outs, axis=0)
        qoff = pl.multiple_of(rl * w, w)
        gate = gate_ref[pl.ds(qoff, w), :].astype(F32)
        o_ref[pl.ds(qoff, w), :] = (o * _silu(gate)).astype(o_ref.dtype)

    n_groups = rows_per_step // NA_ROW_GROUP
    for j in range(NA_ROW_GROUP):
        scores_store(0, j, j)

    def trip(g, carry):
        slot = g % 2
        for j in range(NA_ROW_GROUP):
            scores_store(slot, j, g * NA_ROW_GROUP + j)
        for j in range(NA_ROW_GROUP):
            softmax_values(1 - slot, j, (g - 1) * NA_ROW_GROUP + j)
        return carry

    lax.fori_loop(1, n_groups, trip, 0)
    for j in range(NA_ROW_GROUP):
        softmax_values((n_groups - 1) % 2, j, (n_groups - 1) * NA_ROW_GROUP + j)


def _na_call(nq, nk_t, nv_cb, zmain, nk_m, nv_m, rpb_pad, meta_bias, *, batch, seq, rows_per_step):
    n_rows = seq // GRID_W
    total_rows = batch * n_rows
    assert n_rows % NA_WIN_H == 0 and total_rows % rows_per_step == 0 and rows_per_step % NA_ROW_GROUP == 0
    nsteps = total_rows // rows_per_step
    hp = NA_HEADS // 2
    tq = rows_per_step * GRID_W
    n_idx = 2 * NA_WIN_H - 1
    in_specs = [
        pl.BlockSpec((tq, LANES), lambda p, i: (i, p)),
        pl.BlockSpec((batch * seq // LANES, LANES, LANES), lambda p, i: (0, p, 0)),
        pl.BlockSpec((NA_CB, batch * seq // NA_CB, LANES), lambda p, i: (0, 0, p)),
        pl.BlockSpec((tq, LANES), lambda p, i: (i, COL_NG // LANES + p)),
        pl.BlockSpec((N_META, LANES), lambda p, i: (0, p)),
        pl.BlockSpec((N_META, LANES), lambda p, i: (0, p)),
        pl.BlockSpec((2, n_idx, LANES), lambda p, i: (p, 0, 0)),
        pl.BlockSpec((2, 1, N_META), lambda p, i: (p, 0, 0)),
    ]
    return pl.pallas_call(
        functools.partial(_na_kernel, rows_per_step=rows_per_step, n_rows=n_rows),
        name="na_attn",
        grid=(hp, nsteps),
        in_specs=in_specs,
        out_specs=pl.BlockSpec((tq, LANES), lambda p, i: (i, p)),
        out_shape=jax.ShapeDtypeStruct((batch * seq, NA_WIDTH), BF16),
        scratch_shapes=[pltpu.VMEM((NA_VARIANTS, NA_STACK, 2 * LANES), F32),
                        pltpu.VMEM((2, NA_ROW_GROUP, NA_STACK, 2 * LANES), F32),
                        pltpu.VMEM((2, NA_ROW_GROUP, NA_STACK, N_META), F32)],
        compiler_params=pltpu.CompilerParams(
            dimension_semantics=("arbitrary", "arbitrary"), vmem_limit_bytes=VMEM_LIMIT),
    )(nq, nk_t, nv_cb, zmain, nk_m, nv_m, rpb_pad, meta_bias)


def _out_kernel(x_ref, os_ref, gg_ref, na_ref, w_ref, o_ref):
    d = o_ref.shape[1]
    na = na_ref[...]

    def na_tile(c):
        cols = slice(c * OUT_TN, (c + 1) * OUT_TN)
        o_ref[:, cols] = x_ref[:, cols] + _dot(na, w_ref[GLA_WIDTH:, cols])

    def gla_head(h):
        cols = slice(h * GLA_DV, (h + 1) * GLA_DV)
        sh = os_ref[:, cols].astype(F32)
        ms = jnp.mean(sh * sh, axis=-1, keepdims=True)
        return (sh * lax.rsqrt(ms + RMS_EPS) * _silu(gg_ref[:, cols].astype(F32))).astype(BF16)

    heads = []
    for i in range(max(d // OUT_TN, GLA_HEADS)):
        if i < d // OUT_TN:
            na_tile(i)
        if i < GLA_HEADS:
            heads.append(gla_head(i))
    o_gla = jnp.concatenate(heads, axis=1)
    for c in range(d // OUT_TN):
        cols = slice(c * OUT_TN, (c + 1) * OUT_TN)
        o_ref[:, cols] += _dot(o_gla, w_ref[:GLA_WIDTH, cols])


def _out_call(x2d, o_sum, zmain, o_na, w_out, *, tm):
    m, d = x2d.shape
    assert m % tm == 0 and d % OUT_TN == 0
    return pl.pallas_call(
        _out_kernel,
        name="out_proj",
        grid=(m // tm,),
        in_specs=[
            pl.BlockSpec((tm, d), lambda i: (i, 0)),
            pl.BlockSpec((tm, GLA_WIDTH), lambda i: (i, 0)),
            pl.BlockSpec((tm, GLA_WIDTH), lambda i: (i, COL_GG // GLA_WIDTH)),
            pl.BlockSpec((tm, NA_WIDTH), lambda i: (i, 0)),
            pl.BlockSpec((GLA_WIDTH + NA_WIDTH, d), lambda i: (0, 0), pipeline_mode=pl.Buffered(1)),
        ],
        out_specs=pl.BlockSpec((tm, d), lambda i: (i, 0)),
        out_shape=jax.ShapeDtypeStruct((m, d), F32),
        compiler_params=pltpu.CompilerParams(
            dimension_semantics=("parallel",), vmem_limit_bytes=VMEM_LIMIT),
    )(x2d, o_sum, zmain, o_na, w_out)


def _pick(n, cands):
    for c in cands:
        if n % c == 0:
            return c
    raise ValueError(f"no tile for {n}")


def kernel(x, meta_tokens, norm_g, w_in, w_decay_fwd, b_decay_fwd, w_decay_bwd, b_decay_bwd,
           gla_out_norm_g, q_norm_g, k_norm_g, rpb, meta_bias, w_out):
    batch, seq, d = x.shape
    assert d == D_MODEL and seq % TILE == 0 and seq % GRID_W == 0
    depth = norm_g.shape[0]
    assert depth == 1
    l = 0
    x2d = x.reshape(batch * seq, d)
    g = norm_g[l].reshape(1, d)

    assert w_in.shape[1:] == (d, D_IN_PROJ)
    wt = jnp.swapaxes(w_in[l], 0, 1).astype(BF16)
    gain_qk = jnp.concatenate([jnp.tile(q_norm_g[l], NA_HEADS) * (NA_DH ** -0.5 * LOG2_E),
                               jnp.tile(k_norm_g[l], NA_HEADS)]).reshape(1, 2 * NA_WIDTH)

    zr = jnp.zeros((GLA_RANK, GLA_KEY_WIDTH), F32)
    ztail = jnp.zeros((LANES - 2 * GLA_RANK, GLA_KEY_WIDTH), F32)
    wd = jnp.stack([jnp.concatenate([w_decay_fwd[l], zr, ztail], axis=0),
                    jnp.concatenate([zr, w_decay_bwd[l], ztail], axis=0)]).astype(BF16)
    bias = jnp.stack([b_decay_fwd[l], b_decay_bwd[l]]).reshape(2, 1, GLA_KEY_WIDTH)
    tg, tg_m = _gla_constants()

    assert seq % (NA_WIN_H * GRID_W) == 0
    kgain_col = jnp.tile(k_norm_g[l], NA_HEADS).reshape(NA_WIDTH, 1)
    zmain, nq, nk_t, nv_cb, gtot = _proj_call(x2d, g, wt, gain_qk, kgain_col, wd, bias, tg,
                                              tm=NA_WIN_H * GRID_W, tokens=True, name="proj_tokens")
    zmain_m, _, nk_m, nv_m, _ = _proj_call(meta_tokens, g, wt, gain_qk, kgain_col, wd, bias, tg,
                                           tm=N_META, tokens=False, name="proj_meta")
    zmain_m_pad = jnp.pad(zmain_m, ((0, TILE - N_META), (0, 0)))

    o_sum = _gla_call(zmain, gtot, zmain_m_pad, wd, bias, tg_m, batch=batch, seq=seq)

    rpb_pad = jnp.pad(rpb[l] * LOG2_E, ((0, 0), (0, 0), (0, 2 * NA_DH - rpb.shape[-1])))
    mb = (meta_bias[l] * LOG2_E).reshape(NA_HEADS, 1, N_META)
    o_na = _na_call(nq, nk_t, nv_cb, zmain, nk_m, nv_m, rpb_pad, mb, batch=batch, seq=seq,
                    rows_per_step=_pick(batch * seq // GRID_W, (256, 128, 64, 32, 16)))

    gn_rows = jnp.concatenate([jnp.tile(gla_out_norm_g[l], GLA_HEADS), jnp.ones((NA_WIDTH,), F32)])
    w_o = (w_out[l] * gn_rows[:, None]).astype(BF16)
    out = _out_call(x2d, o_sum, zmain, o_na, w_o, tm=_pick(batch * seq, (1024, 512, 256, 128)))
    return out.reshape(batch, seq, d)
```

```python
import functools

import jax
import jax.numpy as jnp
import numpy as np
from jax import lax
from jax.experimental import pallas as pl
from jax.experimental.pallas import tpu as pltpu

F32 = jnp.float32
BF16 = jnp.bfloat16

D_MODEL = 1024
N_META = 16
GRID_W = 64
GLA_HEADS = 4
GLA_DK = 128
GLA_DV = 256
GLA_KEY_WIDTH = GLA_HEADS * GLA_DK
GLA_WIDTH = GLA_HEADS * GLA_DV
GLA_RANK = 16
GLA_TAU = 16.0
GLA_CHUNK = 64
NA_HEADS = 16
NA_DH = 64
NA_WIDTH = NA_HEADS * NA_DH
NA_WIN_H = 8
NA_WIN_W = 16
RMS_EPS = 1e-6
NEG_BIG = -1e30
LOG2_E = 1.4426950408889634
LANES = 128

COL_GV, COL_GG, COL_NG, COL_GQ, COL_GK, COL_R = 0, 1024, 2048, 3072, 3584, 4096
MAIN_W = COL_R + LANES
COL_DIR0, DIR_W = 3072, 3 * 512
MAIN_TOK_W = COL_DIR0 + 2 * DIR_W
NA_CBW = 16
NA_CB = GRID_W // NA_CBW
IN_GQ, IN_GK, IN_GV, IN_R, IN_GG, IN_NQ, IN_NK, IN_NV, IN_NG = 0, 512, 1024, 2048, 2080, 3104, 4128, 5152, 6176
D_IN_PROJ = 7200
PROJ_TN = 512
OUT_TN = 256

TILE = 2 * GLA_CHUNK
NA_ROW_GROUP = 16
VMEM_LIMIT = 58 * 1024 * 1024


def _split_bf16(x):
    hi = x.astype(BF16)
    lo = (x - hi.astype(F32)).astype(BF16)
    return hi, lo


def _dot(a, b):
    return jnp.dot(a, b, preferred_element_type=F32)


def _dot_nt(a, b):
    return lax.dot_general(a, b, (((1,), (1,)), ((), ())), preferred_element_type=F32)


def _dot_tn(a, b):
    return lax.dot_general(a, b, (((0,), (0,)), ((), ())), preferred_element_type=F32)


def _log_sigmoid(x):
    return jnp.minimum(x, 0.0) - jnp.log(1.0 + jnp.exp(-jnp.abs(x)))


def _silu(x):
    return x * (1.0 / (1.0 + jnp.exp(-x)))


def _silu_tanh(x):
    h = 0.5 * x
    return h + h * jnp.tanh(h)


SHARED_SEGMENTS = ((COL_GV, IN_GV, 1024), (COL_GG, IN_GG, 1024), (COL_NG, IN_NG, 1024))
META_SEGMENTS = ((COL_GQ, IN_GQ, 512), (COL_GK, IN_GK, 512),
                 (COL_R, IN_R, LANES))


def _store_rows(o_ref, c0, y, permute):
    n = y.shape[1]
    if not permute:
        o_ref[:, c0:c0 + n] = y
        return
    for rl in range(y.shape[0] // GRID_W):
        for cb in range(NA_CB):
            src = rl * GRID_W + cb * NA_CBW
            o_ref[cb, rl * NA_CBW:(rl + 1) * NA_CBW, c0:c0 + n] = y[src:src + NA_CBW]


def _gla_log_decay(r, wd_ref, bdec_ref, d):
    return (_log_sigmoid(_dot(r, wd_ref[d]) + bdec_ref[d]) * (1.0 / GLA_TAU)).astype(BF16)


def _gla_operands_store(q, k, log_decay, d, t, tg_ref, zm_ref, gtot_ref):
    rows = slice(t * TILE, (t + 1) * TILE)
    g = _dot(tg_ref[d], log_decay[rows])
    col = COL_DIR0 + d * DIR_W
    tots = []
    for c in range(TILE // GLA_CHUNK):
        last = c * GLA_CHUNK + (GLA_CHUNK - 1 if d == 0 else 0)
        tot = g[last:last + 1]
        gtot_ref[(t * TILE) // GLA_CHUNK + c:(t * TILE) // GLA_CHUNK + c + 1,
                 d * GLA_KEY_WIDTH:(d + 1) * GLA_KEY_WIDTH] = tot
        tots.append(jnp.broadcast_to(tot, (GLA_CHUNK, GLA_KEY_WIDTH)))
    dd = jnp.concatenate(tots, axis=0) - g
    zm_ref[rows, col:col + 512] = (q[rows] * jnp.exp(g)).astype(BF16)
    zm_ref[rows, col + 512:col + 1024] = (k[rows] * jnp.exp(-g)).astype(BF16)
    zm_ref[rows, col + 1024:col + 1536] = (k[rows] * jnp.exp(dd)).astype(BF16)


def _proj_kernel(x0_ref, xn_ref, g_ref, wt_ref, gain_ref, kgain_ref, wd_ref, bdec_ref, tg_ref,
                 zm_ref, nq_ref, nk_ref, nv_ref, gtot_ref, u_scr, *, tokens):
    def normed(x_ref):
        x = x_ref[...]
        ms = jnp.mean(x * x, axis=-1, keepdims=True)
        return (x * lax.rsqrt(ms + RMS_EPS) * g_ref[...]).astype(BF16)

    step = pl.program_id(0)

    @pl.when(step == 0)
    def _():
        u_scr[0] = normed(x0_ref)

    slot = step % 2
    u = u_scr[slot]
    tm = u.shape[0]

    def prepare_next():
        u_scr[1 - slot] = normed(xn_ref)

    def main_tile(dst, src, off, n):
        acc = _dot_nt(u, wt_ref[src + off:src + off + n, :])
        if dst == COL_GQ:
            acc = acc * (GLA_DK ** -0.5)
        zm_ref[:, dst + off:dst + off + n] = acc.astype(zm_ref.dtype)

    def nv_tile(off):
        acc = _dot_nt(u, wt_ref[IN_NV + off:IN_NV + off + PROJ_TN, :])
        _store_rows(nv_ref, off, acc.astype(nv_ref.dtype), tokens)

    def head_norm_rows(acc, goff):
        first = lax.broadcasted_iota(jnp.int32, (acc.shape[0], LANES), 1) < NA_DH
        cols = []
        for c in range(0, acc.shape[1], LANES):
            a = acc[:, c:c + LANES]
            sq = a * a
            lo = jnp.sum(jnp.where(first, sq, 0.0), axis=-1, keepdims=True)
            hi = jnp.sum(jnp.where(first, 0.0, sq), axis=-1, keepdims=True)
            inv = jnp.where(first, lax.rsqrt(lo * (1.0 / NA_DH) + RMS_EPS), lax.rsqrt(hi * (1.0 / NA_DH) + RMS_EPS))
            cols.append(a * inv)
        return (jnp.concatenate(cols, axis=1) * gain_ref[:, goff:goff + PROJ_TN]).astype(BF16)

    def nq_tile(off):
        acc = _dot_nt(u, wt_ref[IN_NQ + off:IN_NQ + off + PROJ_TN, :])
        nq_ref[:, off:off + PROJ_TN] = head_norm_rows(acc, off)

    def nk_rows_tile(off):
        acc = _dot_nt(u, wt_ref[IN_NK + off:IN_NK + off + PROJ_TN, :])
        nk_ref[:, off:off + PROJ_TN] = head_norm_rows(acc, NA_WIDTH + off)

    def nk_transposed_tile(off):
        u_cb = jnp.concatenate([u[rl * GRID_W + cb * NA_CBW:rl * GRID_W + (cb + 1) * NA_CBW]
                                for cb in range(NA_CB) for rl in range(tm // GRID_W)], axis=0)
        acc_t = _dot_nt(wt_ref[IN_NK + off:IN_NK + off + PROJ_TN, :], u_cb)
        for h in range(PROJ_TN // NA_DH):
            rows = slice(off + h * NA_DH, off + (h + 1) * NA_DH)
            blk = acc_t[h * NA_DH:(h + 1) * NA_DH]
            ms = jnp.sum(blk * blk, axis=0, keepdims=True) * (1.0 / NA_DH)
            y = (blk * lax.rsqrt(ms + RMS_EPS) * kgain_ref[rows, :]).astype(BF16)
            for cb in range(NA_CB):
                nk_ref[cb, rows, :] = y[:, cb * LANES:(cb + 1) * LANES]

    nk_tile = nk_transposed_tile if tokens else nk_rows_tile
    tiles = [functools.partial(nk_tile, off) for off in range(0, NA_WIDTH, PROJ_TN)]
    tiles += [functools.partial(nq_tile, off) for off in range(0, NA_WIDTH, PROJ_TN)]
    tiles += [functools.partial(nv_tile, off) for off in range(0, NA_WIDTH, PROJ_TN)]
    for dst, src, width in SHARED_SEGMENTS + (() if tokens else META_SEGMENTS):
        for off in range(0, width, PROJ_TN):
            tiles.append(functools.partial(main_tile, dst, src, off, min(PROJ_TN, width - off)))
    if not tokens:
        gtot_ref[...] = jnp.zeros_like(gtot_ref)
        for tile in tiles:
            tile()
        prepare_next()
        return

    r = _dot_nt(u, wt_ref[IN_R:IN_R + LANES, :]).astype(BF16)
    q = _dot_nt(u, wt_ref[IN_GQ:IN_GQ + GLA_KEY_WIDTH, :]) * (GLA_DK ** -0.5)
    k = _dot_nt(u, wt_ref[IN_GK:IN_GK + GLA_KEY_WIDTH, :])
    log_decay = [None, None]

    def decay_stage(d):
        log_decay[d] = _gla_log_decay(r, wd_ref, bdec_ref, d)

    stages = [functools.partial(decay_stage, 0), functools.partial(decay_stage, 1), None]
    stages += [functools.partial(lambda d, t: _gla_operands_store(q, k, log_decay[d], d, t, tg_ref, zm_ref, gtot_ref),
                                 d, t)
               for d in range(2) for t in range(tm // TILE)]
    for i, tile in enumerate(tiles):
        if i < len(stages) and stages[i] is not None:
            stages[i]()
        tile()
    prepare_next()
    assert len(stages) <= len(tiles)


def _proj_call(x2d, g, wt, gain_qk, kgain_col, wd, bdec, tg, *, tm, tokens, name):
    m, d = x2d.shape
    assert m % tm == 0 and wt.shape == (D_IN_PROJ, d)
    resident = lambda shape: pl.BlockSpec(shape, lambda i: (0,) * len(shape), pipeline_mode=pl.Buffered(1))
    plain = pl.BlockSpec((tm, NA_WIDTH), lambda i: (i, 0))
    plain_shape = jax.ShapeDtypeStruct((m, NA_WIDTH), BF16)
    if tokens:
        assert tm == NA_WIN_H * GRID_W and NA_CB * LANES == tm
        main_w = MAIN_TOK_W
        v_spec = pl.BlockSpec((NA_CB, tm // NA_CB, NA_WIDTH), lambda i: (0, i, 0))
        v_shape = jax.ShapeDtypeStruct((NA_CB, m // NA_CB, NA_WIDTH), BF16)
        k_spec = pl.BlockSpec((NA_CB, NA_WIDTH, LANES), lambda i: (i, 0, 0))
        k_shape = jax.ShapeDtypeStruct((m // LANES, NA_WIDTH, LANES), BF16)
        n_chunk_rows = tm // GLA_CHUNK
    else:
        main_w = MAIN_W
        v_spec, v_shape, k_spec, k_shape = plain, plain_shape, plain, plain_shape
        n_chunk_rows = 8
    gt_rows = (m // tm) * n_chunk_rows
    return pl.pallas_call(
        functools.partial(_proj_kernel, tokens=tokens),
        name=name,
        grid=(m // tm,),
        in_specs=[
            resident((tm, d)),
            pl.BlockSpec((tm, d), lambda i: (jnp.minimum(i + 1, m // tm - 1), 0)),
            resident((1, d)),
            resident((D_IN_PROJ, d)),
            resident((1, 2 * NA_WIDTH)),
            resident((NA_WIDTH, 1)),
            resident((2, LANES, GLA_KEY_WIDTH)),
            resident((2, 1, GLA_KEY_WIDTH)),
            resident((2, TILE, TILE)),
        ],
        out_specs=[pl.BlockSpec((tm, main_w), lambda i: (i, 0)), plain, k_spec, v_spec,
                   pl.BlockSpec((n_chunk_rows, 2 * GLA_KEY_WIDTH), lambda i: (i, 0))],
        out_shape=[jax.ShapeDtypeStruct((m, main_w), BF16), plain_shape, k_shape, v_shape,
                   jax.ShapeDtypeStruct((gt_rows, 2 * GLA_KEY_WIDTH), F32)],
        scratch_shapes=[pltpu.VMEM((2, tm, d), BF16)],
        compiler_params=pltpu.CompilerParams(
            dimension_semantics=("arbitrary",), vmem_limit_bytes=VMEM_LIMIT),
    )(x2d, x2d, g, wt, gain_qk, kgain_col, wd, bdec, tg)


def _gla_kernel(qf_ref, kif_ref, ksf_ref, vf_ref, gtf_ref, qb_ref, kib_ref, ksb_ref, vb_ref, gtb_ref,
                km_ref, vm_ref, rm_ref, wd_ref, b_ref, tgm_ref,
                o_ref, sf_ref, sb_ref, *, n_tiles):
    row = lax.broadcasted_iota(jnp.int32, (TILE, TILE), 0)
    col = lax.broadcasted_iota(jnp.int32, (TILE, TILE), 1)
    same = (row // GLA_CHUNK) == (col // GLA_CHUNK)
    masks = (same & (row >= col), same & (row <= col))

    @pl.when(pl.program_id(2) == 0)
    def _():
        rowm = lax.broadcasted_iota(jnp.int32, (TILE, GLA_DK), 0)
        lg = _log_sigmoid(_dot(rm_ref[...], wd_ref[...]) + b_ref[...]) * (1.0 / GLA_TAU)
        hi, lo = _split_bf16(jnp.where(rowm < N_META, lg, 0.0))
        d_m = _dot(tgm_ref[...], jnp.concatenate([hi, lo], axis=0))
        k_st = (km_ref[...].astype(F32) * jnp.exp(d_m)).astype(BF16)
        sf_ref[...] = _dot_tn(k_st, vm_ref[...])
        sb_ref[...] = jnp.zeros_like(sb_ref)

    def scan(step):
        items = []
        for t in range(n_tiles):
            items.append((0, t))
            items.append((1, n_tiles - 1 - t))
        refs = ((qf_ref, kif_ref, ksf_ref, vf_ref), (qb_ref, kib_ref, ksb_ref, vb_ref))
        out_half = (step, 1 - step)
        a_cols = (jnp.exp(jnp.transpose(gtf_ref[...])), jnp.exp(jnp.transpose(gtb_ref[...])))

        def rows_of(t):
            return slice(t * TILE, (t + 1) * TILE)

        amats = [jnp.where(masks[d], _dot_nt(refs[d][0][rows_of(t), :], refs[d][1][rows_of(t), :]), 0.0).astype(BF16)
                 for d, t in items]

        def state_independent(idx):
            d, t = items[idx]
            v = refs[d][3][rows_of(t), :]
            k_st = refs[d][2][rows_of(t), :]
            u = [_dot_tn(k_st[h * GLA_CHUNK:(h + 1) * GLA_CHUNK], v[h * GLA_CHUNK:(h + 1) * GLA_CHUNK])
                 for h in range(2)]
            return _dot(amats[idx], v), u

        state = [sf_ref[...], sb_ref[...]]
        ahead = state_independent(0)
        for idx, (d, t) in enumerate(items):
            o_intra, u = ahead
            if idx + 1 < len(items):
                ahead = state_independent(idx + 1)
            s = state[d]
            outs = [None, None]
            for h in ((0, 1) if d == 0 else (1, 0)):
                rows = slice(t * TILE + h * GLA_CHUNK, t * TILE + (h + 1) * GLA_CHUNK)
                outs[h] = o_intra[h * GLA_CHUNK:(h + 1) * GLA_CHUNK] + _dot(refs[d][0][rows, :], s.astype(BF16))
                chunk = 2 * t + h
                s = a_cols[d][:, chunk:chunk + 1] * s + u[h]
            state[d] = s
            o = jnp.concatenate(outs, axis=0)
            out_rows = rows_of(out_half[d] * n_tiles + t)
            if step == 1:
                o = o + o_ref[out_rows, :].astype(F32)
            o_ref[out_rows, :] = o.astype(o_ref.dtype)
        sf_ref[...] = state[0]
        sb_ref[...] = state[1]

    for step in range(2):
        pl.when(pl.program_id(2) == step)(functools.partial(scan, step))


def _gla_constants():
    i = np.arange(TILE)
    same = (i[:, None] // GLA_CHUNK) == (i[None, :] // GLA_CHUNK)
    inc_f = same & (i[None, :] <= i[:, None])
    inc_b = same & (i[None, :] >= i[:, None])
    to_end_meta = i[None, :] > i[:, None]
    dup = lambda m: np.concatenate([m, m], axis=1)
    tg = jnp.asarray(np.stack([inc_f, inc_b]), BF16)
    return tg, jnp.asarray(dup(to_end_meta), BF16)


def _gla_call(zmain, gtot, zmain_m, wd, bias, tg_m, *, batch, seq):
    nb = 2
    assert seq % (nb * TILE) == 0
    t_blk = seq // nb
    n_tiles = t_blk // TILE
    n_chunks = t_blk // GLA_CHUNK

    def tok(fwd):
        if fwd:
            return lambda b, hh, i: b * nb + i
        return lambda b, hh, i: b * nb + (nb - 1 - i)

    def dir_specs(d):
        t = tok(d == 0)
        c0 = (COL_DIR0 + d * DIR_W) // GLA_DK
        return [
            pl.BlockSpec((t_blk, GLA_DK), lambda b, hh, i: (t(b, hh, i), c0 + hh)),
            pl.BlockSpec((t_blk, GLA_DK), lambda b, hh, i: (t(b, hh, i), c0 + GLA_HEADS + hh)),
            pl.BlockSpec((t_blk, GLA_DK), lambda b, hh, i: (t(b, hh, i), c0 + 2 * GLA_HEADS + hh)),
            pl.BlockSpec((t_blk, GLA_DV), lambda b, hh, i: (t(b, hh, i), COL_GV // GLA_DV + hh)),
            pl.BlockSpec((n_chunks, GLA_DK), lambda b, hh, i: (t(b, hh, i), d * GLA_HEADS + hh)),
        ]

    in_specs = (
        dir_specs(0) + dir_specs(1)
        + [pl.BlockSpec((TILE, GLA_DK), lambda b, hh, i: (0, COL_GK // GLA_DK + hh)),
           pl.BlockSpec((TILE, GLA_DV), lambda b, hh, i: (0, COL_GV // GLA_DV + hh)),
           pl.BlockSpec((TILE, LANES), lambda b, hh, i: (0, COL_R // LANES)),
           pl.BlockSpec((None, LANES, GLA_DK), lambda b, hh, i: (0, 0, hh)),
           pl.BlockSpec((None, 1, GLA_DK), lambda b, hh, i: (0, 0, hh)),
           pl.BlockSpec((TILE, 2 * TILE), lambda b, hh, i: (0, 0))]
    )
    args = [zmain] * 4 + [gtot] + [zmain] * 4 + [gtot] + [zmain_m] * 3 + [wd, bias, tg_m]
    return pl.pallas_call(
        functools.partial(_gla_kernel, n_tiles=n_tiles),
        name="gla_scan",
        grid=(batch, GLA_HEADS, nb),
        in_specs=in_specs,
        out_specs=pl.BlockSpec((seq, GLA_DV), lambda b, hh, i: (b, hh)),
        out_shape=jax.ShapeDtypeStruct((batch * seq, GLA_WIDTH), BF16),
        scratch_shapes=[pltpu.VMEM((GLA_DK, GLA_DV), F32), pltpu.VMEM((GLA_DK, GLA_DV), F32)],
        compiler_params=pltpu.CompilerParams(
            dimension_semantics=("parallel", "parallel", "arbitrary"), vmem_limit_bytes=VMEM_LIMIT),
    )(*args)


NA_SEG_COLS = ((0, 24), (24, 40), (40, 64))
NA_SEG_ROW0 = (0, 48, 80)
NA_STACK = 2 * GRID_W


NA_BASE_MID = NA_WIN_H // 2 - 1
NA_VARIANTS = 2 * NA_WIN_H - 1


def _na_variant(base, a):
    return jnp.where(base == NA_BASE_MID, a, jnp.where(base < NA_BASE_MID, NA_WIN_H + base, NA_WIN_H - 1 + base))


def _na_build_bias(rpb_ref, tab_ref):
    def build(v, carry):
        a = jnp.where(v < NA_WIN_H, v, 0)
        base = jnp.where(v < NA_WIN_H, NA_BASE_MID,
                         jnp.where(v < NA_WIN_H + NA_BASE_MID, v - NA_WIN_H, v - (NA_WIN_H - 1)))
        for hh in range(2):
            bias_rows = [rpb_ref[hh, pl.ds(base + ((t - a + NA_WIN_H) & (NA_WIN_H - 1)), 1), :]
                         for t in range(NA_WIN_H)]
            for s, (c0, c1) in enumerate(NA_SEG_COLS):
                n = c1 - c0
                vrows = [jnp.broadcast_to(br, (n, LANES)) for br in bias_rows]
                row0 = NA_SEG_ROW0[s] + hh * n
                rho = lax.broadcasted_iota(jnp.int32, (n, LANES), 0)
                lane = lax.broadcasted_iota(jnp.int32, (n, LANES), 1)
                cs = jnp.clip(c0 + rho - NA_WIN_W // 2, 0, GRID_W - NA_WIN_W)
                for half in range(2):
                    cb = s + half
                    cp = NA_CBW * cb + lane % NA_CBW
                    valid = (cp >= cs) & (cp < cs + NA_WIN_W)
                    acc = None
                    for i in range(NA_WIN_H):
                        shift = (NA_CBW * i - NA_CBW * cb + c0 - (NA_WIN_W - 1)) % LANES
                        rolled = pltpu.roll(vrows[i], shift, 1, stride=1, stride_axis=0)
                        acc = rolled if acc is None else jnp.where(lane // NA_CBW == i, rolled, acc)
                    tab_ref[v, row0:row0 + n, half * LANES:(half + 1) * LANES] = jnp.where(valid, acc, NEG_BIG)
        return carry

    lax.fori_loop(0, NA_VARIANTS, build, 0)


def _na_kernel(q_ref, kt_ref, v_ref, gate_ref, km_ref, vm_ref, rpb_ref, mb_ref, o_ref, tab_ref, sc_ref, sm_ref,
               *, rows_per_step, n_rows):
    w = GRID_W

    @pl.when(pl.program_id(1) == 0)
    def _():
        _na_build_bias(rpb_ref, tab_ref)

    head0_q = lax.broadcasted_iota(jnp.int32, (w, LANES), 1) < NA_DH
    srow = lax.broadcasted_iota(jnp.int32, (NA_STACK, N_META), 0)
    is_h1 = jnp.zeros((NA_STACK, N_META), jnp.bool_)
    for s, (c0, c1) in enumerate(NA_SEG_COLS):
        lo = NA_SEG_ROW0[s] + (c1 - c0)
        is_h1 = is_h1 | ((srow >= lo) & (srow < lo + (c1 - c0)))
    mb = jnp.where(is_h1, mb_ref[1], mb_ref[0])
    km = km_ref[...]
    vm = vm_ref[...]
    seg_rows = [slice(NA_SEG_ROW0[s], NA_SEG_ROW0[s] + 2 * (c1 - c0)) for s, (c0, c1) in enumerate(NA_SEG_COLS)]
    blk_rows = [slice(seg_rows[max(cb - 1, 0)].start, seg_rows[min(cb, len(seg_rows) - 1)].stop)
                for cb in range(NA_CB)]

    n_tiles = n_rows // NA_WIN_H
    key_lane = lax.broadcasted_iota(jnp.int32, (LANES, LANES), 1)

    def window(rl):
        row = pl.program_id(1) * rows_per_step + rl
        b = row // n_rows
        r = row - b * n_rows
        rs = jnp.clip(r - NA_WIN_H // 2, 0, n_rows - NA_WIN_H)
        a = rs & (NA_WIN_H - 1)
        return (b * n_tiles + rs // NA_WIN_H, a, (b + 1) * n_tiles - 1,
                _na_variant(rs - r + (NA_WIN_H - 1), a))

    def scores_store(slot, j, rl):
        m, a, m_last, var = window(rl)
        q = q_ref[pl.ds(pl.multiple_of(rl * w, w), w), :].astype(F32)
        q0 = jnp.where(head0_q, q, 0.0)
        q1 = q - q0
        qs = jnp.concatenate([piece[c0:c1] for c0, c1 in NA_SEG_COLS for piece in (q0, q1)],
                             axis=0).astype(BF16)
        from_m = key_lane >= a * NA_CBW
        m1 = jnp.minimum(m + 1, m_last)
        blocks = [_dot(qs[blk_rows[cb]],
                       jnp.where(from_m, kt_ref[NA_CB * m + cb], kt_ref[NA_CB * m1 + cb]))
                  for cb in range(NA_CB)]
        for s in range(len(NA_SEG_COLS)):
            n2 = seg_rows[s].stop - seg_rows[s].start
            for half in range(2):
                cb = s + half
                a0 = seg_rows[s].start - blk_rows[cb].start
                lanes = slice(half * LANES, (half + 1) * LANES)
                sc_ref[slot, j, seg_rows[s], lanes] = blocks[cb][a0:a0 + n2] + tab_ref[var, seg_rows[s], lanes]
        sm_ref[slot, j] = _dot_nt(qs, km) + mb

    def value_block(cb, m, a):
        groups = []
        for t in range(NA_WIN_H):
            row = NA_WIN_H * m + t + jnp.where(t < a, NA_WIN_H, 0)
            groups.append(v_ref[cb, pl.ds(pl.multiple_of(row * NA_CBW, NA_CBW), NA_CBW), :])
        return jnp.concatenate(groups, axis=0)

    def softmax_values(slot, j, rl):
        m_tile, a, _, _ = window(rl)
        sc = sc_ref[slot, j]
        sm = sm_ref[slot, j]
        m = jnp.maximum(sc[:, :LANES], sc[:, LANES:])
        m = jnp.maximum(jnp.max(m, axis=-1, keepdims=True), jnp.max(sm, axis=-1, keepdims=True))
        p = jnp.exp2(sc - m)
        pm = jnp.exp2(sm - m)
        den = (jnp.sum(p[:, :LANES] + p[:, LANES:], axis=-1, keepdims=True)
               + jnp.sum(pm, axis=-1, keepdims=True))
        p = p.astype(BF16)
        vals = [value_block(cb, m_tile, a) for cb in range(NA_CB)]
        o_meta = _dot(pm.astype(BF16), vm)
        inv = 1.0 / den
        outs = []
        for s, (c0, c1) in enumerate(NA_SEG_COLS):
            o_loc = _dot(p[seg_rows[s]], jnp.concatenate([vals[s], vals[s + 1]], axis=0))
            o = (o_loc + o_meta[seg_rows[s]]) * inv[seg_rows[s]]
            n = c1 - c0
            head0 = lax.broadcasted_iota(jnp.int32, (n, LANES), 1) < NA_DH
            outs.append(jnp.where(head0, o[:n], o[n:]))
        o = jnp.concatenate(outs, axis=0)
        qoff = pl.multiple_of(rl * w, w)
        gate = gate_ref[pl.ds(qoff, w), :].astype(F32)
        o_ref[pl.ds(qoff, w), :] = (o * _silu(gate)).astype(o_ref.dtype)

    n_groups = rows_per_step // NA_ROW_GROUP
    for j in range(NA_ROW_GROUP):
        scores_store(0, j, j)

    def trip(g, carry):
        slot = g % 2
        for j in range(NA_ROW_GROUP):
            scores_store(slot, j, g * NA_ROW_GROUP + j)
        for j in range(NA_ROW_GROUP):
            softmax_values(1 - slot, j, (g - 1) * NA_ROW_GROUP + j)
        return carry

    lax.fori_loop(1, n_groups, trip, 0)
    for j in range(NA_ROW_GROUP):
        softmax_values((n_groups - 1) % 2, j, (n_groups - 1) * NA_ROW_GROUP + j)


def _na_call(nq, nk_t, nv_cb, zmain, nk_m, nv_m, rpb_pad, meta_bias, *, batch, seq, rows_per_step):
    n_rows = seq // GRID_W
    total_rows = batch * n_rows
    assert n_rows % NA_WIN_H == 0 and total_rows % rows_per_step == 0 and rows_per_step % NA_ROW_GROUP == 0
    nsteps = total_rows // rows_per_step
    hp = NA_HEADS // 2
    tq = rows_per_step * GRID_W
    n_idx = 2 * NA_WIN_H - 1
    in_specs = [
        pl.BlockSpec((tq, LANES), lambda p, i: (i, p)),
        pl.BlockSpec((batch * seq // LANES, LANES, LANES), lambda p, i: (0, p, 0)),
        pl.BlockSpec((NA_CB, batch * seq // NA_CB, LANES), lambda p, i: (0, 0, p)),
        pl.BlockSpec((tq, LANES), lambda p, i: (i, COL_NG // LANES + p)),
        pl.BlockSpec((N_META, LANES), lambda p, i: (0, p)),
        pl.BlockSpec((N_META, LANES), lambda p, i: (0, p)),
        pl.BlockSpec((2, n_idx, LANES), lambda p, i: (p, 0, 0)),
        pl.BlockSpec((2, 1, N_META), lambda p, i: (p, 0, 0)),
    ]
    return pl.pallas_call(
        functools.partial(_na_kernel, rows_per_step=rows_per_step, n_rows=n_rows),
        name="na_attn",
        grid=(hp, nsteps),
        in_specs=in_specs,
        out_specs=pl.BlockSpec((tq, LANES), lambda p, i: (i, p)),
        out_shape=jax.ShapeDtypeStruct((batch * seq, NA_WIDTH), BF16),
        scratch_shapes=[pltpu.VMEM((NA_VARIANTS, NA_STACK, 2 * LANES), F32),
                        pltpu.VMEM((2, NA_ROW_GROUP, NA_STACK, 2 * LANES), F32),
                        pltpu.VMEM((2, NA_ROW_GROUP, NA_STACK, N_META), F32)],
        compiler_params=pltpu.CompilerParams(
            dimension_semantics=("arbitrary", "arbitrary"), vmem_limit_bytes=VMEM_LIMIT),
    )(nq, nk_t, nv_cb, zmain, nk_m, nv_m, rpb_pad, meta_bias)


def _out_kernel(x_ref, os_ref, gg_ref, na_ref, w_ref, o_ref):
    d = o_ref.shape[1]
    na = na_ref[...]

    def na_tile(c):
        cols = slice(c * OUT_TN, (c + 1) * OUT_TN)
        o_ref[:, cols] = x_ref[:, cols] + _dot(na, w_ref[GLA_WIDTH:, cols])

    def gla_head(h):
        cols = slice(h * GLA_DV, (h + 1) * GLA_DV)
        sh = os_ref[:, cols].astype(F32)
        ms = jnp.mean(sh * sh, axis=-1, keepdims=True)
        return (sh * lax.rsqrt(ms + RMS_EPS) * _silu_tanh(gg_ref[:, cols].astype(F32))).astype(BF16)

    heads = []
    for i in range(max(d // OUT_TN, GLA_HEADS)):
        if i < d // OUT_TN:
            na_tile(i)
        if i < GLA_HEADS:
            heads.append(gla_head(i))
    o_gla = jnp.concatenate(heads, axis=1)
    for c in range(d // OUT_TN):
        cols = slice(c * OUT_TN, (c + 1) * OUT_TN)
        o_ref[:, cols] += _dot(o_gla, w_ref[:GLA_WIDTH, cols])


def _out_call(x2d, o_sum, zmain, o_na, w_out, *, tm):
    m, d = x2d.shape
    assert m % tm == 0 and d % OUT_TN == 0
    return pl.pallas_call(
        _out_kernel,
        name="out_proj",
        grid=(m // tm,),
        in_specs=[
            pl.BlockSpec((tm, d), lambda i: (i, 0)),
            pl.BlockSpec((tm, GLA_WIDTH), lambda i: (i, 0)),
            pl.BlockSpec((tm, GLA_WIDTH), lambda i: (i, COL_GG // GLA_WIDTH)),
            pl.BlockSpec((tm, NA_WIDTH), lambda i: (i, 0)),
            pl.BlockSpec((GLA_WIDTH + NA_WIDTH, d), lambda i: (0, 0), pipeline_mode=pl.Buffered(1)),
        ],
        out_specs=pl.BlockSpec((tm, d), lambda i: (i, 0)),
        out_shape=jax.ShapeDtypeStruct((m, d), F32),
        compiler_params=pltpu.CompilerParams(
            dimension_semantics=("parallel",), vmem_limit_bytes=VMEM_LIMIT),
    )(x2d, o_sum, zmain, o_na, w_out)


def _pick(n, cands):
    for c in cands:
        if n % c == 0:
            return c
    raise ValueError(f"no tile for {n}")


def kernel(x, meta_tokens, norm_g, w_in, w_decay_fwd, b_decay_fwd, w_decay_bwd, b_decay_bwd,
           gla_out_norm_g, q_norm_g, k_norm_g, rpb, meta_bias, w_out):
    batch, seq, d = x.shape
    assert d == D_MODEL and seq % TILE == 0 and seq % GRID_W == 0
    depth = norm_g.shape[0]
    assert depth == 1
    l = 0
    x2d = x.reshape(batch * seq, d)
    g = norm_g[l].reshape(1, d)

    assert w_in.shape[1:] == (d, D_IN_PROJ)
    wt = jnp.swapaxes(w_in[l], 0, 1).astype(BF16)
    gain_qk = jnp.concatenate([jnp.tile(q_norm_g[l], NA_HEADS) * (NA_DH ** -0.5 * LOG2_E),
                               jnp.tile(k_norm_g[l], NA_HEADS)]).reshape(1, 2 * NA_WIDTH)

    zr = jnp.zeros((GLA_RANK, GLA_KEY_WIDTH), F32)
    ztail = jnp.zeros((LANES - 2 * GLA_RANK, GLA_KEY_WIDTH), F32)
    wd = jnp.stack([jnp.concatenate([w_decay_fwd[l], zr, ztail], axis=0),
                    jnp.concatenate([zr, w_decay_bwd[l], ztail], axis=0)]).astype(BF16)
    bias = jnp.stack([b_decay_fwd[l], b_decay_bwd[l]]).reshape(2, 1, GLA_KEY_WIDTH)
    tg, tg_m = _gla_constants()

    assert seq % (NA_WIN_H * GRID_W) == 0
    kgain_col = jnp.tile(k_norm_g[l], NA_HEADS).reshape(NA_WIDTH, 1)
    zmain, nq, nk_t, nv_cb, gtot = _proj_call(x2d, g, wt, gain_qk, kgain_col, wd, bias, tg,
                                              tm=NA_WIN_H * GRID_W, tokens=True, name="proj_tokens")
    zmain_m, _, nk_m, nv_m, _ = _proj_call(meta_tokens, g, wt, gain_qk, kgain_col, wd, bias, tg,
                                           tm=N_META, tokens=False, name="proj_meta")
    zmain_m_pad = jnp.pad(zmain_m, ((0, TILE - N_META), (0, 0)))

    o_sum = _gla_call(zmain, gtot, zmain_m_pad, wd, bias, tg_m, batch=batch, seq=seq)

    rpb_pad = jnp.pad(rpb[l] * LOG2_E, ((0, 0), (0, 0), (0, 2 * NA_DH - rpb.shape[-1])))
    mb = (meta_bias[l] * LOG2_E).reshape(NA_HEADS, 1, N_META)
    o_na = _na_call(nq, nk_t, nv_cb, zmain, nk_m, nv_m, rpb_pad, mb, batch=batch, seq=seq,
                    rows_per_step=_pick(batch * seq // GRID_W, (256, 128, 64, 32, 16)))

    gn_rows = jnp.concatenate([jnp.tile(gla_out_norm_g[l], GLA_HEADS), jnp.ones((NA_WIDTH,), F32)])
    w_o = (w_out[l] * gn_rows[:, None]).astype(BF16)
    out = _out_call(x2d, o_sum, zmain, o_na, w_o, tm=_pick(batch * seq, (1024, 512, 256, 128)))
    return out.reshape(batch, seq, d)
```

```python
import functools

import jax
import jax.numpy as jnp
import numpy as np
from jax import lax
from jax.experimental import pallas as pl
from jax.experimental.pallas import tpu as pltpu

F32 = jnp.float32
BF16 = jnp.bfloat16

D_MODEL = 1024
N_META = 16
GRID_W = 64
GLA_HEADS = 4
GLA_DK = 128
GLA_DV = 256
GLA_KEY_WIDTH = GLA_HEADS * GLA_DK
GLA_WIDTH = GLA_HEADS * GLA_DV
GLA_RANK = 16
GLA_TAU = 16.0
GLA_CHUNK = 64
NA_HEADS = 16
NA_DH = 64
NA_WIDTH = NA_HEADS * NA_DH
NA_WIN_H = 8
NA_WIN_W = 16
RMS_EPS = 1e-6
NEG_BIG = -1e30
LOG2_E = 1.4426950408889634
LANES = 128

COL_GV, COL_GG, COL_NG, COL_GQ, COL_GK, COL_R = 0, 1024, 2048, 3072, 3584, 4096
MAIN_W = COL_R + LANES
COL_DIR0, DIR_W = 3072, 3 * 512
MAIN_TOK_W = COL_DIR0 + 2 * DIR_W
NA_CBW = 16
NA_CB = GRID_W // NA_CBW
IN_GQ, IN_GK, IN_GV, IN_R, IN_GG, IN_NQ, IN_NK, IN_NV, IN_NG = 0, 512, 1024, 2048, 2080, 3104, 4128, 5152, 6176
D_IN_PROJ = 7200
PROJ_TN = 512
OUT_TN = 256

TILE = 2 * GLA_CHUNK
NA_ROW_GROUP = 16
VMEM_LIMIT = 58 * 1024 * 1024


def _split_bf16(x):
    hi = x.astype(BF16)
    lo = (x - hi.astype(F32)).astype(BF16)
    return hi, lo


def _dot(a, b):
    return jnp.dot(a, b, preferred_element_type=F32)


def _dot_nt(a, b):
    return lax.dot_general(a, b, (((1,), (1,)), ((), ())), preferred_element_type=F32)


def _dot_tn(a, b):
    return lax.dot_general(a, b, (((0,), (0,)), ((), ())), preferred_element_type=F32)


def _log_sigmoid(x):
    return jnp.minimum(x, 0.0) - jnp.log(1.0 + jnp.exp(-jnp.abs(x)))


def _silu(x):
    return x * (1.0 / (1.0 + jnp.exp(-x)))


def _silu_tanh(x):
    h = 0.5 * x
    return h + h * jnp.tanh(h)


SHARED_SEGMENTS = ((COL_GV, IN_GV, 1024), (COL_GG, IN_GG, 1024), (COL_NG, IN_NG, 1024))
META_SEGMENTS = ((COL_GQ, IN_GQ, 512), (COL_GK, IN_GK, 512),
                 (COL_R, IN_R, LANES))


def _store_rows(o_ref, c0, y, permute):
    n = y.shape[1]
    if not permute:
        o_ref[:, c0:c0 + n] = y
        return
    for rl in range(y.shape[0] // GRID_W):
        for cb in range(NA_CB):
            src = rl * GRID_W + cb * NA_CBW
            for c in range(0, n, LANES):
                o_ref[(c0 + c) // LANES, cb, rl * NA_CBW:(rl + 1) * NA_CBW, :] = y[src:src + NA_CBW, c:c + LANES]


def _gla_log_decay(r, wd_ref, bdec_ref, d):
    return (_log_sigmoid(_dot(r, wd_ref[d]) + bdec_ref[d]) * (1.0 / GLA_TAU)).astype(BF16)


def _gla_operands_store(q, k, log_decay, d, t, tg_ref, zm_ref, gtot_ref):
    rows = slice(t * TILE, (t + 1) * TILE)
    g = _dot(tg_ref[d], log_decay[rows])
    col = COL_DIR0 + d * DIR_W
    tots = []
    for c in range(TILE // GLA_CHUNK):
        last = c * GLA_CHUNK + (GLA_CHUNK - 1 if d == 0 else 0)
        tot = g[last:last + 1]
        gtot_ref[(t * TILE) // GLA_CHUNK + c:(t * TILE) // GLA_CHUNK + c + 1,
                 d * GLA_KEY_WIDTH:(d + 1) * GLA_KEY_WIDTH] = tot
        tots.append(jnp.broadcast_to(tot, (GLA_CHUNK, GLA_KEY_WIDTH)))
    dd = jnp.concatenate(tots, axis=0) - g
    zm_ref[rows, col:col + 512] = (q[rows] * jnp.exp(g)).astype(BF16)
    zm_ref[rows, col + 512:col + 1024] = (k[rows] * jnp.exp(-g)).astype(BF16)
    zm_ref[rows, col + 1024:col + 1536] = (k[rows] * jnp.exp(dd)).astype(BF16)


def _proj_kernel(x0_ref, xn_ref, g_ref, wt_ref, gain_ref, kgain_ref, wd_ref, bdec_ref, tg_ref,
                 zm_ref, nq_ref, nk_ref, nv_ref, gtot_ref, u_scr, *, tokens):
    def normed(x_ref):
        x = x_ref[...]
        ms = jnp.mean(x * x, axis=-1, keepdims=True)
        return (x * lax.rsqrt(ms + RMS_EPS) * g_ref[...]).astype(BF16)

    step = pl.program_id(0)

    @pl.when(step == 0)
    def _():
        u_scr[0] = normed(x0_ref)

    slot = step % 2
    u = u_scr[slot]
    tm = u.shape[0]

    def prepare_next():
        u_scr[1 - slot] = normed(xn_ref)

    def main_tile(dst, src, off, n):
        acc = _dot_nt(u, wt_ref[src + off:src + off + n, :])
        if dst == COL_GQ:
            acc = acc * (GLA_DK ** -0.5)
        zm_ref[:, dst + off:dst + off + n] = acc.astype(zm_ref.dtype)

    def nv_tile(off):
        acc = _dot_nt(u, wt_ref[IN_NV + off:IN_NV + off + PROJ_TN, :])
        _store_rows(nv_ref, off, acc.astype(nv_ref.dtype), tokens)

    def head_norm_rows(acc, goff):
        first = lax.broadcasted_iota(jnp.int32, (acc.shape[0], LANES), 1) < NA_DH
        cols = []
        for c in range(0, acc.shape[1], LANES):
            a = acc[:, c:c + LANES]
            sq = a * a
            lo = jnp.sum(jnp.where(first, sq, 0.0), axis=-1, keepdims=True)
            hi = jnp.sum(jnp.where(first, 0.0, sq), axis=-1, keepdims=True)
            inv = jnp.where(first, lax.rsqrt(lo * (1.0 / NA_DH) + RMS_EPS), lax.rsqrt(hi * (1.0 / NA_DH) + RMS_EPS))
            cols.append(a * inv)
        return (jnp.concatenate(cols, axis=1) * gain_ref[:, goff:goff + PROJ_TN]).astype(BF16)

    def nq_tile(off):
        acc = _dot_nt(u, wt_ref[IN_NQ + off:IN_NQ + off + PROJ_TN, :])
        y = head_norm_rows(acc, off)
        if not tokens:
            nq_ref[:, off:off + PROJ_TN] = y
            return
        for c in range(0, PROJ_TN, LANES):
            nq_ref[(off + c) // LANES] = y[:, c:c + LANES]

    def nk_rows_tile(off):
        acc = _dot_nt(u, wt_ref[IN_NK + off:IN_NK + off + PROJ_TN, :])
        nk_ref[:, off:off + PROJ_TN] = head_norm_rows(acc, NA_WIDTH + off)

    def nk_transposed_tile(off):
        u_cb = jnp.concatenate([u[rl * GRID_W + cb * NA_CBW:rl * GRID_W + (cb + 1) * NA_CBW]
                                for cb in range(NA_CB) for rl in range(tm // GRID_W)], axis=0)
        acc_t = _dot_nt(wt_ref[IN_NK + off:IN_NK + off + PROJ_TN, :], u_cb)
        for h in range(PROJ_TN // NA_DH):
            rows = slice(off + h * NA_DH, off + (h + 1) * NA_DH)
            blk = acc_t[h * NA_DH:(h + 1) * NA_DH]
            ms = jnp.sum(blk * blk, axis=0, keepdims=True) * (1.0 / NA_DH)
            y = (blk * lax.rsqrt(ms + RMS_EPS) * kgain_ref[rows, :]).astype(BF16)
            for cb in range(NA_CB):
                nk_ref[cb, rows, :] = y[:, cb * LANES:(cb + 1) * LANES]

    nk_tile = nk_transposed_tile if tokens else nk_rows_tile
    tiles = [functools.partial(nk_tile, off) for off in range(0, NA_WIDTH, PROJ_TN)]
    tiles += [functools.partial(nq_tile, off) for off in range(0, NA_WIDTH, PROJ_TN)]
    tiles += [functools.partial(nv_tile, off) for off in range(0, NA_WIDTH, PROJ_TN)]
    for dst, src, width in SHARED_SEGMENTS + (() if tokens else META_SEGMENTS):
        for off in range(0, width, PROJ_TN):
            tiles.append(functools.partial(main_tile, dst, src, off, min(PROJ_TN, width - off)))
    if not tokens:
        gtot_ref[...] = jnp.zeros_like(gtot_ref)
        for tile in tiles:
            tile()
        prepare_next()
        return

    r = _dot_nt(u, wt_ref[IN_R:IN_R + LANES, :]).astype(BF16)
    q = _dot_nt(u, wt_ref[IN_GQ:IN_GQ + GLA_KEY_WIDTH, :]) * (GLA_DK ** -0.5)
    k = _dot_nt(u, wt_ref[IN_GK:IN_GK + GLA_KEY_WIDTH, :])
    log_decay = [None, None]

    def decay_stage(d):
        log_decay[d] = _gla_log_decay(r, wd_ref, bdec_ref, d)

    stages = [functools.partial(decay_stage, 0), functools.partial(decay_stage, 1), None]
    stages += [functools.partial(lambda d, t: _gla_operands_store(q, k, log_decay[d], d, t, tg_ref, zm_ref, gtot_ref),
                                 d, t)
               for d in range(2) for t in range(tm // TILE)]
    for i, tile in enumerate(tiles):
        if i < len(stages) and stages[i] is not None:
            stages[i]()
        tile()
    prepare_next()
    assert len(stages) <= len(tiles)


def _proj_call(x2d, g, wt, gain_qk, kgain_col, wd, bdec, tg, *, tm, tokens, name):
    m, d = x2d.shape
    assert m % tm == 0 and wt.shape == (D_IN_PROJ, d)
    resident = lambda shape: pl.BlockSpec(shape, lambda i: (0,) * len(shape), pipeline_mode=pl.Buffered(1))
    plain = pl.BlockSpec((tm, NA_WIDTH), lambda i: (i, 0))
    plain_shape = jax.ShapeDtypeStruct((m, NA_WIDTH), BF16)
    if tokens:
        assert tm == NA_WIN_H * GRID_W and NA_CB * LANES == tm
        main_w = MAIN_TOK_W
        n_pairs = NA_WIDTH // LANES
        q_spec = pl.BlockSpec((n_pairs, tm, LANES), lambda i: (0, i, 0))
        q_shape = jax.ShapeDtypeStruct((n_pairs, m, LANES), BF16)
        v_spec = pl.BlockSpec((n_pairs, NA_CB, tm // NA_CB, LANES), lambda i: (0, 0, i, 0))
        v_shape = jax.ShapeDtypeStruct((n_pairs, NA_CB, m // NA_CB, LANES), BF16)
        k_spec = pl.BlockSpec((NA_CB, NA_WIDTH, LANES), lambda i: (i, 0, 0))
        k_shape = jax.ShapeDtypeStruct((m // LANES, NA_WIDTH, LANES), BF16)
        n_chunk_rows = tm // GLA_CHUNK
    else:
        main_w = MAIN_W
        q_spec, q_shape, v_spec, v_shape, k_spec, k_shape = plain, plain_shape, plain, plain_shape, plain, plain_shape
        n_chunk_rows = 8
    gt_rows = (m // tm) * n_chunk_rows
    return pl.pallas_call(
        functools.partial(_proj_kernel, tokens=tokens),
        name=name,
        grid=(m // tm,),
        in_specs=[
            resident((tm, d)),
            pl.BlockSpec((tm, d), lambda i: (jnp.minimum(i + 1, m // tm - 1), 0)),
            resident((1, d)),
            resident((D_IN_PROJ, d)),
            resident((1, 2 * NA_WIDTH)),
            resident((NA_WIDTH, 1)),
            resident((2, LANES, GLA_KEY_WIDTH)),
            resident((2, 1, GLA_KEY_WIDTH)),
            resident((2, TILE, TILE)),
        ],
        out_specs=[pl.BlockSpec((tm, main_w), lambda i: (i, 0)), q_spec, k_spec, v_spec,
                   pl.BlockSpec((n_chunk_rows, 2 * GLA_KEY_WIDTH), lambda i: (i, 0))],
        out_shape=[jax.ShapeDtypeStruct((m, main_w), BF16), q_shape, k_shape, v_shape,
                   jax.ShapeDtypeStruct((gt_rows, 2 * GLA_KEY_WIDTH), F32)],
        scratch_shapes=[pltpu.VMEM((2, tm, d), BF16)],
        compiler_params=pltpu.CompilerParams(
            dimension_semantics=("arbitrary",), vmem_limit_bytes=VMEM_LIMIT),
    )(x2d, x2d, g, wt, gain_qk, kgain_col, wd, bdec, tg)


def _gla_kernel(qf_ref, kif_ref, ksf_ref, vf_ref, gtf_ref, qb_ref, kib_ref, ksb_ref, vb_ref, gtb_ref,
                km_ref, vm_ref, rm_ref, wd_ref, b_ref, tgm_ref,
                o_ref, sf_ref, sb_ref, *, n_tiles):
    row = lax.broadcasted_iota(jnp.int32, (TILE, TILE), 0)
    col = lax.broadcasted_iota(jnp.int32, (TILE, TILE), 1)
    same = (row // GLA_CHUNK) == (col // GLA_CHUNK)
    masks = (same & (row >= col), same & (row <= col))

    @pl.when(pl.program_id(2) == 0)
    def _():
        rowm = lax.broadcasted_iota(jnp.int32, (TILE, GLA_DK), 0)
        lg = _log_sigmoid(_dot(rm_ref[...], wd_ref[...]) + b_ref[...]) * (1.0 / GLA_TAU)
        hi, lo = _split_bf16(jnp.where(rowm < N_META, lg, 0.0))
        d_m = _dot(tgm_ref[...], jnp.concatenate([hi, lo], axis=0))
        k_st = (km_ref[...].astype(F32) * jnp.exp(d_m)).astype(BF16)
        sf_ref[...] = _dot_tn(k_st, vm_ref[...])
        sb_ref[...] = jnp.zeros_like(sb_ref)

    def scan(step):
        items = []
        for t in range(n_tiles):
            items.append((0, t))
            items.append((1, n_tiles - 1 - t))
        refs = ((qf_ref, kif_ref, ksf_ref, vf_ref), (qb_ref, kib_ref, ksb_ref, vb_ref))
        out_half = (step, 1 - step)
        a_cols = (jnp.exp(jnp.transpose(gtf_ref[...])), jnp.exp(jnp.transpose(gtb_ref[...])))

        def rows_of(t):
            return slice(t * TILE, (t + 1) * TILE)

        amats = [jnp.where(masks[d], _dot_nt(refs[d][0][rows_of(t), :], refs[d][1][rows_of(t), :]), 0.0).astype(BF16)
                 for d, t in items]

        def state_independent(idx):
            d, t = items[idx]
            v = refs[d][3][rows_of(t), :]
            k_st = refs[d][2][rows_of(t), :]
            u = [_dot_tn(k_st[h * GLA_CHUNK:(h + 1) * GLA_CHUNK], v[h * GLA_CHUNK:(h + 1) * GLA_CHUNK])
                 for h in range(2)]
            return _dot(amats[idx], v), u

        state = [sf_ref[...], sb_ref[...]]
        ahead = state_independent(0)
        for idx, (d, t) in enumerate(items):
            o_intra, u = ahead
            if idx + 1 < len(items):
                ahead = state_independent(idx + 1)
            s = state[d]
            outs = [None, None]
            for h in ((0, 1) if d == 0 else (1, 0)):
                rows = slice(t * TILE + h * GLA_CHUNK, t * TILE + (h + 1) * GLA_CHUNK)
                outs[h] = o_intra[h * GLA_CHUNK:(h + 1) * GLA_CHUNK] + _dot(refs[d][0][rows, :], s.astype(BF16))
                chunk = 2 * t + h
                s = a_cols[d][:, chunk:chunk + 1] * s + u[h]
            state[d] = s
            o = jnp.concatenate(outs, axis=0)
            out_rows = rows_of(out_half[d] * n_tiles + t)
            if step == 1:
                o = o + o_ref[out_rows, :].astype(F32)
            o_ref[out_rows, :] = o.astype(o_ref.dtype)
        sf_ref[...] = state[0]
        sb_ref[...] = state[1]

    for step in range(2):
        pl.when(pl.program_id(2) == step)(functools.partial(scan, step))


def _gla_constants():
    i = np.arange(TILE)
    same = (i[:, None] // GLA_CHUNK) == (i[None, :] // GLA_CHUNK)
    inc_f = same & (i[None, :] <= i[:, None])
    inc_b = same & (i[None, :] >= i[:, None])
    to_end_meta = i[None, :] > i[:, None]
    dup = lambda m: np.concatenate([m, m], axis=1)
    tg = jnp.asarray(np.stack([inc_f, inc_b]), BF16)
    return tg, jnp.asarray(dup(to_end_meta), BF16)


def _gla_call(zmain, gtot, zmain_m, wd, bias, tg_m, *, batch, seq):
    nb = 2
    assert seq % (nb * TILE) == 0
    t_blk = seq // nb
    n_tiles = t_blk // TILE
    n_chunks = t_blk // GLA_CHUNK

    def tok(fwd):
        if fwd:
            return lambda b, hh, i: b * nb + i
        return lambda b, hh, i: b * nb + (nb - 1 - i)

    def dir_specs(d):
        t = tok(d == 0)
        c0 = (COL_DIR0 + d * DIR_W) // GLA_DK
        return [
            pl.BlockSpec((t_blk, GLA_DK), lambda b, hh, i: (t(b, hh, i), c0 + hh)),
            pl.BlockSpec((t_blk, GLA_DK), lambda b, hh, i: (t(b, hh, i), c0 + GLA_HEADS + hh)),
            pl.BlockSpec((t_blk, GLA_DK), lambda b, hh, i: (t(b, hh, i), c0 + 2 * GLA_HEADS + hh)),
            pl.BlockSpec((t_blk, GLA_DV), lambda b, hh, i: (t(b, hh, i), COL_GV // GLA_DV + hh)),
            pl.BlockSpec((n_chunks, GLA_DK), lambda b, hh, i: (t(b, hh, i), d * GLA_HEADS + hh)),
        ]

    in_specs = (
        dir_specs(0) + dir_specs(1)
        + [pl.BlockSpec((TILE, GLA_DK), lambda b, hh, i: (0, COL_GK // GLA_DK + hh)),
           pl.BlockSpec((TILE, GLA_DV), lambda b, hh, i: (0, COL_GV // GLA_DV + hh)),
           pl.BlockSpec((TILE, LANES), lambda b, hh, i: (0, COL_R // LANES)),
           pl.BlockSpec((None, LANES, GLA_DK), lambda b, hh, i: (0, 0, hh)),
           pl.BlockSpec((None, 1, GLA_DK), lambda b, hh, i: (0, 0, hh)),
           pl.BlockSpec((TILE, 2 * TILE), lambda b, hh, i: (0, 0))]
    )
    args = [zmain] * 4 + [gtot] + [zmain] * 4 + [gtot] + [zmain_m] * 3 + [wd, bias, tg_m]
    return pl.pallas_call(
        functools.partial(_gla_kernel, n_tiles=n_tiles),
        name="gla_scan",
        grid=(batch, GLA_HEADS, nb),
        in_specs=in_specs,
        out_specs=pl.BlockSpec((seq, GLA_DV), lambda b, hh, i: (b, hh)),
        out_shape=jax.ShapeDtypeStruct((batch * seq, GLA_WIDTH), BF16),
        scratch_shapes=[pltpu.VMEM((GLA_DK, GLA_DV), F32), pltpu.VMEM((GLA_DK, GLA_DV), F32)],
        compiler_params=pltpu.CompilerParams(
            dimension_semantics=("parallel", "parallel", "arbitrary"), vmem_limit_bytes=VMEM_LIMIT),
    )(*args)


NA_SEG_COLS = ((0, 24), (24, 40), (40, 64))
NA_SEG_ROW0 = (0, 48, 80)
NA_STACK = 2 * GRID_W


NA_BASE_MID = NA_WIN_H // 2 - 1
NA_VARIANTS = 2 * NA_WIN_H - 1


def _na_variant(base, a):
    return jnp.where(base == NA_BASE_MID, a, jnp.where(base < NA_BASE_MID, NA_WIN_H + base, NA_WIN_H - 1 + base))


def _na_build_bias(rpb_ref, tab_ref):
    def build(v, carry):
        a = jnp.where(v < NA_WIN_H, v, 0)
        base = jnp.where(v < NA_WIN_H, NA_BASE_MID,
                         jnp.where(v < NA_WIN_H + NA_BASE_MID, v - NA_WIN_H, v - (NA_WIN_H - 1)))
        for hh in range(2):
            bias_rows = [rpb_ref[hh, pl.ds(base + ((t - a + NA_WIN_H) & (NA_WIN_H - 1)), 1), :]
                         for t in range(NA_WIN_H)]
            for s, (c0, c1) in enumerate(NA_SEG_COLS):
                n = c1 - c0
                vrows = [jnp.broadcast_to(br, (n, LANES)) for br in bias_rows]
                row0 = NA_SEG_ROW0[s] + hh * n
                rho = lax.broadcasted_iota(jnp.int32, (n, LANES), 0)
                lane = lax.broadcasted_iota(jnp.int32, (n, LANES), 1)
                cs = jnp.clip(c0 + rho - NA_WIN_W // 2, 0, GRID_W - NA_WIN_W)
                for half in range(2):
                    cb = s + half
                    cp = NA_CBW * cb + lane % NA_CBW
                    valid = (cp >= cs) & (cp < cs + NA_WIN_W)
                    acc = None
                    for i in range(NA_WIN_H):
                        shift = (NA_CBW * i - NA_CBW * cb + c0 - (NA_WIN_W - 1)) % LANES
                        rolled = pltpu.roll(vrows[i], shift, 1, stride=1, stride_axis=0)
                        acc = rolled if acc is None else jnp.where(lane // NA_CBW == i, rolled, acc)
                    tab_ref[v, row0:row0 + n, half * LANES:(half + 1) * LANES] = jnp.where(valid, acc, NEG_BIG)
        return carry

    lax.fori_loop(0, NA_VARIANTS, build, 0)


def _na_kernel(q_ref, kt_ref, v_ref, gate_ref, km_ref, vm_ref, rpb_ref, mb_ref, o_ref, tab_ref, sc_ref, sm_ref,
               *, rows_per_step, n_rows):
    w = GRID_W

    @pl.when(pl.program_id(1) == 0)
    def _():
        _na_build_bias(rpb_ref, tab_ref)

    head0_q = lax.broadcasted_iota(jnp.int32, (w, LANES), 1) < NA_DH
    srow = lax.broadcasted_iota(jnp.int32, (NA_STACK, N_META), 0)
    is_h1 = jnp.zeros((NA_STACK, N_META), jnp.bool_)
    for s, (c0, c1) in enumerate(NA_SEG_COLS):
        lo = NA_SEG_ROW0[s] + (c1 - c0)
        is_h1 = is_h1 | ((srow >= lo) & (srow < lo + (c1 - c0)))
    mb = jnp.where(is_h1, mb_ref[1], mb_ref[0])
    km = km_ref[...]
    vm = vm_ref[...]
    seg_rows = [slice(NA_SEG_ROW0[s], NA_SEG_ROW0[s] + 2 * (c1 - c0)) for s, (c0, c1) in enumerate(NA_SEG_COLS)]
    blk_rows = [slice(seg_rows[max(cb - 1, 0)].start, seg_rows[min(cb, len(seg_rows) - 1)].stop)
                for cb in range(NA_CB)]

    n_tiles = n_rows // NA_WIN_H
    key_lane = lax.broadcasted_iota(jnp.int32, (LANES, LANES), 1)

    def window(rl):
        row = pl.program_id(1) * rows_per_step + rl
        b = row // n_rows
        r = row - b * n_rows
        rs = jnp.clip(r - NA_WIN_H // 2, 0, n_rows - NA_WIN_H)
        a = rs & (NA_WIN_H - 1)
        return (b * n_tiles + rs // NA_WIN_H, a, (b + 1) * n_tiles - 1,
                _na_variant(rs - r + (NA_WIN_H - 1), a))

    def scores_store(slot, j, rl):
        m, a, m_last, var = window(rl)
        q = q_ref[pl.ds(pl.multiple_of(rl * w, w), w), :].astype(F32)
        q0 = jnp.where(head0_q, q, 0.0)
        q1 = q - q0
        qs = jnp.concatenate([piece[c0:c1] for c0, c1 in NA_SEG_COLS for piece in (q0, q1)],
                             axis=0).astype(BF16)
        from_m = key_lane >= a * NA_CBW
        m1 = jnp.minimum(m + 1, m_last)
        blocks = [_dot(qs[blk_rows[cb]],
                       jnp.where(from_m, kt_ref[NA_CB * m + cb], kt_ref[NA_CB * m1 + cb]))
                  for cb in range(NA_CB)]
        for s in range(len(NA_SEG_COLS)):
            n2 = seg_rows[s].stop - seg_rows[s].start
            for half in range(2):
                cb = s + half
                a0 = seg_rows[s].start - blk_rows[cb].start
                lanes = slice(half * LANES, (half + 1) * LANES)
                sc_ref[slot, j, seg_rows[s], lanes] = blocks[cb][a0:a0 + n2] + tab_ref[var, seg_rows[s], lanes]
        sm_ref[slot, j] = _dot_nt(qs, km) + mb

    def value_block(cb, m, a):
        groups = []
        for t in range(NA_WIN_H):
            row = NA_WIN_H * m + t + jnp.where(t < a, NA_WIN_H, 0)
            groups.append(v_ref[cb, pl.ds(pl.multiple_of(row * NA_CBW, NA_CBW), NA_CBW), :])
        return jnp.concatenate(groups, axis=0)

    def softmax_values(slot, j, rl):
        m_tile, a, _, _ = window(rl)
        sc = sc_ref[slot, j]
        sm = sm_ref[slot, j]
        m = jnp.maximum(sc[:, :LANES], sc[:, LANES:])
        m = jnp.maximum(jnp.max(m, axis=-1, keepdims=True), jnp.max(sm, axis=-1, keepdims=True))
        p = jnp.exp2(sc - m)
        pm = jnp.exp2(sm - m)
        den = (jnp.sum(p[:, :LANES] + p[:, LANES:], axis=-1, keepdims=True)
               + jnp.sum(pm, axis=-1, keepdims=True))
        p = p.astype(BF16)
        vals = [value_block(cb, m_tile, a) for cb in range(NA_CB)]
        o_meta = _dot(pm.astype(BF16), vm)
        inv = 1.0 / den
        outs = []
        for s, (c0, c1) in enumerate(NA_SEG_COLS):
            o_loc = _dot(p[seg_rows[s]], jnp.concatenate([vals[s], vals[s + 1]], axis=0))
            o = (o_loc + o_meta[seg_rows[s]]) * inv[seg_rows[s]]
            n = c1 - c0
            head0 = lax.broadcasted_iota(jnp.int32, (n, LANES), 1) < NA_DH
            outs.append(jnp.where(head0, o[:n], o[n:]))
        o = jnp.concatenate(outs, axis=0)
        qoff = pl.multiple_of(rl * w, w)
        gate = gate_ref[pl.ds(qoff, w), :].astype(F32)
        o_ref[pl.ds(qoff, w), :] = (o * _silu(gate)).astype(o_ref.dtype)

    n_groups = rows_per_step // NA_ROW_GROUP
    for j in range(NA_ROW_GROUP):
        scores_store(0, j, j)

    def trip(g, carry):
        slot = g % 2
        for j in range(NA_ROW_GROUP):
            scores_store(slot, j, g * NA_ROW_GROUP + j)
        for j in range(NA_ROW_GROUP):
            softmax_values(1 - slot, j, (g - 1) * NA_ROW_GROUP + j)
        return carry

    lax.fori_loop(1, n_groups, trip, 0)
    for j in range(NA_ROW_GROUP):
        softmax_values((n_groups - 1) % 2, j, (n_groups - 1) * NA_ROW_GROUP + j)


def _na_call(nq, nk_t, nv_cb, zmain, nk_m, nv_m, rpb_pad, meta_bias, *, batch, seq, rows_per_step):
    n_rows = seq // GRID_W
    total_rows = batch * n_rows
    assert n_rows % NA_WIN_H == 0 and total_rows % rows_per_step == 0 and rows_per_step % NA_ROW_GROUP == 0
    nsteps = total_rows // rows_per_step
    hp = NA_HEADS // 2
    tq = rows_per_step * GRID_W
    n_idx = 2 * NA_WIN_H - 1
    in_specs = [
        pl.BlockSpec((None, tq, LANES), lambda p, i: (p, i, 0)),
        pl.BlockSpec((batch * seq // LANES, LANES, LANES), lambda p, i: (0, p, 0)),
        pl.BlockSpec((None, NA_CB, batch * seq // NA_CB, LANES), lambda p, i: (p, 0, 0, 0)),
        pl.BlockSpec((tq, LANES), lambda p, i: (i, COL_NG // LANES + p)),
        pl.BlockSpec((N_META, LANES), lambda p, i: (0, p)),
        pl.BlockSpec((N_META, LANES), lambda p, i: (0, p)),
        pl.BlockSpec((2, n_idx, LANES), lambda p, i: (p, 0, 0)),
        pl.BlockSpec((2, 1, N_META), lambda p, i: (p, 0, 0)),
    ]
    return pl.pallas_call(
        functools.partial(_na_kernel, rows_per_step=rows_per_step, n_rows=n_rows),
        name="na_attn",
        grid=(hp, nsteps),
        in_specs=in_specs,
        out_specs=pl.BlockSpec((None, tq, LANES), lambda p, i: (p, i, 0)),
        out_shape=jax.ShapeDtypeStruct((hp, batch * seq, LANES), BF16),
        scratch_shapes=[pltpu.VMEM((NA_VARIANTS, NA_STACK, 2 * LANES), F32),
                        pltpu.VMEM((2, NA_ROW_GROUP, NA_STACK, 2 * LANES), F32),
                        pltpu.VMEM((2, NA_ROW_GROUP, NA_STACK, N_META), F32)],
        compiler_params=pltpu.CompilerParams(
            dimension_semantics=("arbitrary", "arbitrary"), vmem_limit_bytes=VMEM_LIMIT),
    )(nq, nk_t, nv_cb, zmain, nk_m, nv_m, rpb_pad, meta_bias)


def _out_kernel(x_ref, os_ref, gg_ref, na_ref, w_ref, o_ref):
    d = o_ref.shape[1]
    na = jnp.concatenate([na_ref[p] for p in range(na_ref.shape[0])], axis=1)

    def na_tile(c):
        cols = slice(c * OUT_TN, (c + 1) * OUT_TN)
        o_ref[:, cols] = x_ref[:, cols] + _dot(na, w_ref[GLA_WIDTH:, cols])

    def gla_head(h):
        cols = slice(h * GLA_DV, (h + 1) * GLA_DV)
        sh = os_ref[:, cols].astype(F32)
        ms = jnp.mean(sh * sh, axis=-1, keepdims=True)
        return (sh * lax.rsqrt(ms + RMS_EPS) * _silu_tanh(gg_ref[:, cols].astype(F32))).astype(BF16)

    heads = []
    for i in range(max(d // OUT_TN, GLA_HEADS)):
        if i < d // OUT_TN:
            na_tile(i)
        if i < GLA_HEADS:
            heads.append(gla_head(i))
    o_gla = jnp.concatenate(heads, axis=1)
    for c in range(d // OUT_TN):
        cols = slice(c * OUT_TN, (c + 1) * OUT_TN)
        o_ref[:, cols] += _dot(o_gla, w_ref[:GLA_WIDTH, cols])


def _out_call(x2d, o_sum, zmain, o_na, w_out, *, tm):
    m, d = x2d.shape
    assert m % tm == 0 and d % OUT_TN == 0
    return pl.pallas_call(
        _out_kernel,
        name="out_proj",
        grid=(m // tm,),
        in_specs=[
            pl.BlockSpec((tm, d), lambda i: (i, 0)),
            pl.BlockSpec((tm, GLA_WIDTH), lambda i: (i, 0)),
            pl.BlockSpec((tm, GLA_WIDTH), lambda i: (i, COL_GG // GLA_WIDTH)),
            pl.BlockSpec((NA_WIDTH // LANES, tm, LANES), lambda i: (0, i, 0)),
            pl.BlockSpec((GLA_WIDTH + NA_WIDTH, d), lambda i: (0, 0), pipeline_mode=pl.Buffered(1)),
        ],
        out_specs=pl.BlockSpec((tm, d), lambda i: (i, 0)),
        out_shape=jax.ShapeDtypeStruct((m, d), F32),
        compiler_params=pltpu.CompilerParams(
            dimension_semantics=("parallel",), vmem_limit_bytes=VMEM_LIMIT),
    )(x2d, o_sum, zmain, o_na, w_out)


def _pick(n, cands):
    for c in cands:
        if n % c == 0:
            return c
    raise ValueError(f"no tile for {n}")


def kernel(x, meta_tokens, norm_g, w_in, w_decay_fwd, b_decay_fwd, w_decay_bwd, b_decay_bwd,
           gla_out_norm_g, q_norm_g, k_norm_g, rpb, meta_bias, w_out):
    batch, seq, d = x.shape
    assert d == D_MODEL and seq % TILE == 0 and seq % GRID_W == 0
    depth = norm_g.shape[0]
    assert depth == 1
    l = 0
    x2d = x.reshape(batch * seq, d)
    g = norm_g[l].reshape(1, d)

    assert w_in.shape[1:] == (d, D_IN_PROJ)
    wt = jnp.swapaxes(w_in[l], 0, 1).astype(BF16)
    gain_qk = jnp.concatenate([jnp.tile(q_norm_g[l], NA_HEADS) * (NA_DH ** -0.5 * LOG2_E),
                               jnp.tile(k_norm_g[l], NA_HEADS)]).reshape(1, 2 * NA_WIDTH)

    zr = jnp.zeros((GLA_RANK, GLA_KEY_WIDTH), F32)
    ztail = jnp.zeros((LANES - 2 * GLA_RANK, GLA_KEY_WIDTH), F32)
    wd = jnp.stack([jnp.concatenate([w_decay_fwd[l], zr, ztail], axis=0),
                    jnp.concatenate([zr, w_decay_bwd[l], ztail], axis=0)]).astype(BF16)
    bias = jnp.stack([b_decay_fwd[l], b_decay_bwd[l]]).reshape(2, 1, GLA_KEY_WIDTH)
    tg, tg_m = _gla_constants()

    assert seq % (NA_WIN_H * GRID_W) == 0
    kgain_col = jnp.tile(k_norm_g[l], NA_HEADS).reshape(NA_WIDTH, 1)
    zmain, nq, nk_t, nv_cb, gtot = _proj_call(x2d, g, wt, gain_qk, kgain_col, wd, bias, tg,
                                              tm=NA_WIN_H * GRID_W, tokens=True, name="proj_tokens")
    zmain_m, _, nk_m, nv_m, _ = _proj_call(meta_tokens, g, wt, gain_qk, kgain_col, wd, bias, tg,
                                           tm=N_META, tokens=False, name="proj_meta")
    zmain_m_pad = jnp.pad(zmain_m, ((0, TILE - N_META), (0, 0)))

    o_sum = _gla_call(zmain, gtot, zmain_m_pad, wd, bias, tg_m, batch=batch, seq=seq)

    rpb_pad = jnp.pad(rpb[l] * LOG2_E, ((0, 0), (0, 0), (0, 2 * NA_DH - rpb.shape[-1])))
    mb = (meta_bias[l] * LOG2_E).reshape(NA_HEADS, 1, N_META)
    o_na = _na_call(nq, nk_t, nv_cb, zmain, nk_m, nv_m, rpb_pad, mb, batch=batch, seq=seq,
                    rows_per_step=_pick(batch * seq // GRID_W, (256, 128, 64, 32, 16)))

    gn_rows = jnp.concatenate([jnp.tile(gla_out_norm_g[l], GLA_HEADS), jnp.ones((NA_WIDTH,), F32)])
    w_o = (w_out[l] * gn_rows[:, None]).astype(BF16)
    out = _out_call(x2d, o_sum, zmain, o_na, w_o, tm=_pick(batch * seq, (1024, 512, 256, 128)))
    return out.reshape(batch, seq, d)
```

```python
import functools

import jax
import jax.numpy as jnp
import numpy as np
from jax import lax
from jax.experimental import pallas as pl
from jax.experimental.pallas import tpu as pltpu

F32 = jnp.float32
BF16 = jnp.bfloat16

D_MODEL = 1024
N_META = 16
GRID_W = 64
GLA_HEADS = 4
GLA_DK = 128
GLA_DV = 256
GLA_KEY_WIDTH = GLA_HEADS * GLA_DK
GLA_WIDTH = GLA_HEADS * GLA_DV
GLA_RANK = 16
GLA_TAU = 16.0
GLA_CHUNK = 64
NA_HEADS = 16
NA_DH = 64
NA_WIDTH = NA_HEADS * NA_DH
NA_WIN_H = 8
NA_WIN_W = 16
RMS_EPS = 1e-6
NEG_BIG = -1e30
LOG2_E = 1.4426950408889634
LANES = 128

COL_GV, COL_GG, COL_NG, COL_GQ, COL_GK, COL_R = 0, 1024, 2048, 3072, 3584, 4096
MAIN_W = COL_R + LANES
COL_DIR0, DIR_W = 3072, 3 * 512
MAIN_TOK_W = COL_DIR0 + 2 * DIR_W
NA_CBW = 16
NA_CB = GRID_W // NA_CBW
IN_GQ, IN_GK, IN_GV, IN_R, IN_GG, IN_NQ, IN_NK, IN_NV, IN_NG = 0, 512, 1024, 2048, 2080, 3104, 4128, 5152, 6176
D_IN_PROJ = 7200
PROJ_TN = 512
OUT_TN = 256

TILE = 2 * GLA_CHUNK
NA_ROW_GROUP = 16
VMEM_LIMIT = 58 * 1024 * 1024


def _split_bf16(x):
    hi = x.astype(BF16)
    lo = (x - hi.astype(F32)).astype(BF16)
    return hi, lo


def _dot(a, b):
    return jnp.dot(a, b, preferred_element_type=F32)


def _dot_nt(a, b):
    return lax.dot_general(a, b, (((1,), (1,)), ((), ())), preferred_element_type=F32)


def _dot_tn(a, b):
    return lax.dot_general(a, b, (((0,), (0,)), ((), ())), preferred_element_type=F32)


def _log_sigmoid(x):
    return jnp.minimum(x, 0.0) - jnp.log(1.0 + jnp.exp(-jnp.abs(x)))


def _silu(x):
    h = 0.5 * x
    return h + h * jnp.tanh(h)


SHARED_SEGMENTS = ((COL_GV, IN_GV, 1024), (COL_GG, IN_GG, 1024), (COL_NG, IN_NG, 1024))
META_SEGMENTS = ((COL_GQ, IN_GQ, 512), (COL_GK, IN_GK, 512),
                 (COL_R, IN_R, LANES))


def _store_rows(o_ref, c0, y):
    n = y.shape[1]
    for rl in range(y.shape[0] // GRID_W):
        for cb in range(NA_CB):
            src = rl * GRID_W + cb * NA_CBW
            for c in range(0, n, LANES):
                o_ref[(c0 + c) // LANES, cb, rl * NA_CBW:(rl + 1) * NA_CBW, :] = y[src:src + NA_CBW, c:c + LANES]


def _gla_log_decay(r, wd_ref, bdec_ref, d):
    return (_log_sigmoid(_dot(r, wd_ref[d]) + bdec_ref[d]) * (1.0 / GLA_TAU)).astype(BF16)


def _gla_operands_store(q, k, log_decay, d, t, tg_ref, zm_ref, gtot_ref):
    rows = slice(t * TILE, (t + 1) * TILE)
    g = _dot(tg_ref[d], log_decay[rows])
    col = COL_DIR0 + d * DIR_W
    tots = []
    for c in range(TILE // GLA_CHUNK):
        last = c * GLA_CHUNK + (GLA_CHUNK - 1 if d == 0 else 0)
        tot = g[last:last + 1]
        gtot_ref[(t * TILE) // GLA_CHUNK + c:(t * TILE) // GLA_CHUNK + c + 1,
                 d * GLA_KEY_WIDTH:(d + 1) * GLA_KEY_WIDTH] = tot
        tots.append(jnp.broadcast_to(tot, (GLA_CHUNK, GLA_KEY_WIDTH)))
    dd = jnp.concatenate(tots, axis=0) - g
    zm_ref[rows, col:col + 512] = (q[rows] * jnp.exp(g)).astype(BF16)
    zm_ref[rows, col + 512:col + 1024] = (k[rows] * jnp.exp(-g)).astype(BF16)
    zm_ref[rows, col + 1024:col + 1536] = (k[rows] * jnp.exp(dd)).astype(BF16)


def _proj_kernel(x0_ref, xn_ref, xm_ref, g_ref, wt_ref, gain_ref, kgain_ref, wd_ref, bdec_ref, tg_ref,
                 zm_ref, nq_ref, nk_ref, nv_ref, gtot_ref, zmm_ref, nkm_ref, nvm_ref, u_scr):
    def normed(x_ref):
        x = x_ref[...]
        ms = jnp.mean(x * x, axis=-1, keepdims=True)
        return (x * lax.rsqrt(ms + RMS_EPS) * g_ref[...]).astype(BF16)

    def head_norm_rows(acc, goff):
        first = lax.broadcasted_iota(jnp.int32, (acc.shape[0], LANES), 1) < NA_DH
        cols = []
        for c in range(0, acc.shape[1], LANES):
            a = acc[:, c:c + LANES]
            sq = a * a
            lo = jnp.sum(jnp.where(first, sq, 0.0), axis=-1, keepdims=True)
            hi = jnp.sum(jnp.where(first, 0.0, sq), axis=-1, keepdims=True)
            inv = jnp.where(first, lax.rsqrt(lo * (1.0 / NA_DH) + RMS_EPS), lax.rsqrt(hi * (1.0 / NA_DH) + RMS_EPS))
            cols.append(a * inv)
        return (jnp.concatenate(cols, axis=1) * gain_ref[:, goff:goff + PROJ_TN]).astype(BF16)

    def meta_tokens():
        um = normed(xm_ref)
        zmm_ref[...] = jnp.zeros_like(zmm_ref)
        for off in range(0, NA_WIDTH, PROJ_TN):
            acc = _dot_nt(um, wt_ref[IN_NK + off:IN_NK + off + PROJ_TN, :])
            nkm_ref[:, off:off + PROJ_TN] = head_norm_rows(acc, NA_WIDTH + off)
            acc = _dot_nt(um, wt_ref[IN_NV + off:IN_NV + off + PROJ_TN, :])
            nvm_ref[:, off:off + PROJ_TN] = acc.astype(nvm_ref.dtype)
        for dst, src, width in SHARED_SEGMENTS + META_SEGMENTS:
            acc = _dot_nt(um, wt_ref[src:src + width, :])
            if dst == COL_GQ:
                acc = acc * (GLA_DK ** -0.5)
            zmm_ref[:N_META, dst:dst + width] = acc.astype(zmm_ref.dtype)

    step = pl.program_id(0)

    @pl.when(step == 0)
    def _():
        u_scr[0] = normed(x0_ref)
        meta_tokens()

    slot = step % 2
    u = u_scr[slot]
    tm = u.shape[0]

    def prepare_next():
        u_scr[1 - slot] = normed(xn_ref)

    def main_tile(dst, src, off, n):
        acc = _dot_nt(u, wt_ref[src + off:src + off + n, :])
        zm_ref[:, dst + off:dst + off + n] = acc.astype(zm_ref.dtype)

    def nv_tile(off):
        acc = _dot_nt(u, wt_ref[IN_NV + off:IN_NV + off + PROJ_TN, :])
        _store_rows(nv_ref, off, acc.astype(nv_ref.dtype))

    def nq_tile(off):
        acc = _dot_nt(u, wt_ref[IN_NQ + off:IN_NQ + off + PROJ_TN, :])
        y = head_norm_rows(acc, off)
        for c in range(0, PROJ_TN, LANES):
            nq_ref[(off + c) // LANES] = y[:, c:c + LANES]

    def nk_tile(off):
        u_cb = jnp.concatenate([u[rl * GRID_W + cb * NA_CBW:rl * GRID_W + (cb + 1) * NA_CBW]
                                for cb in range(NA_CB) for rl in range(tm // GRID_W)], axis=0)
        acc_t = _dot_nt(wt_ref[IN_NK + off:IN_NK + off + PROJ_TN, :], u_cb)
        for h in range(PROJ_TN // NA_DH):
            rows = slice(off + h * NA_DH, off + (h + 1) * NA_DH)
            blk = acc_t[h * NA_DH:(h + 1) * NA_DH]
            ms = jnp.sum(blk * blk, axis=0, keepdims=True) * (1.0 / NA_DH)
            y = (blk * lax.rsqrt(ms + RMS_EPS) * kgain_ref[rows, :]).astype(BF16)
            for cb in range(NA_CB):
                nk_ref[cb, rows, :] = y[:, cb * LANES:(cb + 1) * LANES]

    tiles = [functools.partial(nk_tile, off) for off in range(0, NA_WIDTH, PROJ_TN)]
    tiles += [functools.partial(nq_tile, off) for off in range(0, NA_WIDTH, PROJ_TN)]
    tiles += [functools.partial(nv_tile, off) for off in range(0, NA_WIDTH, PROJ_TN)]
    for dst, src, width in SHARED_SEGMENTS:
        for off in range(0, width, PROJ_TN):
            tiles.append(functools.partial(main_tile, dst, src, off, min(PROJ_TN, width - off)))

    r = _dot_nt(u, wt_ref[IN_R:IN_R + LANES, :]).astype(BF16)
    q = _dot_nt(u, wt_ref[IN_GQ:IN_GQ + GLA_KEY_WIDTH, :]) * (GLA_DK ** -0.5)
    k = _dot_nt(u, wt_ref[IN_GK:IN_GK + GLA_KEY_WIDTH, :])
    log_decay = [None, None]

    def decay_stage(d):
        log_decay[d] = _gla_log_decay(r, wd_ref, bdec_ref, d)

    stages = [functools.partial(decay_stage, 0), functools.partial(decay_stage, 1), None]
    stages += [functools.partial(lambda d, t: _gla_operands_store(q, k, log_decay[d], d, t, tg_ref, zm_ref, gtot_ref),
                                 d, t)
               for d in range(2) for t in range(tm // TILE)]
    for i, tile in enumerate(tiles):
        if i < len(stages) and stages[i] is not None:
            stages[i]()
        tile()
    prepare_next()
    assert len(stages) <= len(tiles)


def _proj_call(x2d, xm, g, wt, gain_qk, kgain_col, wd, bdec, tg, *, tm):
    m, d = x2d.shape
    assert m % tm == 0 and wt.shape == (D_IN_PROJ, d) and xm.shape == (N_META, d)
    assert tm == NA_WIN_H * GRID_W and NA_CB * LANES == tm
    resident = lambda shape: pl.BlockSpec(shape, lambda i: (0,) * len(shape), pipeline_mode=pl.Buffered(1))
    whole = lambda shape: pl.BlockSpec(shape, lambda i: (0,) * len(shape))
    n_pairs = NA_WIDTH // LANES
    q_spec = pl.BlockSpec((n_pairs, tm, LANES), lambda i: (0, i, 0))
    q_shape = jax.ShapeDtypeStruct((n_pairs, m, LANES), BF16)
    v_spec = pl.BlockSpec((n_pairs, NA_CB, tm // NA_CB, LANES), lambda i: (0, 0, i, 0))
    v_shape = jax.ShapeDtypeStruct((n_pairs, NA_CB, m // NA_CB, LANES), BF16)
    k_spec = pl.BlockSpec((NA_CB, NA_WIDTH, LANES), lambda i: (i, 0, 0))
    k_shape = jax.ShapeDtypeStruct((m // LANES, NA_WIDTH, LANES), BF16)
    n_chunk_rows = tm // GLA_CHUNK
    meta_plain = jax.ShapeDtypeStruct((N_META, NA_WIDTH), BF16)
    return pl.pallas_call(
        _proj_kernel,
        name="proj_tokens",
        grid=(m // tm,),
        in_specs=[
            resident((tm, d)),
            pl.BlockSpec((tm, d), lambda i: (jnp.minimum(i + 1, m // tm - 1), 0)),
            resident((N_META, d)),
            resident((1, d)),
            resident((D_IN_PROJ, d)),
            resident((1, 2 * NA_WIDTH)),
            resident((NA_WIDTH, 1)),
            resident((2, LANES, GLA_KEY_WIDTH)),
            resident((2, 1, GLA_KEY_WIDTH)),
            resident((2, TILE, TILE)),
        ],
        out_specs=[pl.BlockSpec((tm, MAIN_TOK_W), lambda i: (i, 0)), q_spec, k_spec, v_spec,
                   pl.BlockSpec((n_chunk_rows, 2 * GLA_KEY_WIDTH), lambda i: (i, 0)),
                   whole((TILE, MAIN_W)), whole((N_META, NA_WIDTH)), whole((N_META, NA_WIDTH))],
        out_shape=[jax.ShapeDtypeStruct((m, MAIN_TOK_W), BF16), q_shape, k_shape, v_shape,
                   jax.ShapeDtypeStruct((m // GLA_CHUNK, 2 * GLA_KEY_WIDTH), F32),
                   jax.ShapeDtypeStruct((TILE, MAIN_W), BF16), meta_plain, meta_plain],
        scratch_shapes=[pltpu.VMEM((2, tm, d), BF16)],
        compiler_params=pltpu.CompilerParams(
            dimension_semantics=("arbitrary",), vmem_limit_bytes=VMEM_LIMIT),
    )(x2d, x2d, xm, g, wt, gain_qk, kgain_col, wd, bdec, tg)


def _gla_kernel(qf_ref, kif_ref, ksf_ref, vf_ref, gtf_ref, qb_ref, kib_ref, ksb_ref, vb_ref, gtb_ref,
                km_ref, vm_ref, rm_ref, wd_ref, b_ref, tgm_ref,
                o_ref, sf_ref, sb_ref, *, n_tiles):
    row = lax.broadcasted_iota(jnp.int32, (TILE, TILE), 0)
    col = lax.broadcasted_iota(jnp.int32, (TILE, TILE), 1)
    same = (row // GLA_CHUNK) == (col // GLA_CHUNK)
    masks = (same & (row >= col), same & (row <= col))

    @pl.when(pl.program_id(2) == 0)
    def _():
        rowm = lax.broadcasted_iota(jnp.int32, (TILE, GLA_DK), 0)
        lg = _log_sigmoid(_dot(rm_ref[...], wd_ref[...]) + b_ref[...]) * (1.0 / GLA_TAU)
        hi, lo = _split_bf16(jnp.where(rowm < N_META, lg, 0.0))
        d_m = _dot(tgm_ref[...], jnp.concatenate([hi, lo], axis=0))
        k_st = (km_ref[...].astype(F32) * jnp.exp(d_m)).astype(BF16)
        sf_ref[...] = _dot_tn(k_st, vm_ref[...])
        sb_ref[...] = jnp.zeros_like(sb_ref)

    def scan(step):
        items = []
        for t in range(n_tiles):
            items.append((0, t))
            items.append((1, n_tiles - 1 - t))
        refs = ((qf_ref, kif_ref, ksf_ref, vf_ref), (qb_ref, kib_ref, ksb_ref, vb_ref))
        out_half = (step, 1 - step)
        a_cols = (jnp.exp(jnp.transpose(gtf_ref[...])), jnp.exp(jnp.transpose(gtb_ref[...])))

        def rows_of(t):
            return slice(t * TILE, (t + 1) * TILE)

        amats = [jnp.where(masks[d], _dot_nt(refs[d][0][rows_of(t), :], refs[d][1][rows_of(t), :]), 0.0).astype(BF16)
                 for d, t in items]

        def state_independent(idx):
            d, t = items[idx]
            v = refs[d][3][rows_of(t), :]
            k_st = refs[d][2][rows_of(t), :]
            u = [_dot_tn(k_st[h * GLA_CHUNK:(h + 1) * GLA_CHUNK], v[h * GLA_CHUNK:(h + 1) * GLA_CHUNK])
                 for h in range(2)]
            return _dot(amats[idx], v), u

        state = [sf_ref[...], sb_ref[...]]
        ahead = state_independent(0)
        for idx, (d, t) in enumerate(items):
            o_intra, u = ahead
            if idx + 1 < len(items):
                ahead = state_independent(idx + 1)
            s = state[d]
            outs = [None, None]
            for h in ((0, 1) if d == 0 else (1, 0)):
                rows = slice(t * TILE + h * GLA_CHUNK, t * TILE + (h + 1) * GLA_CHUNK)
                outs[h] = o_intra[h * GLA_CHUNK:(h + 1) * GLA_CHUNK] + _dot(refs[d][0][rows, :], s.astype(BF16))
                chunk = 2 * t + h
                s = a_cols[d][:, chunk:chunk + 1] * s + u[h]
            state[d] = s
            o = jnp.concatenate(outs, axis=0)
            out_rows = rows_of(out_half[d] * n_tiles + t)
            if step == 1:
                o = o + o_ref[out_rows, :].astype(F32)
            o_ref[out_rows, :] = o.astype(o_ref.dtype)
        sf_ref[...] = state[0]
        sb_ref[...] = state[1]

    for step in range(2):
        pl.when(pl.program_id(2) == step)(functools.partial(scan, step))


def _gla_constants():
    i = np.arange(TILE)
    same = (i[:, None] // GLA_CHUNK) == (i[None, :] // GLA_CHUNK)
    inc_f = same & (i[None, :] <= i[:, None])
    inc_b = same & (i[None, :] >= i[:, None])
    to_end_meta = i[None, :] > i[:, None]
    dup = lambda m: np.concatenate([m, m], axis=1)
    tg = jnp.asarray(np.stack([inc_f, inc_b]), BF16)
    return tg, jnp.asarray(dup(to_end_meta), BF16)


def _gla_call(zmain, gtot, zmain_m, wd, bias, tg_m, *, batch, seq):
    nb = 2
    assert seq % (nb * TILE) == 0
    t_blk = seq // nb
    n_tiles = t_blk // TILE
    n_chunks = t_blk // GLA_CHUNK

    def tok(fwd):
        if fwd:
            return lambda b, hh, i: b * nb + i
        return lambda b, hh, i: b * nb + (nb - 1 - i)

    def dir_specs(d):
        t = tok(d == 0)
        c0 = (COL_DIR0 + d * DIR_W) // GLA_DK
        return [
            pl.BlockSpec((t_blk, GLA_DK), lambda b, hh, i: (t(b, hh, i), c0 + hh)),
            pl.BlockSpec((t_blk, GLA_DK), lambda b, hh, i: (t(b, hh, i), c0 + GLA_HEADS + hh)),
            pl.BlockSpec((t_blk, GLA_DK), lambda b, hh, i: (t(b, hh, i), c0 + 2 * GLA_HEADS + hh)),
            pl.BlockSpec((t_blk, GLA_DV), lambda b, hh, i: (t(b, hh, i), COL_GV // GLA_DV + hh)),
            pl.BlockSpec((n_chunks, GLA_DK), lambda b, hh, i: (t(b, hh, i), d * GLA_HEADS + hh)),
        ]

    in_specs = (
        dir_specs(0) + dir_specs(1)
        + [pl.BlockSpec((TILE, GLA_DK), lambda b, hh, i: (0, COL_GK // GLA_DK + hh)),
           pl.BlockSpec((TILE, GLA_DV), lambda b, hh, i: (0, COL_GV // GLA_DV + hh)),
           pl.BlockSpec((TILE, LANES), lambda b, hh, i: (0, COL_R // LANES)),
           pl.BlockSpec((None, LANES, GLA_DK), lambda b, hh, i: (0, 0, hh)),
           pl.BlockSpec((None, 1, GLA_DK), lambda b, hh, i: (0, 0, hh)),
           pl.BlockSpec((TILE, 2 * TILE), lambda b, hh, i: (0, 0))]
    )
    args = [zmain] * 4 + [gtot] + [zmain] * 4 + [gtot] + [zmain_m] * 3 + [wd, bias, tg_m]
    return pl.pallas_call(
        functools.partial(_gla_kernel, n_tiles=n_tiles),
        name="gla_scan",
        grid=(batch, GLA_HEADS, nb),
        in_specs=in_specs,
        out_specs=pl.BlockSpec((seq, GLA_DV), lambda b, hh, i: (b, hh)),
        out_shape=jax.ShapeDtypeStruct((batch * seq, GLA_WIDTH), BF16),
        scratch_shapes=[pltpu.VMEM((GLA_DK, GLA_DV), F32), pltpu.VMEM((GLA_DK, GLA_DV), F32)],
        compiler_params=pltpu.CompilerParams(
            dimension_semantics=("parallel", "parallel", "arbitrary"), vmem_limit_bytes=VMEM_LIMIT),
    )(*args)


NA_SEG_COLS = ((0, 24), (24, 40), (40, 64))
NA_SEG_ROW0 = (0, 48, 80)
NA_STACK = 2 * GRID_W


NA_BASE_MID = NA_WIN_H // 2 - 1
NA_VARIANTS = 2 * NA_WIN_H - 1


def _na_variant(base, a):
    return jnp.where(base == NA_BASE_MID, a, jnp.where(base < NA_BASE_MID, NA_WIN_H + base, NA_WIN_H - 1 + base))


def _na_build_bias(rpb_ref, tab_ref):
    def build(v, carry):
        a = jnp.where(v < NA_WIN_H, v, 0)
        base = jnp.where(v < NA_WIN_H, NA_BASE_MID,
                         jnp.where(v < NA_WIN_H + NA_BASE_MID, v - NA_WIN_H, v - (NA_WIN_H - 1)))
        for hh in range(2):
            bias_rows = [rpb_ref[hh, pl.ds(base + ((t - a + NA_WIN_H) & (NA_WIN_H - 1)), 1), :]
                         for t in range(NA_WIN_H)]
            for s, (c0, c1) in enumerate(NA_SEG_COLS):
                n = c1 - c0
                vrows = [jnp.broadcast_to(br, (n, LANES)) for br in bias_rows]
                row0 = NA_SEG_ROW0[s] + hh * n
                rho = lax.broadcasted_iota(jnp.int32, (n, LANES), 0)
                lane = lax.broadcasted_iota(jnp.int32, (n, LANES), 1)
                cs = jnp.clip(c0 + rho - NA_WIN_W // 2, 0, GRID_W - NA_WIN_W)
                for half in range(2):
                    cb = s + half
                    cp = NA_CBW * cb + lane % NA_CBW
                    valid = (cp >= cs) & (cp < cs + NA_WIN_W)
                    acc = None
                    for i in range(NA_WIN_H):
                        shift = (NA_CBW * i - NA_CBW * cb + c0 - (NA_WIN_W - 1)) % LANES
                        rolled = pltpu.roll(vrows[i], shift, 1, stride=1, stride_axis=0)
                        acc = rolled if acc is None else jnp.where(lane // NA_CBW == i, rolled, acc)
                    tab_ref[v, row0:row0 + n, half * LANES:(half + 1) * LANES] = jnp.where(valid, acc, NEG_BIG)
        return carry

    lax.fori_loop(0, NA_VARIANTS, build, 0)


def _na_kernel(q_ref, kt_ref, v_ref, gate_ref, km_ref, vm_ref, rpb_ref, mb_ref, o_ref, tab_ref, sc_ref, sm_ref,
               *, rows_per_step, n_rows):
    w = GRID_W

    @pl.when(pl.program_id(1) == 0)
    def _():
        _na_build_bias(rpb_ref, tab_ref)

    head0_q = lax.broadcasted_iota(jnp.int32, (w, LANES), 1) < NA_DH
    srow = lax.broadcasted_iota(jnp.int32, (NA_STACK, N_META), 0)
    is_h1 = jnp.zeros((NA_STACK, N_META), jnp.bool_)
    for s, (c0, c1) in enumerate(NA_SEG_COLS):
        lo = NA_SEG_ROW0[s] + (c1 - c0)
        is_h1 = is_h1 | ((srow >= lo) & (srow < lo + (c1 - c0)))
    mb = jnp.where(is_h1, mb_ref[1], mb_ref[0])
    km = km_ref[...]
    vm = vm_ref[...]
    seg_rows = [slice(NA_SEG_ROW0[s], NA_SEG_ROW0[s] + 2 * (c1 - c0)) for s, (c0, c1) in enumerate(NA_SEG_COLS)]
    blk_rows = [slice(seg_rows[max(cb - 1, 0)].start, seg_rows[min(cb, len(seg_rows) - 1)].stop)
                for cb in range(NA_CB)]

    n_tiles = n_rows // NA_WIN_H
    key_lane = lax.broadcasted_iota(jnp.int32, (LANES, LANES), 1)

    def window(rl):
        row = pl.program_id(1) * rows_per_step + rl
        b = row // n_rows
        r = row - b * n_rows
        rs = jnp.clip(r - NA_WIN_H // 2, 0, n_rows - NA_WIN_H)
        a = rs & (NA_WIN_H - 1)
        return (b * n_tiles + rs // NA_WIN_H, a, (b + 1) * n_tiles - 1,
                _na_variant(rs - r + (NA_WIN_H - 1), a))

    def scores_store(slot, j, rl):
        m, a, m_last, var = window(rl)
        q = q_ref[pl.ds(pl.multiple_of(rl * w, w), w), :].astype(F32)
        q0 = jnp.where(head0_q, q, 0.0)
        q1 = q - q0
        qs = jnp.concatenate([piece[c0:c1] for c0, c1 in NA_SEG_COLS for piece in (q0, q1)],
                             axis=0).astype(BF16)
        from_m = key_lane >= a * NA_CBW
        m1 = jnp.minimum(m + 1, m_last)
        blocks = [_dot(qs[blk_rows[cb]],
                       jnp.where(from_m, kt_ref[NA_CB * m + cb], kt_ref[NA_CB * m1 + cb]))
                  for cb in range(NA_CB)]
        for s in range(len(NA_SEG_COLS)):
            n2 = seg_rows[s].stop - seg_rows[s].start
            for half in range(2):
                cb = s + half
                a0 = seg_rows[s].start - blk_rows[cb].start
                lanes = slice(half * LANES, (half + 1) * LANES)
                sc_ref[slot, j, seg_rows[s], lanes] = blocks[cb][a0:a0 + n2] + tab_ref[var, seg_rows[s], lanes]
        sm_ref[slot, j] = _dot_nt(qs, km) + mb

    def value_block(cb, m, a):
        groups = []
        for t in range(NA_WIN_H):
            row = NA_WIN_H * m + t + jnp.where(t < a, NA_WIN_H, 0)
            groups.append(v_ref[cb, pl.ds(pl.multiple_of(row * NA_CBW, NA_CBW), NA_CBW), :])
        return jnp.concatenate(groups, axis=0)

    def softmax_values(slot, j, rl):
        m_tile, a, _, _ = window(rl)
        sc = sc_ref[slot, j]
        sm = sm_ref[slot, j]
        m = jnp.maximum(sc[:, :LANES], sc[:, LANES:])
        m = jnp.maximum(jnp.max(m, axis=-1, keepdims=True), jnp.max(sm, axis=-1, keepdims=True))
        p = jnp.exp2(sc - m)
        pm = jnp.exp2(sm - m)
        den = (jnp.sum(p[:, :LANES] + p[:, LANES:], axis=-1, keepdims=True)
               + jnp.sum(pm, axis=-1, keepdims=True))
        p = p.astype(BF16)
        vals = [value_block(cb, m_tile, a) for cb in range(NA_CB)]
        o_meta = _dot(pm.astype(BF16), vm)
        inv = 1.0 / den
        outs = []
        for s, (c0, c1) in enumerate(NA_SEG_COLS):
            o_loc = _dot(p[seg_rows[s]], jnp.concatenate([vals[s], vals[s + 1]], axis=0))
            o = (o_loc + o_meta[seg_rows[s]]) * inv[seg_rows[s]]
            n = c1 - c0
            head0 = lax.broadcasted_iota(jnp.int32, (n, LANES), 1) < NA_DH
            outs.append(jnp.where(head0, o[:n], o[n:]))
        o = jnp.concatenate(outs, axis=0)
        qoff = pl.multiple_of(rl * w, w)
        gate = gate_ref[pl.ds(qoff, w), :].astype(F32)
        o_ref[pl.ds(qoff, w), :] = (o * _silu(gate)).astype(o_ref.dtype)

    n_groups = rows_per_step // NA_ROW_GROUP
    for j in range(NA_ROW_GROUP):
        scores_store(0, j, j)

    def trip(g, carry):
        slot = g % 2
        for j in range(NA_ROW_GROUP):
            scores_store(slot, j, g * NA_ROW_GROUP + j)
        for j in range(NA_ROW_GROUP):
            softmax_values(1 - slot, j, (g - 1) * NA_ROW_GROUP + j)
        return carry

    lax.fori_loop(1, n_groups, trip, 0)
    for j in range(NA_ROW_GROUP):
        softmax_values((n_groups - 1) % 2, j, (n_groups - 1) * NA_ROW_GROUP + j)


def _na_call(nq, nk_t, nv_cb, zmain, nk_m, nv_m, rpb_pad, meta_bias, *, batch, seq, rows_per_step):
    n_rows = seq // GRID_W
    total_rows = batch * n_rows
    assert n_rows % NA_WIN_H == 0 and total_rows % rows_per_step == 0 and rows_per_step % NA_ROW_GROUP == 0
    nsteps = total_rows // rows_per_step
    hp = NA_HEADS // 2
    tq = rows_per_step * GRID_W
    n_idx = 2 * NA_WIN_H - 1
    in_specs = [
        pl.BlockSpec((None, tq, LANES), lambda p, i: (p, i, 0)),
        pl.BlockSpec((batch * seq // LANES, LANES, LANES), lambda p, i: (0, p, 0)),
        pl.BlockSpec((None, NA_CB, batch * seq // NA_CB, LANES), lambda p, i: (p, 0, 0, 0)),
        pl.BlockSpec((tq, LANES), lambda p, i: (i, COL_NG // LANES + p)),
        pl.BlockSpec((N_META, LANES), lambda p, i: (0, p)),
        pl.BlockSpec((N_META, LANES), lambda p, i: (0, p)),
        pl.BlockSpec((2, n_idx, LANES), lambda p, i: (p, 0, 0)),
        pl.BlockSpec((2, 1, N_META), lambda p, i: (p, 0, 0)),
    ]
    return pl.pallas_call(
        functools.partial(_na_kernel, rows_per_step=rows_per_step, n_rows=n_rows),
        name="na_attn",
        grid=(hp, nsteps),
        in_specs=in_specs,
        out_specs=pl.BlockSpec((None, tq, LANES), lambda p, i: (p, i, 0)),
        out_shape=jax.ShapeDtypeStruct((hp, batch * seq, LANES), BF16),
        scratch_shapes=[pltpu.VMEM((NA_VARIANTS, NA_STACK, 2 * LANES), F32),
                        pltpu.VMEM((2, NA_ROW_GROUP, NA_STACK, 2 * LANES), F32),
                        pltpu.VMEM((2, NA_ROW_GROUP, NA_STACK, N_META), F32)],
        compiler_params=pltpu.CompilerParams(
            dimension_semantics=("arbitrary", "arbitrary"), vmem_limit_bytes=VMEM_LIMIT),
    )(nq, nk_t, nv_cb, zmain, nk_m, nv_m, rpb_pad, meta_bias)


def _out_kernel(x_ref, os_ref, gg_ref, na_ref, w_ref, o_ref):
    d = o_ref.shape[1]
    na = jnp.concatenate([na_ref[p] for p in range(na_ref.shape[0])], axis=1)

    def na_tile(c):
        cols = slice(c * OUT_TN, (c + 1) * OUT_TN)
        o_ref[:, cols] = x_ref[:, cols] + _dot(na, w_ref[GLA_WIDTH:, cols])

    def gla_head(h):
        cols = slice(h * GLA_DV, (h + 1) * GLA_DV)
        sh = os_ref[:, cols].astype(F32)
        ms = jnp.mean(sh * sh, axis=-1, keepdims=True)
        return (sh * lax.rsqrt(ms + RMS_EPS) * _silu(gg_ref[:, cols].astype(F32))).astype(BF16)

    heads = []
    for i in range(max(d // OUT_TN, GLA_HEADS)):
        if i < d // OUT_TN:
            na_tile(i)
        if i < GLA_HEADS:
            heads.append(gla_head(i))
    o_gla = jnp.concatenate(heads, axis=1)
    for c in range(d // OUT_TN):
        cols = slice(c * OUT_TN, (c + 1) * OUT_TN)
        o_ref[:, cols] += _dot(o_gla, w_ref[:GLA_WIDTH, cols])


def _out_call(x2d, o_sum, zmain, o_na, w_out, *, tm):
    m, d = x2d.shape
    assert m % tm == 0 and d % OUT_TN == 0
    return pl.pallas_call(
        _out_kernel,
        name="out_proj",
        grid=(m // tm,),
        in_specs=[
            pl.BlockSpec((tm, d), lambda i: (i, 0)),
            pl.BlockSpec((tm, GLA_WIDTH), lambda i: (i, 0)),
            pl.BlockSpec((tm, GLA_WIDTH), lambda i: (i, COL_GG // GLA_WIDTH)),
            pl.BlockSpec((NA_WIDTH // LANES, tm, LANES), lambda i: (0, i, 0)),
            pl.BlockSpec((GLA_WIDTH + NA_WIDTH, d), lambda i: (0, 0), pipeline_mode=pl.Buffered(1)),
        ],
        out_specs=pl.BlockSpec((tm, d), lambda i: (i, 0)),
        out_shape=jax.ShapeDtypeStruct((m, d), F32),
        compiler_params=pltpu.CompilerParams(
            dimension_semantics=("parallel",), vmem_limit_bytes=VMEM_LIMIT),
    )(x2d, o_sum, zmain, o_na, w_out)


def _pick(n, cands):
    for c in cands:
        if n % c == 0:
            return c
    raise ValueError(f"no tile for {n}")


def kernel(x, meta_tokens, norm_g, w_in, w_decay_fwd, b_decay_fwd, w_decay_bwd, b_decay_bwd,
           gla_out_norm_g, q_norm_g, k_norm_g, rpb, meta_bias, w_out):
    batch, seq, d = x.shape
    assert d == D_MODEL and seq % TILE == 0 and seq % GRID_W == 0
    depth = norm_g.shape[0]
    assert depth == 1
    l = 0
    x2d = x.reshape(batch * seq, d)
    g = norm_g[l].reshape(1, d)

    assert w_in.shape[1:] == (d, D_IN_PROJ)
    wt = jnp.swapaxes(w_in[l], 0, 1).astype(BF16)
    gain_qk = jnp.concatenate([jnp.tile(q_norm_g[l], NA_HEADS) * (NA_DH ** -0.5 * LOG2_E),
                               jnp.tile(k_norm_g[l], NA_HEADS)]).reshape(1, 2 * NA_WIDTH)

    zr = jnp.zeros((GLA_RANK, GLA_KEY_WIDTH), F32)
    ztail = jnp.zeros((LANES - 2 * GLA_RANK, GLA_KEY_WIDTH), F32)
    wd = jnp.stack([jnp.concatenate([w_decay_fwd[l], zr, ztail], axis=0),
                    jnp.concatenate([zr, w_decay_bwd[l], ztail], axis=0)]).astype(BF16)
    bias = jnp.stack([b_decay_fwd[l], b_decay_bwd[l]]).reshape(2, 1, GLA_KEY_WIDTH)
    tg, tg_m = _gla_constants()

    assert seq % (NA_WIN_H * GRID_W) == 0
    kgain_col = jnp.tile(k_norm_g[l], NA_HEADS).reshape(NA_WIDTH, 1)
    zmain, nq, nk_t, nv_cb, gtot, zmain_m_pad, nk_m, nv_m = _proj_call(
        x2d, meta_tokens, g, wt, gain_qk, kgain_col, wd, bias, tg, tm=NA_WIN_H * GRID_W)

    o_sum = _gla_call(zmain, gtot, zmain_m_pad, wd, bias, tg_m, batch=batch, seq=seq)

    rpb_pad = jnp.pad(rpb[l] * LOG2_E, ((0, 0), (0, 0), (0, 2 * NA_DH - rpb.shape[-1])))
    mb = (meta_bias[l] * LOG2_E).reshape(NA_HEADS, 1, N_META)
    o_na = _na_call(nq, nk_t, nv_cb, zmain, nk_m, nv_m, rpb_pad, mb, batch=batch, seq=seq,
                    rows_per_step=_pick(batch * seq // GRID_W, (256, 128, 64, 32, 16)))

    gn_rows = jnp.concatenate([jnp.tile(gla_out_norm_g[l], GLA_HEADS), jnp.ones((NA_WIDTH,), F32)])
    w_o = (w_out[l] * gn_rows[:, None]).astype(BF16)
    out = _out_call(x2d, o_sum, zmain, o_na, w_o, tm=_pick(batch * seq, (1024, 512, 256, 128)))
    return out.reshape(batch, seq, d)
```

```python
import functools

import jax
import jax.numpy as jnp
import numpy as np
from jax import lax
from jax.experimental import pallas as pl
from jax.experimental.pallas import tpu as pltpu

F32 = jnp.float32
BF16 = jnp.bfloat16

D_MODEL = 1024
N_META = 16
GRID_W = 64
GLA_HEADS = 4
GLA_DK = 128
GLA_DV = 256
GLA_KEY_WIDTH = GLA_HEADS * GLA_DK
GLA_WIDTH = GLA_HEADS * GLA_DV
GLA_RANK = 16
GLA_TAU = 16.0
GLA_CHUNK = 64
NA_HEADS = 16
NA_DH = 64
NA_WIDTH = NA_HEADS * NA_DH
NA_WIN_H = 8
NA_WIN_W = 16
RMS_EPS = 1e-6
NEG_BIG = -1e30
LOG2_E = 1.4426950408889634
LANES = 128

COL_GV, COL_GG, COL_NG, COL_GQ, COL_GK, COL_R = 0, 1024, 2048, 3072, 3584, 4096
MAIN_W = COL_R + LANES
MAIN_TOK_W = COL_GQ
GLA_Q_IN, GLA_K_IN, GLA_K_ST = 0, 1, 2
NA_CBW = 16
NA_CB = GRID_W // NA_CBW
IN_GQ, IN_GK, IN_GV, IN_R, IN_GG, IN_NQ, IN_NK, IN_NV, IN_NG = 0, 512, 1024, 2048, 2080, 3104, 4128, 5152, 6176
D_IN_PROJ = 7200
PROJ_TN = 512
OUT_TN = 256

TILE = 2 * GLA_CHUNK
NA_ROW_GROUP = 16
VMEM_LIMIT = 58 * 1024 * 1024


def _split_bf16(x):
    hi = x.astype(BF16)
    lo = (x - hi.astype(F32)).astype(BF16)
    return hi, lo


def _dot(a, b):
    return jnp.dot(a, b, preferred_element_type=F32)


def _dot_nt(a, b):
    return lax.dot_general(a, b, (((1,), (1,)), ((), ())), preferred_element_type=F32)


def _dot_tn(a, b):
    return lax.dot_general(a, b, (((0,), (0,)), ((), ())), preferred_element_type=F32)


def _log_sigmoid(x):
    return jnp.minimum(x, 0.0) - jnp.log(1.0 + jnp.exp(-jnp.abs(x)))


def _silu(x):
    h = 0.5 * x
    return h + h * jnp.tanh(h)


SHARED_SEGMENTS = ((COL_GV, IN_GV, 1024), (COL_GG, IN_GG, 1024), (COL_NG, IN_NG, 1024))
META_SEGMENTS = ((COL_GQ, IN_GQ, 512), (COL_GK, IN_GK, 512),
                 (COL_R, IN_R, LANES))


def _store_rows(o_ref, c0, y):
    n = y.shape[1]
    for rl in range(y.shape[0] // GRID_W):
        for cb in range(NA_CB):
            src = rl * GRID_W + cb * NA_CBW
            for c in range(0, n, LANES):
                o_ref[(c0 + c) // LANES, cb, rl * NA_CBW:(rl + 1) * NA_CBW, :] = y[src:src + NA_CBW, c:c + LANES]


def _gla_log_decay(r, wd_ref, bdec_ref, d):
    return (_log_sigmoid(_dot(r, wd_ref[d]) + bdec_ref[d]) * (1.0 / GLA_TAU)).astype(BF16)


def _gla_operands_store(q, k, log_decay, d, t, tg_ref, go_ref, gtot_ref):
    rows = slice(t * TILE, (t + 1) * TILE)
    g = _dot(tg_ref[d], log_decay[rows])
    tots = []
    for c in range(TILE // GLA_CHUNK):
        last = c * GLA_CHUNK + (GLA_CHUNK - 1 if d == 0 else 0)
        tot = g[last:last + 1]
        gtot_ref[(t * TILE) // GLA_CHUNK + c:(t * TILE) // GLA_CHUNK + c + 1,
                 d * GLA_KEY_WIDTH:(d + 1) * GLA_KEY_WIDTH] = tot
        tots.append(jnp.broadcast_to(tot, (GLA_CHUNK, GLA_KEY_WIDTH)))
    dd = jnp.concatenate(tots, axis=0) - g
    operands = ((GLA_Q_IN, q[rows] * jnp.exp(g)), (GLA_K_IN, k[rows] * jnp.exp(-g)), (GLA_K_ST, k[rows] * jnp.exp(dd)))
    for kind, val in operands:
        val = val.astype(go_ref.dtype)
        for h in range(GLA_HEADS):
            go_ref[d, kind, h, rows, :] = val[:, h * GLA_DK:(h + 1) * GLA_DK]


def _proj_kernel(x0_ref, xn_ref, xm_ref, g_ref, wt_ref, gain_ref, kgain_ref, wd_ref, bdec_ref, tg_ref,
                 zm_ref, go_ref, nq_ref, nk_ref, nv_ref, gtot_ref, zmm_ref, nkm_ref, nvm_ref, u_scr):
    def normed(x_ref):
        x = x_ref[...]
        ms = jnp.mean(x * x, axis=-1, keepdims=True)
        return (x * lax.rsqrt(ms + RMS_EPS) * g_ref[...]).astype(BF16)

    def head_norm_rows(acc, goff):
        first = lax.broadcasted_iota(jnp.int32, (acc.shape[0], LANES), 1) < NA_DH
        cols = []
        for c in range(0, acc.shape[1], LANES):
            a = acc[:, c:c + LANES]
            sq = a * a
            lo = jnp.sum(jnp.where(first, sq, 0.0), axis=-1, keepdims=True)
            hi = jnp.sum(jnp.where(first, 0.0, sq), axis=-1, keepdims=True)
            inv = jnp.where(first, lax.rsqrt(lo * (1.0 / NA_DH) + RMS_EPS), lax.rsqrt(hi * (1.0 / NA_DH) + RMS_EPS))
            cols.append(a * inv)
        return (jnp.concatenate(cols, axis=1) * gain_ref[:, goff:goff + PROJ_TN]).astype(BF16)

    def meta_tokens():
        um = normed(xm_ref)
        zmm_ref[...] = jnp.zeros_like(zmm_ref)
        for off in range(0, NA_WIDTH, PROJ_TN):
            acc = _dot_nt(um, wt_ref[IN_NK + off:IN_NK + off + PROJ_TN, :])
            nkm_ref[:, off:off + PROJ_TN] = head_norm_rows(acc, NA_WIDTH + off)
            acc = _dot_nt(um, wt_ref[IN_NV + off:IN_NV + off + PROJ_TN, :])
            nvm_ref[:, off:off + PROJ_TN] = acc.astype(nvm_ref.dtype)
        for dst, src, width in SHARED_SEGMENTS + META_SEGMENTS:
            acc = _dot_nt(um, wt_ref[src:src + width, :])
            if dst == COL_GQ:
                acc = acc * (GLA_DK ** -0.5)
            zmm_ref[:N_META, dst:dst + width] = acc.astype(zmm_ref.dtype)

    step = pl.program_id(0)

    @pl.when(step == 0)
    def _():
        u_scr[0] = normed(x0_ref)
        meta_tokens()

    slot = step % 2
    u = u_scr[slot]
    tm = u.shape[0]

    def prepare_next():
        u_scr[1 - slot] = normed(xn_ref)

    def main_tile(dst, src, off, n):
        acc = _dot_nt(u, wt_ref[src + off:src + off + n, :])
        zm_ref[:, dst + off:dst + off + n] = acc.astype(zm_ref.dtype)

    def nv_tile(off):
        acc = _dot_nt(u, wt_ref[IN_NV + off:IN_NV + off + PROJ_TN, :])
        _store_rows(nv_ref, off, acc.astype(nv_ref.dtype))

    def nq_tile(off):
        acc = _dot_nt(u, wt_ref[IN_NQ + off:IN_NQ + off + PROJ_TN, :])
        y = head_norm_rows(acc, off)
        for c in range(0, PROJ_TN, LANES):
            nq_ref[(off + c) // LANES] = y[:, c:c + LANES]

    def nk_tile(off):
        u_cb = jnp.concatenate([u[rl * GRID_W + cb * NA_CBW:rl * GRID_W + (cb + 1) * NA_CBW]
                                for cb in range(NA_CB) for rl in range(tm // GRID_W)], axis=0)
        acc_t = _dot_nt(wt_ref[IN_NK + off:IN_NK + off + PROJ_TN, :], u_cb)
        for h in range(PROJ_TN // NA_DH):
            rows = slice(off + h * NA_DH, off + (h + 1) * NA_DH)
            blk = acc_t[h * NA_DH:(h + 1) * NA_DH]
            ms = jnp.sum(blk * blk, axis=0, keepdims=True) * (1.0 / NA_DH)
            y = (blk * lax.rsqrt(ms + RMS_EPS) * kgain_ref[rows, :]).astype(BF16)
            for cb in range(NA_CB):
                nk_ref[cb, rows, :] = y[:, cb * LANES:(cb + 1) * LANES]

    tiles = [functools.partial(nk_tile, off) for off in range(0, NA_WIDTH, PROJ_TN)]
    tiles += [functools.partial(nq_tile, off) for off in range(0, NA_WIDTH, PROJ_TN)]
    tiles += [functools.partial(nv_tile, off) for off in range(0, NA_WIDTH, PROJ_TN)]
    for dst, src, width in SHARED_SEGMENTS:
        for off in range(0, width, PROJ_TN):
            tiles.append(functools.partial(main_tile, dst, src, off, min(PROJ_TN, width - off)))

    r = _dot_nt(u, wt_ref[IN_R:IN_R + LANES, :]).astype(BF16)
    q = _dot_nt(u, wt_ref[IN_GQ:IN_GQ + GLA_KEY_WIDTH, :]) * (GLA_DK ** -0.5)
    k = _dot_nt(u, wt_ref[IN_GK:IN_GK + GLA_KEY_WIDTH, :])
    log_decay = [None, None]

    def decay_stage(d):
        log_decay[d] = _gla_log_decay(r, wd_ref, bdec_ref, d)

    stages = [functools.partial(decay_stage, 0), functools.partial(decay_stage, 1), None]
    stages += [functools.partial(lambda d, t: _gla_operands_store(q, k, log_decay[d], d, t, tg_ref, go_ref, gtot_ref),
                                 d, t)
               for d in range(2) for t in range(tm // TILE)]
    for i, tile in enumerate(tiles):
        if i < len(stages) and stages[i] is not None:
            stages[i]()
        tile()
    prepare_next()
    assert len(stages) <= len(tiles)


def _proj_call(x2d, xm, g, wt, gain_qk, kgain_col, wd, bdec, tg, *, tm):
    m, d = x2d.shape
    assert m % tm == 0 and wt.shape == (D_IN_PROJ, d) and xm.shape == (N_META, d)
    assert tm == NA_WIN_H * GRID_W and NA_CB * LANES == tm
    resident = lambda shape: pl.BlockSpec(shape, lambda i: (0,) * len(shape), pipeline_mode=pl.Buffered(1))
    whole = lambda shape: pl.BlockSpec(shape, lambda i: (0,) * len(shape))
    n_pairs = NA_WIDTH // LANES
    q_spec = pl.BlockSpec((n_pairs, tm, LANES), lambda i: (0, i, 0))
    q_shape = jax.ShapeDtypeStruct((n_pairs, m, LANES), BF16)
    v_spec = pl.BlockSpec((n_pairs, NA_CB, tm // NA_CB, LANES), lambda i: (0, 0, i, 0))
    v_shape = jax.ShapeDtypeStruct((n_pairs, NA_CB, m // NA_CB, LANES), BF16)
    k_spec = pl.BlockSpec((NA_CB, NA_WIDTH, LANES), lambda i: (i, 0, 0))
    k_shape = jax.ShapeDtypeStruct((m // LANES, NA_WIDTH, LANES), BF16)
    n_chunk_rows = tm // GLA_CHUNK
    meta_plain = jax.ShapeDtypeStruct((N_META, NA_WIDTH), BF16)
    return pl.pallas_call(
        _proj_kernel,
        name="proj_tokens",
        grid=(m // tm,),
        in_specs=[
            resident((tm, d)),
            pl.BlockSpec((tm, d), lambda i: (jnp.minimum(i + 1, m // tm - 1), 0)),
            resident((N_META, d)),
            resident((1, d)),
            resident((D_IN_PROJ, d)),
            resident((1, 2 * NA_WIDTH)),
            resident((NA_WIDTH, 1)),
            resident((2, LANES, GLA_KEY_WIDTH)),
            resident((2, 1, GLA_KEY_WIDTH)),
            resident((2, TILE, TILE)),
        ],
        out_specs=[pl.BlockSpec((tm, MAIN_TOK_W), lambda i: (i, 0)),
                   pl.BlockSpec((2, 3, GLA_HEADS, tm, GLA_DK), lambda i: (0, 0, 0, i, 0)),
                   q_spec, k_spec, v_spec,
                   pl.BlockSpec((n_chunk_rows, 2 * GLA_KEY_WIDTH), lambda i: (i, 0)),
                   whole((TILE, MAIN_W)), whole((N_META, NA_WIDTH)), whole((N_META, NA_WIDTH))],
        out_shape=[jax.ShapeDtypeStruct((m, MAIN_TOK_W), BF16),
                   jax.ShapeDtypeStruct((2, 3, GLA_HEADS, m, GLA_DK), BF16), q_shape, k_shape, v_shape,
                   jax.ShapeDtypeStruct((m // GLA_CHUNK, 2 * GLA_KEY_WIDTH), F32),
                   jax.ShapeDtypeStruct((TILE, MAIN_W), BF16), meta_plain, meta_plain],
        scratch_shapes=[pltpu.VMEM((2, tm, d), BF16)],
        compiler_params=pltpu.CompilerParams(
            dimension_semantics=("arbitrary",), vmem_limit_bytes=VMEM_LIMIT),
    )(x2d, x2d, xm, g, wt, gain_qk, kgain_col, wd, bdec, tg)


def _gla_kernel(qf_ref, kif_ref, ksf_ref, vf_ref, gtf_ref, qb_ref, kib_ref, ksb_ref, vb_ref, gtb_ref,
                km_ref, vm_ref, rm_ref, wd_ref, b_ref, tgm_ref,
                o_ref, sf_ref, sb_ref, *, n_tiles):
    row = lax.broadcasted_iota(jnp.int32, (TILE, TILE), 0)
    col = lax.broadcasted_iota(jnp.int32, (TILE, TILE), 1)
    same = (row // GLA_CHUNK) == (col // GLA_CHUNK)
    masks = (same & (row >= col), same & (row <= col))

    @pl.when(pl.program_id(2) == 0)
    def _():
        rowm = lax.broadcasted_iota(jnp.int32, (TILE, GLA_DK), 0)
        lg = _log_sigmoid(_dot(rm_ref[...], wd_ref[...]) + b_ref[...]) * (1.0 / GLA_TAU)
        hi, lo = _split_bf16(jnp.where(rowm < N_META, lg, 0.0))
        d_m = _dot(tgm_ref[...], jnp.concatenate([hi, lo], axis=0))
        k_st = (km_ref[...].astype(F32) * jnp.exp(d_m)).astype(BF16)
        sf_ref[...] = _dot_tn(k_st, vm_ref[...])
        sb_ref[...] = jnp.zeros_like(sb_ref)

    def scan(step):
        items = []
        for t in range(n_tiles):
            items.append((0, t))
            items.append((1, n_tiles - 1 - t))
        refs = ((qf_ref, kif_ref, ksf_ref, vf_ref), (qb_ref, kib_ref, ksb_ref, vb_ref))
        out_half = (step, 1 - step)
        a_cols = (jnp.exp(jnp.transpose(gtf_ref[...])), jnp.exp(jnp.transpose(gtb_ref[...])))

        def rows_of(t):
            return slice(t * TILE, (t + 1) * TILE)

        amats = [jnp.where(masks[d], _dot_nt(refs[d][0][rows_of(t), :], refs[d][1][rows_of(t), :]), 0.0).astype(BF16)
                 for d, t in items]

        def state_independent(idx):
            d, t = items[idx]
            v = refs[d][3][rows_of(t), :]
            k_st = refs[d][2][rows_of(t), :]
            u = [_dot_tn(k_st[h * GLA_CHUNK:(h + 1) * GLA_CHUNK], v[h * GLA_CHUNK:(h + 1) * GLA_CHUNK])
                 for h in range(2)]
            return _dot(amats[idx], v), u

        state = [sf_ref[...], sb_ref[...]]
        ahead = state_independent(0)
        for idx, (d, t) in enumerate(items):
            o_intra, u = ahead
            if idx + 1 < len(items):
                ahead = state_independent(idx + 1)
            s = state[d]
            outs = [None, None]
            for h in ((0, 1) if d == 0 else (1, 0)):
                rows = slice(t * TILE + h * GLA_CHUNK, t * TILE + (h + 1) * GLA_CHUNK)
                outs[h] = o_intra[h * GLA_CHUNK:(h + 1) * GLA_CHUNK] + _dot(refs[d][0][rows, :], s.astype(BF16))
                chunk = 2 * t + h
                s = a_cols[d][:, chunk:chunk + 1] * s + u[h]
            state[d] = s
            o = jnp.concatenate(outs, axis=0)
            out_rows = rows_of(out_half[d] * n_tiles + t)
            if step == 1:
                o = o + o_ref[out_rows, :].astype(F32)
            o_ref[out_rows, :] = o.astype(o_ref.dtype)
        sf_ref[...] = state[0]
        sb_ref[...] = state[1]

    for step in range(2):
        pl.when(pl.program_id(2) == step)(functools.partial(scan, step))


def _gla_constants():
    i = np.arange(TILE)
    same = (i[:, None] // GLA_CHUNK) == (i[None, :] // GLA_CHUNK)
    inc_f = same & (i[None, :] <= i[:, None])
    inc_b = same & (i[None, :] >= i[:, None])
    to_end_meta = i[None, :] > i[:, None]
    dup = lambda m: np.concatenate([m, m], axis=1)
    tg = jnp.asarray(np.stack([inc_f, inc_b]), BF16)
    return tg, jnp.asarray(dup(to_end_meta), BF16)


def _gla_call(zmain, gla_ops, gtot, zmain_m, wd, bias, tg_m, *, batch, seq):
    nb = 2
    assert seq % (nb * TILE) == 0
    t_blk = seq // nb
    n_tiles = t_blk // TILE
    n_chunks = t_blk // GLA_CHUNK

    def tok(fwd):
        if fwd:
            return lambda b, hh, i: b * nb + i
        return lambda b, hh, i: b * nb + (nb - 1 - i)

    def dir_specs(d):
        t = tok(d == 0)
        operand = lambda kind: pl.BlockSpec((None, None, None, t_blk, GLA_DK),
                                            lambda b, hh, i: (d, kind, hh, t(b, hh, i), 0))
        return [
            operand(GLA_Q_IN), operand(GLA_K_IN), operand(GLA_K_ST),
            pl.BlockSpec((t_blk, GLA_DV), lambda b, hh, i: (t(b, hh, i), COL_GV // GLA_DV + hh)),
            pl.BlockSpec((n_chunks, GLA_DK), lambda b, hh, i: (t(b, hh, i), d * GLA_HEADS + hh)),
        ]

    in_specs = (
        dir_specs(0) + dir_specs(1)
        + [pl.BlockSpec((TILE, GLA_DK), lambda b, hh, i: (0, COL_GK // GLA_DK + hh)),
           pl.BlockSpec((TILE, GLA_DV), lambda b, hh, i: (0, COL_GV // GLA_DV + hh)),
           pl.BlockSpec((TILE, LANES), lambda b, hh, i: (0, COL_R // LANES)),
           pl.BlockSpec((None, LANES, GLA_DK), lambda b, hh, i: (0, 0, hh)),
           pl.BlockSpec((None, 1, GLA_DK), lambda b, hh, i: (0, 0, hh)),
           pl.BlockSpec((TILE, 2 * TILE), lambda b, hh, i: (0, 0))]
    )
    args = ([gla_ops] * 3 + [zmain, gtot]) * 2 + [zmain_m] * 3 + [wd, bias, tg_m]
    return pl.pallas_call(
        functools.partial(_gla_kernel, n_tiles=n_tiles),
        name="gla_scan",
        grid=(batch, GLA_HEADS, nb),
        in_specs=in_specs,
        out_specs=pl.BlockSpec((seq, GLA_DV), lambda b, hh, i: (b, hh)),
        out_shape=jax.ShapeDtypeStruct((batch * seq, GLA_WIDTH), BF16),
        scratch_shapes=[pltpu.VMEM((GLA_DK, GLA_DV), F32), pltpu.VMEM((GLA_DK, GLA_DV), F32)],
        compiler_params=pltpu.CompilerParams(
            dimension_semantics=("parallel", "parallel", "arbitrary"), vmem_limit_bytes=VMEM_LIMIT),
    )(*args)


NA_SEG_COLS = ((0, 24), (24, 40), (40, 64))
NA_SEG_ROW0 = (0, 48, 80)
NA_STACK = 2 * GRID_W


NA_BASE_MID = NA_WIN_H // 2 - 1
NA_VARIANTS = 2 * NA_WIN_H - 1


def _na_variant(base, a):
    return jnp.where(base == NA_BASE_MID, a, jnp.where(base < NA_BASE_MID, NA_WIN_H + base, NA_WIN_H - 1 + base))


def _na_build_bias(rpb_ref, tab_ref):
    def build(v, carry):
        a = jnp.where(v < NA_WIN_H, v, 0)
        base = jnp.where(v < NA_WIN_H, NA_BASE_MID,
                         jnp.where(v < NA_WIN_H + NA_BASE_MID, v - NA_WIN_H, v - (NA_WIN_H - 1)))
        for hh in range(2):
            bias_rows = [rpb_ref[hh, pl.ds(base + ((t - a + NA_WIN_H) & (NA_WIN_H - 1)), 1), :]
                         for t in range(NA_WIN_H)]
            for s, (c0, c1) in enumerate(NA_SEG_COLS):
                n = c1 - c0
                vrows = [jnp.broadcast_to(br, (n, LANES)) for br in bias_rows]
                row0 = NA_SEG_ROW0[s] + hh * n
                rho = lax.broadcasted_iota(jnp.int32, (n, LANES), 0)
                lane = lax.broadcasted_iota(jnp.int32, (n, LANES), 1)
                cs = jnp.clip(c0 + rho - NA_WIN_W // 2, 0, GRID_W - NA_WIN_W)
                for half in range(2):
                    cb = s + half
                    cp = NA_CBW * cb + lane % NA_CBW
                    valid = (cp >= cs) & (cp < cs + NA_WIN_W)
                    acc = None
                    for i in range(NA_WIN_H):
                        shift = (NA_CBW * i - NA_CBW * cb + c0 - (NA_WIN_W - 1)) % LANES
                        rolled = pltpu.roll(vrows[i], shift, 1, stride=1, stride_axis=0)
                        acc = rolled if acc is None else jnp.where(lane // NA_CBW == i, rolled, acc)
                    tab_ref[v, row0:row0 + n, half * LANES:(half + 1) * LANES] = jnp.where(valid, acc, NEG_BIG)
        return carry

    lax.fori_loop(0, NA_VARIANTS, build, 0)


def _na_kernel(q_ref, kt_ref, v_ref, gate_ref, km_ref, vm_ref, rpb_ref, mb_ref, o_ref, tab_ref, sc_ref, sm_ref,
               *, rows_per_step, n_rows):
    w = GRID_W

    @pl.when(pl.program_id(1) == 0)
    def _():
        _na_build_bias(rpb_ref, tab_ref)

    head0_q = lax.broadcasted_iota(jnp.int32, (w, LANES), 1) < NA_DH
    srow = lax.broadcasted_iota(jnp.int32, (NA_STACK, N_META), 0)
    is_h1 = jnp.zeros((NA_STACK, N_META), jnp.bool_)
    for s, (c0, c1) in enumerate(NA_SEG_COLS):
        lo = NA_SEG_ROW0[s] + (c1 - c0)
        is_h1 = is_h1 | ((srow >= lo) & (srow < lo + (c1 - c0)))
    mb = jnp.where(is_h1, mb_ref[1], mb_ref[0])
    km = km_ref[...]
    vm = vm_ref[...]
    seg_rows = [slice(NA_SEG_ROW0[s], NA_SEG_ROW0[s] + 2 * (c1 - c0)) for s, (c0, c1) in enumerate(NA_SEG_COLS)]
    blk_rows = [slice(seg_rows[max(cb - 1, 0)].start, seg_rows[min(cb, len(seg_rows) - 1)].stop)
                for cb in range(NA_CB)]

    n_tiles = n_rows // NA_WIN_H
    key_lane = lax.broadcasted_iota(jnp.int32, (LANES, LANES), 1)

    def window(rl):
        row = pl.program_id(1) * rows_per_step + rl
        b = row // n_rows
        r = row - b * n_rows
        rs = jnp.clip(r - NA_WIN_H // 2, 0, n_rows - NA_WIN_H)
        a = rs & (NA_WIN_H - 1)
        return (b * n_tiles + rs // NA_WIN_H, a, (b + 1) * n_tiles - 1,
                _na_variant(rs - r + (NA_WIN_H - 1), a))

    def scores_store(slot, j, rl):
        m, a, m_last, var = window(rl)
        q = q_ref[pl.ds(pl.multiple_of(rl * w, w), w), :].astype(F32)
        q0 = jnp.where(head0_q, q, 0.0)
        q1 = q - q0
        qs = jnp.concatenate([piece[c0:c1] for c0, c1 in NA_SEG_COLS for piece in (q0, q1)],
                             axis=0).astype(BF16)
        from_m = key_lane >= a * NA_CBW
        m1 = jnp.minimum(m + 1, m_last)
        blocks = [_dot(qs[blk_rows[cb]],
                       jnp.where(from_m, kt_ref[NA_CB * m + cb], kt_ref[NA_CB * m1 + cb]))
                  for cb in range(NA_CB)]
        for s in range(len(NA_SEG_COLS)):
            n2 = seg_rows[s].stop - seg_rows[s].start
            for half in range(2):
                cb = s + half
                a0 = seg_rows[s].start - blk_rows[cb].start
                lanes = slice(half * LANES, (half + 1) * LANES)
                sc_ref[slot, j, seg_rows[s], lanes] = blocks[cb][a0:a0 + n2] + tab_ref[var, seg_rows[s], lanes]
        sm_ref[slot, j] = _dot_nt(qs, km) + mb

    def value_block(cb, m, a):
        groups = []
        for t in range(NA_WIN_H):
            row = NA_WIN_H * m + t + jnp.where(t < a, NA_WIN_H, 0)
            groups.append(v_ref[cb, pl.ds(pl.multiple_of(row * NA_CBW, NA_CBW), NA_CBW), :])
        return jnp.concatenate(groups, axis=0)

    def softmax_values(slot, j, rl):
        m_tile, a, _, _ = window(rl)
        sc = sc_ref[slot, j]
        sm = sm_ref[slot, j]
        m = jnp.maximum(sc[:, :LANES], sc[:, LANES:])
        m = jnp.maximum(jnp.max(m, axis=-1, keepdims=True), jnp.max(sm, axis=-1, keepdims=True))
        p = jnp.exp2(sc - m)
        pm = jnp.exp2(sm - m)
        den = (jnp.sum(p[:, :LANES] + p[:, LANES:], axis=-1, keepdims=True)
               + jnp.sum(pm, axis=-1, keepdims=True))
        p = p.astype(BF16)
        vals = [value_block(cb, m_tile, a) for cb in range(NA_CB)]
        o_meta = _dot(pm.astype(BF16), vm)
        inv = 1.0 / den
        outs = []
        for s, (c0, c1) in enumerate(NA_SEG_COLS):
            o_loc = _dot(p[seg_rows[s]], jnp.concatenate([vals[s], vals[s + 1]], axis=0))
            o = (o_loc + o_meta[seg_rows[s]]) * inv[seg_rows[s]]
            n = c1 - c0
            head0 = lax.broadcasted_iota(jnp.int32, (n, LANES), 1) < NA_DH
            outs.append(jnp.where(head0, o[:n], o[n:]))
        o = jnp.concatenate(outs, axis=0)
        qoff = pl.multiple_of(rl * w, w)
        gate = gate_ref[pl.ds(qoff, w), :].astype(F32)
        o_ref[pl.ds(qoff, w), :] = (o * _silu(gate)).astype(o_ref.dtype)

    n_groups = rows_per_step // NA_ROW_GROUP
    for j in range(NA_ROW_GROUP):
        scores_store(0, j, j)

    def trip(g, carry):
        slot = g % 2
        for j in range(NA_ROW_GROUP):
            scores_store(slot, j, g * NA_ROW_GROUP + j)
        for j in range(NA_ROW_GROUP):
            softmax_values(1 - slot, j, (g - 1) * NA_ROW_GROUP + j)
        return carry

    lax.fori_loop(1, n_groups, trip, 0)
    for j in range(NA_ROW_GROUP):
        softmax_values((n_groups - 1) % 2, j, (n_groups - 1) * NA_ROW_GROUP + j)


def _na_call(nq, nk_t, nv_cb, zmain, nk_m, nv_m, rpb_pad, meta_bias, *, batch, seq, rows_per_step):
    n_rows = seq // GRID_W
    total_rows = batch * n_rows
    assert n_rows % NA_WIN_H == 0 and total_rows % rows_per_step == 0 and rows_per_step % NA_ROW_GROUP == 0
    nsteps = total_rows // rows_per_step
    hp = NA_HEADS // 2
    tq = rows_per_step * GRID_W
    n_idx = 2 * NA_WIN_H - 1
    in_specs = [
        pl.BlockSpec((None, tq, LANES), lambda p, i: (p, i, 0)),
        pl.BlockSpec((batch * seq // LANES, LANES, LANES), lambda p, i: (0, p, 0)),
        pl.BlockSpec((None, NA_CB, batch * seq // NA_CB, LANES), lambda p, i: (p, 0, 0, 0)),
        pl.BlockSpec((tq, LANES), lambda p, i: (i, COL_NG // LANES + p)),
        pl.BlockSpec((N_META, LANES), lambda p, i: (0, p)),
        pl.BlockSpec((N_META, LANES), lambda p, i: (0, p)),
        pl.BlockSpec((2, n_idx, LANES), lambda p, i: (p, 0, 0)),
        pl.BlockSpec((2, 1, N_META), lambda p, i: (p, 0, 0)),
    ]
    return pl.pallas_call(
        functools.partial(_na_kernel, rows_per_step=rows_per_step, n_rows=n_rows),
        name="na_attn",
        grid=(hp, nsteps),
        in_specs=in_specs,
        out_specs=pl.BlockSpec((None, tq, LANES), lambda p, i: (p, i, 0)),
        out_shape=jax.ShapeDtypeStruct((hp, batch * seq, LANES), BF16),
        scratch_shapes=[pltpu.VMEM((NA_VARIANTS, NA_STACK, 2 * LANES), F32),
                        pltpu.VMEM((2, NA_ROW_GROUP, NA_STACK, 2 * LANES), F32),
                        pltpu.VMEM((2, NA_ROW_GROUP, NA_STACK, N_META), F32)],
        compiler_params=pltpu.CompilerParams(
            dimension_semantics=("arbitrary", "arbitrary"), vmem_limit_bytes=VMEM_LIMIT),
    )(nq, nk_t, nv_cb, zmain, nk_m, nv_m, rpb_pad, meta_bias)


def _out_kernel(x_ref, os_ref, gg_ref, na_ref, w_ref, o_ref):
    d = o_ref.shape[1]
    na = jnp.concatenate([na_ref[p] for p in range(na_ref.shape[0])], axis=1)

    def na_tile(c):
        cols = slice(c * OUT_TN, (c + 1) * OUT_TN)
        o_ref[:, cols] = x_ref[:, cols] + _dot(na, w_ref[GLA_WIDTH:, cols])

    def gla_head(h):
        cols = slice(h * GLA_DV, (h + 1) * GLA_DV)
        sh = os_ref[:, cols].astype(F32)
        ms = jnp.mean(sh * sh, axis=-1, keepdims=True)
        return (sh * lax.rsqrt(ms + RMS_EPS) * _silu(gg_ref[:, cols].astype(F32))).astype(BF16)

    heads = []
    for i in range(max(d // OUT_TN, GLA_HEADS)):
        if i < d // OUT_TN:
            na_tile(i)
        if i < GLA_HEADS:
            heads.append(gla_head(i))
    o_gla = jnp.concatenate(heads, axis=1)
    for c in range(d // OUT_TN):
        cols = slice(c * OUT_TN, (c + 1) * OUT_TN)
        o_ref[:, cols] += _dot(o_gla, w_ref[:GLA_WIDTH, cols])


def _out_call(x2d, o_sum, zmain, o_na, w_out, *, tm):
    m, d = x2d.shape
    assert m % tm == 0 and d % OUT_TN == 0
    return pl.pallas_call(
        _out_kernel,
        name="out_proj",
        grid=(m // tm,),
        in_specs=[
            pl.BlockSpec((tm, d), lambda i: (i, 0)),
            pl.BlockSpec((tm, GLA_WIDTH), lambda i: (i, 0)),
            pl.BlockSpec((tm, GLA_WIDTH), lambda i: (i, COL_GG // GLA_WIDTH)),
            pl.BlockSpec((NA_WIDTH // LANES, tm, LANES), lambda i: (0, i, 0)),
            pl.BlockSpec((GLA_WIDTH + NA_WIDTH, d), lambda i: (0, 0), pipeline_mode=pl.Buffered(1)),
        ],
        out_specs=pl.BlockSpec((tm, d), lambda i: (i, 0)),
        out_shape=jax.ShapeDtypeStruct((m, d), F32),
        compiler_params=pltpu.CompilerParams(
            dimension_semantics=("parallel",), vmem_limit_bytes=VMEM_LIMIT),
    )(x2d, o_sum, zmain, o_na, w_out)


def _pick(n, cands):
    for c in cands:
        if n % c == 0:
            return c
    raise ValueError(f"no tile for {n}")


def kernel(x, meta_tokens, norm_g, w_in, w_decay_fwd, b_decay_fwd, w_decay_bwd, b_decay_bwd,
           gla_out_norm_g, q_norm_g, k_norm_g, rpb, meta_bias, w_out):
    batch, seq, d = x.shape
    assert d == D_MODEL and seq % TILE == 0 and seq % GRID_W == 0
    depth = norm_g.shape[0]
    assert depth == 1
    l = 0
    x2d = x.reshape(batch * seq, d)
    g = norm_g[l].reshape(1, d)

    assert w_in.shape[1:] == (d, D_IN_PROJ)
    wt = jnp.swapaxes(w_in[l], 0, 1).astype(BF16)
    gain_qk = jnp.concatenate([jnp.tile(q_norm_g[l], NA_HEADS) * (NA_DH ** -0.5 * LOG2_E),
                               jnp.tile(k_norm_g[l], NA_HEADS)]).reshape(1, 2 * NA_WIDTH)

    zr = jnp.zeros((GLA_RANK, GLA_KEY_WIDTH), F32)
    ztail = jnp.zeros((LANES - 2 * GLA_RANK, GLA_KEY_WIDTH), F32)
    wd = jnp.stack([jnp.concatenate([w_decay_fwd[l], zr, ztail], axis=0),
                    jnp.concatenate([zr, w_decay_bwd[l], ztail], axis=0)]).astype(BF16)
    bias = jnp.stack([b_decay_fwd[l], b_decay_bwd[l]]).reshape(2, 1, GLA_KEY_WIDTH)
    tg, tg_m = _gla_constants()

    assert seq % (NA_WIN_H * GRID_W) == 0
    kgain_col = jnp.tile(k_norm_g[l], NA_HEADS).reshape(NA_WIDTH, 1)
    zmain, gla_ops, nq, nk_t, nv_cb, gtot, zmain_m_pad, nk_m, nv_m = _proj_call(
        x2d, meta_tokens, g, wt, gain_qk, kgain_col, wd, bias, tg, tm=NA_WIN_H * GRID_W)

    o_sum = _gla_call(zmain, gla_ops, gtot, zmain_m_pad, wd, bias, tg_m, batch=batch, seq=seq)

    rpb_pad = jnp.pad(rpb[l] * LOG2_E, ((0, 0), (0, 0), (0, 2 * NA_DH - rpb.shape[-1])))
    mb = (meta_bias[l] * LOG2_E).reshape(NA_HEADS, 1, N_META)
    o_na = _na_call(nq, nk_t, nv_cb, zmain, nk_m, nv_m, rpb_pad, mb, batch=batch, seq=seq,
                    rows_per_step=_pick(batch * seq // GRID_W, (256, 128, 64, 32, 16)))

    gn_rows = jnp.concatenate([jnp.tile(gla_out_norm_g[l], GLA_HEADS), jnp.ones((NA_WIDTH,), F32)])
    w_o = (w_out[l] * gn_rows[:, None]).astype(BF16)
    out = _out_call(x2d, o_sum, zmain, o_na, w_o, tm=_pick(batch * seq, (1024, 512, 256, 128)))
    return out.reshape(batch, seq, d)
```

```python
import functools

import jax
import jax.numpy as jnp
import numpy as np
from jax import lax
from jax.experimental import pallas as pl
from jax.experimental.pallas import tpu as pltpu

F32 = jnp.float32
BF16 = jnp.bfloat16

D_MODEL = 1024
N_META = 16
GRID_W = 64
GLA_HEADS = 4
GLA_DK = 128
GLA_DV = 256
GLA_KEY_WIDTH = GLA_HEADS * GLA_DK
GLA_WIDTH = GLA_HEADS * GLA_DV
GLA_RANK = 16
GLA_TAU = 16.0
GLA_CHUNK = 64
NA_HEADS = 16
NA_DH = 64
NA_WIDTH = NA_HEADS * NA_DH
NA_WIN_H = 8
NA_WIN_W = 16
RMS_EPS = 1e-6
NEG_BIG = -1e30
LOG2_E = 1.4426950408889634
LANES = 128

COL_GV, COL_GG, COL_NG, COL_GQ, COL_GK, COL_R = 0, 1024, 2048, 3072, 3584, 4096
MAIN_W = COL_R + LANES
MAIN_TOK_W = COL_GQ
GLA_Q_IN, GLA_K_IN, GLA_K_ST = 0, 1, 2
NA_CBW = 16
NA_CB = GRID_W // NA_CBW
IN_GQ, IN_GK, IN_GV, IN_R, IN_GG, IN_NQ, IN_NK, IN_NV, IN_NG = 0, 512, 1024, 2048, 2080, 3104, 4128, 5152, 6176
D_IN_PROJ = 7200
PROJ_TN = 512
OUT_TN = 256

TILE = 2 * GLA_CHUNK
NA_ROW_GROUP = 16
VMEM_LIMIT = 58 * 1024 * 1024


def _split_bf16(x):
    hi = x.astype(BF16)
    lo = (x - hi.astype(F32)).astype(BF16)
    return hi, lo


def _dot(a, b):
    return jnp.dot(a, b, preferred_element_type=F32)


def _dot_nt(a, b):
    return lax.dot_general(a, b, (((1,), (1,)), ((), ())), preferred_element_type=F32)


def _dot_tn(a, b):
    return lax.dot_general(a, b, (((0,), (0,)), ((), ())), preferred_element_type=F32)


def _log_sigmoid(x):
    return jnp.minimum(x, 0.0) - jnp.log(1.0 + jnp.exp(-jnp.abs(x)))


def _silu(x):
    h = 0.5 * x
    return h + h * jnp.tanh(h)


SHARED_SEGMENTS = ((COL_GV, IN_GV, 1024), (COL_GG, IN_GG, 1024), (COL_NG, IN_NG, 1024))
META_SEGMENTS = ((COL_GQ, IN_GQ, 512), (COL_GK, IN_GK, 512),
                 (COL_R, IN_R, LANES))


def _store_rows(o_ref, c0, y):
    n = y.shape[1]
    for rl in range(y.shape[0] // GRID_W):
        for cb in range(NA_CB):
            src = rl * GRID_W + cb * NA_CBW
            for c in range(0, n, LANES):
                o_ref[(c0 + c) // LANES, cb, rl * NA_CBW:(rl + 1) * NA_CBW, :] = y[src:src + NA_CBW, c:c + LANES]


def _gla_log_decay(r, wd_ref, bdec_ref, d):
    return (_log_sigmoid(_dot(r, wd_ref[d]) + bdec_ref[d]) * (1.0 / GLA_TAU)).astype(BF16)


def _gla_operands_store(q, k, log_decay, d, t, tg_ref, go_ref, gtot_ref):
    rows = slice(t * TILE, (t + 1) * TILE)
    g = _dot(tg_ref[d], log_decay[rows])
    tots = []
    for c in range(TILE // GLA_CHUNK):
        last = c * GLA_CHUNK + (GLA_CHUNK - 1 if d == 0 else 0)
        tot = g[last:last + 1]
        gtot_ref[(t * TILE) // GLA_CHUNK + c:(t * TILE) // GLA_CHUNK + c + 1,
                 d * GLA_KEY_WIDTH:(d + 1) * GLA_KEY_WIDTH] = tot
        tots.append(jnp.broadcast_to(tot, (GLA_CHUNK, GLA_KEY_WIDTH)))
    dd = jnp.concatenate(tots, axis=0) - g
    operands = ((GLA_Q_IN, q[rows] * jnp.exp(g)), (GLA_K_IN, k[rows] * jnp.exp(-g)), (GLA_K_ST, k[rows] * jnp.exp(dd)))
    for kind, val in operands:
        val = val.astype(go_ref.dtype)
        for h in range(GLA_HEADS):
            go_ref[d, kind, h, rows, :] = val[:, h * GLA_DK:(h + 1) * GLA_DK]


def _proj_kernel(x0_ref, xn_ref, xm_ref, g_ref, wt_ref, gain_ref, kgain_ref, wd_ref, bdec_ref, tg_ref,
                 zm_ref, go_ref, nq_ref, nk_ref, nv_ref, gtot_ref, zmm_ref, nkm_ref, nvm_ref, u_scr):
    def normed(x_ref):
        x = x_ref[...]
        ms = jnp.mean(x * x, axis=-1, keepdims=True)
        return (x * lax.rsqrt(ms + RMS_EPS) * g_ref[...]).astype(BF16)

    def head_norm_rows(acc, goff):
        first = lax.broadcasted_iota(jnp.int32, (acc.shape[0], LANES), 1) < NA_DH
        cols = []
        for c in range(0, acc.shape[1], LANES):
            a = acc[:, c:c + LANES]
            sq = a * a
            lo = jnp.sum(jnp.where(first, sq, 0.0), axis=-1, keepdims=True)
            hi = jnp.sum(jnp.where(first, 0.0, sq), axis=-1, keepdims=True)
            inv = jnp.where(first, lax.rsqrt(lo * (1.0 / NA_DH) + RMS_EPS), lax.rsqrt(hi * (1.0 / NA_DH) + RMS_EPS))
            cols.append(a * inv)
        return (jnp.concatenate(cols, axis=1) * gain_ref[:, goff:goff + PROJ_TN]).astype(BF16)

    def meta_tokens():
        um = normed(xm_ref)
        zmm_ref[...] = jnp.zeros_like(zmm_ref)
        for off in range(0, NA_WIDTH, PROJ_TN):
            acc = _dot_nt(um, wt_ref[IN_NK + off:IN_NK + off + PROJ_TN, :])
            nkm_ref[:, off:off + PROJ_TN] = head_norm_rows(acc, NA_WIDTH + off)
            acc = _dot_nt(um, wt_ref[IN_NV + off:IN_NV + off + PROJ_TN, :])
            nvm_ref[:, off:off + PROJ_TN] = acc.astype(nvm_ref.dtype)
        for dst, src, width in SHARED_SEGMENTS + META_SEGMENTS:
            acc = _dot_nt(um, wt_ref[src:src + width, :])
            if dst == COL_GQ:
                acc = acc * (GLA_DK ** -0.5)
            zmm_ref[:N_META, dst:dst + width] = acc.astype(zmm_ref.dtype)

    step = pl.program_id(0)

    @pl.when(step == 0)
    def _():
        u_scr[0] = normed(x0_ref)
        meta_tokens()

    slot = step % 2
    u = u_scr[slot]
    tm = u.shape[0]

    def prepare_next():
        u_scr[1 - slot] = normed(xn_ref)

    def main_tile(dst, src, off, n):
        acc = _dot_nt(u, wt_ref[src + off:src + off + n, :])
        zm_ref[:, dst + off:dst + off + n] = acc.astype(zm_ref.dtype)

    def nv_tile(off):
        acc = _dot_nt(u, wt_ref[IN_NV + off:IN_NV + off + PROJ_TN, :])
        _store_rows(nv_ref, off, acc.astype(nv_ref.dtype))

    def nq_tile(off):
        acc = _dot_nt(u, wt_ref[IN_NQ + off:IN_NQ + off + PROJ_TN, :])
        y = head_norm_rows(acc, off)
        for c in range(0, PROJ_TN, LANES):
            nq_ref[(off + c) // LANES] = y[:, c:c + LANES]

    def nk_tile(off):
        u_cb = jnp.concatenate([u[rl * GRID_W + cb * NA_CBW:rl * GRID_W + (cb + 1) * NA_CBW]
                                for cb in range(NA_CB) for rl in range(tm // GRID_W)], axis=0)
        acc_t = _dot_nt(wt_ref[IN_NK + off:IN_NK + off + PROJ_TN, :], u_cb)
        for h in range(PROJ_TN // NA_DH):
            rows = slice(off + h * NA_DH, off + (h + 1) * NA_DH)
            blk = acc_t[h * NA_DH:(h + 1) * NA_DH]
            ms = jnp.sum(blk * blk, axis=0, keepdims=True) * (1.0 / NA_DH)
            y = (blk * lax.rsqrt(ms + RMS_EPS) * kgain_ref[rows, :]).astype(BF16)
            for cb in range(NA_CB):
                nk_ref[cb, rows, :] = y[:, cb * LANES:(cb + 1) * LANES]

    tiles = [functools.partial(nk_tile, off) for off in range(0, NA_WIDTH, PROJ_TN)]
    tiles += [functools.partial(nq_tile, off) for off in range(0, NA_WIDTH, PROJ_TN)]
    tiles += [functools.partial(nv_tile, off) for off in range(0, NA_WIDTH, PROJ_TN)]
    for dst, src, width in SHARED_SEGMENTS:
        for off in range(0, width, PROJ_TN):
            tiles.append(functools.partial(main_tile, dst, src, off, min(PROJ_TN, width - off)))

    r = _dot_nt(u, wt_ref[IN_R:IN_R + LANES, :]).astype(BF16)
    q = _dot_nt(u, wt_ref[IN_GQ:IN_GQ + GLA_KEY_WIDTH, :]) * (GLA_DK ** -0.5)
    k = _dot_nt(u, wt_ref[IN_GK:IN_GK + GLA_KEY_WIDTH, :])
    log_decay = [None, None]

    def decay_stage(d):
        log_decay[d] = _gla_log_decay(r, wd_ref, bdec_ref, d)

    stages = [functools.partial(decay_stage, 0), functools.partial(decay_stage, 1), None]
    stages += [functools.partial(lambda d, t: _gla_operands_store(q, k, log_decay[d], d, t, tg_ref, go_ref, gtot_ref),
                                 d, t)
               for d in range(2) for t in range(tm // TILE)]
    for i, tile in enumerate(tiles):
        if i < len(stages) and stages[i] is not None:
            stages[i]()
        tile()
    prepare_next()
    assert len(stages) <= len(tiles)


def _proj_call(x2d, xm, g, wt, gain_qk, kgain_col, wd, bdec, tg, *, tm):
    m, d = x2d.shape
    assert m % tm == 0 and wt.shape == (D_IN_PROJ, d) and xm.shape == (N_META, d)
    assert tm == NA_WIN_H * GRID_W and NA_CB * LANES == tm
    resident = lambda shape: pl.BlockSpec(shape, lambda i: (0,) * len(shape), pipeline_mode=pl.Buffered(1))
    whole = lambda shape: pl.BlockSpec(shape, lambda i: (0,) * len(shape))
    n_pairs = NA_WIDTH // LANES
    q_spec = pl.BlockSpec((n_pairs, tm, LANES), lambda i: (0, i, 0))
    q_shape = jax.ShapeDtypeStruct((n_pairs, m, LANES), BF16)
    v_spec = pl.BlockSpec((n_pairs, NA_CB, tm // NA_CB, LANES), lambda i: (0, 0, i, 0))
    v_shape = jax.ShapeDtypeStruct((n_pairs, NA_CB, m // NA_CB, LANES), BF16)
    k_spec = pl.BlockSpec((NA_CB, NA_WIDTH, LANES), lambda i: (i, 0, 0))
    k_shape = jax.ShapeDtypeStruct((m // LANES, NA_WIDTH, LANES), BF16)
    n_chunk_rows = tm // GLA_CHUNK
    meta_plain = jax.ShapeDtypeStruct((N_META, NA_WIDTH), BF16)
    return pl.pallas_call(
        _proj_kernel,
        name="proj_tokens",
        grid=(m // tm,),
        in_specs=[
            resident((tm, d)),
            pl.BlockSpec((tm, d), lambda i: (jnp.minimum(i + 1, m // tm - 1), 0)),
            resident((N_META, d)),
            resident((1, d)),
            resident((D_IN_PROJ, d)),
            resident((1, 2 * NA_WIDTH)),
            resident((NA_WIDTH, 1)),
            resident((2, LANES, GLA_KEY_WIDTH)),
            resident((2, 1, GLA_KEY_WIDTH)),
            resident((2, TILE, TILE)),
        ],
        out_specs=[pl.BlockSpec((tm, MAIN_TOK_W), lambda i: (i, 0)),
                   pl.BlockSpec((2, 3, GLA_HEADS, tm, GLA_DK), lambda i: (0, 0, 0, i, 0)),
                   q_spec, k_spec, v_spec,
                   pl.BlockSpec((n_chunk_rows, 2 * GLA_KEY_WIDTH), lambda i: (i, 0)),
                   whole((TILE, MAIN_W)), whole((N_META, NA_WIDTH)), whole((N_META, NA_WIDTH))],
        out_shape=[jax.ShapeDtypeStruct((m, MAIN_TOK_W), BF16),
                   jax.ShapeDtypeStruct((2, 3, GLA_HEADS, m, GLA_DK), BF16), q_shape, k_shape, v_shape,
                   jax.ShapeDtypeStruct((m // GLA_CHUNK, 2 * GLA_KEY_WIDTH), F32),
                   jax.ShapeDtypeStruct((TILE, MAIN_W), BF16), meta_plain, meta_plain],
        scratch_shapes=[pltpu.VMEM((2, tm, d), BF16)],
        compiler_params=pltpu.CompilerParams(
            dimension_semantics=("arbitrary",), vmem_limit_bytes=VMEM_LIMIT),
    )(x2d, x2d, xm, g, wt, gain_qk, kgain_col, wd, bdec, tg)


def _gla_kernel(qf_ref, kif_ref, ksf_ref, vf_ref, gtf_ref, qb_ref, kib_ref, ksb_ref, vb_ref, gtb_ref,
                km_ref, vm_ref, rm_ref, wd_ref, b_ref, tgm_ref,
                o_ref, sf_ref, sb_ref, *, n_tiles):
    row = lax.broadcasted_iota(jnp.int32, (TILE, TILE), 0)
    col = lax.broadcasted_iota(jnp.int32, (TILE, TILE), 1)
    same = (row // GLA_CHUNK) == (col // GLA_CHUNK)
    masks = (same & (row >= col), same & (row <= col))

    @pl.when(pl.program_id(2) == 0)
    def _():
        rowm = lax.broadcasted_iota(jnp.int32, (TILE, GLA_DK), 0)
        lg = _log_sigmoid(_dot(rm_ref[...], wd_ref[...]) + b_ref[...]) * (1.0 / GLA_TAU)
        hi, lo = _split_bf16(jnp.where(rowm < N_META, lg, 0.0))
        d_m = _dot(tgm_ref[...], jnp.concatenate([hi, lo], axis=0))
        k_st = (km_ref[...].astype(F32) * jnp.exp(d_m)).astype(BF16)
        sf_ref[...] = _dot_tn(k_st, vm_ref[...])
        sb_ref[...] = jnp.zeros_like(sb_ref)

    def scan(step):
        items = []
        for t in range(n_tiles):
            items.append((0, t))
            items.append((1, n_tiles - 1 - t))
        refs = ((qf_ref, kif_ref, ksf_ref, vf_ref), (qb_ref, kib_ref, ksb_ref, vb_ref))
        out_half = (step, 1 - step)
        a_cols = (jnp.exp(jnp.transpose(gtf_ref[...])), jnp.exp(jnp.transpose(gtb_ref[...])))

        def rows_of(t):
            return slice(t * TILE, (t + 1) * TILE)

        amats = [jnp.where(masks[d], _dot_nt(refs[d][0][rows_of(t), :], refs[d][1][rows_of(t), :]), 0.0).astype(BF16)
                 for d, t in items]

        def state_independent(idx):
            d, t = items[idx]
            v = refs[d][3][rows_of(t), :]
            k_st = refs[d][2][rows_of(t), :]
            u = [_dot_tn(k_st[h * GLA_CHUNK:(h + 1) * GLA_CHUNK], v[h * GLA_CHUNK:(h + 1) * GLA_CHUNK])
                 for h in range(2)]
            return _dot(amats[idx], v), u

        state = [sf_ref[...], sb_ref[...]]
        ahead = state_independent(0)
        for idx, (d, t) in enumerate(items):
            o_intra, u = ahead
            if idx + 1 < len(items):
                ahead = state_independent(idx + 1)
            s = state[d]
            outs = [None, None]
            for h in ((0, 1) if d == 0 else (1, 0)):
                rows = slice(t * TILE + h * GLA_CHUNK, t * TILE + (h + 1) * GLA_CHUNK)
                outs[h] = o_intra[h * GLA_CHUNK:(h + 1) * GLA_CHUNK] + _dot(refs[d][0][rows, :], s.astype(BF16))
                chunk = 2 * t + h
                s = a_cols[d][:, chunk:chunk + 1] * s + u[h]
            state[d] = s
            o = jnp.concatenate(outs, axis=0)
            out_rows = rows_of(out_half[d] * n_tiles + t)
            if step == 1:
                o = o + o_ref[out_rows, :].astype(F32)
            o_ref[out_rows, :] = o.astype(o_ref.dtype)
        sf_ref[...] = state[0]
        sb_ref[...] = state[1]

    for step in range(2):
        pl.when(pl.program_id(2) == step)(functools.partial(scan, step))


def _gla_constants():
    i = np.arange(TILE)
    same = (i[:, None] // GLA_CHUNK) == (i[None, :] // GLA_CHUNK)
    inc_f = same & (i[None, :] <= i[:, None])
    inc_b = same & (i[None, :] >= i[:, None])
    to_end_meta = i[None, :] > i[:, None]
    dup = lambda m: np.concatenate([m, m], axis=1)
    tg = jnp.asarray(np.stack([inc_f, inc_b]), BF16)
    return tg, jnp.asarray(dup(to_end_meta), BF16)


def _gla_call(zmain, gla_ops, gtot, zmain_m, wd, bias, tg_m, *, batch, seq):
    nb = 2
    assert seq % (nb * TILE) == 0
    t_blk = seq // nb
    n_tiles = t_blk // TILE
    n_chunks = t_blk // GLA_CHUNK

    def tok(fwd):
        if fwd:
            return lambda b, hh, i: b * nb + i
        return lambda b, hh, i: b * nb + (nb - 1 - i)

    def dir_specs(d):
        t = tok(d == 0)
        operand = lambda kind: pl.BlockSpec((None, None, None, t_blk, GLA_DK),
                                            lambda b, hh, i: (d, kind, hh, t(b, hh, i), 0))
        return [
            operand(GLA_Q_IN), operand(GLA_K_IN), operand(GLA_K_ST),
            pl.BlockSpec((t_blk, GLA_DV), lambda b, hh, i: (t(b, hh, i), COL_GV // GLA_DV + hh)),
            pl.BlockSpec((n_chunks, GLA_DK), lambda b, hh, i: (t(b, hh, i), d * GLA_HEADS + hh)),
        ]

    in_specs = (
        dir_specs(0) + dir_specs(1)
        + [pl.BlockSpec((TILE, GLA_DK), lambda b, hh, i: (0, COL_GK // GLA_DK + hh)),
           pl.BlockSpec((TILE, GLA_DV), lambda b, hh, i: (0, COL_GV // GLA_DV + hh)),
           pl.BlockSpec((TILE, LANES), lambda b, hh, i: (0, COL_R // LANES)),
           pl.BlockSpec((None, LANES, GLA_DK), lambda b, hh, i: (0, 0, hh)),
           pl.BlockSpec((None, 1, GLA_DK), lambda b, hh, i: (0, 0, hh)),
           pl.BlockSpec((TILE, 2 * TILE), lambda b, hh, i: (0, 0))]
    )
    args = ([gla_ops] * 3 + [zmain, gtot]) * 2 + [zmain_m] * 3 + [wd, bias, tg_m]
    return pl.pallas_call(
        functools.partial(_gla_kernel, n_tiles=n_tiles),
        name="gla_scan",
        grid=(batch, GLA_HEADS, nb),
        in_specs=in_specs,
        out_specs=pl.BlockSpec((seq, GLA_DV), lambda b, hh, i: (b, hh)),
        out_shape=jax.ShapeDtypeStruct((batch * seq, GLA_WIDTH), BF16),
        scratch_shapes=[pltpu.VMEM((GLA_DK, GLA_DV), F32), pltpu.VMEM((GLA_DK, GLA_DV), F32)],
        compiler_params=pltpu.CompilerParams(
            dimension_semantics=("parallel", "parallel", "arbitrary"), vmem_limit_bytes=VMEM_LIMIT),
    )(*args)


NA_SEG_COLS = ((0, 24), (24, 40), (40, 64))
NA_SEG_ROW0 = (0, 48, 80)
NA_STACK = 2 * GRID_W


NA_BASE_MID = NA_WIN_H // 2 - 1
NA_VARIANTS = 2 * NA_WIN_H - 1


def _na_variant(base, a):
    return jnp.where(base == NA_BASE_MID, a, jnp.where(base < NA_BASE_MID, NA_WIN_H + base, NA_WIN_H - 1 + base))


def _na_build_bias(rpb_ref, tab_ref):
    def build(v, carry):
        a = jnp.where(v < NA_WIN_H, v, 0)
        base = jnp.where(v < NA_WIN_H, NA_BASE_MID,
                         jnp.where(v < NA_WIN_H + NA_BASE_MID, v - NA_WIN_H, v - (NA_WIN_H - 1)))
        for hh in range(2):
            bias_rows = [rpb_ref[hh, pl.ds(base + ((t - a + NA_WIN_H) & (NA_WIN_H - 1)), 1), :]
                         for t in range(NA_WIN_H)]
            for s, (c0, c1) in enumerate(NA_SEG_COLS):
                n = c1 - c0
                vrows = [jnp.broadcast_to(br, (n, LANES)) for br in bias_rows]
                row0 = NA_SEG_ROW0[s] + hh * n
                rho = lax.broadcasted_iota(jnp.int32, (n, LANES), 0)
                lane = lax.broadcasted_iota(jnp.int32, (n, LANES), 1)
                cs = jnp.clip(c0 + rho - NA_WIN_W // 2, 0, GRID_W - NA_WIN_W)
                for half in range(2):
                    cb = s + half
                    cp = NA_CBW * cb + lane % NA_CBW
                    valid = (cp >= cs) & (cp < cs + NA_WIN_W)
                    acc = None
                    for i in range(NA_WIN_H):
                        shift = (NA_CBW * i - NA_CBW * cb + c0 - (NA_WIN_W - 1)) % LANES
                        rolled = pltpu.roll(vrows[i], shift, 1, stride=1, stride_axis=0)
                        acc = rolled if acc is None else jnp.where(lane // NA_CBW == i, rolled, acc)
                    tab_ref[v, row0:row0 + n, half * LANES:(half + 1) * LANES] = jnp.where(valid, acc, NEG_BIG)
        return carry

    lax.fori_loop(0, NA_VARIANTS, build, 0)


def _na_kernel(q_ref, kt_ref, v_ref, gate_ref, km_ref, vm_ref, rpb_ref, mb_ref, o_ref, tab_ref, sc_ref, sm_ref,
               *, rows_per_step, n_rows):
    w = GRID_W

    @pl.when(pl.program_id(1) == 0)
    def _():
        _na_build_bias(rpb_ref, tab_ref)

    head0_q = lax.broadcasted_iota(jnp.int32, (w, LANES), 1) < NA_DH
    srow = lax.broadcasted_iota(jnp.int32, (NA_STACK, N_META), 0)
    is_h1 = jnp.zeros((NA_STACK, N_META), jnp.bool_)
    for s, (c0, c1) in enumerate(NA_SEG_COLS):
        lo = NA_SEG_ROW0[s] + (c1 - c0)
        is_h1 = is_h1 | ((srow >= lo) & (srow < lo + (c1 - c0)))
    mb = jnp.where(is_h1, mb_ref[1], mb_ref[0])
    km = km_ref[...]
    vm = vm_ref[...]
    seg_rows = [slice(NA_SEG_ROW0[s], NA_SEG_ROW0[s] + 2 * (c1 - c0)) for s, (c0, c1) in enumerate(NA_SEG_COLS)]
    blk_rows = [slice(seg_rows[max(cb - 1, 0)].start, seg_rows[min(cb, len(seg_rows) - 1)].stop)
                for cb in range(NA_CB)]

    n_tiles = n_rows // NA_WIN_H
    key_lane = lax.broadcasted_iota(jnp.int32, (LANES, LANES), 1)

    def window(rl):
        row = pl.program_id(1) * rows_per_step + rl
        b = row // n_rows
        r = row - b * n_rows
        rs = jnp.clip(r - NA_WIN_H // 2, 0, n_rows - NA_WIN_H)
        a = rs & (NA_WIN_H - 1)
        return (b * n_tiles + rs // NA_WIN_H, a, (b + 1) * n_tiles - 1,
                _na_variant(rs - r + (NA_WIN_H - 1), a))

    def scores_store(slot, j, rl):
        m, a, m_last, var = window(rl)
        q = q_ref[pl.ds(pl.multiple_of(rl * w, w), w), :].astype(F32)
        q0 = jnp.where(head0_q, q, 0.0)
        q1 = q - q0
        qs = jnp.concatenate([piece[c0:c1] for c0, c1 in NA_SEG_COLS for piece in (q0, q1)],
                             axis=0).astype(BF16)
        from_m = key_lane >= a * NA_CBW
        m1 = jnp.minimum(m + 1, m_last)
        blocks = [_dot(qs[blk_rows[cb]],
                       jnp.where(from_m, kt_ref[NA_CB * m + cb], kt_ref[NA_CB * m1 + cb]))
                  for cb in range(NA_CB)]
        for s in range(len(NA_SEG_COLS)):
            n2 = seg_rows[s].stop - seg_rows[s].start
            for half in range(2):
                cb = s + half
                a0 = seg_rows[s].start - blk_rows[cb].start
                lanes = slice(half * LANES, (half + 1) * LANES)
                sc_ref[slot, j, seg_rows[s], lanes] = blocks[cb][a0:a0 + n2] + tab_ref[var, seg_rows[s], lanes]
        sm_ref[slot, j] = _dot_nt(qs, km) + mb

    def value_block(cb, m, a):
        groups = []
        for t in range(NA_WIN_H):
            row = NA_WIN_H * m + t + jnp.where(t < a, NA_WIN_H, 0)
            groups.append(v_ref[cb, pl.ds(pl.multiple_of(row * NA_CBW, NA_CBW), NA_CBW), :])
        return jnp.concatenate(groups, axis=0)

    def softmax_values(slot, j, rl):
        m_tile, a, _, _ = window(rl)
        sc = sc_ref[slot, j]
        sm = sm_ref[slot, j]
        m = jnp.maximum(sc[:, :LANES], sc[:, LANES:])
        m = jnp.maximum(jnp.max(m, axis=-1, keepdims=True), jnp.max(sm, axis=-1, keepdims=True))
        p = jnp.exp2(sc - m)
        pm = jnp.exp2(sm - m)
        den = (jnp.sum(p[:, :LANES] + p[:, LANES:], axis=-1, keepdims=True)
               + jnp.sum(pm, axis=-1, keepdims=True))
        p = p.astype(BF16)
        vals = [value_block(cb, m_tile, a) for cb in range(NA_CB)]
        o_meta = _dot(pm.astype(BF16), vm)
        inv = 1.0 / den
        outs = []
        for s, (c0, c1) in enumerate(NA_SEG_COLS):
            o_loc = _dot(p[seg_rows[s]], jnp.concatenate([vals[s], vals[s + 1]], axis=0))
            o = (o_loc + o_meta[seg_rows[s]]) * inv[seg_rows[s]]
            n = c1 - c0
            head0 = lax.broadcasted_iota(jnp.int32, (n, LANES), 1) < NA_DH
            outs.append(jnp.where(head0, o[:n], o[n:]))
        o = jnp.concatenate(outs, axis=0)
        qoff = pl.multiple_of(rl * w, w)
        gate = gate_ref[pl.ds(qoff, w), :].astype(F32)
        o_ref[pl.ds(qoff, w), :] = (o * _silu(gate)).astype(o_ref.dtype)

    n_groups = rows_per_step // NA_ROW_GROUP
    for j in range(NA_ROW_GROUP):
        scores_store(0, j, j)

    def trip(g, carry):
        slot = g % 2
        for j in range(NA_ROW_GROUP):
            scores_store(slot, j, g * NA_ROW_GROUP + j)
        for j in range(NA_ROW_GROUP):
            softmax_values(1 - slot, j, (g - 1) * NA_ROW_GROUP + j)
        return carry

    lax.fori_loop(1, n_groups, trip, 0)
    for j in range(NA_ROW_GROUP):
        softmax_values((n_groups - 1) % 2, j, (n_groups - 1) * NA_ROW_GROUP + j)


def _na_call(nq, nk_t, nv_cb, zmain, nk_m, nv_m, rpb_pad, meta_bias, *, batch, seq, rows_per_step):
    n_rows = seq // GRID_W
    total_rows = batch * n_rows
    assert n_rows % NA_WIN_H == 0 and total_rows % rows_per_step == 0 and rows_per_step % NA_ROW_GROUP == 0
    nsteps = total_rows // rows_per_step
    hp = NA_HEADS // 2
    tq = rows_per_step * GRID_W
    n_idx = 2 * NA_WIN_H - 1
    in_specs = [
        pl.BlockSpec((None, tq, LANES), lambda p, i: (p, i, 0)),
        pl.BlockSpec((batch * seq // LANES, LANES, LANES), lambda p, i: (0, p, 0)),
        pl.BlockSpec((None, NA_CB, batch * seq // NA_CB, LANES), lambda p, i: (p, 0, 0, 0)),
        pl.BlockSpec((tq, LANES), lambda p, i: (i, COL_NG // LANES + p)),
        pl.BlockSpec((N_META, LANES), lambda p, i: (0, p)),
        pl.BlockSpec((N_META, LANES), lambda p, i: (0, p)),
        pl.BlockSpec((2, n_idx, LANES), lambda p, i: (p, 0, 0)),
        pl.BlockSpec((2, 1, N_META), lambda p, i: (p, 0, 0)),
    ]
    return pl.pallas_call(
        functools.partial(_na_kernel, rows_per_step=rows_per_step, n_rows=n_rows),
        name="na_attn",
        grid=(hp, nsteps),
        in_specs=in_specs,
        out_specs=pl.BlockSpec((None, tq, LANES), lambda p, i: (p, i, 0)),
        out_shape=jax.ShapeDtypeStruct((hp, batch * seq, LANES), BF16),
        scratch_shapes=[pltpu.VMEM((NA_VARIANTS, NA_STACK, 2 * LANES), F32),
                        pltpu.VMEM((2, NA_ROW_GROUP, NA_STACK, 2 * LANES), F32),
                        pltpu.VMEM((2, NA_ROW_GROUP, NA_STACK, N_META), F32)],
        compiler_params=pltpu.CompilerParams(
            dimension_semantics=("arbitrary", "arbitrary"), vmem_limit_bytes=VMEM_LIMIT),
    )(nq, nk_t, nv_cb, zmain, nk_m, nv_m, rpb_pad, meta_bias)


def _out_kernel(x_ref, os_ref, gg_ref, na_ref, w_ref, o_ref):
    d = o_ref.shape[1]
    na = jnp.concatenate([na_ref[p] for p in range(na_ref.shape[0])], axis=1)

    def na_tile(c):
        cols = slice(c * OUT_TN, (c + 1) * OUT_TN)
        o_ref[:, cols] = x_ref[:, cols] + _dot(na, w_ref[GLA_WIDTH:, cols])

    def gla_head(h):
        cols = slice(h * GLA_DV, (h + 1) * GLA_DV)
        sh = os_ref[:, cols].astype(F32)
        ms = jnp.mean(sh * sh, axis=-1, keepdims=True)
        return (sh * lax.rsqrt(ms + RMS_EPS) * _silu(gg_ref[:, cols].astype(F32))).astype(BF16)

    for c in range(d // OUT_TN):
        na_tile(c)
    for h in range(GLA_HEADS):
        y = gla_head(h)
        for c in range(d // OUT_TN):
            cols = slice(c * OUT_TN, (c + 1) * OUT_TN)
            o_ref[:, cols] += _dot(y, w_ref[h * GLA_DV:(h + 1) * GLA_DV, cols])


def _out_call(x2d, o_sum, zmain, o_na, w_out, *, tm):
    m, d = x2d.shape
    assert m % tm == 0 and d % OUT_TN == 0
    return pl.pallas_call(
        _out_kernel,
        name="out_proj",
        grid=(m // tm,),
        in_specs=[
            pl.BlockSpec((tm, d), lambda i: (i, 0)),
            pl.BlockSpec((tm, GLA_WIDTH), lambda i: (i, 0)),
            pl.BlockSpec((tm, GLA_WIDTH), lambda i: (i, COL_GG // GLA_WIDTH)),
            pl.BlockSpec((NA_WIDTH // LANES, tm, LANES), lambda i: (0, i, 0)),
            pl.BlockSpec((GLA_WIDTH + NA_WIDTH, d), lambda i: (0, 0), pipeline_mode=pl.Buffered(1)),
        ],
        out_specs=pl.BlockSpec((tm, d), lambda i: (i, 0)),
        out_shape=jax.ShapeDtypeStruct((m, d), F32),
        compiler_params=pltpu.CompilerParams(
            dimension_semantics=("parallel",), vmem_limit_bytes=VMEM_LIMIT),
    )(x2d, o_sum, zmain, o_na, w_out)


def _pick(n, cands):
    for c in cands:
        if n % c == 0:
            return c
    raise ValueError(f"no tile for {n}")


def kernel(x, meta_tokens, norm_g, w_in, w_decay_fwd, b_decay_fwd, w_decay_bwd, b_decay_bwd,
           gla_out_norm_g, q_norm_g, k_norm_g, rpb, meta_bias, w_out):
    batch, seq, d = x.shape
    assert d == D_MODEL and seq % TILE == 0 and seq % GRID_W == 0
    depth = norm_g.shape[0]
    assert depth == 1
    l = 0
    x2d = x.reshape(batch * seq, d)
    g = norm_g[l].reshape(1, d)

    assert w_in.shape[1:] == (d, D_IN_PROJ)
    wt = jnp.swapaxes(w_in[l], 0, 1).astype(BF16)
    gain_qk = jnp.concatenate([jnp.tile(q_norm_g[l], NA_HEADS) * (NA_DH ** -0.5 * LOG2_E),
                               jnp.tile(k_norm_g[l], NA_HEADS)]).reshape(1, 2 * NA_WIDTH)

    zr = jnp.zeros((GLA_RANK, GLA_KEY_WIDTH), F32)
    ztail = jnp.zeros((LANES - 2 * GLA_RANK, GLA_KEY_WIDTH), F32)
    wd = jnp.stack([jnp.concatenate([w_decay_fwd[l], zr, ztail], axis=0),
                    jnp.concatenate([zr, w_decay_bwd[l], ztail], axis=0)]).astype(BF16)
    bias = jnp.stack([b_decay_fwd[l], b_decay_bwd[l]]).reshape(2, 1, GLA_KEY_WIDTH)
    tg, tg_m = _gla_constants()

    assert seq % (NA_WIN_H * GRID_W) == 0
    kgain_col = jnp.tile(k_norm_g[l], NA_HEADS).reshape(NA_WIDTH, 1)
    zmain, gla_ops, nq, nk_t, nv_cb, gtot, zmain_m_pad, nk_m, nv_m = _proj_call(
        x2d, meta_tokens, g, wt, gain_qk, kgain_col, wd, bias, tg, tm=NA_WIN_H * GRID_W)

    o_sum = _gla_call(zmain, gla_ops, gtot, zmain_m_pad, wd, bias, tg_m, batch=batch, seq=seq)

    rpb_pad = jnp.pad(rpb[l] * LOG2_E, ((0, 0), (0, 0), (0, 2 * NA_DH - rpb.shape[-1])))
    mb = (meta_bias[l] * LOG2_E).reshape(NA_HEADS, 1, N_META)
    o_na = _na_call(nq, nk_t, nv_cb, zmain, nk_m, nv_m, rpb_pad, mb, batch=batch, seq=seq,
                    rows_per_step=_pick(batch * seq // GRID_W, (256, 128, 64, 32, 16)))

    gn_rows = jnp.concatenate([jnp.tile(gla_out_norm_g[l], GLA_HEADS), jnp.ones((NA_WIDTH,), F32)])
    w_o = (w_out[l] * gn_rows[:, None]).astype(BF16)
    out = _out_call(x2d, o_sum, zmain, o_na, w_o, tm=_pick(batch * seq, (1024, 512, 256, 128)))
    return out.reshape(batch, seq, d)
```

```python
import functools

import jax
import jax.numpy as jnp
import numpy as np
from jax import lax
from jax.experimental import pallas as pl
from jax.experimental.pallas import tpu as pltpu

F32 = jnp.float32
BF16 = jnp.bfloat16

D_MODEL = 1024
N_META = 16
GRID_W = 64
GLA_HEADS = 4
GLA_DK = 128
GLA_DV = 256
GLA_KEY_WIDTH = GLA_HEADS * GLA_DK
GLA_WIDTH = GLA_HEADS * GLA_DV
GLA_RANK = 16
GLA_TAU = 16.0
GLA_CHUNK = 64
NA_HEADS = 16
NA_DH = 64
NA_WIDTH = NA_HEADS * NA_DH
NA_WIN_H = 8
NA_WIN_W = 16
RMS_EPS = 1e-6
NEG_BIG = -1e30
LOG2_E = 1.4426950408889634
LANES = 128

COL_GV, COL_GG, COL_NG, COL_GQ, COL_GK, COL_R = 0, 1024, 2048, 3072, 3584, 4096
MAIN_W = COL_R + LANES
MAIN_TOK_W = COL_GQ
GLA_Q_IN, GLA_K_IN, GLA_K_ST = 0, 1, 2
NA_CBW = 16
NA_CB = GRID_W // NA_CBW
IN_GQ, IN_GK, IN_GV, IN_R, IN_GG, IN_NQ, IN_NK, IN_NV, IN_NG = 0, 512, 1024, 2048, 2080, 3104, 4128, 5152, 6176
D_IN_PROJ = 7200
PROJ_TN = 512
OUT_TN = 256

TILE = 2 * GLA_CHUNK
NA_ROW_GROUP = 16
VMEM_LIMIT = 58 * 1024 * 1024


def _split_bf16(x):
    hi = x.astype(BF16)
    lo = (x - hi.astype(F32)).astype(BF16)
    return hi, lo


def _dot(a, b):
    return jnp.dot(a, b, preferred_element_type=F32)


def _dot_nt(a, b):
    return lax.dot_general(a, b, (((1,), (1,)), ((), ())), preferred_element_type=F32)


def _dot_tn(a, b):
    return lax.dot_general(a, b, (((0,), (0,)), ((), ())), preferred_element_type=F32)


def _log_sigmoid(x):
    return jnp.minimum(x, 0.0) - jnp.log(1.0 + jnp.exp(-jnp.abs(x)))


def _silu(x):
    h = 0.5 * x
    return h + h * jnp.tanh(h)


SHARED_SEGMENTS = ((COL_GV, IN_GV, 1024), (COL_GG, IN_GG, 1024), (COL_NG, IN_NG, 1024))
META_SEGMENTS = ((COL_GQ, IN_GQ, 512), (COL_GK, IN_GK, 512),
                 (COL_R, IN_R, LANES))


def _store_rows(o_ref, c0, y):
    n = y.shape[1]
    for rl in range(y.shape[0] // GRID_W):
        for cb in range(NA_CB):
            src = rl * GRID_W + cb * NA_CBW
            for c in range(0, n, LANES):
                o_ref[(c0 + c) // LANES, cb, rl * NA_CBW:(rl + 1) * NA_CBW, :] = y[src:src + NA_CBW, c:c + LANES]


def _gla_log_decay(r, wd_ref, bdec_ref, d):
    return (_log_sigmoid(_dot(r, wd_ref[d]) + bdec_ref[d]) * (1.0 / GLA_TAU)).astype(BF16)


def _gla_operands_store(q, k, log_decay, d, t, tg_ref, go_ref, gtot_ref):
    rows = slice(t * TILE, (t + 1) * TILE)
    g = _dot(tg_ref[d], log_decay[rows])
    tots = []
    for c in range(TILE // GLA_CHUNK):
        last = c * GLA_CHUNK + (GLA_CHUNK - 1 if d == 0 else 0)
        tot = g[last:last + 1]
        gtot_ref[(t * TILE) // GLA_CHUNK + c:(t * TILE) // GLA_CHUNK + c + 1,
                 d * GLA_KEY_WIDTH:(d + 1) * GLA_KEY_WIDTH] = tot
        tots.append(jnp.broadcast_to(tot, (GLA_CHUNK, GLA_KEY_WIDTH)))
    dd = jnp.concatenate(tots, axis=0) - g
    operands = ((GLA_Q_IN, q[rows] * jnp.exp(g)), (GLA_K_IN, k[rows] * jnp.exp(-g)), (GLA_K_ST, k[rows] * jnp.exp(dd)))
    for kind, val in operands:
        val = val.astype(go_ref.dtype)
        for h in range(GLA_HEADS):
            go_ref[d, kind, h, rows, :] = val[:, h * GLA_DK:(h + 1) * GLA_DK]


def _proj_kernel(x0_ref, xn_ref, xm_ref, g_ref, wt_ref, gain_ref, kgain_ref, wd_ref, bdec_ref, tg_ref,
                 zm_ref, go_ref, nq_ref, nk_ref, nv_ref, gtot_ref, zmm_ref, nkm_ref, nvm_ref, u_scr):
    def normed(x_ref):
        x = x_ref[...]
        ms = jnp.mean(x * x, axis=-1, keepdims=True)
        return (x * lax.rsqrt(ms + RMS_EPS) * g_ref[...]).astype(BF16)

    def head_norm_rows(acc, goff):
        first = lax.broadcasted_iota(jnp.int32, (acc.shape[0], LANES), 1) < NA_DH
        cols = []
        for c in range(0, acc.shape[1], LANES):
            a = acc[:, c:c + LANES]
            sq = a * a
            lo = jnp.sum(jnp.where(first, sq, 0.0), axis=-1, keepdims=True)
            hi = jnp.sum(jnp.where(first, 0.0, sq), axis=-1, keepdims=True)
            inv = jnp.where(first, lax.rsqrt(lo * (1.0 / NA_DH) + RMS_EPS), lax.rsqrt(hi * (1.0 / NA_DH) + RMS_EPS))
            cols.append(a * inv)
        return (jnp.concatenate(cols, axis=1) * gain_ref[:, goff:goff + PROJ_TN]).astype(BF16)

    def meta_tokens():
        um = normed(xm_ref)
        zmm_ref[...] = jnp.zeros_like(zmm_ref)
        for off in range(0, NA_WIDTH, PROJ_TN):
            acc = _dot_nt(um, wt_ref[IN_NK + off:IN_NK + off + PROJ_TN, :])
            nkm_ref[:, off:off + PROJ_TN] = head_norm_rows(acc, NA_WIDTH + off)
            acc = _dot_nt(um, wt_ref[IN_NV + off:IN_NV + off + PROJ_TN, :])
            nvm_ref[:, off:off + PROJ_TN] = acc.astype(nvm_ref.dtype)
        for dst, src, width in SHARED_SEGMENTS + META_SEGMENTS:
            acc = _dot_nt(um, wt_ref[src:src + width, :])
            if dst == COL_GQ:
                acc = acc * (GLA_DK ** -0.5)
            zmm_ref[:N_META, dst:dst + width] = acc.astype(zmm_ref.dtype)

    step = pl.program_id(0)

    @pl.when(step == 0)
    def _():
        u_scr[0] = normed(x0_ref)
        meta_tokens()

    slot = step % 2
    u = u_scr[slot]
    tm = u.shape[0]

    def prepare_next():
        u_scr[1 - slot] = normed(xn_ref)

    def main_tile(dst, src, off, n):
        acc = _dot_nt(u, wt_ref[src + off:src + off + n, :])
        zm_ref[:, dst + off:dst + off + n] = acc.astype(zm_ref.dtype)

    def nv_tile(off):
        acc = _dot_nt(u, wt_ref[IN_NV + off:IN_NV + off + PROJ_TN, :])
        _store_rows(nv_ref, off, acc.astype(nv_ref.dtype))

    def nq_tile(off):
        acc = _dot_nt(u, wt_ref[IN_NQ + off:IN_NQ + off + PROJ_TN, :])
        y = head_norm_rows(acc, off)
        for c in range(0, PROJ_TN, LANES):
            nq_ref[(off + c) // LANES] = y[:, c:c + LANES]

    def nk_tile(off):
        u_cb = jnp.concatenate([u[rl * GRID_W + cb * NA_CBW:rl * GRID_W + (cb + 1) * NA_CBW]
                                for cb in range(NA_CB) for rl in range(tm // GRID_W)], axis=0)
        acc_t = _dot_nt(wt_ref[IN_NK + off:IN_NK + off + PROJ_TN, :], u_cb)
        for h in range(PROJ_TN // NA_DH):
            rows = slice(off + h * NA_DH, off + (h + 1) * NA_DH)
            blk = acc_t[h * NA_DH:(h + 1) * NA_DH]
            ms = jnp.sum(blk * blk, axis=0, keepdims=True) * (1.0 / NA_DH)
            y = (blk * lax.rsqrt(ms + RMS_EPS) * kgain_ref[rows, :]).astype(BF16)
            for cb in range(NA_CB):
                nk_ref[cb, rows, :] = y[:, cb * LANES:(cb + 1) * LANES]

    tiles = [functools.partial(nk_tile, off) for off in range(0, NA_WIDTH, PROJ_TN)]
    tiles += [functools.partial(nq_tile, off) for off in range(0, NA_WIDTH, PROJ_TN)]
    tiles += [functools.partial(nv_tile, off) for off in range(0, NA_WIDTH, PROJ_TN)]
    for dst, src, width in SHARED_SEGMENTS:
        for off in range(0, width, PROJ_TN):
            tiles.append(functools.partial(main_tile, dst, src, off, min(PROJ_TN, width - off)))

    r = _dot_nt(u, wt_ref[IN_R:IN_R + LANES, :]).astype(BF16)
    q = _dot_nt(u, wt_ref[IN_GQ:IN_GQ + GLA_KEY_WIDTH, :]) * (GLA_DK ** -0.5)
    k = _dot_nt(u, wt_ref[IN_GK:IN_GK + GLA_KEY_WIDTH, :])
    log_decay = [None, None]

    def decay_stage(d):
        log_decay[d] = _gla_log_decay(r, wd_ref, bdec_ref, d)

    stages = [functools.partial(decay_stage, 0), functools.partial(decay_stage, 1), None]
    stages += [functools.partial(lambda d, t: _gla_operands_store(q, k, log_decay[d], d, t, tg_ref, go_ref, gtot_ref),
                                 d, t)
               for d in range(2) for t in range(tm // TILE)]
    for i, tile in enumerate(tiles):
        if i < len(stages) and stages[i] is not None:
            stages[i]()
        tile()
    prepare_next()
    assert len(stages) <= len(tiles)


def _proj_call(x2d, xm, g, wt, gain_qk, kgain_col, wd, bdec, tg, *, tm):
    m, d = x2d.shape
    assert m % tm == 0 and wt.shape == (D_IN_PROJ, d) and xm.shape == (N_META, d)
    assert tm == NA_WIN_H * GRID_W and NA_CB * LANES == tm
    resident = lambda shape: pl.BlockSpec(shape, lambda i: (0,) * len(shape), pipeline_mode=pl.Buffered(1))
    whole = lambda shape: pl.BlockSpec(shape, lambda i: (0,) * len(shape))
    n_pairs = NA_WIDTH // LANES
    q_spec = pl.BlockSpec((n_pairs, tm, LANES), lambda i: (0, i, 0))
    q_shape = jax.ShapeDtypeStruct((n_pairs, m, LANES), BF16)
    v_spec = pl.BlockSpec((n_pairs, NA_CB, tm // NA_CB, LANES), lambda i: (0, 0, i, 0))
    v_shape = jax.ShapeDtypeStruct((n_pairs, NA_CB, m // NA_CB, LANES), BF16)
    k_spec = pl.BlockSpec((NA_CB, NA_WIDTH, LANES), lambda i: (i, 0, 0))
    k_shape = jax.ShapeDtypeStruct((m // LANES, NA_WIDTH, LANES), BF16)
    n_chunk_rows = tm // GLA_CHUNK
    meta_plain = jax.ShapeDtypeStruct((N_META, NA_WIDTH), BF16)
    return pl.pallas_call(
        _proj_kernel,
        name="proj_tokens",
        grid=(m // tm,),
        in_specs=[
            resident((tm, d)),
            pl.BlockSpec((tm, d), lambda i: (jnp.minimum(i + 1, m // tm - 1), 0)),
            resident((N_META, d)),
            resident((1, d)),
            resident((D_IN_PROJ, d)),
            resident((1, 2 * NA_WIDTH)),
            resident((NA_WIDTH, 1)),
            resident((2, LANES, GLA_KEY_WIDTH)),
            resident((2, 1, GLA_KEY_WIDTH)),
            resident((2, TILE, TILE)),
        ],
        out_specs=[pl.BlockSpec((tm, MAIN_TOK_W), lambda i: (i, 0)),
                   pl.BlockSpec((2, 3, GLA_HEADS, tm, GLA_DK), lambda i: (0, 0, 0, i, 0)),
                   q_spec, k_spec, v_spec,
                   pl.BlockSpec((n_chunk_rows, 2 * GLA_KEY_WIDTH), lambda i: (i, 0)),
                   whole((TILE, MAIN_W)), whole((N_META, NA_WIDTH)), whole((N_META, NA_WIDTH))],
        out_shape=[jax.ShapeDtypeStruct((m, MAIN_TOK_W), BF16),
                   jax.ShapeDtypeStruct((2, 3, GLA_HEADS, m, GLA_DK), BF16), q_shape, k_shape, v_shape,
                   jax.ShapeDtypeStruct((m // GLA_CHUNK, 2 * GLA_KEY_WIDTH), F32),
                   jax.ShapeDtypeStruct((TILE, MAIN_W), BF16), meta_plain, meta_plain],
        scratch_shapes=[pltpu.VMEM((2, tm, d), BF16)],
        compiler_params=pltpu.CompilerParams(
            dimension_semantics=("arbitrary",), vmem_limit_bytes=VMEM_LIMIT),
    )(x2d, x2d, xm, g, wt, gain_qk, kgain_col, wd, bdec, tg)


def _gla_kernel(qf_ref, kif_ref, ksf_ref, vf_ref, gtf_ref, qb_ref, kib_ref, ksb_ref, vb_ref, gtb_ref,
                km_ref, vm_ref, rm_ref, wd_ref, b_ref, tgm_ref,
                o_ref, sf_ref, sb_ref, *, n_tiles):
    row = lax.broadcasted_iota(jnp.int32, (TILE, TILE), 0)
    col = lax.broadcasted_iota(jnp.int32, (TILE, TILE), 1)
    same = (row // GLA_CHUNK) == (col // GLA_CHUNK)
    masks = (same & (row >= col), same & (row <= col))

    @pl.when(pl.program_id(2) == 0)
    def _():
        rowm = lax.broadcasted_iota(jnp.int32, (TILE, GLA_DK), 0)
        lg = _log_sigmoid(_dot(rm_ref[...], wd_ref[...]) + b_ref[...]) * (1.0 / GLA_TAU)
        hi, lo = _split_bf16(jnp.where(rowm < N_META, lg, 0.0))
        d_m = _dot(tgm_ref[...], jnp.concatenate([hi, lo], axis=0))
        k_st = (km_ref[...].astype(F32) * jnp.exp(d_m)).astype(BF16)
        sf_ref[...] = _dot_tn(k_st, vm_ref[...])
        sb_ref[...] = jnp.zeros_like(sb_ref)

    def scan(step):
        items = []
        for t in range(n_tiles):
            items.append((0, t))
            items.append((1, n_tiles - 1 - t))
        refs = ((qf_ref, kif_ref, ksf_ref, vf_ref), (qb_ref, kib_ref, ksb_ref, vb_ref))
        out_half = (step, 1 - step)
        a_cols = (jnp.exp(jnp.transpose(gtf_ref[...])), jnp.exp(jnp.transpose(gtb_ref[...])))

        def rows_of(t):
            return slice(t * TILE, (t + 1) * TILE)

        amats = [jnp.where(masks[d], _dot_nt(refs[d][0][rows_of(t), :], refs[d][1][rows_of(t), :]), 0.0).astype(BF16)
                 for d, t in items]

        def state_independent(idx):
            d, t = items[idx]
            v = refs[d][3][rows_of(t), :]
            k_st = refs[d][2][rows_of(t), :]
            u = [_dot_tn(k_st[h * GLA_CHUNK:(h + 1) * GLA_CHUNK], v[h * GLA_CHUNK:(h + 1) * GLA_CHUNK])
                 for h in range(2)]
            return _dot(amats[idx], v), u

        state = [sf_ref[...], sb_ref[...]]
        ahead = state_independent(0)
        for idx, (d, t) in enumerate(items):
            o_intra, u = ahead
            if idx + 1 < len(items):
                ahead = state_independent(idx + 1)
            s = state[d]
            outs = [None, None]
            for h in ((0, 1) if d == 0 else (1, 0)):
                rows = slice(t * TILE + h * GLA_CHUNK, t * TILE + (h + 1) * GLA_CHUNK)
                outs[h] = o_intra[h * GLA_CHUNK:(h + 1) * GLA_CHUNK] + _dot(refs[d][0][rows, :], s.astype(BF16))
                chunk = 2 * t + h
                s = a_cols[d][:, chunk:chunk + 1] * s + u[h]
            state[d] = s
            o = jnp.concatenate(outs, axis=0)
            out_rows = rows_of(out_half[d] * n_tiles + t)
            if step == 1:
                o = o + o_ref[out_rows, :].astype(F32)
            o_ref[out_rows, :] = o.astype(o_ref.dtype)
        sf_ref[...] = state[0]
        sb_ref[...] = state[1]

    for step in range(2):
        pl.when(pl.program_id(2) == step)(functools.partial(scan, step))


def _gla_constants():
    i = np.arange(TILE)
    same = (i[:, None] // GLA_CHUNK) == (i[None, :] // GLA_CHUNK)
    inc_f = same & (i[None, :] <= i[:, None])
    inc_b = same & (i[None, :] >= i[:, None])
    to_end_meta = i[None, :] > i[:, None]
    dup = lambda m: np.concatenate([m, m], axis=1)
    tg = jnp.asarray(np.stack([inc_f, inc_b]), BF16)
    return tg, jnp.asarray(dup(to_end_meta), BF16)


def _gla_call(zmain, gla_ops, gtot, zmain_m, wd, bias, tg_m, *, batch, seq):
    nb = 2
    assert seq % (nb * TILE) == 0
    t_blk = seq // nb
    n_tiles = t_blk // TILE
    n_chunks = t_blk // GLA_CHUNK

    def tok(fwd):
        if fwd:
            return lambda b, hh, i: b * nb + i
        return lambda b, hh, i: b * nb + (nb - 1 - i)

    def dir_specs(d):
        t = tok(d == 0)
        operand = lambda kind: pl.BlockSpec((None, None, None, t_blk, GLA_DK),
                                            lambda b, hh, i: (d, kind, hh, t(b, hh, i), 0))
        return [
            operand(GLA_Q_IN), operand(GLA_K_IN), operand(GLA_K_ST),
            pl.BlockSpec((t_blk, GLA_DV), lambda b, hh, i: (t(b, hh, i), COL_GV // GLA_DV + hh)),
            pl.BlockSpec((n_chunks, GLA_DK), lambda b, hh, i: (t(b, hh, i), d * GLA_HEADS + hh)),
        ]

    in_specs = (
        dir_specs(0) + dir_specs(1)
        + [pl.BlockSpec((TILE, GLA_DK), lambda b, hh, i: (0, COL_GK // GLA_DK + hh)),
           pl.BlockSpec((TILE, GLA_DV), lambda b, hh, i: (0, COL_GV // GLA_DV + hh)),
           pl.BlockSpec((TILE, LANES), lambda b, hh, i: (0, COL_R // LANES)),
           pl.BlockSpec((None, LANES, GLA_DK), lambda b, hh, i: (0, 0, hh)),
           pl.BlockSpec((None, 1, GLA_DK), lambda b, hh, i: (0, 0, hh)),
           pl.BlockSpec((TILE, 2 * TILE), lambda b, hh, i: (0, 0))]
    )
    args = ([gla_ops] * 3 + [zmain, gtot]) * 2 + [zmain_m] * 3 + [wd, bias, tg_m]
    return pl.pallas_call(
        functools.partial(_gla_kernel, n_tiles=n_tiles),
        name="gla_scan",
        grid=(batch, GLA_HEADS, nb),
        in_specs=in_specs,
        out_specs=pl.BlockSpec((seq, GLA_DV), lambda b, hh, i: (b, hh)),
        out_shape=jax.ShapeDtypeStruct((batch * seq, GLA_WIDTH), BF16),
        scratch_shapes=[pltpu.VMEM((GLA_DK, GLA_DV), F32), pltpu.VMEM((GLA_DK, GLA_DV), F32)],
        compiler_params=pltpu.CompilerParams(
            dimension_semantics=("parallel", "parallel", "arbitrary"), vmem_limit_bytes=VMEM_LIMIT),
    )(*args)


NA_SEG_COLS = ((0, 24), (24, 40), (40, 64))
NA_SEG_ROW0 = (0, 48, 80)
NA_STACK = 2 * GRID_W


NA_BASE_MID = NA_WIN_H // 2 - 1
NA_VARIANTS = 2 * NA_WIN_H - 1


def _na_variant(base, a):
    return jnp.where(base == NA_BASE_MID, a, jnp.where(base < NA_BASE_MID, NA_WIN_H + base, NA_WIN_H - 1 + base))


def _na_build_bias(rpb_ref, tab_ref):
    def build(v, carry):
        a = jnp.where(v < NA_WIN_H, v, 0)
        base = jnp.where(v < NA_WIN_H, NA_BASE_MID,
                         jnp.where(v < NA_WIN_H + NA_BASE_MID, v - NA_WIN_H, v - (NA_WIN_H - 1)))
        for hh in range(2):
            bias_rows = [rpb_ref[hh, pl.ds(base + ((t - a + NA_WIN_H) & (NA_WIN_H - 1)), 1), :]
                         for t in range(NA_WIN_H)]
            for s, (c0, c1) in enumerate(NA_SEG_COLS):
                n = c1 - c0
                vrows = [jnp.broadcast_to(br, (n, LANES)) for br in bias_rows]
                row0 = NA_SEG_ROW0[s] + hh * n
                rho = lax.broadcasted_iota(jnp.int32, (n, LANES), 0)
                lane = lax.broadcasted_iota(jnp.int32, (n, LANES), 1)
                cs = jnp.clip(c0 + rho - NA_WIN_W // 2, 0, GRID_W - NA_WIN_W)
                for half in range(2):
                    cb = s + half
                    cp = NA_CBW * cb + lane % NA_CBW
                    valid = (cp >= cs) & (cp < cs + NA_WIN_W)
                    acc = None
                    for i in range(NA_WIN_H):
                        shift = (NA_CBW * i - NA_CBW * cb + c0 - (NA_WIN_W - 1)) % LANES
                        rolled = pltpu.roll(vrows[i], shift, 1, stride=1, stride_axis=0)
                        acc = rolled if acc is None else jnp.where(lane // NA_CBW == i, rolled, acc)
                    tab_ref[v, row0:row0 + n, half * LANES:(half + 1) * LANES] = jnp.where(valid, acc, NEG_BIG)
        return carry

    lax.fori_loop(0, NA_VARIANTS, build, 0)


def _na_kernel(q_ref, kt_ref, v_ref, gate_ref, km_ref, vm_ref, rpb_ref, mb_ref, o_ref, tab_ref, sc_ref, sm_ref,
               *, rows_per_step, n_rows):
    w = GRID_W

    @pl.when(pl.program_id(1) == 0)
    def _():
        _na_build_bias(rpb_ref, tab_ref)

    head0_q = lax.broadcasted_iota(jnp.int32, (w, LANES), 1) < NA_DH
    srow = lax.broadcasted_iota(jnp.int32, (NA_STACK, N_META), 0)
    is_h1 = jnp.zeros((NA_STACK, N_META), jnp.bool_)
    for s, (c0, c1) in enumerate(NA_SEG_COLS):
        lo = NA_SEG_ROW0[s] + (c1 - c0)
        is_h1 = is_h1 | ((srow >= lo) & (srow < lo + (c1 - c0)))
    mb = jnp.where(is_h1, mb_ref[1], mb_ref[0])
    km = km_ref[...]
    vm = vm_ref[...]
    seg_rows = [slice(NA_SEG_ROW0[s], NA_SEG_ROW0[s] + 2 * (c1 - c0)) for s, (c0, c1) in enumerate(NA_SEG_COLS)]
    blk_rows = [slice(seg_rows[max(cb - 1, 0)].start, seg_rows[min(cb, len(seg_rows) - 1)].stop)
                for cb in range(NA_CB)]

    n_tiles = n_rows // NA_WIN_H
    key_lane = lax.broadcasted_iota(jnp.int32, (LANES, LANES), 1)

    def window(rl):
        row = pl.program_id(1) * rows_per_step + rl
        b = row // n_rows
        r = row - b * n_rows
        rs = jnp.clip(r - NA_WIN_H // 2, 0, n_rows - NA_WIN_H)
        a = rs & (NA_WIN_H - 1)
        return (b * n_tiles + rs // NA_WIN_H, a, (b + 1) * n_tiles - 1,
                _na_variant(rs - r + (NA_WIN_H - 1), a))

    def scores_store(slot, j, rl):
        m, a, m_last, var = window(rl)
        q = q_ref[pl.ds(pl.multiple_of(rl * w, w), w), :].astype(F32)
        q0 = jnp.where(head0_q, q, 0.0)
        q1 = q - q0
        qs = jnp.concatenate([piece[c0:c1] for c0, c1 in NA_SEG_COLS for piece in (q0, q1)],
                             axis=0).astype(BF16)
        from_m = key_lane >= a * NA_CBW
        m1 = jnp.minimum(m + 1, m_last)
        blocks = [_dot(qs[blk_rows[cb]],
                       jnp.where(from_m, kt_ref[NA_CB * m + cb], kt_ref[NA_CB * m1 + cb]))
                  for cb in range(NA_CB)]
        for s in range(len(NA_SEG_COLS)):
            n2 = seg_rows[s].stop - seg_rows[s].start
            for half in range(2):
                cb = s + half
                a0 = seg_rows[s].start - blk_rows[cb].start
                lanes = slice(half * LANES, (half + 1) * LANES)
                sc_ref[slot, j, seg_rows[s], lanes] = blocks[cb][a0:a0 + n2] + tab_ref[var, seg_rows[s], lanes]
        sm_ref[slot, j] = _dot_nt(qs, km) + mb

    def value_block(cb, m, a):
        groups = []
        for t in range(NA_WIN_H):
            row = NA_WIN_H * m + t + jnp.where(t < a, NA_WIN_H, 0)
            groups.append(v_ref[cb, pl.ds(pl.multiple_of(row * NA_CBW, NA_CBW), NA_CBW), :])
        return jnp.concatenate(groups, axis=0)

    def softmax_values(slot, j, rl):
        m_tile, a, _, _ = window(rl)
        sc = sc_ref[slot, j]
        sm = sm_ref[slot, j]
        m = jnp.maximum(sc[:, :LANES], sc[:, LANES:])
        m = jnp.maximum(jnp.max(m, axis=-1, keepdims=True), jnp.max(sm, axis=-1, keepdims=True))
        p = jnp.exp2(sc - m)
        pm = jnp.exp2(sm - m)
        den = (jnp.sum(p[:, :LANES] + p[:, LANES:], axis=-1, keepdims=True)
               + jnp.sum(pm, axis=-1, keepdims=True))
        p = p.astype(BF16)
        vals = [value_block(cb, m_tile, a) for cb in range(NA_CB)]
        o_meta = _dot(pm.astype(BF16), vm)
        inv = 1.0 / den
        outs = []
        for s, (c0, c1) in enumerate(NA_SEG_COLS):
            o_loc = _dot(p[seg_rows[s]], jnp.concatenate([vals[s], vals[s + 1]], axis=0))
            o = (o_loc + o_meta[seg_rows[s]]) * inv[seg_rows[s]]
            n = c1 - c0
            head0 = lax.broadcasted_iota(jnp.int32, (n, LANES), 1) < NA_DH
            outs.append(jnp.where(head0, o[:n], o[n:]))
        o = jnp.concatenate(outs, axis=0)
        qoff = pl.multiple_of(rl * w, w)
        gate = gate_ref[pl.ds(qoff, w), :].astype(F32)
        o_ref[pl.ds(qoff, w), :] = (o * _silu(gate)).astype(o_ref.dtype)

    n_groups = rows_per_step // NA_ROW_GROUP
    for j in range(NA_ROW_GROUP):
        scores_store(0, j, j)

    def trip(g, carry):
        slot = g % 2
        for j in range(NA_ROW_GROUP):
            scores_store(slot, j, g * NA_ROW_GROUP + j)
        for j in range(NA_ROW_GROUP):
            softmax_values(1 - slot, j, (g - 1) * NA_ROW_GROUP + j)
        return carry

    lax.fori_loop(1, n_groups, trip, 0)
    for j in range(NA_ROW_GROUP):
        softmax_values((n_groups - 1) % 2, j, (n_groups - 1) * NA_ROW_GROUP + j)


def _na_call(nq, nk_t, nv_cb, zmain, nk_m, nv_m, rpb_pad, meta_bias, *, batch, seq, rows_per_step):
    n_rows = seq // GRID_W
    total_rows = batch * n_rows
    assert n_rows % NA_WIN_H == 0 and total_rows % rows_per_step == 0 and rows_per_step % NA_ROW_GROUP == 0
    nsteps = total_rows // rows_per_step
    hp = NA_HEADS // 2
    tq = rows_per_step * GRID_W
    n_idx = 2 * NA_WIN_H - 1
    in_specs = [
        pl.BlockSpec((None, tq, LANES), lambda p, i: (p, i, 0)),
        pl.BlockSpec((batch * seq // LANES, LANES, LANES), lambda p, i: (0, p, 0)),
        pl.BlockSpec((None, NA_CB, batch * seq // NA_CB, LANES), lambda p, i: (p, 0, 0, 0)),
        pl.BlockSpec((tq, LANES), lambda p, i: (i, COL_NG // LANES + p)),
        pl.BlockSpec((N_META, LANES), lambda p, i: (0, p)),
        pl.BlockSpec((N_META, LANES), lambda p, i: (0, p)),
        pl.BlockSpec((2, n_idx, LANES), lambda p, i: (p, 0, 0)),
        pl.BlockSpec((2, 1, N_META), lambda p, i: (p, 0, 0)),
    ]
    return pl.pallas_call(
        functools.partial(_na_kernel, rows_per_step=rows_per_step, n_rows=n_rows),
        name="na_attn",
        grid=(hp, nsteps),
        in_specs=in_specs,
        out_specs=pl.BlockSpec((None, tq, LANES), lambda p, i: (p, i, 0)),
        out_shape=jax.ShapeDtypeStruct((hp, batch * seq, LANES), BF16),
        scratch_shapes=[pltpu.VMEM((NA_VARIANTS, NA_STACK, 2 * LANES), F32),
                        pltpu.VMEM((2, NA_ROW_GROUP, NA_STACK, 2 * LANES), F32),
                        pltpu.VMEM((2, NA_ROW_GROUP, NA_STACK, N_META), F32)],
        compiler_params=pltpu.CompilerParams(
            dimension_semantics=("arbitrary", "arbitrary"), vmem_limit_bytes=VMEM_LIMIT),
    )(nq, nk_t, nv_cb, zmain, nk_m, nv_m, rpb_pad, meta_bias)


def _out_kernel(x_ref, os_ref, gg_ref, na_ref, w_ref, o_ref):
    n_col = o_ref.shape[1] // OUT_TN
    col = lambda c: slice(c * OUT_TN, (c + 1) * OUT_TN)
    na = jnp.concatenate([na_ref[p] for p in range(na_ref.shape[0])], axis=1)

    def na_part(c):
        return _dot(na, w_ref[GLA_WIDTH:, col(c)])

    def gla_head(h):
        cols = slice(h * GLA_DV, (h + 1) * GLA_DV)
        sh = os_ref[:, cols].astype(F32)
        ms = jnp.mean(sh * sh, axis=-1, keepdims=True)
        return (sh * lax.rsqrt(ms + RMS_EPS) * _silu(gg_ref[:, cols].astype(F32))).astype(BF16)

    o_ref[:, col(0)] = x_ref[:, col(0)] + na_part(0)
    for h in range(GLA_HEADS):
        y = gla_head(h)
        for c in range(n_col):
            base = x_ref[:, col(c)] if (h == 0 and c > 0) else o_ref[:, col(c)]
            o_ref[:, col(c)] = base + _dot(y, w_ref[h * GLA_DV:(h + 1) * GLA_DV, col(c)])
    for c in range(1, n_col):
        o_ref[:, col(c)] += na_part(c)


def _out_call(x2d, o_sum, zmain, o_na, w_out, *, tm):
    m, d = x2d.shape
    assert m % tm == 0 and d % OUT_TN == 0
    return pl.pallas_call(
        _out_kernel,
        name="out_proj",
        grid=(m // tm,),
        in_specs=[
            pl.BlockSpec((tm, d), lambda i: (i, 0)),
            pl.BlockSpec((tm, GLA_WIDTH), lambda i: (i, 0)),
            pl.BlockSpec((tm, GLA_WIDTH), lambda i: (i, COL_GG // GLA_WIDTH)),
            pl.BlockSpec((NA_WIDTH // LANES, tm, LANES), lambda i: (0, i, 0)),
            pl.BlockSpec((GLA_WIDTH + NA_WIDTH, d), lambda i: (0, 0), pipeline_mode=pl.Buffered(1)),
        ],
        out_specs=pl.BlockSpec((tm, d), lambda i: (i, 0)),
        out_shape=jax.ShapeDtypeStruct((m, d), F32),
        compiler_params=pltpu.CompilerParams(
            dimension_semantics=("parallel",), vmem_limit_bytes=VMEM_LIMIT),
    )(x2d, o_sum, zmain, o_na, w_out)


def _pick(n, cands):
    for c in cands:
        if n % c == 0:
            return c
    raise ValueError(f"no tile for {n}")


def kernel(x, meta_tokens, norm_g, w_in, w_decay_fwd, b_decay_fwd, w_decay_bwd, b_decay_bwd,
           gla_out_norm_g, q_norm_g, k_norm_g, rpb, meta_bias, w_out):
    batch, seq, d = x.shape
    assert d == D_MODEL and seq % TILE == 0 and seq % GRID_W == 0
    depth = norm_g.shape[0]
    assert depth == 1
    l = 0
    x2d = x.reshape(batch * seq, d)
    g = norm_g[l].reshape(1, d)

    assert w_in.shape[1:] == (d, D_IN_PROJ)
    wt = jnp.swapaxes(w_in[l], 0, 1).astype(BF16)
    gain_qk = jnp.concatenate([jnp.tile(q_norm_g[l], NA_HEADS) * (NA_DH ** -0.5 * LOG2_E),
                               jnp.tile(k_norm_g[l], NA_HEADS)]).reshape(1, 2 * NA_WIDTH)

    zr = jnp.zeros((GLA_RANK, GLA_KEY_WIDTH), F32)
    ztail = jnp.zeros((LANES - 2 * GLA_RANK, GLA_KEY_WIDTH), F32)
    wd = jnp.stack([jnp.concatenate([w_decay_fwd[l], zr, ztail], axis=0),
                    jnp.concatenate([zr, w_decay_bwd[l], ztail], axis=0)]).astype(BF16)
    bias = jnp.stack([b_decay_fwd[l], b_decay_bwd[l]]).reshape(2, 1, GLA_KEY_WIDTH)
    tg, tg_m = _gla_constants()

    assert seq % (NA_WIN_H * GRID_W) == 0
    kgain_col = jnp.tile(k_norm_g[l], NA_HEADS).reshape(NA_WIDTH, 1)
    zmain, gla_ops, nq, nk_t, nv_cb, gtot, zmain_m_pad, nk_m, nv_m = _proj_call(
        x2d, meta_tokens, g, wt, gain_qk, kgain_col, wd, bias, tg, tm=NA_WIN_H * GRID_W)

    o_sum = _gla_call(zmain, gla_ops, gtot, zmain_m_pad, wd, bias, tg_m, batch=batch, seq=seq)

    rpb_pad = jnp.pad(rpb[l] * LOG2_E, ((0, 0), (0, 0), (0, 2 * NA_DH - rpb.shape[-1])))
    mb = (meta_bias[l] * LOG2_E).reshape(NA_HEADS, 1, N_META)
    o_na = _na_call(nq, nk_t, nv_cb, zmain, nk_m, nv_m, rpb_pad, mb, batch=batch, seq=seq,
                    rows_per_step=_pick(batch * seq // GRID_W, (256, 128, 64, 32, 16)))

    gn_rows = jnp.concatenate([jnp.tile(gla_out_norm_g[l], GLA_HEADS), jnp.ones((NA_WIDTH,), F32)])
    w_o = (w_out[l] * gn_rows[:, None]).astype(BF16)
    out = _out_call(x2d, o_sum, zmain, o_na, w_o, tm=_pick(batch * seq, (1024, 512, 256, 128)))
    return out.reshape(batch, seq, d)
```

```python
import functools

import jax
import jax.numpy as jnp
import numpy as np
from jax import lax
from jax.experimental import pallas as pl
from jax.experimental.pallas import tpu as pltpu

F32 = jnp.float32
BF16 = jnp.bfloat16

D_MODEL = 1024
N_META = 16
GRID_W = 64
GLA_HEADS = 4
GLA_DK = 128
GLA_DV = 256
GLA_KEY_WIDTH = GLA_HEADS * GLA_DK
GLA_WIDTH = GLA_HEADS * GLA_DV
GLA_RANK = 16
GLA_TAU = 16.0
GLA_CHUNK = 64
NA_HEADS = 16
NA_DH = 64
NA_WIDTH = NA_HEADS * NA_DH
NA_WIN_H = 8
NA_WIN_W = 16
RMS_EPS = 1e-6
NEG_BIG = -1e30
LOG2_E = 1.4426950408889634
LANES = 128

COL_GV, COL_GG, COL_NG, COL_GQ, COL_GK, COL_R = 0, 1024, 2048, 3072, 3584, 4096
MAIN_W = COL_R + LANES
MAIN_TOK_W = COL_GQ
GLA_Q_IN, GLA_K_IN, GLA_K_ST = 0, 1, 2
NA_CBW = 16
NA_CB = GRID_W // NA_CBW
IN_GQ, IN_GK, IN_GV, IN_R, IN_GG, IN_NQ, IN_NK, IN_NV, IN_NG = 0, 512, 1024, 2048, 2080, 3104, 4128, 5152, 6176
D_IN_PROJ = 7200
PROJ_TN = 512
OUT_TN = 256

TILE = 2 * GLA_CHUNK
NA_ROW_GROUP = 16
VMEM_LIMIT = 58 * 1024 * 1024


def _split_bf16(x):
    hi = x.astype(BF16)
    lo = (x - hi.astype(F32)).astype(BF16)
    return hi, lo


def _dot(a, b):
    return jnp.dot(a, b, preferred_element_type=F32)


def _dot_nt(a, b):
    return lax.dot_general(a, b, (((1,), (1,)), ((), ())), preferred_element_type=F32)


def _dot_tn(a, b):
    return lax.dot_general(a, b, (((0,), (0,)), ((), ())), preferred_element_type=F32)


def _log_sigmoid(x):
    return jnp.minimum(x, 0.0) - jnp.log(1.0 + jnp.exp(-jnp.abs(x)))


def _silu(x):
    h = 0.5 * x
    return h + h * jnp.tanh(h)


SHARED_SEGMENTS = ((COL_GV, IN_GV, 1024), (COL_GG, IN_GG, 1024), (COL_NG, IN_NG, 1024))
META_SEGMENTS = ((COL_GQ, IN_GQ, 512), (COL_GK, IN_GK, 512),
                 (COL_R, IN_R, LANES))


def _store_rows(o_ref, c0, y):
    n = y.shape[1]
    for rl in range(y.shape[0] // GRID_W):
        for cb in range(NA_CB):
            src = rl * GRID_W + cb * NA_CBW
            for c in range(0, n, LANES):
                o_ref[(c0 + c) // LANES, cb, rl * NA_CBW:(rl + 1) * NA_CBW, :] = y[src:src + NA_CBW, c:c + LANES]


def _gla_log_decay(r, wd_ref, bdec_ref, d):
    return (_log_sigmoid(_dot(r, wd_ref[d]) + bdec_ref[d]) * (1.0 / GLA_TAU)).astype(BF16)


def _gla_operands_store(q, k, log_decay, d, t, tg_ref, go_ref, gtot_ref):
    rows = slice(t * TILE, (t + 1) * TILE)
    g = _dot(tg_ref[d], log_decay[rows])
    tots = []
    for c in range(TILE // GLA_CHUNK):
        last = c * GLA_CHUNK + (GLA_CHUNK - 1 if d == 0 else 0)
        tot = g[last:last + 1]
        gtot_ref[(t * TILE) // GLA_CHUNK + c:(t * TILE) // GLA_CHUNK + c + 1,
                 d * GLA_KEY_WIDTH:(d + 1) * GLA_KEY_WIDTH] = tot
        tots.append(jnp.broadcast_to(tot, (GLA_CHUNK, GLA_KEY_WIDTH)))
    dd = jnp.concatenate(tots, axis=0) - g
    operands = ((GLA_Q_IN, q[rows] * jnp.exp(g)), (GLA_K_IN, k[rows] * jnp.exp(-g)), (GLA_K_ST, k[rows] * jnp.exp(dd)))
    for kind, val in operands:
        val = val.astype(go_ref.dtype)
        for h in range(GLA_HEADS):
            go_ref[d, kind, h, rows, :] = val[:, h * GLA_DK:(h + 1) * GLA_DK]


def _proj_kernel(x0_ref, xn_ref, xm_ref, g_ref, wt_ref, gain_ref, kgain_ref, wd_ref, bdec_ref, tg_ref,
                 zm_ref, go_ref, nq_ref, nk_ref, nv_ref, gtot_ref, zmm_ref, nkm_ref, nvm_ref, u_scr):
    def normed(x_ref):
        x = x_ref[...]
        ms = jnp.mean(x * x, axis=-1, keepdims=True)
        return (x * lax.rsqrt(ms + RMS_EPS) * g_ref[...]).astype(BF16)

    def head_norm_rows(acc, goff):
        first = lax.broadcasted_iota(jnp.int32, (acc.shape[0], LANES), 1) < NA_DH
        cols = []
        for c in range(0, acc.shape[1], LANES):
            a = acc[:, c:c + LANES]
            sq = a * a
            lo = jnp.sum(jnp.where(first, sq, 0.0), axis=-1, keepdims=True)
            hi = jnp.sum(jnp.where(first, 0.0, sq), axis=-1, keepdims=True)
            inv = jnp.where(first, lax.rsqrt(lo * (1.0 / NA_DH) + RMS_EPS), lax.rsqrt(hi * (1.0 / NA_DH) + RMS_EPS))
            cols.append(a * inv)
        return (jnp.concatenate(cols, axis=1) * gain_ref[:, goff:goff + PROJ_TN]).astype(BF16)

    def meta_tokens():
        um = normed(xm_ref)
        zmm_ref[...] = jnp.zeros_like(zmm_ref)
        for off in range(0, NA_WIDTH, PROJ_TN):
            acc = _dot_nt(um, wt_ref[IN_NK + off:IN_NK + off + PROJ_TN, :])
            nkm_ref[:, off:off + PROJ_TN] = head_norm_rows(acc, NA_WIDTH + off)
            acc = _dot_nt(um, wt_ref[IN_NV + off:IN_NV + off + PROJ_TN, :])
            nvm_ref[:, off:off + PROJ_TN] = acc.astype(nvm_ref.dtype)
        for dst, src, width in SHARED_SEGMENTS + META_SEGMENTS:
            acc = _dot_nt(um, wt_ref[src:src + width, :])
            if dst == COL_GQ:
                acc = acc * (GLA_DK ** -0.5)
            zmm_ref[:N_META, dst:dst + width] = acc.astype(zmm_ref.dtype)

    step = pl.program_id(0)

    @pl.when(step == 0)
    def _():
        u_scr[0] = normed(x0_ref)
        meta_tokens()

    slot = step % 2
    u = u_scr[slot]
    tm = u.shape[0]

    def prepare_next():
        u_scr[1 - slot] = normed(xn_ref)

    def main_tile(dst, src, off, n):
        acc = _dot_nt(u, wt_ref[src + off:src + off + n, :])
        zm_ref[:, dst + off:dst + off + n] = acc.astype(zm_ref.dtype)

    def nv_tile(off):
        acc = _dot_nt(u, wt_ref[IN_NV + off:IN_NV + off + PROJ_TN, :])
        _store_rows(nv_ref, off, acc.astype(nv_ref.dtype))

    def nq_tile(off):
        acc = _dot_nt(u, wt_ref[IN_NQ + off:IN_NQ + off + PROJ_TN, :])
        y = head_norm_rows(acc, off)
        for c in range(0, PROJ_TN, LANES):
            nq_ref[(off + c) // LANES] = y[:, c:c + LANES]

    def nk_tile(off):
        u_cb = jnp.concatenate([u[rl * GRID_W + cb * NA_CBW:rl * GRID_W + (cb + 1) * NA_CBW]
                                for cb in range(NA_CB) for rl in range(tm // GRID_W)], axis=0)
        acc_t = _dot_nt(wt_ref[IN_NK + off:IN_NK + off + PROJ_TN, :], u_cb)
        for h in range(PROJ_TN // NA_DH):
            rows = slice(off + h * NA_DH, off + (h + 1) * NA_DH)
            blk = acc_t[h * NA_DH:(h + 1) * NA_DH]
            ms = jnp.sum(blk * blk, axis=0, keepdims=True) * (1.0 / NA_DH)
            y = (blk * lax.rsqrt(ms + RMS_EPS) * kgain_ref[rows, :]).astype(BF16)
            for cb in range(NA_CB):
                nk_ref[cb, rows, :] = y[:, cb * LANES:(cb + 1) * LANES]

    tiles = [functools.partial(nk_tile, off) for off in range(0, NA_WIDTH, PROJ_TN)]
    tiles += [functools.partial(nq_tile, off) for off in range(0, NA_WIDTH, PROJ_TN)]
    tiles += [functools.partial(nv_tile, off) for off in range(0, NA_WIDTH, PROJ_TN)]
    for dst, src, width in SHARED_SEGMENTS:
        for off in range(0, width, PROJ_TN):
            tiles.append(functools.partial(main_tile, dst, src, off, min(PROJ_TN, width - off)))

    r = _dot_nt(u, wt_ref[IN_R:IN_R + LANES, :]).astype(BF16)
    q = _dot_nt(u, wt_ref[IN_GQ:IN_GQ + GLA_KEY_WIDTH, :]) * (GLA_DK ** -0.5)
    k = _dot_nt(u, wt_ref[IN_GK:IN_GK + GLA_KEY_WIDTH, :])
    log_decay = [None, None]

    def decay_stage(d):
        log_decay[d] = _gla_log_decay(r, wd_ref, bdec_ref, d)

    stages = [functools.partial(decay_stage, 0), functools.partial(decay_stage, 1), None]
    stages += [functools.partial(lambda d, t: _gla_operands_store(q, k, log_decay[d], d, t, tg_ref, go_ref, gtot_ref),
                                 d, t)
               for d in range(2) for t in range(tm // TILE)]
    for i, tile in enumerate(tiles):
        if i < len(stages) and stages[i] is not None:
            stages[i]()
        tile()
    prepare_next()
    assert len(stages) <= len(tiles)


def _proj_call(x2d, xm, g, wt, gain_qk, kgain_col, wd, bdec, tg, *, tm):
    m, d = x2d.shape
    assert m % tm == 0 and wt.shape == (D_IN_PROJ, d) and xm.shape == (N_META, d)
    assert tm == NA_WIN_H * GRID_W and NA_CB * LANES == tm
    resident = lambda shape: pl.BlockSpec(shape, lambda i: (0,) * len(shape), pipeline_mode=pl.Buffered(1))
    whole = lambda shape: pl.BlockSpec(shape, lambda i: (0,) * len(shape))
    n_pairs = NA_WIDTH // LANES
    q_spec = pl.BlockSpec((n_pairs, tm, LANES), lambda i: (0, i, 0))
    q_shape = jax.ShapeDtypeStruct((n_pairs, m, LANES), BF16)
    v_spec = pl.BlockSpec((n_pairs, NA_CB, tm // NA_CB, LANES), lambda i: (0, 0, i, 0))
    v_shape = jax.ShapeDtypeStruct((n_pairs, NA_CB, m // NA_CB, LANES), BF16)
    k_spec = pl.BlockSpec((NA_CB, NA_WIDTH, LANES), lambda i: (i, 0, 0))
    k_shape = jax.ShapeDtypeStruct((m // LANES, NA_WIDTH, LANES), BF16)
    n_chunk_rows = tm // GLA_CHUNK
    meta_plain = jax.ShapeDtypeStruct((N_META, NA_WIDTH), BF16)
    return pl.pallas_call(
        _proj_kernel,
        name="proj_tokens",
        grid=(m // tm,),
        in_specs=[
            resident((tm, d)),
            pl.BlockSpec((tm, d), lambda i: (jnp.minimum(i + 1, m // tm - 1), 0)),
            resident((N_META, d)),
            resident((1, d)),
            resident((D_IN_PROJ, d)),
            resident((1, 2 * NA_WIDTH)),
            resident((NA_WIDTH, 1)),
            resident((2, LANES, GLA_KEY_WIDTH)),
            resident((2, 1, GLA_KEY_WIDTH)),
            resident((2, TILE, TILE)),
        ],
        out_specs=[pl.BlockSpec((tm, MAIN_TOK_W), lambda i: (i, 0)),
                   pl.BlockSpec((2, 3, GLA_HEADS, tm, GLA_DK), lambda i: (0, 0, 0, i, 0)),
                   q_spec, k_spec, v_spec,
                   pl.BlockSpec((n_chunk_rows, 2 * GLA_KEY_WIDTH), lambda i: (i, 0)),
                   whole((TILE, MAIN_W)), whole((N_META, NA_WIDTH)), whole((N_META, NA_WIDTH))],
        out_shape=[jax.ShapeDtypeStruct((m, MAIN_TOK_W), BF16),
                   jax.ShapeDtypeStruct((2, 3, GLA_HEADS, m, GLA_DK), BF16), q_shape, k_shape, v_shape,
                   jax.ShapeDtypeStruct((m // GLA_CHUNK, 2 * GLA_KEY_WIDTH), F32),
                   jax.ShapeDtypeStruct((TILE, MAIN_W), BF16), meta_plain, meta_plain],
        scratch_shapes=[pltpu.VMEM((2, tm, d), BF16)],
        compiler_params=pltpu.CompilerParams(
            dimension_semantics=("arbitrary",), vmem_limit_bytes=VMEM_LIMIT),
    )(x2d, x2d, xm, g, wt, gain_qk, kgain_col, wd, bdec, tg)


def _gla_kernel(qf_ref, kif_ref, ksf_ref, vf_ref, gtf_ref, qb_ref, kib_ref, ksb_ref, vb_ref, gtb_ref,
                km_ref, vm_ref, rm_ref, wd_ref, b_ref, tgm_ref,
                o_ref, sf_ref, sb_ref, *, n_tiles):
    row = lax.broadcasted_iota(jnp.int32, (TILE, TILE), 0)
    col = lax.broadcasted_iota(jnp.int32, (TILE, TILE), 1)
    same = (row // GLA_CHUNK) == (col // GLA_CHUNK)
    masks = (same & (row >= col), same & (row <= col))

    @pl.when(pl.program_id(2) == 0)
    def _():
        rowm = lax.broadcasted_iota(jnp.int32, (TILE, GLA_DK), 0)
        lg = _log_sigmoid(_dot(rm_ref[...], wd_ref[...]) + b_ref[...]) * (1.0 / GLA_TAU)
        hi, lo = _split_bf16(jnp.where(rowm < N_META, lg, 0.0))
        d_m = _dot(tgm_ref[...], jnp.concatenate([hi, lo], axis=0))
        k_st = (km_ref[...].astype(F32) * jnp.exp(d_m)).astype(BF16)
        sf_ref[...] = _dot_tn(k_st, vm_ref[...])
        sb_ref[...] = jnp.zeros_like(sb_ref)

    def scan(step):
        items = []
        for t in range(n_tiles):
            items.append((0, t))
            items.append((1, n_tiles - 1 - t))
        refs = ((qf_ref, kif_ref, ksf_ref, vf_ref), (qb_ref, kib_ref, ksb_ref, vb_ref))
        out_half = (step, 1 - step)
        a_cols = (jnp.exp(jnp.transpose(gtf_ref[...])), jnp.exp(jnp.transpose(gtb_ref[...])))

        def rows_of(t):
            return slice(t * TILE, (t + 1) * TILE)

        amats = [jnp.where(masks[d], _dot_nt(refs[d][0][rows_of(t), :], refs[d][1][rows_of(t), :]), 0.0).astype(BF16)
                 for d, t in items]

        def state_independent(idx):
            d, t = items[idx]
            v = refs[d][3][rows_of(t), :]
            k_st = refs[d][2][rows_of(t), :]
            u = [_dot_tn(k_st[h * GLA_CHUNK:(h + 1) * GLA_CHUNK], v[h * GLA_CHUNK:(h + 1) * GLA_CHUNK])
                 for h in range(2)]
            return _dot(amats[idx], v), u

        state = [sf_ref[...], sb_ref[...]]
        ahead = state_independent(0)
        for idx, (d, t) in enumerate(items):
            o_intra, u = ahead
            if idx + 1 < len(items):
                ahead = state_independent(idx + 1)
            s = state[d]
            outs = [None, None]
            for h in ((0, 1) if d == 0 else (1, 0)):
                rows = slice(t * TILE + h * GLA_CHUNK, t * TILE + (h + 1) * GLA_CHUNK)
                outs[h] = o_intra[h * GLA_CHUNK:(h + 1) * GLA_CHUNK] + _dot(refs[d][0][rows, :], s.astype(BF16))
                chunk = 2 * t + h
                s = a_cols[d][:, chunk:chunk + 1] * s + u[h]
            state[d] = s
            o = jnp.concatenate(outs, axis=0)
            out_rows = rows_of(out_half[d] * n_tiles + t)
            if step == 1:
                o = o + o_ref[out_rows, :].astype(F32)
            o_ref[out_rows, :] = o.astype(o_ref.dtype)
        sf_ref[...] = state[0]
        sb_ref[...] = state[1]

    for step in range(2):
        pl.when(pl.program_id(2) == step)(functools.partial(scan, step))


def _gla_constants():
    i = np.arange(TILE)
    same = (i[:, None] // GLA_CHUNK) == (i[None, :] // GLA_CHUNK)
    inc_f = same & (i[None, :] <= i[:, None])
    inc_b = same & (i[None, :] >= i[:, None])
    to_end_meta = i[None, :] > i[:, None]
    dup = lambda m: np.concatenate([m, m], axis=1)
    tg = jnp.asarray(np.stack([inc_f, inc_b]), BF16)
    return tg, jnp.asarray(dup(to_end_meta), BF16)


def _gla_call(zmain, gla_ops, gtot, zmain_m, wd, bias, tg_m, *, batch, seq):
    nb = 2
    assert seq % (nb * TILE) == 0
    t_blk = seq // nb
    n_tiles = t_blk // TILE
    n_chunks = t_blk // GLA_CHUNK

    def tok(fwd):
        if fwd:
            return lambda b, hh, i: b * nb + i
        return lambda b, hh, i: b * nb + (nb - 1 - i)

    def dir_specs(d):
        t = tok(d == 0)
        operand = lambda kind: pl.BlockSpec((None, None, None, t_blk, GLA_DK),
                                            lambda b, hh, i: (d, kind, hh, t(b, hh, i), 0))
        return [
            operand(GLA_Q_IN), operand(GLA_K_IN), operand(GLA_K_ST),
            pl.BlockSpec((t_blk, GLA_DV), lambda b, hh, i: (t(b, hh, i), COL_GV // GLA_DV + hh)),
            pl.BlockSpec((n_chunks, GLA_DK), lambda b, hh, i: (t(b, hh, i), d * GLA_HEADS + hh)),
        ]

    in_specs = (
        dir_specs(0) + dir_specs(1)
        + [pl.BlockSpec((TILE, GLA_DK), lambda b, hh, i: (0, COL_GK // GLA_DK + hh)),
           pl.BlockSpec((TILE, GLA_DV), lambda b, hh, i: (0, COL_GV // GLA_DV + hh)),
           pl.BlockSpec((TILE, LANES), lambda b, hh, i: (0, COL_R // LANES)),
           pl.BlockSpec((None, LANES, GLA_DK), lambda b, hh, i: (0, 0, hh)),
           pl.BlockSpec((None, 1, GLA_DK), lambda b, hh, i: (0, 0, hh)),
           pl.BlockSpec((TILE, 2 * TILE), lambda b, hh, i: (0, 0))]
    )
    args = ([gla_ops] * 3 + [zmain, gtot]) * 2 + [zmain_m] * 3 + [wd, bias, tg_m]
    return pl.pallas_call(
        functools.partial(_gla_kernel, n_tiles=n_tiles),
        name="gla_scan",
        grid=(batch, GLA_HEADS, nb),
        in_specs=in_specs,
        out_specs=pl.BlockSpec((seq, GLA_DV), lambda b, hh, i: (b, hh)),
        out_shape=jax.ShapeDtypeStruct((batch * seq, GLA_WIDTH), BF16),
        scratch_shapes=[pltpu.VMEM((GLA_DK, GLA_DV), F32), pltpu.VMEM((GLA_DK, GLA_DV), F32)],
        compiler_params=pltpu.CompilerParams(
            dimension_semantics=("parallel", "parallel", "arbitrary"), vmem_limit_bytes=VMEM_LIMIT),
    )(*args)


NA_SEG_COLS = ((0, 24), (24, 40), (40, 64))
NA_SEG_ROW0 = (0, 48, 80)
NA_STACK = 2 * GRID_W


NA_BASE_MID = NA_WIN_H // 2 - 1
NA_VARIANTS = 2 * NA_WIN_H - 1


def _na_variant(base, a):
    return jnp.where(base == NA_BASE_MID, a, jnp.where(base < NA_BASE_MID, NA_WIN_H + base, NA_WIN_H - 1 + base))


def _na_build_bias(rpb_ref, tab_ref):
    def build(v, carry):
        a = jnp.where(v < NA_WIN_H, v, 0)
        base = jnp.where(v < NA_WIN_H, NA_BASE_MID,
                         jnp.where(v < NA_WIN_H + NA_BASE_MID, v - NA_WIN_H, v - (NA_WIN_H - 1)))
        for hh in range(2):
            bias_rows = [rpb_ref[hh, pl.ds(base + ((t - a + NA_WIN_H) & (NA_WIN_H - 1)), 1), :]
                         for t in range(NA_WIN_H)]
            for s, (c0, c1) in enumerate(NA_SEG_COLS):
                n = c1 - c0
                vrows = [jnp.broadcast_to(br, (n, LANES)) for br in bias_rows]
                row0 = NA_SEG_ROW0[s] + hh * n
                rho = lax.broadcasted_iota(jnp.int32, (n, LANES), 0)
                lane = lax.broadcasted_iota(jnp.int32, (n, LANES), 1)
                cs = jnp.clip(c0 + rho - NA_WIN_W // 2, 0, GRID_W - NA_WIN_W)
                for half in range(2):
                    cb = s + half
                    cp = NA_CBW * cb + lane % NA_CBW
                    valid = (cp >= cs) & (cp < cs + NA_WIN_W)
                    acc = None
                    for i in range(NA_WIN_H):
                        shift = (NA_CBW * i - NA_CBW * cb + c0 - (NA_WIN_W - 1)) % LANES
                        rolled = pltpu.roll(vrows[i], shift, 1, stride=1, stride_axis=0)
                        acc = rolled if acc is None else jnp.where(lane // NA_CBW == i, rolled, acc)
                    tab_ref[v, row0:row0 + n, half * LANES:(half + 1) * LANES] = jnp.where(valid, acc, NEG_BIG)
        return carry

    lax.fori_loop(0, NA_VARIANTS, build, 0)


def _na_kernel(q_ref, kt_ref, v_ref, gate_ref, km_ref, vm_ref, rpb_ref, mb_ref, o_ref, tab_ref, sc_ref, sm_ref,
               *, rows_per_step, n_rows):
    w = GRID_W

    @pl.when(pl.program_id(1) == 0)
    def _():
        _na_build_bias(rpb_ref, tab_ref)

    head0_q = lax.broadcasted_iota(jnp.int32, (w, LANES), 1) < NA_DH
    srow = lax.broadcasted_iota(jnp.int32, (NA_STACK, N_META), 0)
    is_h1 = jnp.zeros((NA_STACK, N_META), jnp.bool_)
    for s, (c0, c1) in enumerate(NA_SEG_COLS):
        lo = NA_SEG_ROW0[s] + (c1 - c0)
        is_h1 = is_h1 | ((srow >= lo) & (srow < lo + (c1 - c0)))
    mb = jnp.where(is_h1, mb_ref[1], mb_ref[0])
    km = km_ref[...]
    vm = vm_ref[...]
    seg_rows = [slice(NA_SEG_ROW0[s], NA_SEG_ROW0[s] + 2 * (c1 - c0)) for s, (c0, c1) in enumerate(NA_SEG_COLS)]
    blk_rows = [slice(seg_rows[max(cb - 1, 0)].start, seg_rows[min(cb, len(seg_rows) - 1)].stop)
                for cb in range(NA_CB)]

    n_tiles = n_rows // NA_WIN_H
    key_lane = lax.broadcasted_iota(jnp.int32, (LANES, LANES), 1)

    def window(rl):
        row = pl.program_id(1) * rows_per_step + rl
        b = row // n_rows
        r = row - b * n_rows
        rs = jnp.clip(r - NA_WIN_H // 2, 0, n_rows - NA_WIN_H)
        a = rs & (NA_WIN_H - 1)
        return (b * n_tiles + rs // NA_WIN_H, a, (b + 1) * n_tiles - 1,
                _na_variant(rs - r + (NA_WIN_H - 1), a))

    def scores_store(slot, j, rl):
        m, a, m_last, var = window(rl)
        q = q_ref[pl.ds(pl.multiple_of(rl * w, w), w), :].astype(F32)
        q0 = jnp.where(head0_q, q, 0.0)
        q1 = q - q0
        qs = jnp.concatenate([piece[c0:c1] for c0, c1 in NA_SEG_COLS for piece in (q0, q1)],
                             axis=0).astype(BF16)
        from_m = key_lane >= a * NA_CBW
        m1 = jnp.minimum(m + 1, m_last)
        blocks = [_dot(qs[blk_rows[cb]],
                       jnp.where(from_m, kt_ref[NA_CB * m + cb], kt_ref[NA_CB * m1 + cb]))
                  for cb in range(NA_CB)]
        for s in range(len(NA_SEG_COLS)):
            n2 = seg_rows[s].stop - seg_rows[s].start
            for half in range(2):
                cb = s + half
                a0 = seg_rows[s].start - blk_rows[cb].start
                lanes = slice(half * LANES, (half + 1) * LANES)
                sc_ref[slot, j, seg_rows[s], lanes] = blocks[cb][a0:a0 + n2] + tab_ref[var, seg_rows[s], lanes]
        sm_ref[slot, j] = _dot_nt(qs, km) + mb

    def value_block(cb, m, a):
        groups = []
        for t in range(NA_WIN_H):
            row = NA_WIN_H * m + t + jnp.where(t < a, NA_WIN_H, 0)
            groups.append(v_ref[cb, pl.ds(pl.multiple_of(row * NA_CBW, NA_CBW), NA_CBW), :])
        return jnp.concatenate(groups, axis=0)

    def softmax_values(slot, j, rl):
        m_tile, a, _, _ = window(rl)
        sc = sc_ref[slot, j]
        sm = sm_ref[slot, j]
        m = jnp.maximum(sc[:, :LANES], sc[:, LANES:])
        m = jnp.maximum(jnp.max(m, axis=-1, keepdims=True), jnp.max(sm, axis=-1, keepdims=True))
        p = jnp.exp2(sc - m)
        pm = jnp.exp2(sm - m)
        den = (jnp.sum(p[:, :LANES] + p[:, LANES:], axis=-1, keepdims=True)
               + jnp.sum(pm, axis=-1, keepdims=True))
        p = p.astype(BF16)
        vals = [value_block(cb, m_tile, a) for cb in range(NA_CB)]
        o_meta = _dot(pm.astype(BF16), vm)
        inv = 1.0 / den
        outs = []
        for s, (c0, c1) in enumerate(NA_SEG_COLS):
            o_loc = _dot(p[seg_rows[s]], jnp.concatenate([vals[s], vals[s + 1]], axis=0))
            o = (o_loc + o_meta[seg_rows[s]]) * inv[seg_rows[s]]
            n = c1 - c0
            head0 = lax.broadcasted_iota(jnp.int32, (n, LANES), 1) < NA_DH
            outs.append(jnp.where(head0, o[:n], o[n:]))
        o = jnp.concatenate(outs, axis=0)
        qoff = pl.multiple_of(rl * w, w)
        gate = gate_ref[pl.ds(qoff, w), :].astype(F32)
        o_ref[pl.ds(qoff, w), :] = (o * _silu(gate)).astype(o_ref.dtype)

    n_groups = rows_per_step // NA_ROW_GROUP
    for j in range(NA_ROW_GROUP):
        scores_store(0, j, j)

    def trip(g, carry):
        slot = g % 2
        for j in range(NA_ROW_GROUP):
            scores_store(slot, j, g * NA_ROW_GROUP + j)
        for j in range(NA_ROW_GROUP):
            softmax_values(1 - slot, j, (g - 1) * NA_ROW_GROUP + j)
        return carry

    lax.fori_loop(1, n_groups, trip, 0)
    for j in range(NA_ROW_GROUP):
        softmax_values((n_groups - 1) % 2, j, (n_groups - 1) * NA_ROW_GROUP + j)


def _na_call(nq, nk_t, nv_cb, zmain, nk_m, nv_m, rpb_pad, meta_bias, *, batch, seq, rows_per_step):
    n_rows = seq // GRID_W
    total_rows = batch * n_rows
    assert n_rows % NA_WIN_H == 0 and total_rows % rows_per_step == 0 and rows_per_step % NA_ROW_GROUP == 0
    nsteps = total_rows // rows_per_step
    hp = NA_HEADS // 2
    tq = rows_per_step * GRID_W
    n_idx = 2 * NA_WIN_H - 1
    in_specs = [
        pl.BlockSpec((None, tq, LANES), lambda p, i: (p, i, 0)),
        pl.BlockSpec((batch * seq // LANES, LANES, LANES), lambda p, i: (0, p, 0)),
        pl.BlockSpec((None, NA_CB, batch * seq // NA_CB, LANES), lambda p, i: (p, 0, 0, 0)),
        pl.BlockSpec((tq, LANES), lambda p, i: (i, COL_NG // LANES + p)),
        pl.BlockSpec((N_META, LANES), lambda p, i: (0, p)),
        pl.BlockSpec((N_META, LANES), lambda p, i: (0, p)),
        pl.BlockSpec((2, n_idx, LANES), lambda p, i: (p, 0, 0)),
        pl.BlockSpec((2, 1, N_META), lambda p, i: (p, 0, 0)),
    ]
    return pl.pallas_call(
        functools.partial(_na_kernel, rows_per_step=rows_per_step, n_rows=n_rows),
        name="na_attn",
        grid=(hp, nsteps),
        in_specs=in_specs,
        out_specs=pl.BlockSpec((None, tq, LANES), lambda p, i: (p, i, 0)),
        out_shape=jax.ShapeDtypeStruct((hp, batch * seq, LANES), BF16),
        scratch_shapes=[pltpu.VMEM((NA_VARIANTS, NA_STACK, 2 * LANES), F32),
                        pltpu.VMEM((2, NA_ROW_GROUP, NA_STACK, 2 * LANES), F32),
                        pltpu.VMEM((2, NA_ROW_GROUP, NA_STACK, N_META), F32)],
        compiler_params=pltpu.CompilerParams(
            dimension_semantics=("arbitrary", "arbitrary"), vmem_limit_bytes=VMEM_LIMIT),
    )(nq, nk_t, nv_cb, zmain, nk_m, nv_m, rpb_pad, meta_bias)


def _out_kernel(x_ref, os_ref, gg_ref, na_ref, wf_ref, rowgain_ref, o_ref, w_ref):
    @pl.when(pl.program_id(0) == 0)
    def _():
        w_ref[...] = (wf_ref[...] * rowgain_ref[...]).astype(w_ref.dtype)

    d = o_ref.shape[1]
    na = jnp.concatenate([na_ref[p] for p in range(na_ref.shape[0])], axis=1)

    def na_tile(c):
        cols = slice(c * OUT_TN, (c + 1) * OUT_TN)
        o_ref[:, cols] = x_ref[:, cols] + _dot(na, w_ref[GLA_WIDTH:, cols])

    def gla_head(h):
        cols = slice(h * GLA_DV, (h + 1) * GLA_DV)
        sh = os_ref[:, cols].astype(F32)
        ms = jnp.mean(sh * sh, axis=-1, keepdims=True)
        return (sh * lax.rsqrt(ms + RMS_EPS) * _silu(gg_ref[:, cols].astype(F32))).astype(BF16)

    for c in range(d // OUT_TN):
        na_tile(c)
    for h in range(GLA_HEADS):
        y = gla_head(h)
        for c in range(d // OUT_TN):
            cols = slice(c * OUT_TN, (c + 1) * OUT_TN)
            o_ref[:, cols] += _dot(y, w_ref[h * GLA_DV:(h + 1) * GLA_DV, cols])


def _out_call(x2d, o_sum, zmain, o_na, w_out, row_gain, *, tm):
    m, d = x2d.shape
    assert m % tm == 0 and d % OUT_TN == 0
    resident = lambda shape: pl.BlockSpec(shape, lambda i: (0, 0), pipeline_mode=pl.Buffered(1))
    return pl.pallas_call(
        _out_kernel,
        name="out_proj",
        grid=(m // tm,),
        in_specs=[
            pl.BlockSpec((tm, d), lambda i: (i, 0)),
            pl.BlockSpec((tm, GLA_WIDTH), lambda i: (i, 0)),
            pl.BlockSpec((tm, GLA_WIDTH), lambda i: (i, COL_GG // GLA_WIDTH)),
            pl.BlockSpec((NA_WIDTH // LANES, tm, LANES), lambda i: (0, i, 0)),
            resident((GLA_WIDTH + NA_WIDTH, d)),
            resident((GLA_WIDTH + NA_WIDTH, 1)),
        ],
        out_specs=pl.BlockSpec((tm, d), lambda i: (i, 0)),
        out_shape=jax.ShapeDtypeStruct((m, d), F32),
        scratch_shapes=[pltpu.VMEM((GLA_WIDTH + NA_WIDTH, d), BF16)],
        compiler_params=pltpu.CompilerParams(
            dimension_semantics=("arbitrary",), vmem_limit_bytes=VMEM_LIMIT),
    )(x2d, o_sum, zmain, o_na, w_out, row_gain)


def _pick(n, cands):
    for c in cands:
        if n % c == 0:
            return c
    raise ValueError(f"no tile for {n}")


def kernel(x, meta_tokens, norm_g, w_in, w_decay_fwd, b_decay_fwd, w_decay_bwd, b_decay_bwd,
           gla_out_norm_g, q_norm_g, k_norm_g, rpb, meta_bias, w_out):
    batch, seq, d = x.shape
    assert d == D_MODEL and seq % TILE == 0 and seq % GRID_W == 0
    depth = norm_g.shape[0]
    assert depth == 1
    l = 0
    x2d = x.reshape(batch * seq, d)
    g = norm_g[l].reshape(1, d)

    assert w_in.shape[1:] == (d, D_IN_PROJ)
    wt = jnp.swapaxes(w_in[l], 0, 1).astype(BF16)
    gain_qk = jnp.concatenate([jnp.tile(q_norm_g[l], NA_HEADS) * (NA_DH ** -0.5 * LOG2_E),
                               jnp.tile(k_norm_g[l], NA_HEADS)]).reshape(1, 2 * NA_WIDTH)

    zr = jnp.zeros((GLA_RANK, GLA_KEY_WIDTH), F32)
    ztail = jnp.zeros((LANES - 2 * GLA_RANK, GLA_KEY_WIDTH), F32)
    wd = jnp.stack([jnp.concatenate([w_decay_fwd[l], zr, ztail], axis=0),
                    jnp.concatenate([zr, w_decay_bwd[l], ztail], axis=0)]).astype(BF16)
    bias = jnp.stack([b_decay_fwd[l], b_decay_bwd[l]]).reshape(2, 1, GLA_KEY_WIDTH)
    tg, tg_m = _gla_constants()

    assert seq % (NA_WIN_H * GRID_W) == 0
    kgain_col = jnp.tile(k_norm_g[l], NA_HEADS).reshape(NA_WIDTH, 1)
    zmain, gla_ops, nq, nk_t, nv_cb, gtot, zmain_m_pad, nk_m, nv_m = _proj_call(
        x2d, meta_tokens, g, wt, gain_qk, kgain_col, wd, bias, tg, tm=NA_WIN_H * GRID_W)

    o_sum = _gla_call(zmain, gla_ops, gtot, zmain_m_pad, wd, bias, tg_m, batch=batch, seq=seq)

    rpb_pad = jnp.pad(rpb[l] * LOG2_E, ((0, 0), (0, 0), (0, 2 * NA_DH - rpb.shape[-1])))
    mb = (meta_bias[l] * LOG2_E).reshape(NA_HEADS, 1, N_META)
    o_na = _na_call(nq, nk_t, nv_cb, zmain, nk_m, nv_m, rpb_pad, mb, batch=batch, seq=seq,
                    rows_per_step=_pick(batch * seq // GRID_W, (256, 128, 64, 32, 16)))

    row_gain = jnp.concatenate([jnp.tile(gla_out_norm_g[l], GLA_HEADS), jnp.ones((NA_WIDTH,), F32)])
    out = _out_call(x2d, o_sum, zmain, o_na, w_out[l], row_gain.reshape(GLA_WIDTH + NA_WIDTH, 1),
                    tm=_pick(batch * seq, (1024, 512, 256, 128)))
    return out.reshape(batch, seq, d)
```

```python
import functools

import jax
import jax.numpy as jnp
import numpy as np
from jax import lax
from jax.experimental import pallas as pl
from jax.experimental.pallas import tpu as pltpu

F32 = jnp.float32
BF16 = jnp.bfloat16

D_MODEL = 1024
N_META = 16
GRID_W = 64
GLA_HEADS = 4
GLA_DK = 128
GLA_DV = 256
GLA_KEY_WIDTH = GLA_HEADS * GLA_DK
GLA_WIDTH = GLA_HEADS * GLA_DV
GLA_RANK = 16
GLA_TAU = 16.0
GLA_CHUNK = 64
NA_HEADS = 16
NA_DH = 64
NA_WIDTH = NA_HEADS * NA_DH
NA_WIN_H = 8
NA_WIN_W = 16
RMS_EPS = 1e-6
NEG_BIG = -1e30
LOG2_E = 1.4426950408889634
LANES = 128

COL_GV, COL_GG, COL_NG, COL_GQ, COL_GK, COL_R = 0, 1024, 2048, 3072, 3584, 4096
MAIN_W = COL_R + LANES
MAIN_TOK_W = COL_GQ
GLA_Q_IN, GLA_K_IN, GLA_K_ST = 0, 1, 2
NA_CBW = 16
NA_CB = GRID_W // NA_CBW
IN_GQ, IN_GK, IN_GV, IN_R, IN_GG, IN_NQ, IN_NK, IN_NV, IN_NG = 0, 512, 1024, 2048, 2080, 3104, 4128, 5152, 6176
D_IN_PROJ = 7200
PROJ_TN = 512
PROJ_W_CHUNK = 720
OUT_TN = 256

TILE = 2 * GLA_CHUNK
NA_ROW_GROUP = 16
VMEM_LIMIT = 58 * 1024 * 1024


def _split_bf16(x):
    hi = x.astype(BF16)
    lo = (x - hi.astype(F32)).astype(BF16)
    return hi, lo


def _dot(a, b):
    return jnp.dot(a, b, preferred_element_type=F32)


def _dot_nt(a, b):
    return lax.dot_general(a, b, (((1,), (1,)), ((), ())), preferred_element_type=F32)


def _dot_tn(a, b):
    return lax.dot_general(a, b, (((0,), (0,)), ((), ())), preferred_element_type=F32)


def _log_sigmoid(x):
    return jnp.minimum(x, 0.0) - jnp.log(1.0 + jnp.exp(-jnp.abs(x)))


def _silu(x):
    h = 0.5 * x
    return h + h * jnp.tanh(h)


SHARED_SEGMENTS = ((COL_GV, IN_GV, 1024), (COL_GG, IN_GG, 1024), (COL_NG, IN_NG, 1024))
META_SEGMENTS = ((COL_GQ, IN_GQ, 512), (COL_GK, IN_GK, 512),
                 (COL_R, IN_R, LANES))


def _store_rows(o_ref, c0, y):
    n = y.shape[1]
    for rl in range(y.shape[0] // GRID_W):
        for cb in range(NA_CB):
            src = rl * GRID_W + cb * NA_CBW
            for c in range(0, n, LANES):
                o_ref[(c0 + c) // LANES, cb, rl * NA_CBW:(rl + 1) * NA_CBW, :] = y[src:src + NA_CBW, c:c + LANES]


def _gla_log_decay(r, wd_ref, bdec_ref, d):
    return (_log_sigmoid(_dot(r, wd_ref[d]) + bdec_ref[d]) * (1.0 / GLA_TAU)).astype(BF16)


def _gla_operands_store(q, k, log_decay, d, t, tg_ref, go_ref, gtot_ref):
    rows = slice(t * TILE, (t + 1) * TILE)
    g = _dot(tg_ref[d], log_decay[rows])
    tots = []
    for c in range(TILE // GLA_CHUNK):
        last = c * GLA_CHUNK + (GLA_CHUNK - 1 if d == 0 else 0)
        tot = g[last:last + 1]
        gtot_ref[(t * TILE) // GLA_CHUNK + c:(t * TILE) // GLA_CHUNK + c + 1,
                 d * GLA_KEY_WIDTH:(d + 1) * GLA_KEY_WIDTH] = tot
        tots.append(jnp.broadcast_to(tot, (GLA_CHUNK, GLA_KEY_WIDTH)))
    dd = jnp.concatenate(tots, axis=0) - g
    operands = ((GLA_Q_IN, q[rows] * jnp.exp(g)), (GLA_K_IN, k[rows] * jnp.exp(-g)), (GLA_K_ST, k[rows] * jnp.exp(dd)))
    for kind, val in operands:
        val = val.astype(go_ref.dtype)
        for h in range(GLA_HEADS):
            go_ref[d, kind, h, rows, :] = val[:, h * GLA_DK:(h + 1) * GLA_DK]


def _proj_kernel(x0_ref, xn_ref, xm_ref, g_ref, wf_hbm, gain_ref, kgain_ref, wd_ref, bdec_ref, tg_ref,
                 zm_ref, go_ref, nq_ref, nk_ref, nv_ref, gtot_ref, zmm_ref, nkm_ref, nvm_ref,
                 u_scr, wt_ref, stage_ref, sem):
    def normed(x_ref):
        x = x_ref[...]
        ms = jnp.mean(x * x, axis=-1, keepdims=True)
        return (x * lax.rsqrt(ms + RMS_EPS) * g_ref[...]).astype(BF16)

    def head_norm_rows(acc, goff):
        first = lax.broadcasted_iota(jnp.int32, (acc.shape[0], LANES), 1) < NA_DH
        cols = []
        for c in range(0, acc.shape[1], LANES):
            a = acc[:, c:c + LANES]
            sq = a * a
            lo = jnp.sum(jnp.where(first, sq, 0.0), axis=-1, keepdims=True)
            hi = jnp.sum(jnp.where(first, 0.0, sq), axis=-1, keepdims=True)
            inv = jnp.where(first, lax.rsqrt(lo * (1.0 / NA_DH) + RMS_EPS), lax.rsqrt(hi * (1.0 / NA_DH) + RMS_EPS))
            cols.append(a * inv)
        return (jnp.concatenate(cols, axis=1) * gain_ref[:, goff:goff + PROJ_TN]).astype(BF16)

    def meta_tokens():
        um = normed(xm_ref)
        zmm_ref[...] = jnp.zeros_like(zmm_ref)
        for off in range(0, NA_WIDTH, PROJ_TN):
            acc = _dot_nt(um, wt_ref[IN_NK + off:IN_NK + off + PROJ_TN, :])
            nkm_ref[:, off:off + PROJ_TN] = head_norm_rows(acc, NA_WIDTH + off)
            acc = _dot_nt(um, wt_ref[IN_NV + off:IN_NV + off + PROJ_TN, :])
            nvm_ref[:, off:off + PROJ_TN] = acc.astype(nvm_ref.dtype)
        for dst, src, width in SHARED_SEGMENTS + META_SEGMENTS:
            acc = _dot_nt(um, wt_ref[src:src + width, :])
            if dst == COL_GQ:
                acc = acc * (GLA_DK ** -0.5)
            zmm_ref[:N_META, dst:dst + width] = acc.astype(zmm_ref.dtype)

    step = pl.program_id(0)

    def load_weights():
        rows = stage_ref.shape[1]
        n_chunks = wt_ref.shape[0] // rows

        def chunk_copy(c):
            return pltpu.make_async_copy(wf_hbm.at[pl.ds(c * rows, rows), :], stage_ref.at[c % 2], sem.at[c % 2])

        chunk_copy(0).start()
        for c in range(n_chunks):
            if c + 1 < n_chunks:
                chunk_copy(c + 1).start()
            chunk_copy(c).wait()
            wt_ref[c * rows:(c + 1) * rows, :] = stage_ref[c % 2].astype(wt_ref.dtype)

    @pl.when(step == 0)
    def _():
        load_weights()
        u_scr[0] = normed(x0_ref)
        meta_tokens()

    slot = step % 2
    u = u_scr[slot]
    tm = u.shape[0]

    def prepare_next():
        u_scr[1 - slot] = normed(xn_ref)

    def main_tile(dst, src, off, n):
        acc = _dot_nt(u, wt_ref[src + off:src + off + n, :])
        zm_ref[:, dst + off:dst + off + n] = acc.astype(zm_ref.dtype)

    def nv_tile(off):
        acc = _dot_nt(u, wt_ref[IN_NV + off:IN_NV + off + PROJ_TN, :])
        _store_rows(nv_ref, off, acc.astype(nv_ref.dtype))

    def nq_tile(off):
        acc = _dot_nt(u, wt_ref[IN_NQ + off:IN_NQ + off + PROJ_TN, :])
        y = head_norm_rows(acc, off)
        for c in range(0, PROJ_TN, LANES):
            nq_ref[(off + c) // LANES] = y[:, c:c + LANES]

    def nk_tile(off):
        u_cb = jnp.concatenate([u[rl * GRID_W + cb * NA_CBW:rl * GRID_W + (cb + 1) * NA_CBW]
                                for cb in range(NA_CB) for rl in range(tm // GRID_W)], axis=0)
        acc_t = _dot_nt(wt_ref[IN_NK + off:IN_NK + off + PROJ_TN, :], u_cb)
        for h in range(PROJ_TN // NA_DH):
            rows = slice(off + h * NA_DH, off + (h + 1) * NA_DH)
            blk = acc_t[h * NA_DH:(h + 1) * NA_DH]
            ms = jnp.sum(blk * blk, axis=0, keepdims=True) * (1.0 / NA_DH)
            y = (blk * lax.rsqrt(ms + RMS_EPS) * kgain_ref[rows, :]).astype(BF16)
            for cb in range(NA_CB):
                nk_ref[cb, rows, :] = y[:, cb * LANES:(cb + 1) * LANES]

    tiles = [functools.partial(nk_tile, off) for off in range(0, NA_WIDTH, PROJ_TN)]
    tiles += [functools.partial(nq_tile, off) for off in range(0, NA_WIDTH, PROJ_TN)]
    tiles += [functools.partial(nv_tile, off) for off in range(0, NA_WIDTH, PROJ_TN)]
    for dst, src, width in SHARED_SEGMENTS:
        for off in range(0, width, PROJ_TN):
            tiles.append(functools.partial(main_tile, dst, src, off, min(PROJ_TN, width - off)))

    r = _dot_nt(u, wt_ref[IN_R:IN_R + LANES, :]).astype(BF16)
    q = _dot_nt(u, wt_ref[IN_GQ:IN_GQ + GLA_KEY_WIDTH, :]) * (GLA_DK ** -0.5)
    k = _dot_nt(u, wt_ref[IN_GK:IN_GK + GLA_KEY_WIDTH, :])
    log_decay = [None, None]

    def decay_stage(d):
        log_decay[d] = _gla_log_decay(r, wd_ref, bdec_ref, d)

    stages = [functools.partial(decay_stage, 0), functools.partial(decay_stage, 1), None]
    stages += [functools.partial(lambda d, t: _gla_operands_store(q, k, log_decay[d], d, t, tg_ref, go_ref, gtot_ref),
                                 d, t)
               for d in range(2) for t in range(tm // TILE)]
    for i, tile in enumerate(tiles):
        if i < len(stages) and stages[i] is not None:
            stages[i]()
        tile()
    prepare_next()
    assert len(stages) <= len(tiles)


def _proj_call(x2d, xm, g, wt, gain_qk, kgain_col, wd, bdec, tg, *, tm):
    m, d = x2d.shape
    assert m % tm == 0 and wt.shape == (D_IN_PROJ, d) and xm.shape == (N_META, d)
    assert tm == NA_WIN_H * GRID_W and NA_CB * LANES == tm
    resident = lambda shape: pl.BlockSpec(shape, lambda i: (0,) * len(shape), pipeline_mode=pl.Buffered(1))
    whole = lambda shape: pl.BlockSpec(shape, lambda i: (0,) * len(shape))
    n_pairs = NA_WIDTH // LANES
    q_spec = pl.BlockSpec((n_pairs, tm, LANES), lambda i: (0, i, 0))
    q_shape = jax.ShapeDtypeStruct((n_pairs, m, LANES), BF16)
    v_spec = pl.BlockSpec((n_pairs, NA_CB, tm // NA_CB, LANES), lambda i: (0, 0, i, 0))
    v_shape = jax.ShapeDtypeStruct((n_pairs, NA_CB, m // NA_CB, LANES), BF16)
    k_spec = pl.BlockSpec((NA_CB, NA_WIDTH, LANES), lambda i: (i, 0, 0))
    k_shape = jax.ShapeDtypeStruct((m // LANES, NA_WIDTH, LANES), BF16)
    n_chunk_rows = tm // GLA_CHUNK
    meta_plain = jax.ShapeDtypeStruct((N_META, NA_WIDTH), BF16)
    return pl.pallas_call(
        _proj_kernel,
        name="proj_tokens",
        grid=(m // tm,),
        in_specs=[
            resident((tm, d)),
            pl.BlockSpec((tm, d), lambda i: (jnp.minimum(i + 1, m // tm - 1), 0)),
            resident((N_META, d)),
            resident((1, d)),
            pl.BlockSpec(memory_space=pl.ANY),
            resident((1, 2 * NA_WIDTH)),
            resident((NA_WIDTH, 1)),
            resident((2, LANES, GLA_KEY_WIDTH)),
            resident((2, 1, GLA_KEY_WIDTH)),
            resident((2, TILE, TILE)),
        ],
        out_specs=[pl.BlockSpec((tm, MAIN_TOK_W), lambda i: (i, 0)),
                   pl.BlockSpec((2, 3, GLA_HEADS, tm, GLA_DK), lambda i: (0, 0, 0, i, 0)),
                   q_spec, k_spec, v_spec,
                   pl.BlockSpec((n_chunk_rows, 2 * GLA_KEY_WIDTH), lambda i: (i, 0)),
                   whole((TILE, MAIN_W)), whole((N_META, NA_WIDTH)), whole((N_META, NA_WIDTH))],
        out_shape=[jax.ShapeDtypeStruct((m, MAIN_TOK_W), BF16),
                   jax.ShapeDtypeStruct((2, 3, GLA_HEADS, m, GLA_DK), BF16), q_shape, k_shape, v_shape,
                   jax.ShapeDtypeStruct((m // GLA_CHUNK, 2 * GLA_KEY_WIDTH), F32),
                   jax.ShapeDtypeStruct((TILE, MAIN_W), BF16), meta_plain, meta_plain],
        scratch_shapes=[pltpu.VMEM((2, tm, d), BF16), pltpu.VMEM((D_IN_PROJ, d), BF16),
                        pltpu.VMEM((2, PROJ_W_CHUNK, d), F32), pltpu.SemaphoreType.DMA((2,))],
        compiler_params=pltpu.CompilerParams(
            dimension_semantics=("arbitrary",), vmem_limit_bytes=VMEM_LIMIT),
    )(x2d, x2d, xm, g, wt, gain_qk, kgain_col, wd, bdec, tg)


def _gla_kernel(qf_ref, kif_ref, ksf_ref, vf_ref, gtf_ref, qb_ref, kib_ref, ksb_ref, vb_ref, gtb_ref,
                km_ref, vm_ref, rm_ref, wd_ref, b_ref, tgm_ref,
                o_ref, sf_ref, sb_ref, *, n_tiles):
    row = lax.broadcasted_iota(jnp.int32, (TILE, TILE), 0)
    col = lax.broadcasted_iota(jnp.int32, (TILE, TILE), 1)
    same = (row // GLA_CHUNK) == (col // GLA_CHUNK)
    masks = (same & (row >= col), same & (row <= col))

    @pl.when(pl.program_id(2) == 0)
    def _():
        rowm = lax.broadcasted_iota(jnp.int32, (TILE, GLA_DK), 0)
        lg = _log_sigmoid(_dot(rm_ref[...], wd_ref[...]) + b_ref[...]) * (1.0 / GLA_TAU)
        hi, lo = _split_bf16(jnp.where(rowm < N_META, lg, 0.0))
        d_m = _dot(tgm_ref[...], jnp.concatenate([hi, lo], axis=0))
        k_st = (km_ref[...].astype(F32) * jnp.exp(d_m)).astype(BF16)
        sf_ref[...] = _dot_tn(k_st, vm_ref[...])
        sb_ref[...] = jnp.zeros_like(sb_ref)

    def scan(step):
        items = []
        for t in range(n_tiles):
            items.append((0, t))
            items.append((1, n_tiles - 1 - t))
        refs = ((qf_ref, kif_ref, ksf_ref, vf_ref), (qb_ref, kib_ref, ksb_ref, vb_ref))
        out_half = (step, 1 - step)
        a_cols = (jnp.exp(jnp.transpose(gtf_ref[...])), jnp.exp(jnp.transpose(gtb_ref[...])))

        def rows_of(t):
            return slice(t * TILE, (t + 1) * TILE)

        amats = [jnp.where(masks[d], _dot_nt(refs[d][0][rows_of(t), :], refs[d][1][rows_of(t), :]), 0.0).astype(BF16)
                 for d, t in items]

        def state_independent(idx):
            d, t = items[idx]
            v = refs[d][3][rows_of(t), :]
            k_st = refs[d][2][rows_of(t), :]
            u = [_dot_tn(k_st[h * GLA_CHUNK:(h + 1) * GLA_CHUNK], v[h * GLA_CHUNK:(h + 1) * GLA_CHUNK])
                 for h in range(2)]
            return _dot(amats[idx], v), u

        state = [sf_ref[...], sb_ref[...]]
        ahead = state_independent(0)
        for idx, (d, t) in enumerate(items):
            o_intra, u = ahead
            if idx + 1 < len(items):
                ahead = state_independent(idx + 1)
            s = state[d]
            outs = [None, None]
            for h in ((0, 1) if d == 0 else (1, 0)):
                rows = slice(t * TILE + h * GLA_CHUNK, t * TILE + (h + 1) * GLA_CHUNK)
                outs[h] = o_intra[h * GLA_CHUNK:(h + 1) * GLA_CHUNK] + _dot(refs[d][0][rows, :], s.astype(BF16))
                chunk = 2 * t + h
                s = a_cols[d][:, chunk:chunk + 1] * s + u[h]
            state[d] = s
            o = jnp.concatenate(outs, axis=0)
            out_rows = rows_of(out_half[d] * n_tiles + t)
            if step == 1:
                o = o + o_ref[out_rows, :].astype(F32)
            o_ref[out_rows, :] = o.astype(o_ref.dtype)
        sf_ref[...] = state[0]
        sb_ref[...] = state[1]

    for step in range(2):
        pl.when(pl.program_id(2) == step)(functools.partial(scan, step))


def _gla_constants():
    i = np.arange(TILE)
    same = (i[:, None] // GLA_CHUNK) == (i[None, :] // GLA_CHUNK)
    inc_f = same & (i[None, :] <= i[:, None])
    inc_b = same & (i[None, :] >= i[:, None])
    to_end_meta = i[None, :] > i[:, None]
    dup = lambda m: np.concatenate([m, m], axis=1)
    tg = jnp.asarray(np.stack([inc_f, inc_b]), BF16)
    return tg, jnp.asarray(dup(to_end_meta), BF16)


def _gla_call(zmain, gla_ops, gtot, zmain_m, wd, bias, tg_m, *, batch, seq):
    nb = 2
    assert seq % (nb * TILE) == 0
    t_blk = seq // nb
    n_tiles = t_blk // TILE
    n_chunks = t_blk // GLA_CHUNK

    def tok(fwd):
        if fwd:
            return lambda b, hh, i: b * nb + i
        return lambda b, hh, i: b * nb + (nb - 1 - i)

    def dir_specs(d):
        t = tok(d == 0)
        operand = lambda kind: pl.BlockSpec((None, None, None, t_blk, GLA_DK),
                                            lambda b, hh, i: (d, kind, hh, t(b, hh, i), 0))
        return [
            operand(GLA_Q_IN), operand(GLA_K_IN), operand(GLA_K_ST),
            pl.BlockSpec((t_blk, GLA_DV), lambda b, hh, i: (t(b, hh, i), COL_GV // GLA_DV + hh)),
            pl.BlockSpec((n_chunks, GLA_DK), lambda b, hh, i: (t(b, hh, i), d * GLA_HEADS + hh)),
        ]

    in_specs = (
        dir_specs(0) + dir_specs(1)
        + [pl.BlockSpec((TILE, GLA_DK), lambda b, hh, i: (0, COL_GK // GLA_DK + hh)),
           pl.BlockSpec((TILE, GLA_DV), lambda b, hh, i: (0, COL_GV // GLA_DV + hh)),
           pl.BlockSpec((TILE, LANES), lambda b, hh, i: (0, COL_R // LANES)),
           pl.BlockSpec((None, LANES, GLA_DK), lambda b, hh, i: (0, 0, hh)),
           pl.BlockSpec((None, 1, GLA_DK), lambda b, hh, i: (0, 0, hh)),
           pl.BlockSpec((TILE, 2 * TILE), lambda b, hh, i: (0, 0))]
    )
    args = ([gla_ops] * 3 + [zmain, gtot]) * 2 + [zmain_m] * 3 + [wd, bias, tg_m]
    return pl.pallas_call(
        functools.partial(_gla_kernel, n_tiles=n_tiles),
        name="gla_scan",
        grid=(batch, GLA_HEADS, nb),
        in_specs=in_specs,
        out_specs=pl.BlockSpec((seq, GLA_DV), lambda b, hh, i: (b, hh)),
        out_shape=jax.ShapeDtypeStruct((batch * seq, GLA_WIDTH), BF16),
        scratch_shapes=[pltpu.VMEM((GLA_DK, GLA_DV), F32), pltpu.VMEM((GLA_DK, GLA_DV), F32)],
        compiler_params=pltpu.CompilerParams(
            dimension_semantics=("parallel", "parallel", "arbitrary"), vmem_limit_bytes=VMEM_LIMIT),
    )(*args)


NA_SEG_COLS = ((0, 24), (24, 40), (40, 64))
NA_SEG_ROW0 = (0, 48, 80)
NA_STACK = 2 * GRID_W


NA_BASE_MID = NA_WIN_H // 2 - 1
NA_VARIANTS = 2 * NA_WIN_H - 1


def _na_variant(base, a):
    return jnp.where(base == NA_BASE_MID, a, jnp.where(base < NA_BASE_MID, NA_WIN_H + base, NA_WIN_H - 1 + base))


def _na_build_bias(rpb_ref, tab_ref):
    def build(v, carry):
        a = jnp.where(v < NA_WIN_H, v, 0)
        base = jnp.where(v < NA_WIN_H, NA_BASE_MID,
                         jnp.where(v < NA_WIN_H + NA_BASE_MID, v - NA_WIN_H, v - (NA_WIN_H - 1)))
        for hh in range(2):
            bias_rows = [rpb_ref[hh, pl.ds(base + ((t - a + NA_WIN_H) & (NA_WIN_H - 1)), 1), :]
                         for t in range(NA_WIN_H)]
            for s, (c0, c1) in enumerate(NA_SEG_COLS):
                n = c1 - c0
                vrows = [jnp.broadcast_to(br, (n, LANES)) for br in bias_rows]
                row0 = NA_SEG_ROW0[s] + hh * n
                rho = lax.broadcasted_iota(jnp.int32, (n, LANES), 0)
                lane = lax.broadcasted_iota(jnp.int32, (n, LANES), 1)
                cs = jnp.clip(c0 + rho - NA_WIN_W // 2, 0, GRID_W - NA_WIN_W)
                for half in range(2):
                    cb = s + half
                    cp = NA_CBW * cb + lane % NA_CBW
                    valid = (cp >= cs) & (cp < cs + NA_WIN_W)
                    acc = None
                    for i in range(NA_WIN_H):
                        shift = (NA_CBW * i - NA_CBW * cb + c0 - (NA_WIN_W - 1)) % LANES
                        rolled = pltpu.roll(vrows[i], shift, 1, stride=1, stride_axis=0)
                        acc = rolled if acc is None else jnp.where(lane // NA_CBW == i, rolled, acc)
                    tab_ref[v, row0:row0 + n, half * LANES:(half + 1) * LANES] = jnp.where(valid, acc, NEG_BIG)
        return carry

    lax.fori_loop(0, NA_VARIANTS, build, 0)


def _na_kernel(q_ref, kt_ref, v_ref, gate_ref, km_ref, vm_ref, rpb_ref, mb_ref, o_ref, tab_ref, sc_ref, sm_ref,
               *, rows_per_step, n_rows):
    w = GRID_W

    @pl.when(pl.program_id(1) == 0)
    def _():
        _na_build_bias(rpb_ref, tab_ref)

    head0_q = lax.broadcasted_iota(jnp.int32, (w, LANES), 1) < NA_DH
    srow = lax.broadcasted_iota(jnp.int32, (NA_STACK, N_META), 0)
    is_h1 = jnp.zeros((NA_STACK, N_META), jnp.bool_)
    for s, (c0, c1) in enumerate(NA_SEG_COLS):
        lo = NA_SEG_ROW0[s] + (c1 - c0)
        is_h1 = is_h1 | ((srow >= lo) & (srow < lo + (c1 - c0)))
    mb = jnp.where(is_h1, mb_ref[1], mb_ref[0])
    km = km_ref[...]
    vm = vm_ref[...]
    seg_rows = [slice(NA_SEG_ROW0[s], NA_SEG_ROW0[s] + 2 * (c1 - c0)) for s, (c0, c1) in enumerate(NA_SEG_COLS)]
    blk_rows = [slice(seg_rows[max(cb - 1, 0)].start, seg_rows[min(cb, len(seg_rows) - 1)].stop)
                for cb in range(NA_CB)]

    n_tiles = n_rows // NA_WIN_H
    key_lane = lax.broadcasted_iota(jnp.int32, (LANES, LANES), 1)

    def window(rl):
        row = pl.program_id(1) * rows_per_step + rl
        b = row // n_rows
        r = row - b * n_rows
        rs = jnp.clip(r - NA_WIN_H // 2, 0, n_rows - NA_WIN_H)
        a = rs & (NA_WIN_H - 1)
        return (b * n_tiles + rs // NA_WIN_H, a, (b + 1) * n_tiles - 1,
                _na_variant(rs - r + (NA_WIN_H - 1), a))

    def scores_store(slot, j, rl):
        m, a, m_last, var = window(rl)
        q = q_ref[pl.ds(pl.multiple_of(rl * w, w), w), :].astype(F32)
        q0 = jnp.where(head0_q, q, 0.0)
        q1 = q - q0
        qs = jnp.concatenate([piece[c0:c1] for c0, c1 in NA_SEG_COLS for piece in (q0, q1)],
                             axis=0).astype(BF16)
        from_m = key_lane >= a * NA_CBW
        m1 = jnp.minimum(m + 1, m_last)
        blocks = [_dot(qs[blk_rows[cb]],
                       jnp.where(from_m, kt_ref[NA_CB * m + cb], kt_ref[NA_CB * m1 + cb]))
                  for cb in range(NA_CB)]
        for s in range(len(NA_SEG_COLS)):
            n2 = seg_rows[s].stop - seg_rows[s].start
            for half in range(2):
                cb = s + half
                a0 = seg_rows[s].start - blk_rows[cb].start
                lanes = slice(half * LANES, (half + 1) * LANES)
                sc_ref[slot, j, seg_rows[s], lanes] = blocks[cb][a0:a0 + n2] + tab_ref[var, seg_rows[s], lanes]
        sm_ref[slot, j] = _dot_nt(qs, km) + mb

    def value_block(cb, m, a):
        groups = []
        for t in range(NA_WIN_H):
            row = NA_WIN_H * m + t + jnp.where(t < a, NA_WIN_H, 0)
            groups.append(v_ref[cb, pl.ds(pl.multiple_of(row * NA_CBW, NA_CBW), NA_CBW), :])
        return jnp.concatenate(groups, axis=0)

    def softmax_values(slot, j, rl):
        m_tile, a, _, _ = window(rl)
        sc = sc_ref[slot, j]
        sm = sm_ref[slot, j]
        m = jnp.maximum(sc[:, :LANES], sc[:, LANES:])
        m = jnp.maximum(jnp.max(m, axis=-1, keepdims=True), jnp.max(sm, axis=-1, keepdims=True))
        p = jnp.exp2(sc - m)
        pm = jnp.exp2(sm - m)
        den = (jnp.sum(p[:, :LANES] + p[:, LANES:], axis=-1, keepdims=True)
               + jnp.sum(pm, axis=-1, keepdims=True))
        p = p.astype(BF16)
        vals = [value_block(cb, m_tile, a) for cb in range(NA_CB)]
        o_meta = _dot(pm.astype(BF16), vm)
        inv = 1.0 / den
        outs = []
        for s, (c0, c1) in enumerate(NA_SEG_COLS):
            o_loc = _dot(p[seg_rows[s]], jnp.concatenate([vals[s], vals[s + 1]], axis=0))
            o = (o_loc + o_meta[seg_rows[s]]) * inv[seg_rows[s]]
            n = c1 - c0
            head0 = lax.broadcasted_iota(jnp.int32, (n, LANES), 1) < NA_DH
            outs.append(jnp.where(head0, o[:n], o[n:]))
        o = jnp.concatenate(outs, axis=0)
        qoff = pl.multiple_of(rl * w, w)
        gate = gate_ref[pl.ds(qoff, w), :].astype(F32)
        o_ref[pl.ds(qoff, w), :] = (o * _silu(gate)).astype(o_ref.dtype)

    n_groups = rows_per_step // NA_ROW_GROUP
    for j in range(NA_ROW_GROUP):
        scores_store(0, j, j)

    def trip(g, carry):
        slot = g % 2
        for j in range(NA_ROW_GROUP):
            scores_store(slot, j, g * NA_ROW_GROUP + j)
        for j in range(NA_ROW_GROUP):
            softmax_values(1 - slot, j, (g - 1) * NA_ROW_GROUP + j)
        return carry

    lax.fori_loop(1, n_groups, trip, 0)
    for j in range(NA_ROW_GROUP):
        softmax_values((n_groups - 1) % 2, j, (n_groups - 1) * NA_ROW_GROUP + j)


def _na_call(nq, nk_t, nv_cb, zmain, nk_m, nv_m, rpb_pad, meta_bias, *, batch, seq, rows_per_step):
    n_rows = seq // GRID_W
    total_rows = batch * n_rows
    assert n_rows % NA_WIN_H == 0 and total_rows % rows_per_step == 0 and rows_per_step % NA_ROW_GROUP == 0
    nsteps = total_rows // rows_per_step
    hp = NA_HEADS // 2
    tq = rows_per_step * GRID_W
    n_idx = 2 * NA_WIN_H - 1
    in_specs = [
        pl.BlockSpec((None, tq, LANES), lambda p, i: (p, i, 0)),
        pl.BlockSpec((batch * seq // LANES, LANES, LANES), lambda p, i: (0, p, 0)),
        pl.BlockSpec((None, NA_CB, batch * seq // NA_CB, LANES), lambda p, i: (p, 0, 0, 0)),
        pl.BlockSpec((tq, LANES), lambda p, i: (i, COL_NG // LANES + p)),
        pl.BlockSpec((N_META, LANES), lambda p, i: (0, p)),
        pl.BlockSpec((N_META, LANES), lambda p, i: (0, p)),
        pl.BlockSpec((2, n_idx, LANES), lambda p, i: (p, 0, 0)),
        pl.BlockSpec((2, 1, N_META), lambda p, i: (p, 0, 0)),
    ]
    return pl.pallas_call(
        functools.partial(_na_kernel, rows_per_step=rows_per_step, n_rows=n_rows),
        name="na_attn",
        grid=(hp, nsteps),
        in_specs=in_specs,
        out_specs=pl.BlockSpec((None, tq, LANES), lambda p, i: (p, i, 0)),
        out_shape=jax.ShapeDtypeStruct((hp, batch * seq, LANES), BF16),
        scratch_shapes=[pltpu.VMEM((NA_VARIANTS, NA_STACK, 2 * LANES), F32),
                        pltpu.VMEM((2, NA_ROW_GROUP, NA_STACK, 2 * LANES), F32),
                        pltpu.VMEM((2, NA_ROW_GROUP, NA_STACK, N_META), F32)],
        compiler_params=pltpu.CompilerParams(
            dimension_semantics=("arbitrary", "arbitrary"), vmem_limit_bytes=VMEM_LIMIT),
    )(nq, nk_t, nv_cb, zmain, nk_m, nv_m, rpb_pad, meta_bias)


def _out_kernel(x_ref, os_ref, gg_ref, na_ref, wf_ref, rowgain_ref, o_ref, w_ref):
    @pl.when(pl.program_id(0) == 0)
    def _():
        w_ref[...] = (wf_ref[...] * rowgain_ref[...]).astype(w_ref.dtype)

    d = o_ref.shape[1]
    na = jnp.concatenate([na_ref[p] for p in range(na_ref.shape[0])], axis=1)

    def na_tile(c):
        cols = slice(c * OUT_TN, (c + 1) * OUT_TN)
        o_ref[:, cols] = x_ref[:, cols] + _dot(na, w_ref[GLA_WIDTH:, cols])

    def gla_head(h):
        cols = slice(h * GLA_DV, (h + 1) * GLA_DV)
        sh = os_ref[:, cols].astype(F32)
        ms = jnp.mean(sh * sh, axis=-1, keepdims=True)
        return (sh * lax.rsqrt(ms + RMS_EPS) * _silu(gg_ref[:, cols].astype(F32))).astype(BF16)

    for c in range(d // OUT_TN):
        na_tile(c)
    for h in range(GLA_HEADS):
        y = gla_head(h)
        for c in range(d // OUT_TN):
            cols = slice(c * OUT_TN, (c + 1) * OUT_TN)
            o_ref[:, cols] += _dot(y, w_ref[h * GLA_DV:(h + 1) * GLA_DV, cols])


def _out_call(x2d, o_sum, zmain, o_na, w_out, row_gain, *, tm):
    m, d = x2d.shape
    assert m % tm == 0 and d % OUT_TN == 0
    resident = lambda shape: pl.BlockSpec(shape, lambda i: (0, 0), pipeline_mode=pl.Buffered(1))
    return pl.pallas_call(
        _out_kernel,
        name="out_proj",
        grid=(m // tm,),
        in_specs=[
            pl.BlockSpec((tm, d), lambda i: (i, 0)),
            pl.BlockSpec((tm, GLA_WIDTH), lambda i: (i, 0)),
            pl.BlockSpec((tm, GLA_WIDTH), lambda i: (i, COL_GG // GLA_WIDTH)),
            pl.BlockSpec((NA_WIDTH // LANES, tm, LANES), lambda i: (0, i, 0)),
            resident((GLA_WIDTH + NA_WIDTH, d)),
            resident((GLA_WIDTH + NA_WIDTH, 1)),
        ],
        out_specs=pl.BlockSpec((tm, d), lambda i: (i, 0)),
        out_shape=jax.ShapeDtypeStruct((m, d), F32),
        scratch_shapes=[pltpu.VMEM((GLA_WIDTH + NA_WIDTH, d), BF16)],
        compiler_params=pltpu.CompilerParams(
            dimension_semantics=("arbitrary",), vmem_limit_bytes=VMEM_LIMIT),
    )(x2d, o_sum, zmain, o_na, w_out, row_gain)


def _pick(n, cands):
    for c in cands:
        if n % c == 0:
            return c
    raise ValueError(f"no tile for {n}")


def kernel(x, meta_tokens, norm_g, w_in, w_decay_fwd, b_decay_fwd, w_decay_bwd, b_decay_bwd,
           gla_out_norm_g, q_norm_g, k_norm_g, rpb, meta_bias, w_out):
    batch, seq, d = x.shape
    assert d == D_MODEL and seq % TILE == 0 and seq % GRID_W == 0
    depth = norm_g.shape[0]
    assert depth == 1
    l = 0
    x2d = x.reshape(batch * seq, d)
    g = norm_g[l].reshape(1, d)

    assert w_in.shape[1:] == (d, D_IN_PROJ)
    wt = jnp.swapaxes(w_in[l], 0, 1)
    gain_qk = jnp.concatenate([jnp.tile(q_norm_g[l], NA_HEADS) * (NA_DH ** -0.5 * LOG2_E),
                               jnp.tile(k_norm_g[l], NA_HEADS)]).reshape(1, 2 * NA_WIDTH)

    zr = jnp.zeros((GLA_RANK, GLA_KEY_WIDTH), F32)
    ztail = jnp.zeros((LANES - 2 * GLA_RANK, GLA_KEY_WIDTH), F32)
    wd = jnp.stack([jnp.concatenate([w_decay_fwd[l], zr, ztail], axis=0),
                    jnp.concatenate([zr, w_decay_bwd[l], ztail], axis=0)]).astype(BF16)
    bias = jnp.stack([b_decay_fwd[l], b_decay_bwd[l]]).reshape(2, 1, GLA_KEY_WIDTH)
    tg, tg_m = _gla_constants()

    assert seq % (NA_WIN_H * GRID_W) == 0
    kgain_col = jnp.tile(k_norm_g[l], NA_HEADS).reshape(NA_WIDTH, 1)
    zmain, gla_ops, nq, nk_t, nv_cb, gtot, zmain_m_pad, nk_m, nv_m = _proj_call(
        x2d, meta_tokens, g, wt, gain_qk, kgain_col, wd, bias, tg, tm=NA_WIN_H * GRID_W)

    o_sum = _gla_call(zmain, gla_ops, gtot, zmain_m_pad, wd, bias, tg_m, batch=batch, seq=seq)

    rpb_pad = jnp.pad(rpb[l] * LOG2_E, ((0, 0), (0, 0), (0, 2 * NA_DH - rpb.shape[-1])))
    mb = (meta_bias[l] * LOG2_E).reshape(NA_HEADS, 1, N_META)
    o_na = _na_call(nq, nk_t, nv_cb, zmain, nk_m, nv_m, rpb_pad, mb, batch=batch, seq=seq,
                    rows_per_step=_pick(batch * seq // GRID_W, (256, 128, 64, 32, 16)))

    row_gain = jnp.concatenate([jnp.tile(gla_out_norm_g[l], GLA_HEADS), jnp.ones((NA_WIDTH,), F32)])
    out = _out_call(x2d, o_sum, zmain, o_na, w_out[l], row_gain.reshape(GLA_WIDTH + NA_WIDTH, 1),
                    tm=_pick(batch * seq, (1024, 512, 256, 128)))
    return out.reshape(batch, seq, d)
```

```python
import functools

import jax
import jax.numpy as jnp
import numpy as np
from jax import lax
from jax.experimental import pallas as pl
from jax.experimental.pallas import tpu as pltpu

F32 = jnp.float32
BF16 = jnp.bfloat16

D_MODEL = 1024
N_META = 16
GRID_W = 64
GLA_HEADS = 4
GLA_DK = 128
GLA_DV = 256
GLA_KEY_WIDTH = GLA_HEADS * GLA_DK
GLA_WIDTH = GLA_HEADS * GLA_DV
GLA_RANK = 16
GLA_TAU = 16.0
GLA_CHUNK = 64
NA_HEADS = 16
NA_DH = 64
NA_WIDTH = NA_HEADS * NA_DH
NA_WIN_H = 8
NA_WIN_W = 16
RMS_EPS = 1e-6
NEG_BIG = -1e30
LOG2_E = 1.4426950408889634
LANES = 128

COL_GV, COL_GG, COL_NG, COL_GQ, COL_GK, COL_R = 0, 1024, 2048, 3072, 3584, 4096
MAIN_W = COL_R + LANES
MAIN_TOK_W = COL_GQ
GLA_Q_IN, GLA_K_IN, GLA_K_ST = 0, 1, 2
NA_CBW = 16
NA_CB = GRID_W // NA_CBW
IN_GQ, IN_GK, IN_GV, IN_R, IN_GG, IN_NQ, IN_NK, IN_NV, IN_NG = 0, 512, 1024, 2048, 2080, 3104, 4128, 5152, 6176
D_IN_PROJ = 7200
PROJ_TN = 512
PROJ_W_CHUNK = 720
OUT_TN = 256

TILE = 2 * GLA_CHUNK
NA_ROW_GROUP = 16
VMEM_LIMIT = 58 * 1024 * 1024


def _split_bf16(x):
    hi = x.astype(BF16)
    lo = (x - hi.astype(F32)).astype(BF16)
    return hi, lo


def _dot(a, b):
    return jnp.dot(a, b, preferred_element_type=F32)


def _dot_nt(a, b):
    return lax.dot_general(a, b, (((1,), (1,)), ((), ())), preferred_element_type=F32)


def _dot_tn(a, b):
    return lax.dot_general(a, b, (((0,), (0,)), ((), ())), preferred_element_type=F32)


def _log_sigmoid(x):
    return jnp.minimum(x, 0.0) - jnp.log(1.0 + jnp.exp(-jnp.abs(x)))


def _silu(x):
    h = 0.5 * x
    return h + h * jnp.tanh(h)


SHARED_SEGMENTS = ((COL_GV, IN_GV, 1024), (COL_GG, IN_GG, 1024), (COL_NG, IN_NG, 1024))
META_SEGMENTS = ((COL_GQ, IN_GQ, 512), (COL_GK, IN_GK, 512),
                 (COL_R, IN_R, LANES))


def _store_rows(o_ref, c0, y):
    n = y.shape[1]
    for rl in range(y.shape[0] // GRID_W):
        for cb in range(NA_CB):
            src = rl * GRID_W + cb * NA_CBW
            for c in range(0, n, LANES):
                o_ref[(c0 + c) // LANES, cb, rl * NA_CBW:(rl + 1) * NA_CBW, :] = y[src:src + NA_CBW, c:c + LANES]


def _gla_log_decay(r, wd_ref, bdec_ref, d):
    return (_log_sigmoid(_dot(r, wd_ref[d]) + bdec_ref[d]) * (1.0 / GLA_TAU)).astype(BF16)


def _gla_operands_store(q, k, log_decay, d, t, tg_ref, go_ref, gtot_ref):
    rows = slice(t * TILE, (t + 1) * TILE)
    g = _dot(tg_ref[d], log_decay[rows])
    tots = []
    for c in range(TILE // GLA_CHUNK):
        last = c * GLA_CHUNK + (GLA_CHUNK - 1 if d == 0 else 0)
        tot = g[last:last + 1]
        gtot_ref[(t * TILE) // GLA_CHUNK + c:(t * TILE) // GLA_CHUNK + c + 1,
                 d * GLA_KEY_WIDTH:(d + 1) * GLA_KEY_WIDTH] = tot
        tots.append(jnp.broadcast_to(tot, (GLA_CHUNK, GLA_KEY_WIDTH)))
    dd = jnp.concatenate(tots, axis=0) - g
    operands = ((GLA_Q_IN, q[rows] * jnp.exp(g)), (GLA_K_IN, k[rows] * jnp.exp(-g)), (GLA_K_ST, k[rows] * jnp.exp(dd)))
    for kind, val in operands:
        val = val.astype(go_ref.dtype)
        for h in range(GLA_HEADS):
            go_ref[d, kind, h, rows, :] = val[:, h * GLA_DK:(h + 1) * GLA_DK]


def _proj_kernel(x0_ref, xn_ref, xm_ref, g_ref, wf_hbm, gain_ref, kgain_ref, wd_ref, bdec_ref, tg_ref,
                 zm_ref, go_ref, nq_ref, nk_ref, nv_ref, gtot_ref, zmm_ref, nkm_ref, nvm_ref,
                 u_scr, wt_ref, stage_ref, sem):
    def normed(x_ref):
        x = x_ref[...]
        ms = jnp.mean(x * x, axis=-1, keepdims=True)
        return (x * lax.rsqrt(ms + RMS_EPS) * g_ref[...]).astype(BF16)

    def head_norm_rows(acc, goff):
        first = lax.broadcasted_iota(jnp.int32, (acc.shape[0], LANES), 1) < NA_DH
        cols = []
        for c in range(0, acc.shape[1], LANES):
            a = acc[:, c:c + LANES]
            sq = a * a
            lo = jnp.sum(jnp.where(first, sq, 0.0), axis=-1, keepdims=True)
            hi = jnp.sum(jnp.where(first, 0.0, sq), axis=-1, keepdims=True)
            inv = jnp.where(first, lax.rsqrt(lo * (1.0 / NA_DH) + RMS_EPS), lax.rsqrt(hi * (1.0 / NA_DH) + RMS_EPS))
            cols.append(a * inv)
        return (jnp.concatenate(cols, axis=1) * gain_ref[:, goff:goff + PROJ_TN]).astype(BF16)

    def meta_tokens():
        um = normed(xm_ref)
        zmm_ref[...] = jnp.zeros_like(zmm_ref)
        for off in range(0, NA_WIDTH, PROJ_TN):
            acc = _dot_nt(um, wt_ref[IN_NK + off:IN_NK + off + PROJ_TN, :])
            nkm_ref[:, off:off + PROJ_TN] = head_norm_rows(acc, NA_WIDTH + off)
            acc = _dot_nt(um, wt_ref[IN_NV + off:IN_NV + off + PROJ_TN, :])
            nvm_ref[:, off:off + PROJ_TN] = acc.astype(nvm_ref.dtype)
        for dst, src, width in SHARED_SEGMENTS + META_SEGMENTS:
            acc = _dot_nt(um, wt_ref[src:src + width, :])
            if dst == COL_GQ:
                acc = acc * (GLA_DK ** -0.5)
            zmm_ref[:N_META, dst:dst + width] = acc.astype(zmm_ref.dtype)

    step = pl.program_id(0)

    def load_weights():
        rows = stage_ref.shape[1]
        n_chunks = wt_ref.shape[0] // rows

        def chunk_copy(c):
            return pltpu.make_async_copy(wf_hbm.at[pl.ds(c * rows, rows), :], stage_ref.at[c % 2], sem.at[c % 2])

        chunk_copy(0).start()
        chunk_copy(1).start()
        u_scr[0] = normed(x0_ref)
        for c in range(n_chunks):
            if 1 <= c < n_chunks - 1:
                chunk_copy(c + 1).start()
            chunk_copy(c).wait()
            wt_ref[c * rows:(c + 1) * rows, :] = stage_ref[c % 2].astype(wt_ref.dtype)

    @pl.when(step == 0)
    def _():
        load_weights()
        meta_tokens()

    slot = step % 2
    u = u_scr[slot]
    tm = u.shape[0]

    def prepare_next():
        u_scr[1 - slot] = normed(xn_ref)

    def main_tile(dst, src, off, n):
        acc = _dot_nt(u, wt_ref[src + off:src + off + n, :])
        zm_ref[:, dst + off:dst + off + n] = acc.astype(zm_ref.dtype)

    def nv_tile(off):
        acc = _dot_nt(u, wt_ref[IN_NV + off:IN_NV + off + PROJ_TN, :])
        _store_rows(nv_ref, off, acc.astype(nv_ref.dtype))

    def nq_tile(off):
        acc = _dot_nt(u, wt_ref[IN_NQ + off:IN_NQ + off + PROJ_TN, :])
        y = head_norm_rows(acc, off)
        for c in range(0, PROJ_TN, LANES):
            nq_ref[(off + c) // LANES] = y[:, c:c + LANES]

    def nk_tile(off):
        u_cb = jnp.concatenate([u[rl * GRID_W + cb * NA_CBW:rl * GRID_W + (cb + 1) * NA_CBW]
                                for cb in range(NA_CB) for rl in range(tm // GRID_W)], axis=0)
        acc_t = _dot_nt(wt_ref[IN_NK + off:IN_NK + off + PROJ_TN, :], u_cb)
        for h in range(PROJ_TN // NA_DH):
            rows = slice(off + h * NA_DH, off + (h + 1) * NA_DH)
            blk = acc_t[h * NA_DH:(h + 1) * NA_DH]
            ms = jnp.sum(blk * blk, axis=0, keepdims=True) * (1.0 / NA_DH)
            y = (blk * lax.rsqrt(ms + RMS_EPS) * kgain_ref[rows, :]).astype(BF16)
            for cb in range(NA_CB):
                nk_ref[cb, rows, :] = y[:, cb * LANES:(cb + 1) * LANES]

    tiles = [functools.partial(nk_tile, off) for off in range(0, NA_WIDTH, PROJ_TN)]
    tiles += [functools.partial(nq_tile, off) for off in range(0, NA_WIDTH, PROJ_TN)]
    tiles += [functools.partial(nv_tile, off) for off in range(0, NA_WIDTH, PROJ_TN)]
    for dst, src, width in SHARED_SEGMENTS:
        for off in range(0, width, PROJ_TN):
            tiles.append(functools.partial(main_tile, dst, src, off, min(PROJ_TN, width - off)))

    r = _dot_nt(u, wt_ref[IN_R:IN_R + LANES, :]).astype(BF16)
    q = _dot_nt(u, wt_ref[IN_GQ:IN_GQ + GLA_KEY_WIDTH, :]) * (GLA_DK ** -0.5)
    k = _dot_nt(u, wt_ref[IN_GK:IN_GK + GLA_KEY_WIDTH, :])
    log_decay = [None, None]

    def decay_stage(d):
        log_decay[d] = _gla_log_decay(r, wd_ref, bdec_ref, d)

    stages = [functools.partial(decay_stage, 0), functools.partial(decay_stage, 1), None]
    stages += [functools.partial(lambda d, t: _gla_operands_store(q, k, log_decay[d], d, t, tg_ref, go_ref, gtot_ref),
                                 d, t)
               for d in range(2) for t in range(tm // TILE)]
    for i, tile in enumerate(tiles):
        if i < len(stages) and stages[i] is not None:
            stages[i]()
        tile()
    prepare_next()
    assert len(stages) <= len(tiles)


def _proj_call(x2d, xm, g, wt, gain_qk, kgain_col, wd, bdec, tg, *, tm):
    m, d = x2d.shape
    assert m % tm == 0 and wt.shape == (D_IN_PROJ, d) and xm.shape == (N_META, d)
    assert tm == NA_WIN_H * GRID_W and NA_CB * LANES == tm
    resident = lambda shape: pl.BlockSpec(shape, lambda i: (0,) * len(shape), pipeline_mode=pl.Buffered(1))
    whole = lambda shape: pl.BlockSpec(shape, lambda i: (0,) * len(shape))
    n_pairs = NA_WIDTH // LANES
    q_spec = pl.BlockSpec((n_pairs, tm, LANES), lambda i: (0, i, 0))
    q_shape = jax.ShapeDtypeStruct((n_pairs, m, LANES), BF16)
    v_spec = pl.BlockSpec((n_pairs, NA_CB, tm // NA_CB, LANES), lambda i: (0, 0, i, 0))
    v_shape = jax.ShapeDtypeStruct((n_pairs, NA_CB, m // NA_CB, LANES), BF16)
    k_spec = pl.BlockSpec((NA_CB, NA_WIDTH, LANES), lambda i: (i, 0, 0))
    k_shape = jax.ShapeDtypeStruct((m // LANES, NA_WIDTH, LANES), BF16)
    n_chunk_rows = tm // GLA_CHUNK
    meta_plain = jax.ShapeDtypeStruct((N_META, NA_WIDTH), BF16)
    return pl.pallas_call(
        _proj_kernel,
        name="proj_tokens",
        grid=(m // tm,),
        in_specs=[
            resident((tm, d)),
            pl.BlockSpec((tm, d), lambda i: (jnp.minimum(i + 1, m // tm - 1), 0)),
            resident((N_META, d)),
            resident((1, d)),
            pl.BlockSpec(memory_space=pl.ANY),
            resident((1, 2 * NA_WIDTH)),
            resident((NA_WIDTH, 1)),
            resident((2, LANES, GLA_KEY_WIDTH)),
            resident((2, 1, GLA_KEY_WIDTH)),
            resident((2, TILE, TILE)),
        ],
        out_specs=[pl.BlockSpec((tm, MAIN_TOK_W), lambda i: (i, 0)),
                   pl.BlockSpec((2, 3, GLA_HEADS, tm, GLA_DK), lambda i: (0, 0, 0, i, 0)),
                   q_spec, k_spec, v_spec,
                   pl.BlockSpec((n_chunk_rows, 2 * GLA_KEY_WIDTH), lambda i: (i, 0)),
                   whole((TILE, MAIN_W)), whole((N_META, NA_WIDTH)), whole((N_META, NA_WIDTH))],
        out_shape=[jax.ShapeDtypeStruct((m, MAIN_TOK_W), BF16),
                   jax.ShapeDtypeStruct((2, 3, GLA_HEADS, m, GLA_DK), BF16), q_shape, k_shape, v_shape,
                   jax.ShapeDtypeStruct((m // GLA_CHUNK, 2 * GLA_KEY_WIDTH), F32),
                   jax.ShapeDtypeStruct((TILE, MAIN_W), BF16), meta_plain, meta_plain],
        scratch_shapes=[pltpu.VMEM((2, tm, d), BF16), pltpu.VMEM((D_IN_PROJ, d), BF16),
                        pltpu.VMEM((2, PROJ_W_CHUNK, d), F32), pltpu.SemaphoreType.DMA((2,))],
        compiler_params=pltpu.CompilerParams(
            dimension_semantics=("arbitrary",), vmem_limit_bytes=VMEM_LIMIT),
    )(x2d, x2d, xm, g, wt, gain_qk, kgain_col, wd, bdec, tg)


def _gla_kernel(qf_ref, kif_ref, ksf_ref, vf_ref, gtf_ref, qb_ref, kib_ref, ksb_ref, vb_ref, gtb_ref,
                km_ref, vm_ref, rm_ref, wd_ref, b_ref, tgm_ref,
                o_ref, sf_ref, sb_ref, *, n_tiles):
    row = lax.broadcasted_iota(jnp.int32, (TILE, TILE), 0)
    col = lax.broadcasted_iota(jnp.int32, (TILE, TILE), 1)
    same = (row // GLA_CHUNK) == (col // GLA_CHUNK)
    masks = (same & (row >= col), same & (row <= col))

    @pl.when(pl.program_id(2) == 0)
    def _():
        rowm = lax.broadcasted_iota(jnp.int32, (TILE, GLA_DK), 0)
        lg = _log_sigmoid(_dot(rm_ref[...], wd_ref[...]) + b_ref[...]) * (1.0 / GLA_TAU)
        hi, lo = _split_bf16(jnp.where(rowm < N_META, lg, 0.0))
        d_m = _dot(tgm_ref[...], jnp.concatenate([hi, lo], axis=0))
        k_st = (km_ref[...].astype(F32) * jnp.exp(d_m)).astype(BF16)
        sf_ref[...] = _dot_tn(k_st, vm_ref[...])
        sb_ref[...] = jnp.zeros_like(sb_ref)

    def scan(step):
        items = []
        for t in range(n_tiles):
            items.append((0, t))
            items.append((1, n_tiles - 1 - t))
        refs = ((qf_ref, kif_ref, ksf_ref, vf_ref), (qb_ref, kib_ref, ksb_ref, vb_ref))
        out_half = (step, 1 - step)
        a_cols = (jnp.exp(jnp.transpose(gtf_ref[...])), jnp.exp(jnp.transpose(gtb_ref[...])))

        def rows_of(t):
            return slice(t * TILE, (t + 1) * TILE)

        amats = [jnp.where(masks[d], _dot_nt(refs[d][0][rows_of(t), :], refs[d][1][rows_of(t), :]), 0.0).astype(BF16)
                 for d, t in items]

        def state_independent(idx):
            d, t = items[idx]
            v = refs[d][3][rows_of(t), :]
            k_st = refs[d][2][rows_of(t), :]
            u = [_dot_tn(k_st[h * GLA_CHUNK:(h + 1) * GLA_CHUNK], v[h * GLA_CHUNK:(h + 1) * GLA_CHUNK])
                 for h in range(2)]
            return _dot(amats[idx], v), u

        state = [sf_ref[...], sb_ref[...]]
        ahead = state_independent(0)
        for idx, (d, t) in enumerate(items):
            o_intra, u = ahead
            if idx + 1 < len(items):
                ahead = state_independent(idx + 1)
            s = state[d]
            outs = [None, None]
            for h in ((0, 1) if d == 0 else (1, 0)):
                rows = slice(t * TILE + h * GLA_CHUNK, t * TILE + (h + 1) * GLA_CHUNK)
                outs[h] = o_intra[h * GLA_CHUNK:(h + 1) * GLA_CHUNK] + _dot(refs[d][0][rows, :], s.astype(BF16))
                chunk = 2 * t + h
                s = a_cols[d][:, chunk:chunk + 1] * s + u[h]
            state[d] = s
            o = jnp.concatenate(outs, axis=0)
            out_rows = rows_of(out_half[d] * n_tiles + t)
            if step == 1:
                o = o + o_ref[out_rows, :].astype(F32)
            o_ref[out_rows, :] = o.astype(o_ref.dtype)
        sf_ref[...] = state[0]
        sb_ref[...] = state[1]

    for step in range(2):
        pl.when(pl.program_id(2) == step)(functools.partial(scan, step))


def _gla_constants():
    i = np.arange(TILE)
    same = (i[:, None] // GLA_CHUNK) == (i[None, :] // GLA_CHUNK)
    inc_f = same & (i[None, :] <= i[:, None])
    inc_b = same & (i[None, :] >= i[:, None])
    to_end_meta = i[None, :] > i[:, None]
    dup = lambda m: np.concatenate([m, m], axis=1)
    tg = jnp.asarray(np.stack([inc_f, inc_b]), BF16)
    return tg, jnp.asarray(dup(to_end_meta), BF16)


def _gla_call(zmain, gla_ops, gtot, zmain_m, wd, bias, tg_m, *, batch, seq):
    nb = 2
    assert seq % (nb * TILE) == 0
    t_blk = seq // nb
    n_tiles = t_blk // TILE
    n_chunks = t_blk // GLA_CHUNK

    def tok(fwd):
        if fwd:
            return lambda b, hh, i: b * nb + i
        return lambda b, hh, i: b * nb + (nb - 1 - i)

    def dir_specs(d):
        t = tok(d == 0)
        operand = lambda kind: pl.BlockSpec((None, None, None, t_blk, GLA_DK),
                                            lambda b, hh, i: (d, kind, hh, t(b, hh, i), 0))
        return [
            operand(GLA_Q_IN), operand(GLA_K_IN), operand(GLA_K_ST),
            pl.BlockSpec((t_blk, GLA_DV), lambda b, hh, i: (t(b, hh, i), COL_GV // GLA_DV + hh)),
            pl.BlockSpec((n_chunks, GLA_DK), lambda b, hh, i: (t(b, hh, i), d * GLA_HEADS + hh)),
        ]

    in_specs = (
        dir_specs(0) + dir_specs(1)
        + [pl.BlockSpec((TILE, GLA_DK), lambda b, hh, i: (0, COL_GK // GLA_DK + hh)),
           pl.BlockSpec((TILE, GLA_DV), lambda b, hh, i: (0, COL_GV // GLA_DV + hh)),
           pl.BlockSpec((TILE, LANES), lambda b, hh, i: (0, COL_R // LANES)),
           pl.BlockSpec((None, LANES, GLA_DK), lambda b, hh, i: (0, 0, hh)),
           pl.BlockSpec((None, 1, GLA_DK), lambda b, hh, i: (0, 0, hh)),
           pl.BlockSpec((TILE, 2 * TILE), lambda b, hh, i: (0, 0))]
    )
    args = ([gla_ops] * 3 + [zmain, gtot]) * 2 + [zmain_m] * 3 + [wd, bias, tg_m]
    return pl.pallas_call(
        functools.partial(_gla_kernel, n_tiles=n_tiles),
        name="gla_scan",
        grid=(batch, GLA_HEADS, nb),
        in_specs=in_specs,
        out_specs=pl.BlockSpec((seq, GLA_DV), lambda b, hh, i: (b, hh)),
        out_shape=jax.ShapeDtypeStruct((batch * seq, GLA_WIDTH), BF16),
        scratch_shapes=[pltpu.VMEM((GLA_DK, GLA_DV), F32), pltpu.VMEM((GLA_DK, GLA_DV), F32)],
        compiler_params=pltpu.CompilerParams(
            dimension_semantics=("parallel", "parallel", "arbitrary"), vmem_limit_bytes=VMEM_LIMIT),
    )(*args)


NA_SEG_COLS = ((0, 24), (24, 40), (40, 64))
NA_SEG_ROW0 = (0, 48, 80)
NA_STACK = 2 * GRID_W


NA_BASE_MID = NA_WIN_H // 2 - 1
NA_VARIANTS = 2 * NA_WIN_H - 1


def _na_variant(base, a):
    return jnp.where(base == NA_BASE_MID, a, jnp.where(base < NA_BASE_MID, NA_WIN_H + base, NA_WIN_H - 1 + base))


def _na_build_bias(rpb_ref, tab_ref):
    def build(v, carry):
        a = jnp.where(v < NA_WIN_H, v, 0)
        base = jnp.where(v < NA_WIN_H, NA_BASE_MID,
                         jnp.where(v < NA_WIN_H + NA_BASE_MID, v - NA_WIN_H, v - (NA_WIN_H - 1)))
        for hh in range(2):
            bias_rows = [rpb_ref[hh, pl.ds(base + ((t - a + NA_WIN_H) & (NA_WIN_H - 1)), 1), :]
                         for t in range(NA_WIN_H)]
            for s, (c0, c1) in enumerate(NA_SEG_COLS):
                n = c1 - c0
                vrows = [jnp.broadcast_to(br, (n, LANES)) for br in bias_rows]
                row0 = NA_SEG_ROW0[s] + hh * n
                rho = lax.broadcasted_iota(jnp.int32, (n, LANES), 0)
                lane = lax.broadcasted_iota(jnp.int32, (n, LANES), 1)
                cs = jnp.clip(c0 + rho - NA_WIN_W // 2, 0, GRID_W - NA_WIN_W)
                for half in range(2):
                    cb = s + half
                    cp = NA_CBW * cb + lane % NA_CBW
                    valid = (cp >= cs) & (cp < cs + NA_WIN_W)
                    acc = None
                    for i in range(NA_WIN_H):
                        shift = (NA_CBW * i - NA_CBW * cb + c0 - (NA_WIN_W - 1)) % LANES
                        rolled = pltpu.roll(vrows[i], shift, 1, stride=1, stride_axis=0)
                        acc = rolled if acc is None else jnp.where(lane // NA_CBW == i, rolled, acc)
                    tab_ref[v, row0:row0 + n, half * LANES:(half + 1) * LANES] = jnp.where(valid, acc, NEG_BIG)
        return carry

    lax.fori_loop(0, NA_VARIANTS, build, 0)


def _na_kernel(q_ref, kt_ref, v_ref, gate_ref, km_ref, vm_ref, rpb_ref, mb_ref, o_ref, tab_ref, sc_ref, sm_ref,
               *, rows_per_step, n_rows):
    w = GRID_W

    @pl.when(pl.program_id(1) == 0)
    def _():
        _na_build_bias(rpb_ref, tab_ref)

    head0_q = lax.broadcasted_iota(jnp.int32, (w, LANES), 1) < NA_DH
    srow = lax.broadcasted_iota(jnp.int32, (NA_STACK, N_META), 0)
    is_h1 = jnp.zeros((NA_STACK, N_META), jnp.bool_)
    for s, (c0, c1) in enumerate(NA_SEG_COLS):
        lo = NA_SEG_ROW0[s] + (c1 - c0)
        is_h1 = is_h1 | ((srow >= lo) & (srow < lo + (c1 - c0)))
    mb = jnp.where(is_h1, mb_ref[1], mb_ref[0])
    km = km_ref[...]
    vm = vm_ref[...]
    seg_rows = [slice(NA_SEG_ROW0[s], NA_SEG_ROW0[s] + 2 * (c1 - c0)) for s, (c0, c1) in enumerate(NA_SEG_COLS)]
    blk_rows = [slice(seg_rows[max(cb - 1, 0)].start, seg_rows[min(cb, len(seg_rows) - 1)].stop)
                for cb in range(NA_CB)]

    n_tiles = n_rows // NA_WIN_H
    key_lane = lax.broadcasted_iota(jnp.int32, (LANES, LANES), 1)

    def window(rl):
        row = pl.program_id(1) * rows_per_step + rl
        b = row // n_rows
        r = row - b * n_rows
        rs = jnp.clip(r - NA_WIN_H // 2, 0, n_rows - NA_WIN_H)
        a = rs & (NA_WIN_H - 1)
        return (b * n_tiles + rs // NA_WIN_H, a, (b + 1) * n_tiles - 1,
                _na_variant(rs - r + (NA_WIN_H - 1), a))

    def scores_store(slot, j, rl):
        m, a, m_last, var = window(rl)
        q = q_ref[pl.ds(pl.multiple_of(rl * w, w), w), :].astype(F32)
        q0 = jnp.where(head0_q, q, 0.0)
        q1 = q - q0
        qs = jnp.concatenate([piece[c0:c1] for c0, c1 in NA_SEG_COLS for piece in (q0, q1)],
                             axis=0).astype(BF16)
        from_m = key_lane >= a * NA_CBW
        m1 = jnp.minimum(m + 1, m_last)
        blocks = [_dot(qs[blk_rows[cb]],
                       jnp.where(from_m, kt_ref[NA_CB * m + cb], kt_ref[NA_CB * m1 + cb]))
                  for cb in range(NA_CB)]
        for s in range(len(NA_SEG_COLS)):
            n2 = seg_rows[s].stop - seg_rows[s].start
            for half in range(2):
                cb = s + half
                a0 = seg_rows[s].start - blk_rows[cb].start
                lanes = slice(half * LANES, (half + 1) * LANES)
                sc_ref[slot, j, seg_rows[s], lanes] = blocks[cb][a0:a0 + n2] + tab_ref[var, seg_rows[s], lanes]
        sm_ref[slot, j] = _dot_nt(qs, km) + mb

    def value_block(cb, m, a):
        groups = []
        for t in range(NA_WIN_H):
            row = NA_WIN_H * m + t + jnp.where(t < a, NA_WIN_H, 0)
            groups.append(v_ref[cb, pl.ds(pl.multiple_of(row * NA_CBW, NA_CBW), NA_CBW), :])
        return jnp.concatenate(groups, axis=0)

    def softmax_values(slot, j, rl):
        m_tile, a, _, _ = window(rl)
        sc = sc_ref[slot, j]
        sm = sm_ref[slot, j]
        m = jnp.maximum(sc[:, :LANES], sc[:, LANES:])
        m = jnp.maximum(jnp.max(m, axis=-1, keepdims=True), jnp.max(sm, axis=-1, keepdims=True))
        p = jnp.exp2(sc - m)
        pm = jnp.exp2(sm - m)
        den = (jnp.sum(p[:, :LANES] + p[:, LANES:], axis=-1, keepdims=True)
               + jnp.sum(pm, axis=-1, keepdims=True))
        p = p.astype(BF16)
        vals = [value_block(cb, m_tile, a) for cb in range(NA_CB)]
        o_meta = _dot(pm.astype(BF16), vm)
        inv = 1.0 / den
        outs = []
        for s, (c0, c1) in enumerate(NA_SEG_COLS):
            o_loc = _dot(p[seg_rows[s]], jnp.concatenate([vals[s], vals[s + 1]], axis=0))
            o = (o_loc + o_meta[seg_rows[s]]) * inv[seg_rows[s]]
            n = c1 - c0
            head0 = lax.broadcasted_iota(jnp.int32, (n, LANES), 1) < NA_DH
            outs.append(jnp.where(head0, o[:n], o[n:]))
        o = jnp.concatenate(outs, axis=0)
        qoff = pl.multiple_of(rl * w, w)
        gate = gate_ref[pl.ds(qoff, w), :].astype(F32)
        o_ref[pl.ds(qoff, w), :] = (o * _silu(gate)).astype(o_ref.dtype)

    n_groups = rows_per_step // NA_ROW_GROUP
    for j in range(NA_ROW_GROUP):
        scores_store(0, j, j)

    def trip(g, carry):
        slot = g % 2
        for j in range(NA_ROW_GROUP):
            scores_store(slot, j, g * NA_ROW_GROUP + j)
        for j in range(NA_ROW_GROUP):
            softmax_values(1 - slot, j, (g - 1) * NA_ROW_GROUP + j)
        return carry

    lax.fori_loop(1, n_groups, trip, 0)
    for j in range(NA_ROW_GROUP):
        softmax_values((n_groups - 1) % 2, j, (n_groups - 1) * NA_ROW_GROUP + j)


def _na_call(nq, nk_t, nv_cb, zmain, nk_m, nv_m, rpb_pad, meta_bias, *, batch, seq, rows_per_step):
    n_rows = seq // GRID_W
    total_rows = batch * n_rows
    assert n_rows % NA_WIN_H == 0 and total_rows % rows_per_step == 0 and rows_per_step % NA_ROW_GROUP == 0
    nsteps = total_rows // rows_per_step
    hp = NA_HEADS // 2
    tq = rows_per_step * GRID_W
    n_idx = 2 * NA_WIN_H - 1
    in_specs = [
        pl.BlockSpec((None, tq, LANES), lambda p, i: (p, i, 0)),
        pl.BlockSpec((batch * seq // LANES, LANES, LANES), lambda p, i: (0, p, 0)),
        pl.BlockSpec((None, NA_CB, batch * seq // NA_CB, LANES), lambda p, i: (p, 0, 0, 0)),
        pl.BlockSpec((tq, LANES), lambda p, i: (i, COL_NG // LANES + p)),
        pl.BlockSpec((N_META, LANES), lambda p, i: (0, p)),
        pl.BlockSpec((N_META, LANES), lambda p, i: (0, p)),
        pl.BlockSpec((2, n_idx, LANES), lambda p, i: (p, 0, 0)),
        pl.BlockSpec((2, 1, N_META), lambda p, i: (p, 0, 0)),
    ]
    return pl.pallas_call(
        functools.partial(_na_kernel, rows_per_step=rows_per_step, n_rows=n_rows),
        name="na_attn",
        grid=(hp, nsteps),
        in_specs=in_specs,
        out_specs=pl.BlockSpec((None, tq, LANES), lambda p, i: (p, i, 0)),
        out_shape=jax.ShapeDtypeStruct((hp, batch * seq, LANES), BF16),
        scratch_shapes=[pltpu.VMEM((NA_VARIANTS, NA_STACK, 2 * LANES), F32),
                        pltpu.VMEM((2, NA_ROW_GROUP, NA_STACK, 2 * LANES), F32),
                        pltpu.VMEM((2, NA_ROW_GROUP, NA_STACK, N_META), F32)],
        compiler_params=pltpu.CompilerParams(
            dimension_semantics=("arbitrary", "arbitrary"), vmem_limit_bytes=VMEM_LIMIT),
    )(nq, nk_t, nv_cb, zmain, nk_m, nv_m, rpb_pad, meta_bias)


def _out_kernel(x_ref, os_ref, gg_ref, na_ref, wf_ref, rowgain_ref, o_ref, w_ref):
    @pl.when(pl.program_id(0) == 0)
    def _():
        w_ref[...] = (wf_ref[...] * rowgain_ref[...]).astype(w_ref.dtype)

    d = o_ref.shape[1]
    na = jnp.concatenate([na_ref[p] for p in range(na_ref.shape[0])], axis=1)

    def na_tile(c):
        cols = slice(c * OUT_TN, (c + 1) * OUT_TN)
        o_ref[:, cols] = x_ref[:, cols] + _dot(na, w_ref[GLA_WIDTH:, cols])

    def gla_head(h):
        cols = slice(h * GLA_DV, (h + 1) * GLA_DV)
        sh = os_ref[:, cols].astype(F32)
        ms = jnp.mean(sh * sh, axis=-1, keepdims=True)
        return (sh * lax.rsqrt(ms + RMS_EPS) * _silu(gg_ref[:, cols].astype(F32))).astype(BF16)

    for c in range(d // OUT_TN):
        na_tile(c)
    for h in range(GLA_HEADS):
        y = gla_head(h)
        for c in range(d // OUT_TN):
            cols = slice(c * OUT_TN, (c + 1) * OUT_TN)
            o_ref[:, cols] += _dot(y, w_ref[h * GLA_DV:(h + 1) * GLA_DV, cols])


def _out_call(x2d, o_sum, zmain, o_na, w_out, row_gain, *, tm):
    m, d = x2d.shape
    assert m % tm == 0 and d % OUT_TN == 0
    resident = lambda shape: pl.BlockSpec(shape, lambda i: (0, 0), pipeline_mode=pl.Buffered(1))
    return pl.pallas_call(
        _out_kernel,
        name="out_proj",
        grid=(m // tm,),
        in_specs=[
            pl.BlockSpec((tm, d), lambda i: (i, 0)),
            pl.BlockSpec((tm, GLA_WIDTH), lambda i: (i, 0)),
            pl.BlockSpec((tm, GLA_WIDTH), lambda i: (i, COL_GG // GLA_WIDTH)),
            pl.BlockSpec((NA_WIDTH // LANES, tm, LANES), lambda i: (0, i, 0)),
            resident((GLA_WIDTH + NA_WIDTH, d)),
            resident((GLA_WIDTH + NA_WIDTH, 1)),
        ],
        out_specs=pl.BlockSpec((tm, d), lambda i: (i, 0)),
        out_shape=jax.ShapeDtypeStruct((m, d), F32),
        scratch_shapes=[pltpu.VMEM((GLA_WIDTH + NA_WIDTH, d), BF16)],
        compiler_params=pltpu.CompilerParams(
            dimension_semantics=("arbitrary",), vmem_limit_bytes=VMEM_LIMIT),
    )(x2d, o_sum, zmain, o_na, w_out, row_gain)


def _pick(n, cands):
    for c in cands:
        if n % c == 0:
            return c
    raise ValueError(f"no tile for {n}")


def kernel(x, meta_tokens, norm_g, w_in, w_decay_fwd, b_decay_fwd, w_decay_bwd, b_decay_bwd,
           gla_out_norm_g, q_norm_g, k_norm_g, rpb, meta_bias, w_out):
    batch, seq, d = x.shape
    assert d == D_MODEL and seq % TILE == 0 and seq % GRID_W == 0
    depth = norm_g.shape[0]
    assert depth == 1
    l = 0
    x2d = x.reshape(batch * seq, d)
    g = norm_g[l].reshape(1, d)

    assert w_in.shape[1:] == (d, D_IN_PROJ)
    wt = jnp.swapaxes(w_in[l], 0, 1)
    gain_qk = jnp.concatenate([jnp.tile(q_norm_g[l], NA_HEADS) * (NA_DH ** -0.5 * LOG2_E),
                               jnp.tile(k_norm_g[l], NA_HEADS)]).reshape(1, 2 * NA_WIDTH)

    zr = jnp.zeros((GLA_RANK, GLA_KEY_WIDTH), F32)
    ztail = jnp.zeros((LANES - 2 * GLA_RANK, GLA_KEY_WIDTH), F32)
    wd = jnp.stack([jnp.concatenate([w_decay_fwd[l], zr, ztail], axis=0),
                    jnp.concatenate([zr, w_decay_bwd[l], ztail], axis=0)]).astype(BF16)
    bias = jnp.stack([b_decay_fwd[l], b_decay_bwd[l]]).reshape(2, 1, GLA_KEY_WIDTH)
    tg, tg_m = _gla_constants()

    assert seq % (NA_WIN_H * GRID_W) == 0
    kgain_col = jnp.tile(k_norm_g[l], NA_HEADS).reshape(NA_WIDTH, 1)
    zmain, gla_ops, nq, nk_t, nv_cb, gtot, zmain_m_pad, nk_m, nv_m = _proj_call(
        x2d, meta_tokens, g, wt, gain_qk, kgain_col, wd, bias, tg, tm=NA_WIN_H * GRID_W)

    o_sum = _gla_call(zmain, gla_ops, gtot, zmain_m_pad, wd, bias, tg_m, batch=batch, seq=seq)

    rpb_pad = jnp.pad(rpb[l] * LOG2_E, ((0, 0), (0, 0), (0, 2 * NA_DH - rpb.shape[-1])))
    mb = (meta_bias[l] * LOG2_E).reshape(NA_HEADS, 1, N_META)
    o_na = _na_call(nq, nk_t, nv_cb, zmain, nk_m, nv_m, rpb_pad, mb, batch=batch, seq=seq,
                    rows_per_step=_pick(batch * seq // GRID_W, (256, 128, 64, 32, 16)))

    row_gain = jnp.concatenate([jnp.tile(gla_out_norm_g[l], GLA_HEADS), jnp.ones((NA_WIDTH,), F32)])
    out = _out_call(x2d, o_sum, zmain, o_na, w_out[l], row_gain.reshape(GLA_WIDTH + NA_WIDTH, 1),
                    tm=_pick(batch * seq, (1024, 512, 256, 128)))
    return out.reshape(batch, seq, d)
```
